```python
import math
import jax, jax.numpy as jnp
from jax import lax
import numpy as np

D_MODEL = 1024
BATCH = 16
SEQ = 256
DEPTH = 2
DEC_BATCH = 2
DEC_SEQ = 4096
PAST_LEN = 512

GRID_W = 64
ROPE_BASE = 10000.0
EPS = 1e-6
Q_BLOCK = 128
DA_HEADS = 4
DA_DH = 64
DA_OUT = DA_HEADS * 2 * DA_DH
MLA_HEADS = 8
MLA_Q_RANK = 256
MLA_KV_RANK = 128
MLA_NOPE = 64
MLA_ROPE = 32
MLA_V = 64
MLA_OUT = MLA_HEADS * MLA_V
SWA_HEADS = 8
SWA_KV_HEADS = 2
SWA_DH = 64
WINDOW = 128
W_BLOCK = 128
SWA_OUT = SWA_HEADS * SWA_DH
IN_SIZES = (DA_OUT, DA_OUT, DA_OUT, MLA_Q_RANK, MLA_KV_RANK, MLA_ROPE,
            SWA_HEADS * SWA_DH, SWA_KV_HEADS * SWA_DH, SWA_KV_HEADS * SWA_DH)
IN_COLS = 3 * DA_OUT + MLA_Q_RANK + MLA_KV_RANK + MLA_ROPE + SWA_HEADS * SWA_DH + 2 * SWA_KV_HEADS * SWA_DH
N_EXPERTS = 16
EXPERT_FF = 1024
CAPACITY_FACTOR = 2

kernel_name = 'hybrid_flow_prefix_diffmla_swa_ec'


def rms_norm(x, g):
    xf = x.astype(jnp.float32)
    y = xf * lax.rsqrt(jnp.mean(xf * xf, axis=-1, keepdims=True) + EPS)
    return y.astype(x.dtype) * g


def axial_rope(n_tokens, rot_dim):
    rows = n_tokens // GRID_W
    row = jnp.repeat(jnp.arange(rows, dtype=jnp.float32), GRID_W)
    col = jnp.tile(jnp.arange(GRID_W, dtype=jnp.float32), rows)
    n_freq = rot_dim // 4
    inv = ROPE_BASE ** (-jnp.arange(n_freq, dtype=jnp.float32) / n_freq)
    ang = jnp.concatenate([row[:, None] * inv, col[:, None] * inv], axis=-1)
    return jnp.cos(ang), jnp.sin(ang)


def apply_rope(x, cs):
    cos, sin = cs
    xf = x.astype(jnp.float32).reshape(x.shape[:-1] + (x.shape[-1] // 2, 2))
    x1, x2 = xf[..., 0], xf[..., 1]
    c = cos[None, :, None, :]
    s = sin[None, :, None, :]
    out = jnp.stack([x1 * c - x2 * s, x1 * s + x2 * c], axis=-1).reshape(x.shape)
    return out.astype(x.dtype)


def _q_blocks(t):
    B, S = t.shape[0], t.shape[1]
    return t.reshape((B, S // Q_BLOCK, Q_BLOCK) + t.shape[2:]).swapaxes(0, 1)


def _unblock(o):
    o = o.swapaxes(0, 1)
    return o.reshape((o.shape[0], o.shape[1] * o.shape[2]) + o.shape[3:])


def softmax_attention(q, k, v, scale, sink=None):
    def one(qb):
        s = jnp.einsum('bqhd,bkhd->bhqk', qb, k).astype(jnp.float32) * scale
        if sink is None:
            p = jax.nn.softmax(s, axis=-1)
        else:
            sk = jnp.broadcast_to(sink.astype(jnp.float32)[None, :, None, None], s.shape[:-1] + (1,))
            p = jax.nn.softmax(jnp.concatenate([s, sk], axis=-1), axis=-1)[..., :-1]
        return jnp.einsum('bhqk,bkhd->bqhd', p.astype(v.dtype), v)
    return _unblock(lax.map(one, _q_blocks(q)))


def diff_attention(q1, q2, k1, k2, v, lam, scale):
    def one(qs):
        b1, b2 = qs
        a1 = jax.nn.softmax(jnp.einsum('bqhd,bkhd->bhqk', b1, k1).astype(jnp.float32) * scale, axis=-1)
        a2 = jax.nn.softmax(jnp.einsum('bqhd,bkhd->bhqk', b2, k2).astype(jnp.float32) * scale, axis=-1)
        return jnp.einsum('bhqk,bkhd->bqhd', (a1 - lam * a2).astype(v.dtype), v)
    return _unblock(lax.map(one, (_q_blocks(q1), _q_blocks(q2))))


def window_attention(q, k, v, ck, cv, sink, scale):
    B, N, Hq, dh = q.shape
    Hkv = k.shape[2]
    G = Hq // Hkv
    nb = N // W_BLOCK
    qb = q.reshape(B, nb, W_BLOCK, Hkv, G, dh)
    pad = ((0, 0), (W_BLOCK, W_BLOCK), (0, 0), (0, 0))
    kp, vp = jnp.pad(k, pad), jnp.pad(v, pad)
    idx = jnp.arange(nb)[:, None] * W_BLOCK + jnp.arange(3 * W_BLOCK)[None, :]
    kb, vb = kp[:, idx], vp[:, idx]
    qpos = jnp.arange(N).reshape(nb, W_BLOCK)
    kpos = idx - W_BLOCK
    valid = ((jnp.abs(qpos[:, :, None] - kpos[:, None, :]) <= WINDOW)
             & (kpos[:, None, :] >= 0) & (kpos[:, None, :] < N))
    s_loc = jnp.einsum('bnqhgd,bnjhd->bnhgqj', qb, kb).astype(jnp.float32) * scale
    s_loc = jnp.where(valid[None, :, None, None], s_loc, -jnp.inf)
    s_ctx = jnp.einsum('bnqhgd,bchd->bnhgqc', qb, ck).astype(jnp.float32) * scale
    s_sink = jnp.broadcast_to(sink.astype(jnp.float32).reshape(Hkv, G)[None, None, :, :, None, None],
                              s_loc.shape[:-1] + (1,))
    p = jax.nn.softmax(jnp.concatenate([s_loc, s_ctx, s_sink], axis=-1), axis=-1)
    n_loc = s_loc.shape[-1]
    p_loc = p[..., :n_loc].astype(v.dtype)
    p_ctx = p[..., n_loc:-1].astype(v.dtype)
    o = jnp.einsum('bnhgqj,bnjhd->bnqhgd', p_loc, vb) + jnp.einsum('bnhgqc,bchd->bnqhgd', p_ctx, cv)
    return o.reshape(B, N, Hq, dh)


def mixing_sublayer(h, p, layer, ctx, rope):
    B, S, D = h.shape
    latent = ctx is not None
    offs = np.cumsum(IN_SIZES)[:-1].tolist()
    da_q, da_k, da_v, m_q, m_kv, m_kr, s_q, s_k, s_v = jnp.split(h @ p['w_in'], offs, axis=-1)

    q = da_q.reshape(B, S, DA_HEADS, 2, DA_DH)
    q1, q2 = q[..., 0, :], q[..., 1, :]
    k_own = da_k.reshape(B, S, DA_HEADS, 2 * DA_DH)
    v_own = da_v.reshape(B, S, DA_HEADS, 2 * DA_DH)
    if latent:
        rope64, rope32 = rope
        q1, q2 = apply_rope(q1, rope64), apply_rope(q2, rope64)
        k_lat = jnp.concatenate([apply_rope(k_own[..., :DA_DH], rope64),
                                 apply_rope(k_own[..., DA_DH:], rope64)], axis=-1)
        k_all = jnp.concatenate([ctx[0], k_lat], axis=1)
        v_all = jnp.concatenate([ctx[1], v_own], axis=1)
    else:
        k_all, v_all = k_own, v_own
    lam_init = 0.8 - 0.6 * math.exp(-0.3 * layer)
    lam = (jnp.exp(jnp.sum((p['da_lq1'] * p['da_lk1']).astype(jnp.float32)))
           - jnp.exp(jnp.sum((p['da_lq2'] * p['da_lk2']).astype(jnp.float32))) + lam_init)
    o = diff_attention(q1, q2, k_all[..., :DA_DH], k_all[..., DA_DH:], v_all, lam, DA_DH ** -0.5)
    o_da = (rms_norm(o, p['da_subln']) * (1.0 - lam_init)).reshape(B, S, DA_OUT)

    cq = (rms_norm(m_q, p['mla_q_norm']) @ p['mla_w_qb']).reshape(B, S, MLA_HEADS, MLA_NOPE + MLA_ROPE)
    q_nope, q_rope = cq[..., :MLA_NOPE], cq[..., MLA_NOPE:]
    ckv_own = rms_norm(m_kv, p['mla_kv_norm'])
    kr_own = m_kr
    if latent:
        q_rope = apply_rope(q_rope, rope32)
        kr_lat = apply_rope(kr_own[:, :, None, :], rope32)[:, :, 0]
        ckv_all = jnp.concatenate([ctx[2], ckv_own], axis=1)
        kr_all = jnp.concatenate([ctx[3], kr_lat], axis=1)
    else:
        ckv_all, kr_all = ckv_own, kr_own
    Sk = ckv_all.shape[1]
    kv = (ckv_all @ p['mla_w_kvb']).reshape(B, Sk, MLA_HEADS, MLA_NOPE + MLA_V)
    k_m = jnp.concatenate([kv[..., :MLA_NOPE],
                           jnp.broadcast_to(kr_all[:, :, None, :], (B, Sk, MLA_HEADS, MLA_ROPE))], axis=-1)
    q_m = jnp.concatenate([q_nope, q_rope], axis=-1)
    o_mla = softmax_attention(q_m, k_m, kv[..., MLA_NOPE:], (MLA_NOPE + MLA_ROPE) ** -0.5).reshape(B, S, MLA_OUT)

    q_s = s_q.reshape(B, S, SWA_HEADS, SWA_DH)
    k_s = s_k.reshape(B, S, SWA_KV_HEADS, SWA_DH)
    v_s = s_v.reshape(B, S, SWA_KV_HEADS, SWA_DH)
    if latent:
        o = window_attention(apply_rope(q_s, rope64), apply_rope(k_s, rope64), v_s,
                             ctx[4], ctx[5], p['swa_sink'], SWA_DH ** -0.5)
    else:
        G = SWA_HEADS // SWA_KV_HEADS
        o = softmax_attention(q_s, jnp.repeat(k_s, G, axis=2), jnp.repeat(v_s, G, axis=2),
                              SWA_DH ** -0.5, p['swa_sink'])
    o_swa = o.reshape(B, S, SWA_OUT)

    g_da, g_mla, g_swa = jnp.split(jax.nn.sigmoid(h @ p['w_gate']), 3, axis=-1)
    merged = (g_da * (o_da @ p['w_br_da']) + g_mla * (o_mla @ p['w_br_mla'])
              + g_swa * (o_swa @ p['w_br_swa']))
    own = (k_own, v_own, ckv_own, kr_own, k_s, v_s)
    return merged @ p['w_o'], own


def expert_choice_ffn(h, w_router, w1, w3, w2):
    B, S, D = h.shape
    n_tok = B * S
    cap = CAPACITY_FACTOR * n_tok // N_EXPERTS
    xt = h.reshape(n_tok, D)
    aff = jax.nn.softmax((xt @ w_router).astype(jnp.float32), axis=-1)
    g, idx = lax.top_k(aff.T, cap)
    xe = xt[idx]
    hid = jax.nn.silu(jnp.einsum('ecd,edf->ecf', xe, w1)) * jnp.einsum('ecd,edf->ecf', xe, w3)
    ye = jnp.einsum('ecf,efd->ecd', hid, w2) * g[..., None].astype(h.dtype)
    out = jnp.zeros_like(xt).at[idx.reshape(-1)].add(ye.reshape(-1, D))
    return out.reshape(B, S, D)


def trunk_layer(x, mod, p, layer, ctx, rope):
    shift1, scale1, gate1, shift2, scale2, gate2 = jnp.split(mod, 6, axis=-1)
    h = rms_norm(x, p['norm1']) * (1.0 + scale1) + shift1
    mixed, own = mixing_sublayer(h, p, layer, ctx, rope)
    x = x + gate1 * mixed
    h = rms_norm(x, p['norm2']) * (1.0 + scale2) + shift2
    x = x + gate2 * expert_choice_ffn(h, p['w_router'], p['w_ff1'], p['w_ff3'], p['w_ff2'])
    return x, own


def setup_inputs(seed: int = 0) -> dict:
    key = jax.random.key(seed)
    ks = iter(jax.random.split(key, 48))

    def nrm(shape, scale):
        return jax.random.normal(next(ks), shape, jnp.float32) * scale

    def gain(shape):
        return 1.0 + nrm(shape, 0.05)

    D = D_MODEL
    return {
        'x_prompt': nrm((BATCH, SEQ, D), 1.0),
        'x_sample': nrm((DEC_BATCH, DEC_SEQ, D), 1.0),
        'cache_da_k': nrm((DEC_BATCH, DEPTH, PAST_LEN, DA_HEADS, 2 * DA_DH), 1.0),
        'cache_da_v': nrm((DEC_BATCH, DEPTH, PAST_LEN, DA_HEADS, 2 * DA_DH), 1.0),
        'cache_mla_ckv': nrm((DEC_BATCH, DEPTH, PAST_LEN, MLA_KV_RANK), 1.0),
        'cache_mla_krope': nrm((DEC_BATCH, DEPTH, PAST_LEN, MLA_ROPE), 1.0),
        'cache_swa_k': nrm((DEC_BATCH, DEPTH, PAST_LEN, SWA_KV_HEADS, SWA_DH), 1.0),
        'cache_swa_v': nrm((DEC_BATCH, DEPTH, PAST_LEN, SWA_KV_HEADS, SWA_DH), 1.0),
        'c': nrm((DEC_BATCH, D), 1.0),
        'c_ctx': nrm((D,), 1.0),
        'w_ada': nrm((DEPTH, D, 6 * D), 0.5 * D ** -0.5),
        'b_ada': nrm((DEPTH, 6 * D), 0.01),
        'norm1': gain((DEPTH, D)),
        'norm2': gain((DEPTH, D)),
        'w_in': nrm((DEPTH, D, IN_COLS), D ** -0.5),
        'da_lq1': nrm((DEPTH, DA_DH), 0.1),
        'da_lk1': nrm((DEPTH, DA_DH), 0.1),
        'da_lq2': nrm((DEPTH, DA_DH), 0.1),
        'da_lk2': nrm((DEPTH, DA_DH), 0.1),
        'da_subln': gain((DEPTH, 2 * DA_DH)),
        'mla_q_norm': gain((DEPTH, MLA_Q_RANK)),
        'mla_w_qb': nrm((DEPTH, MLA_Q_RANK, MLA_HEADS * (MLA_NOPE + MLA_ROPE)), MLA_Q_RANK ** -0.5),
        'mla_kv_norm': gain((DEPTH, MLA_KV_RANK)),
        'mla_w_kvb': nrm((DEPTH, MLA_KV_RANK, MLA_HEADS * (MLA_NOPE + MLA_V)), MLA_KV_RANK ** -0.5),
        'swa_sink': nrm((DEPTH, SWA_HEADS), 0.5),
        'w_gate': nrm((DEPTH, D, 3 * D), D ** -0.5),
        'w_br_da': nrm((DEPTH, DA_OUT, D), DA_OUT ** -0.5),
        'w_br_mla': nrm((DEPTH, MLA_OUT, D), MLA_OUT ** -0.5),
        'w_br_swa': nrm((DEPTH, SWA_OUT, D), SWA_OUT ** -0.5),
        'w_o': nrm((DEPTH, D, D), D ** -0.5),
        'w_router': nrm((DEPTH, D, N_EXPERTS), D ** -0.5),
        'w_ff1': nrm((DEPTH, N_EXPERTS, D, EXPERT_FF), D ** -0.5),
        'w_ff3': nrm((DEPTH, N_EXPERTS, D, EXPERT_FF), D ** -0.5),
        'w_ff2': nrm((DEPTH, N_EXPERTS, EXPERT_FF, D), EXPERT_FF ** -0.5),
        'final_norm': gain((D,)),
    }


def reference(x_prompt, x_sample, cache_da_k, cache_da_v, cache_mla_ckv, cache_mla_krope,
              cache_swa_k, cache_swa_v, c, c_ctx, w_ada, b_ada, norm1, norm2, w_in,
              da_lq1, da_lk1, da_lq2, da_lk2, da_subln, mla_q_norm, mla_w_qb, mla_kv_norm,
              mla_w_kvb, swa_sink, w_gate, w_br_da, w_br_mla, w_br_swa, w_o, w_router,
              w_ff1, w_ff3, w_ff2, final_norm):
    n_lat = x_sample.shape[1]
    rope = (axial_rope(n_lat, DA_DH), axial_rope(n_lat, MLA_ROPE))
    xp, xs = x_prompt, x_sample
    new = [[], [], [], [], [], []]
    for l in range(DEPTH):
        p = {'norm1': norm1[l], 'norm2': norm2[l], 'w_in': w_in[l],
             'da_lq1': da_lq1[l], 'da_lk1': da_lk1[l], 'da_lq2': da_lq2[l], 'da_lk2': da_lk2[l],
             'da_subln': da_subln[l], 'mla_q_norm': mla_q_norm[l], 'mla_w_qb': mla_w_qb[l],
             'mla_kv_norm': mla_kv_norm[l], 'mla_w_kvb': mla_w_kvb[l], 'swa_sink': swa_sink[l],
             'w_gate': w_gate[l], 'w_br_da': w_br_da[l], 'w_br_mla': w_br_mla[l],
             'w_br_swa': w_br_swa[l], 'w_o': w_o[l], 'w_router': w_router[l],
             'w_ff1': w_ff1[l], 'w_ff3': w_ff3[l], 'w_ff2': w_ff2[l]}
        mod_ctx = (jax.nn.silu(c_ctx) @ w_ada[l] + b_ada[l])[None, None, :]
        xp, own = trunk_layer(xp, mod_ctx, p, l, None, None)
        for i in range(6):
            new[i].append(own[i])
        mod_lat = (jax.nn.silu(c) @ w_ada[l] + b_ada[l])[:, None, :]
        ctx = (cache_da_k[:, l], cache_da_v[:, l], cache_mla_ckv[:, l], cache_mla_krope[:, l],
               cache_swa_k[:, l], cache_swa_v[:, l])
        xs, _ = trunk_layer(xs, mod_lat, p, l, ctx, rope)
    y_prompt = rms_norm(xp, final_norm)
    y_sample = rms_norm(xs, final_norm)
    return (y_prompt, y_sample, jnp.stack(new[0], axis=1), jnp.stack(new[1], axis=1),
            jnp.stack(new[2], axis=1), jnp.stack(new[3], axis=1), jnp.stack(new[4], axis=1),
            jnp.stack(new[5], axis=1))
```

```python
import functools
import math

import jax
import jax.numpy as jnp
from jax import lax
from jax.experimental import pallas as pl
from jax.experimental.pallas import tpu as pltpu

F32 = jnp.float32
BF = jnp.bfloat16
I32 = jnp.int32

D_MODEL = 1024
BATCH = 16
SEQ = 256
DEPTH = 2
DEC_BATCH = 2
DEC_SEQ = 4096
PAST_LEN = 512
GRID_W = 64
ROPE_BASE = 10000.0
EPS = 1e-6
DA_HEADS = 4
DA_DH = 64
DA_OUT = DA_HEADS * 2 * DA_DH
MLA_HEADS = 8
MLA_Q_RANK = 256
MLA_KV_RANK = 128
MLA_NOPE = 64
MLA_ROPE = 32
MLA_V = 64
MLA_OUT = MLA_HEADS * MLA_V
SWA_HEADS = 8
SWA_KV_HEADS = 2
SWA_DH = 64
W_BLOCK = 128
SWA_OUT = SWA_HEADS * SWA_DH
N_EXPERTS = 16
EXPERT_FF = 1024
CAPACITY_FACTOR = 2

LANE = 128
SUB = 8
HALF = 64
N_CTX = BATCH * SEQ
N_LAT = DEC_BATCH * DEC_SEQ
KEYS = PAST_LEN + DEC_SEQ
CAP_CTX = CAPACITY_FACTOR * N_CTX // N_EXPERTS
CAP_LAT = CAPACITY_FACTOR * N_LAT // N_EXPERTS
SLOTS = CAP_CTX + CAP_LAT
TOK_HALF = 4096
N_HALVES = (N_CTX + N_LAT) // TOK_HALF
D_CHUNKS = D_MODEL // LANE

C_DAQ, C_DAK, C_DAV = 0, 512, 1024
C_MQ, C_MKV, C_MKR = 1536, 1792, 1920
C_SQ, C_SK, C_SV = 2048, 2560, 2688
PROJ_COLS = 2816
GATE_COLS = 3 * D_MODEL
MLA_SCALE = (MLA_NOPE + MLA_ROPE) ** -0.5
HEAD_SCALE = DA_DH ** -0.5

TM_PRE = 256
TM_POST = 512
CHUNK = 512
TQ = 256
ROUTE_TILE = 128
ROUTE_LANES = 2048
FFN_ROWS = 512
FF_SPLIT = 2


def _cp(sem, vmem_mb):
    return pltpu.CompilerParams(dimension_semantics=sem, vmem_limit_bytes=vmem_mb * 1024 * 1024)


def _dot(a, b):
    return jnp.dot(a, b, preferred_element_type=F32)


def _dot_nt(a, b):
    return lax.dot_general(a, b, (((1,), (1,)), ((), ())), preferred_element_type=F32)


def _split(a):
    hi = a.astype(BF)
    lo = (a - hi.astype(F32)).astype(BF)
    return hi, lo


def _dot3(a, w):
    ah, al = _split(a)
    wh, wl = _split(w)
    return _dot(ah, wh) + _dot(ah, wl) + _dot(al, wh)


def _rms(x, g):
    return x * lax.rsqrt(jnp.mean(x * x, axis=-1, keepdims=True) + EPS) * g


def _lane(shape):
    return lax.broadcasted_iota(I32, shape, len(shape) - 1)


def _softmax_pv(segs, sink=None):
    m = None
    for s, _ in segs:
        ms = jnp.max(s, axis=-1, keepdims=True)
        m = ms if m is None else jnp.maximum(m, ms)
    if sink is not None:
        m = jnp.maximum(m, sink)
    l = None
    o = None
    for s, v in segs:
        e = jnp.exp(s - m)
        ls = jnp.sum(e, axis=-1, keepdims=True)
        os_ = _dot(e.astype(BF), v)
        l = ls if l is None else l + ls
        o = os_ if o is None else o + os_
    if sink is not None:
        l = l + jnp.exp(sink - m)
    return o / l


def _mask_halves(q):
    lo = _lane(q.shape) < HALF
    return jnp.where(lo, q, 0.0).astype(BF), jnp.where(lo, 0.0, q).astype(BF)


def _pair(o_a, o_b):
    return jnp.where(_lane(o_a.shape) < HALF, o_a, o_b)


def _dup(x):
    r = pltpu.roll(x, HALF, 1)
    lo = _lane(x.shape) < HALF
    return jnp.where(lo, x, r), jnp.where(lo, r, x)


def _rope(x, c, s):
    n = x.shape[-1]
    even = (_lane(x.shape) % 2) == 0
    sw = jnp.where(even, pltpu.roll(x, n - 1, 1), pltpu.roll(x, 1, 1))
    return x * c + sw * s


def _da_head(q, k, v, lam, subln, lam_init):
    q1, q2 = _mask_halves(q)
    o1 = _softmax_pv([(_dot_nt(q1, k), v)])
    o2 = _softmax_pv([(_dot_nt(q2, k), v)])
    return _rms(o1 - lam * o2, subln) * (1.0 - lam_init)


def _ada_kernel(c_ref, w_ref, b_ref, o_ref):
    c = c_ref[...]
    a = c * jax.nn.sigmoid(c)
    o_ref[...] = _dot3(a, w_ref[...]) + b_ref[...]


def _ada_call(c_all, w_ada, b_ada):
    tn = 1536
    return pl.pallas_call(
        _ada_kernel,
        grid=(DEPTH, 6 * D_MODEL // tn),
        in_specs=[pl.BlockSpec((SUB, D_MODEL), lambda l, j: (0, 0)),
                  pl.BlockSpec((None, D_MODEL, tn), lambda l, j: (l, 0, j)),
                  pl.BlockSpec((None, 1, tn), lambda l, j: (l, 0, j))],
        out_specs=pl.BlockSpec((None, SUB, tn), lambda l, j: (l, 0, j)),
        out_shape=jax.ShapeDtypeStruct((DEPTH, SUB, 6 * D_MODEL), F32),
        compiler_params=_cp(("parallel", "parallel"), 40),
        name="ada",
    )(c_all, w_ada, b_ada.reshape(DEPTH, 1, 6 * D_MODEL))


def _lam_kernel(q1, k1, q2, k2, o_ref):
    s1 = jnp.sum(q1[...] * k1[...], axis=-1, keepdims=True)
    s2 = jnp.sum(q2[...] * k2[...], axis=-1, keepdims=True)
    row = lax.broadcasted_iota(I32, (DEPTH, 1), 0)
    init = jnp.zeros((DEPTH, 1), F32)
    for l in range(DEPTH):
        init = jnp.where(row == l, _lam_init(l), init)
    o_ref[...] = jnp.broadcast_to(jnp.exp(s1) - jnp.exp(s2) + init, o_ref.shape)


def _lam_init(layer):
    return 0.8 - 0.6 * math.exp(-0.3 * layer)


def _lam_call(q1, k1, q2, k2):
    return pl.pallas_call(
        _lam_kernel,
        out_shape=jax.ShapeDtypeStruct((DEPTH, LANE), F32),
        name="lam",
    )(q1, k1, q2, k2)


def _moe_rows(moe_ref, tm):
    return jnp.concatenate([moe_ref[pl.ds(k, tm, stride=D_CHUNKS), :] for k in range(D_CHUNKS)], axis=1)


def _pre_body(x, mod_ref, n1_ref, win_ref, wg_ref, proj_ref, gate_ref):
    m = mod_ref[...]
    h = _rms(x, n1_ref[...]) * (1.0 + m[1:2]) + m[0:1]
    hb = h.astype(BF)
    proj_ref[...] = _dot(hb, win_ref[...])
    gate_ref[...] = jax.nn.sigmoid(_dot(hb, wg_ref[...])).astype(BF)


def _pre_first_kernel(x_ref, mod_ref, n1_ref, win_ref, wg_ref, proj_ref, gate_ref):
    _pre_body(x_ref[...], mod_ref, n1_ref, win_ref, wg_ref, proj_ref, gate_ref)


def _pre_next_kernel(x_ref, moe_ref, modp_ref, mod_ref, n1_ref, win_ref, wg_ref, xo_ref, proj_ref, gate_ref):
    x = x_ref[...] + modp_ref[...][5:6] * _moe_rows(moe_ref, TM_PRE)
    xo_ref[...] = x
    _pre_body(x, mod_ref, n1_ref, win_ref, wg_ref, proj_ref, gate_ref)


def _mod_spec(tm, latent):
    per = DEC_SEQ // tm
    if latent:
        return pl.BlockSpec((None, 6, D_MODEL), lambda i: (1 + i // per, 0, 0))
    return pl.BlockSpec((None, 6, D_MODEL), lambda i: (0, 0, 0))


def _const(shape):
    nd = len(shape)
    return pl.BlockSpec(shape, lambda *_: (0,) * nd)


def _pre_call(x, moe, mod_prev, mod, n1, win, wg, latent):
    n = x.shape[0]
    tm = TM_PRE
    row = pl.BlockSpec((tm, D_MODEL), lambda i: (i, 0))
    w_specs = [_mod_spec(tm, latent), _const((1, D_MODEL)), _const((D_MODEL, PROJ_COLS)),
               _const((D_MODEL, GATE_COLS))]
    outs = [jax.ShapeDtypeStruct((n, PROJ_COLS), F32), jax.ShapeDtypeStruct((n, GATE_COLS), BF)]
    out_specs = [pl.BlockSpec((tm, PROJ_COLS), lambda i: (i, 0)), pl.BlockSpec((tm, GATE_COLS), lambda i: (i, 0))]
    if moe is None:
        proj, gate = pl.pallas_call(
            _pre_first_kernel, grid=(n // tm,), in_specs=[row] + w_specs, out_specs=out_specs, out_shape=outs,
            compiler_params=_cp(("parallel",), 52), name="pre_first",
        )(x, mod, n1, win, wg)
        return x, proj, gate
    moe_off = (N_CTX // tm) if latent else 0
    moe_spec = pl.BlockSpec((tm * D_CHUNKS, LANE), lambda i: (i + moe_off, 0))
    xo, proj, gate = pl.pallas_call(
        _pre_next_kernel, grid=(n // tm,),
        in_specs=[row, moe_spec, _mod_spec(tm, latent)] + w_specs,
        out_specs=[row] + out_specs,
        out_shape=[jax.ShapeDtypeStruct((n, D_MODEL), F32)] + outs,
        compiler_params=_cp(("parallel",), 52), name="pre_next",
    )(x, moe, mod_prev, mod, n1, win, wg)
    return xo, proj, gate


def _ctx_attn_kernel(lam_init, lam_ref, sink_ref, p_ref, subln_ref, qn_ref, wqb_ref, kvn_ref, wkv_ref,
                     oda_ref, omla_ref, oswa_ref, ckv_ref):
    p = p_ref[...]
    lam = lam_ref[0]
    for h in range(DA_HEADS):
        blk = slice(h * LANE, (h + 1) * LANE)
        q = p[:, C_DAQ + h * LANE:C_DAQ + (h + 1) * LANE] * HEAD_SCALE
        k = p[:, C_DAK + h * LANE:C_DAK + (h + 1) * LANE].astype(BF)
        v = p[:, C_DAV + h * LANE:C_DAV + (h + 1) * LANE].astype(BF)
        oda_ref[:, blk] = _da_head(q, k, v, lam, subln_ref[...], lam_init).astype(BF)
    qn = _rms(p[:, C_MQ:C_MQ + MLA_Q_RANK], qn_ref[...]).astype(BF)
    cq = _dot(qn, wqb_ref[...])
    ckv = _rms(p[:, C_MKV:C_MKV + MLA_KV_RANK], kvn_ref[...])
    ckv_ref[...] = ckv
    kv = _dot(ckv.astype(BF), wkv_ref[...])
    kr_sh = pltpu.roll(p[:, C_MKR:C_MKR + LANE], HALF, 1)
    for m in range(MLA_HEADS // 2):
        vpair = kv[:, MLA_HEADS * LANE + m * LANE:MLA_HEADS * LANE + (m + 1) * LANE].astype(BF)
        outs = []
        for h in (2 * m, 2 * m + 1):
            q = (cq[:, h * LANE:(h + 1) * LANE] * MLA_SCALE).astype(BF)
            k = (kv[:, h * LANE:(h + 1) * LANE] + kr_sh).astype(BF)
            outs.append(_softmax_pv([(_dot_nt(q, k), vpair)]))
        omla_ref[:, m * LANE:(m + 1) * LANE] = _pair(outs[0], outs[1]).astype(BF)
    kd = _dup(p[:, C_SK:C_SK + LANE])
    vd = _dup(p[:, C_SV:C_SV + LANE])
    for m in range(SWA_HEADS // 2):
        g = (2 * m) // (SWA_HEADS // SWA_KV_HEADS)
        k = kd[g].astype(BF)
        v = vd[g].astype(BF)
        qa, qb = _mask_halves(p[:, C_SQ + m * LANE:C_SQ + (m + 1) * LANE] * HEAD_SCALE)
        oa = _softmax_pv([(_dot_nt(qa, k), v)], sink_ref[2 * m])
        ob = _softmax_pv([(_dot_nt(qb, k), v)], sink_ref[2 * m + 1])
        oswa_ref[:, m * LANE:(m + 1) * LANE] = _pair(oa, ob).astype(BF)


def _smem():
    return pl.BlockSpec(memory_space=pltpu.SMEM)


def _ctx_attn_call(layer, lam, sink, proj, subln, qn, wqb, kvn, wkv):
    row512 = pl.BlockSpec((SEQ, 512), lambda b: (b, 0))
    return pl.pallas_call(
        functools.partial(_ctx_attn_kernel, _lam_init(layer)),
        grid=(BATCH,),
        in_specs=[_smem(), _smem(), pl.BlockSpec((SEQ, PROJ_COLS), lambda b: (b, 0)),
                  _const((1, LANE)), _const((1, MLA_Q_RANK)), _const((MLA_Q_RANK, MLA_HEADS * LANE)),
                  _const((1, MLA_KV_RANK)), _const((MLA_KV_RANK, MLA_HEADS * LANE + MLA_OUT))],
        out_specs=[row512, row512, row512, pl.BlockSpec((SEQ, MLA_KV_RANK), lambda b: (b, 0))],
        out_shape=[jax.ShapeDtypeStruct((N_CTX, 512), BF)] * 3 + [jax.ShapeDtypeStruct((N_CTX, MLA_KV_RANK), F32)],
        compiler_params=_cp(("parallel",), 40), name="ctx_attn",
    )(lam, sink, proj, subln, qn, wqb, kvn, wkv)


def _lat_prep_kernel(p_ref, tab_ref, cdk_ref, cdv_ref, cckv_ref, ckr_ref, csk_ref, csv_ref,
                     qn_ref, wqb_ref, kvn_ref, wkv_ref,
                     daq_ref, mlaq_ref, swaq_ref, dak_ref, dav_ref, mlak_ref, mlav_ref, swak_ref, swav_ref):
    j = pl.program_id(1)

    def write_kv(dk, dv, ckv, kr_sh, sk, sv):
        dak_ref[...] = dk.astype(BF)
        dav_ref[...] = dv.astype(BF)
        kv = _dot(ckv.astype(BF), wkv_ref[...])
        for h in range(MLA_HEADS):
            mlak_ref[:, h * LANE:(h + 1) * LANE] = (kv[:, h * LANE:(h + 1) * LANE] + kr_sh).astype(BF)
        mlav_ref[...] = kv[:, MLA_HEADS * LANE:].astype(BF)
        k0, k1 = _dup(sk)
        v0, v1 = _dup(sv)
        swak_ref[:, 0:LANE] = k0.astype(BF)
        swak_ref[:, LANE:2 * LANE] = k1.astype(BF)
        swav_ref[:, 0:LANE] = v0.astype(BF)
        swav_ref[:, LANE:2 * LANE] = v1.astype(BF)

    @pl.when(j == 0)
    def _():
        write_kv(cdk_ref[...], cdv_ref[...], cckv_ref[...], pltpu.roll(ckr_ref[...], HALF, 1),
                 csk_ref[...], csv_ref[...])

    @pl.when(j > 0)
    def _():
        c64, s64 = tab_ref[0], tab_ref[1]
        cmq, smq = tab_ref[2], tab_ref[3]
        ckr, skr = tab_ref[4], tab_ref[5]

        def blk(c0, h):
            return p_ref[:, c0 + h * LANE:c0 + (h + 1) * LANE]

        for h in range(DA_HEADS):
            daq_ref[:, h * LANE:(h + 1) * LANE] = (_rope(blk(C_DAQ, h), c64, s64) * HEAD_SCALE).astype(BF)
            swaq_ref[:, h * LANE:(h + 1) * LANE] = (_rope(blk(C_SQ, h), c64, s64) * HEAD_SCALE).astype(BF)
        dk = jnp.concatenate([_rope(blk(C_DAK, h), c64, s64) for h in range(DA_HEADS)], axis=1)
        qn = _rms(p_ref[:, C_MQ:C_MQ + MLA_Q_RANK], qn_ref[...]).astype(BF)
        cq = _dot(qn, wqb_ref[...])
        for h in range(MLA_HEADS):
            mlaq_ref[:, h * LANE:(h + 1) * LANE] = (
                _rope(cq[:, h * LANE:(h + 1) * LANE], cmq, smq) * MLA_SCALE).astype(BF)
        ckv = _rms(p_ref[:, C_MKV:C_MKV + MLA_KV_RANK], kvn_ref[...])
        kr_sh = pltpu.roll(_rope(p_ref[:, C_MKR:C_MKR + LANE], ckr, skr), HALF, 1)
        sk = _rope(p_ref[:, C_SK:C_SK + LANE], c64, s64)
        write_kv(dk, p_ref[:, C_DAV:C_DAV + 512], ckv, kr_sh, sk, p_ref[:, C_SV:C_SV + LANE])


def _lat_prep_call(proj, tab, caches, qn, wqb, kvn, wkv):
    cdk, cdv, cckv, ckr, csk, csv = caches
    nj = 1 + DEC_SEQ // CHUNK

    def own(width):
        return pl.BlockSpec((None, CHUNK, width), lambda b, j: (b, jnp.maximum(j - 1, 0), 0))

    def cache(width):
        return pl.BlockSpec((None, PAST_LEN, width), lambda b, j: (b, 0, 0))

    def allk(width):
        return pl.BlockSpec((None, CHUNK, width), lambda b, j: (b, j, 0))

    def shp(rows, width):
        return jax.ShapeDtypeStruct((DEC_BATCH, rows, width), BF)

    return pl.pallas_call(
        _lat_prep_kernel,
        grid=(DEC_BATCH, nj),
        in_specs=[own(PROJ_COLS),
                  pl.BlockSpec((6, CHUNK, LANE), lambda b, j: (0, jnp.maximum(j - 1, 0), 0)),
                  cache(512), cache(512), cache(LANE), cache(LANE), cache(LANE), cache(LANE),
                  _const((1, MLA_Q_RANK)), _const((MLA_Q_RANK, MLA_HEADS * LANE)),
                  _const((1, MLA_KV_RANK)), _const((MLA_KV_RANK, MLA_HEADS * LANE + MLA_OUT))],
        out_specs=[own(512), own(MLA_HEADS * LANE), own(512),
                   allk(512), allk(512), allk(MLA_HEADS * LANE), allk(512), allk(2 * LANE), allk(2 * LANE)],
        out_shape=[shp(DEC_SEQ, 512), shp(DEC_SEQ, MLA_HEADS * LANE), shp(DEC_SEQ, 512),
                   shp(KEYS, 512), shp(KEYS, 512), shp(KEYS, MLA_HEADS * LANE), shp(KEYS, 512),
                   shp(KEYS, 2 * LANE), shp(KEYS, 2 * LANE)],
        compiler_params=_cp(("parallel", "arbitrary"), 48), name="lat_prep",
    )(proj.reshape(DEC_BATCH, DEC_SEQ, PROJ_COLS), tab, cdk, cdv, cckv, ckr, csk, csv, qn, wqb, kvn, wkv)


def _lat_da_kernel(lam_init, lam_ref, q_ref, k_ref, v_ref, subln_ref, o_ref):
    o_ref[...] = _da_head(q_ref[...].astype(F32), k_ref[...], v_ref[...], lam_ref[0], subln_ref[...],
                          lam_init).astype(BF)


def _lat_da_call(layer, lam, q, k, v, subln):
    return pl.pallas_call(
        functools.partial(_lat_da_kernel, _lam_init(layer)),
        grid=(DEC_BATCH, DA_HEADS, DEC_SEQ // TQ),
        in_specs=[_smem(),
                  pl.BlockSpec((None, TQ, LANE), lambda b, h, i: (b, i, h)),
                  pl.BlockSpec((None, KEYS, LANE), lambda b, h, i: (b, 0, h)),
                  pl.BlockSpec((None, KEYS, LANE), lambda b, h, i: (b, 0, h)),
                  _const((1, LANE))],
        out_specs=pl.BlockSpec((None, TQ, LANE), lambda b, h, i: (b, i, h)),
        out_shape=jax.ShapeDtypeStruct((DEC_BATCH, DEC_SEQ, 512), BF),
        compiler_params=_cp(("parallel", "parallel", "arbitrary"), 48), name="lat_da",
    )(lam, q, k, v, subln)


def _lat_mla_kernel(q_ref, k_ref, v_ref, o_ref):
    v = v_ref[...]
    oa = _softmax_pv([(_dot_nt(q_ref[:, 0:LANE], k_ref[:, 0:LANE]), v)])
    ob = _softmax_pv([(_dot_nt(q_ref[:, LANE:2 * LANE], k_ref[:, LANE:2 * LANE]), v)])
    o_ref[...] = _pair(oa, ob).astype(BF)


def _lat_mla_call(q, k, v):
    return pl.pallas_call(
        _lat_mla_kernel,
        grid=(DEC_BATCH, MLA_HEADS // 2, DEC_SEQ // TQ),
        in_specs=[pl.BlockSpec((None, TQ, 2 * LANE), lambda b, m, i: (b, i, m)),
                  pl.BlockSpec((None, KEYS, 2 * LANE), lambda b, m, i: (b, 0, m)),
                  pl.BlockSpec((None, KEYS, LANE), lambda b, m, i: (b, 0, m))],
        out_specs=pl.BlockSpec((None, TQ, LANE), lambda b, m, i: (b, i, m)),
        out_shape=jax.ShapeDtypeStruct((DEC_BATCH, DEC_SEQ, 512), BF),
        compiler_params=_cp(("parallel", "parallel", "arbitrary"), 48), name="lat_mla",
    )(q, k, v)


def _lat_swa_kernel(sink_ref, q_ref, kc_ref, kp_ref, k0_ref, kn_ref, vc_ref, vp_ref, v0_ref, vn_ref, o_ref):
    n = pl.program_id(1)
    nb = pl.num_programs(1)
    qi = lax.broadcasted_iota(I32, (W_BLOCK, W_BLOCK), 0)
    kj = lax.broadcasted_iota(I32, (W_BLOCK, W_BLOCK), 1)
    neg = -jnp.inf
    ok_prev = (kj >= qi) & (n > 0)
    ok_next = (kj <= qi) & (n < nb - 1)
    for m in range(SWA_HEADS // 2):
        g = (2 * m) // (SWA_HEADS // SWA_KV_HEADS)
        gs = slice(g * LANE, (g + 1) * LANE)
        outs = []
        for half, q in enumerate(_mask_halves(q_ref[:, m * LANE:(m + 1) * LANE].astype(F32))):
            segs = [(_dot_nt(q, kc_ref[:, gs]), vc_ref[:, gs]),
                    (jnp.where(ok_prev, _dot_nt(q, kp_ref[:, gs]), neg), vp_ref[:, gs]),
                    (_dot_nt(q, k0_ref[:, gs]), v0_ref[:, gs]),
                    (jnp.where(ok_next, _dot_nt(q, kn_ref[:, gs]), neg), vn_ref[:, gs])]
            outs.append(_softmax_pv(segs, sink_ref[2 * m + half]))
        o_ref[:, m * LANE:(m + 1) * LANE] = _pair(outs[0], outs[1]).astype(BF)


def _lat_swa_call(sink, q, k, v):
    nb = DEC_SEQ // W_BLOCK
    cb = PAST_LEN // W_BLOCK

    def ctx():
        return pl.BlockSpec((None, PAST_LEN, 2 * LANE), lambda b, n: (b, 0, 0))

    def loc(d):
        return pl.BlockSpec((None, W_BLOCK, 2 * LANE),
                            lambda b, n: (b, cb + jnp.clip(n + d, 0, nb - 1), 0))

    return pl.pallas_call(
        _lat_swa_kernel,
        grid=(DEC_BATCH, nb),
        in_specs=[_smem(), pl.BlockSpec((None, W_BLOCK, 512), lambda b, n: (b, n, 0)),
                  ctx(), loc(-1), loc(0), loc(1), ctx(), loc(-1), loc(0), loc(1)],
        out_specs=pl.BlockSpec((None, W_BLOCK, 512), lambda b, n: (b, n, 0)),
        out_shape=jax.ShapeDtypeStruct((DEC_BATCH, DEC_SEQ, 512), BF),
        compiler_params=_cp(("parallel", "arbitrary"), 40), name="lat_swa",
    )(sink, q, k, k, k, k, v, v, v, v)


def _rows_to_tiles(o_ref, val, tm):
    for k in range(D_CHUNKS):
        o_ref[pl.ds(k, tm, stride=D_CHUNKS), :] = val[:, k * LANE:(k + 1) * LANE]


def _post_kernel(x_ref, oda_ref, omla_ref, oswa_ref, gate_ref, mod_ref, n2_ref, wda_ref, wmla_ref, wswa_ref,
                 wo_ref, wrh_ref, wrl_ref, xo_ref, h2_ref, lg_ref):
    m = mod_ref[...]
    g = gate_ref[...].astype(F32)
    merged = (g[:, 0:D_MODEL] * _dot(oda_ref[...], wda_ref[...])
              + g[:, D_MODEL:2 * D_MODEL] * _dot(omla_ref[...], wmla_ref[...])
              + g[:, 2 * D_MODEL:] * _dot(oswa_ref[...], wswa_ref[...]))
    x = x_ref[...] + m[2:3] * _dot(merged.astype(BF), wo_ref[...])
    xo_ref[...] = x
    h2 = _rms(x, n2_ref[...]) * (1.0 + m[4:5]) + m[3:4]
    _rows_to_tiles(h2_ref, h2, TM_POST)
    hh, hl = _split(h2)
    lg_ref[...] = _dot_nt(wrh_ref[...], hh) + _dot_nt(wrl_ref[...], hh) + _dot_nt(wrh_ref[...], hl)


def _post_call(x, oda, omla, oswa, gate, mod, n2, wda, wmla, wswa, wo, wrh, wrl, latent):
    n = x.shape[0]
    tm = TM_POST
    row = pl.BlockSpec((tm, D_MODEL), lambda i: (i, 0))
    o512 = pl.BlockSpec((tm, 512), lambda i: (i, 0))
    return pl.pallas_call(
        _post_kernel, grid=(n // tm,),
        in_specs=[row, o512, o512, o512, pl.BlockSpec((tm, GATE_COLS), lambda i: (i, 0)),
                  _mod_spec(tm, latent), _const((1, D_MODEL)),
                  _const((512, D_MODEL)), _const((512, D_MODEL)), _const((512, D_MODEL)),
                  _const((D_MODEL, D_MODEL)), _const((N_EXPERTS, D_MODEL)), _const((N_EXPERTS, D_MODEL))],
        out_specs=[row, pl.BlockSpec((tm * D_CHUNKS, LANE), lambda i: (i, 0)),
                   pl.BlockSpec((N_EXPERTS, tm), lambda i: (0, i))],
        out_shape=[jax.ShapeDtypeStruct((n, D_MODEL), F32),
                   jax.ShapeDtypeStruct((n * D_CHUNKS, LANE), F32),
                   jax.ShapeDtypeStruct((N_EXPERTS, n), F32)],
        compiler_params=_cp(("parallel",), 48), name="post",
    )(x, oda, omla, oswa, gate, mod, n2, wda, wmla, wswa, wo, wrh, wrl)


def _router_kernel(cap, lg_ref, idx_ref, g_ref, cnt_ref, q_scr, a_scr):
    e_n, n = lg_ref.shape
    lg = lg_ref[...]
    ex = jnp.exp(lg - jnp.max(lg, axis=0, keepdims=True))
    aff = ex / jnp.sum(ex, axis=0, keepdims=True)
    a_scr[...] = aff
    bits = pltpu.bitcast(aff, I32)
    capf = float(cap)

    def count(mask):
        return jnp.sum(jnp.where(mask, 1.0, 0.0), axis=1, keepdims=True)

    def thr_bit(i, t):
        cand = t | jnp.left_shift(jnp.int32(1), 30 - i)
        return jnp.where(count(bits >= cand) >= capf, cand, t)

    thr = lax.fori_loop(0, 31, thr_bit, jnp.zeros((e_n, 1), I32))
    gt = bits > thr
    eq = bits == thr
    need = capf - count(gt)
    tok = lax.broadcasted_iota(I32, (e_n, n), 1)
    n_bits = n.bit_length()

    def tie_bit(i, bound):
        cand = bound | jnp.left_shift(jnp.int32(1), n_bits - 1 - i)
        ok = (cand <= n) & (count(eq & (tok < cand)) <= need)
        return jnp.where(ok, cand, bound)

    bound = lax.fori_loop(0, n_bits, tie_bit, jnp.zeros((e_n, 1), I32))
    sel = gt | (eq & (tok < bound))
    cnt_ref[...] = jnp.broadcast_to(count(sel & (tok < TOK_HALF)), cnt_ref.shape)
    blk = 2 * LANE
    upper = (lax.broadcasted_iota(I32, (blk, blk), 0) <= lax.broadcasted_iota(I32, (blk, blk), 1))
    upper = jnp.where(upper, 1.0, 0.0).astype(BF)
    carry = jnp.zeros((e_n, 1), F32)
    for c in range(n // blk):
        s_blk = sel[:, c * blk:(c + 1) * blk]
        rank = _dot(jnp.where(s_blk, 1.0, 0.0).astype(BF), upper) + carry
        q_scr[:, c * blk:(c + 1) * blk] = jnp.where(s_blk, rank, 0.0)
        carry = rank[:, blk - 1:blk]
    idx_ref[...] = jnp.zeros(idx_ref.shape, I32)
    g_ref[...] = jnp.zeros(g_ref.shape, F32)

    def per_expert(e, _):
        def per_tile(rt, _):
            off = pl.multiple_of(rt * ROUTE_TILE, ROUTE_TILE)
            slot = (off + 1 + lax.broadcasted_iota(I32, (ROUTE_TILE, 1), 0)).astype(F32)
            idx_v = jnp.zeros((ROUTE_TILE, 1), F32)
            g_v = jnp.zeros((ROUTE_TILE, 1), F32)
            for c in range(n // ROUTE_LANES):
                cs = slice(c * ROUTE_LANES, (c + 1) * ROUTE_LANES)
                hit = q_scr[pl.ds(e, 1), cs] == slot
                tok_f = (c * ROUTE_LANES + lax.broadcasted_iota(I32, (1, ROUTE_LANES), 1)).astype(F32)
                idx_v = idx_v + jnp.sum(jnp.where(hit, tok_f, 0.0), axis=1, keepdims=True)
                g_v = g_v + jnp.sum(jnp.where(hit, a_scr[pl.ds(e, 1), cs], 0.0), axis=1, keepdims=True)
            mine = _lane((ROUTE_TILE, LANE)) == e
            idx_ref[pl.ds(off, ROUTE_TILE), :] = jnp.where(mine, idx_v.astype(I32), idx_ref[pl.ds(off, ROUTE_TILE), :])
            g_ref[pl.ds(off, ROUTE_TILE), :] = jnp.where(mine, g_v, g_ref[pl.ds(off, ROUTE_TILE), :])
            return 0

        return lax.fori_loop(0, cap // ROUTE_TILE, per_tile, 0)

    lax.fori_loop(0, e_n, per_expert, 0)


def _router_call(logits_t, cap):
    n = logits_t.shape[1]
    return pl.pallas_call(
        functools.partial(_router_kernel, cap),
        out_shape=[jax.ShapeDtypeStruct((cap, LANE), I32), jax.ShapeDtypeStruct((cap, LANE), F32),
                   jax.ShapeDtypeStruct((N_EXPERTS, LANE), F32)],
        scratch_shapes=[pltpu.VMEM((N_EXPERTS, n), F32), pltpu.VMEM((N_EXPERTS, n), F32)],
        compiler_params=pltpu.CompilerParams(vmem_limit_bytes=40 * 1024 * 1024), name="router",
    )(logits_t)


def _row_copy(src_hbm, xe, sem, tok, slot):
    return pltpu.make_async_copy(src_hbm.at[tok], xe.at[pl.ds(pl.multiple_of(slot * SUB, SUB), SUB)], sem)


def _ffn_kernel(idx_ref, hc_hbm, hl_hbm, g_ref, w1_ref, w3_ref, w2_ref, y_ref, xe, xb, w1b, w3b, w2b, sem):
    f = pl.program_id(1)

    @pl.when(f == 0)
    def _():
        def start_ctx(r, _):
            _row_copy(hc_hbm, xe, sem, idx_ref[0, r], r).start()
            return 0

        def start_lat(r, _):
            _row_copy(hl_hbm, xe, sem, idx_ref[0, r], r).start()
            return 0

        def wait_ctx(r, _):
            _row_copy(hc_hbm, xe, sem, idx_ref[0, r], r).wait()
            return 0

        def wait_lat(r, _):
            _row_copy(hl_hbm, xe, sem, idx_ref[0, r], r).wait()
            return 0

        lax.fori_loop(0, CAP_CTX, start_ctx, 0)
        lax.fori_loop(CAP_CTX, SLOTS, start_lat, 0)
        lax.fori_loop(0, CAP_CTX, wait_ctx, 0)
        lax.fori_loop(CAP_CTX, SLOTS, wait_lat, 0)
        for k in range(D_CHUNKS):
            xb[:, k * LANE:(k + 1) * LANE] = xe[pl.ds(k, SLOTS, stride=D_CHUNKS), :].astype(BF)

    w1b[...] = w1_ref[...].astype(BF)
    w3b[...] = w3_ref[...].astype(BF)
    w2b[...] = w2_ref[...].astype(BF)
    for t in range(SLOTS // FFN_ROWS):
        rows = slice(t * FFN_ROWS, (t + 1) * FFN_ROWS)
        x = xb[rows, :]
        a = _dot(x, w1b[...])
        hid = (a * jax.nn.sigmoid(a) * _dot(x, w3b[...])).astype(BF)
        y = _dot(hid, w2b[...]) * g_ref[rows, :]

        def tile(k):
            return pl.ds(t * FFN_ROWS * D_CHUNKS + k, FFN_ROWS, stride=D_CHUNKS)

        @pl.when(f == 0)
        def _():
            for k in range(D_CHUNKS):
                y_ref[tile(k), :] = y[:, k * LANE:(k + 1) * LANE]

        @pl.when(f > 0)
        def _():
            for k in range(D_CHUNKS):
                y_ref[tile(k), :] = y_ref[tile(k), :] + y[:, k * LANE:(k + 1) * LANE]


def _ffn_call(idx, h_ctx, h_lat, g, w1, w3, w2):
    ff = EXPERT_FF // FF_SPLIT
    wspec = pl.BlockSpec((None, D_MODEL, ff), lambda e, f: (e, 0, f))
    return pl.pallas_call(
        _ffn_kernel,
        grid=(N_EXPERTS, FF_SPLIT),
        in_specs=[pl.BlockSpec((None, 1, SLOTS), lambda e, f: (e, 0, 0), memory_space=pltpu.SMEM),
                  pl.BlockSpec(memory_space=pl.ANY), pl.BlockSpec(memory_space=pl.ANY),
                  pl.BlockSpec((None, SLOTS, 1), lambda e, f: (e, 0, 0)),
                  wspec, wspec, pl.BlockSpec((None, ff, D_MODEL), lambda e, f: (e, f, 0))],
        out_specs=pl.BlockSpec((None, SLOTS * D_CHUNKS, LANE), lambda e, f: (e, 0, 0)),
        out_shape=jax.ShapeDtypeStruct((N_EXPERTS, SLOTS * D_CHUNKS, LANE), F32),
        scratch_shapes=[pltpu.VMEM((SLOTS * D_CHUNKS, LANE), F32), pltpu.VMEM((SLOTS, D_MODEL), BF),
                        pltpu.VMEM((D_MODEL, ff), BF), pltpu.VMEM((D_MODEL, ff), BF),
                        pltpu.VMEM((ff, D_MODEL), BF), pltpu.SemaphoreType.DMA(())],
        compiler_params=_cp(("arbitrary", "arbitrary"), 48), name="ffn",
    )(idx, h_ctx, h_lat, g, w1, w3, w2)


def _combine_kernel(idx_ref, bnd_ref, y_ref, acc_ref):
    h = pl.program_id(0)
    e = pl.program_id(1)

    @pl.when(e == 0)
    def _():
        acc_ref[...] = jnp.zeros(acc_ref.shape, F32)

    base = jnp.where(h == 0, 0, (h - 1) * TOK_HALF)

    def body(r, _):
        t = idx_ref[0, r] - base
        acc_ref[t] = acc_ref[t] + y_ref[r]
        return 0

    lax.fori_loop(bnd_ref[0, h], bnd_ref[0, h + 1], body, 0)


def _combine_call(idx, bnd, y):
    return pl.pallas_call(
        _combine_kernel,
        grid=(N_HALVES, N_EXPERTS),
        in_specs=[pl.BlockSpec((None, 1, SLOTS), lambda h, e: (e, 0, 0), memory_space=pltpu.SMEM),
                  pl.BlockSpec((None, 1, N_HALVES + 1), lambda h, e: (e, 0, 0), memory_space=pltpu.SMEM),
                  pl.BlockSpec((None, SLOTS, SUB, LANE), lambda h, e: (e, 0, 0, 0))],
        out_specs=pl.BlockSpec((None, TOK_HALF, SUB, LANE), lambda h, e: (h, 0, 0, 0)),
        out_shape=jax.ShapeDtypeStruct((N_HALVES, TOK_HALF, SUB, LANE), F32),
        compiler_params=_cp(("parallel", "arbitrary"), 52), name="combine",
    )(idx, bnd, y.reshape(N_EXPERTS, SLOTS, SUB, LANE))


def _final_kernel(x_ref, moe_ref, mod_ref, g_ref, o_ref):
    x = x_ref[...] + mod_ref[...][5:6] * _moe_rows(moe_ref, TM_PRE)
    o_ref[...] = _rms(x, g_ref[...])


def _final_call(x, moe, mod, g, latent):
    n = x.shape[0]
    tm = TM_PRE
    moe_off = (N_CTX // tm) if latent else 0
    row = pl.BlockSpec((tm, D_MODEL), lambda i: (i, 0))
    return pl.pallas_call(
        _final_kernel, grid=(n // tm,),
        in_specs=[row, pl.BlockSpec((tm * D_CHUNKS, LANE), lambda i: (i + moe_off, 0)),
                  _mod_spec(tm, latent), _const((1, D_MODEL))],
        out_specs=row, out_shape=jax.ShapeDtypeStruct((n, D_MODEL), F32),
        compiler_params=_cp(("parallel",), 32), name="final",
    )(x, moe, mod, g)


def _axial_tables(rot_dim):
    rows = DEC_SEQ // GRID_W
    row = jnp.repeat(jnp.arange(rows, dtype=F32), GRID_W)
    col = jnp.tile(jnp.arange(GRID_W, dtype=F32), rows)
    n_freq = rot_dim // 4
    inv = ROPE_BASE ** (-jnp.arange(n_freq, dtype=F32) / n_freq)
    ang = jnp.concatenate([row[:, None] * inv, col[:, None] * inv], axis=-1)
    sign = jnp.tile(jnp.array([-1.0, 1.0], F32), rot_dim // 2)
    return jnp.repeat(jnp.cos(ang), 2, axis=1), jnp.repeat(jnp.sin(ang), 2, axis=1) * sign


def _rope_tables():
    c64, s64 = _axial_tables(DA_DH)
    c32, s32 = _axial_tables(MLA_ROPE)
    one = lambda w: jnp.ones((DEC_SEQ, w), F32)
    zero = lambda w: jnp.zeros((DEC_SEQ, w), F32)
    pad = LANE - MLA_NOPE - MLA_ROPE
    return jnp.stack([
        jnp.tile(c64, (1, 2)), jnp.tile(s64, (1, 2)),
        jnp.concatenate([one(MLA_NOPE), c32, one(pad)], axis=1),
        jnp.concatenate([zero(MLA_NOPE), s32, zero(pad)], axis=1),
        jnp.concatenate([c32, one(LANE - MLA_ROPE)], axis=1),
        jnp.concatenate([s32, zero(LANE - MLA_ROPE)], axis=1)])


def _layer_weights(l, w_in, w_gate, mla_w_qb, mla_w_kvb, w_br_da, w_br_mla, w_br_swa, w_o, w_router):
    kr_end = C_MKR + MLA_ROPE
    win = jnp.concatenate([w_in[l][:, :kr_end], jnp.zeros((D_MODEL, LANE - MLA_ROPE), F32),
                           w_in[l][:, kr_end:]], axis=1).astype(BF)
    dk = MLA_NOPE + MLA_ROPE
    wqb = jnp.pad(mla_w_qb[l].reshape(MLA_Q_RANK, MLA_HEADS, dk), ((0, 0), (0, 0), (0, LANE - dk)))
    wqb = wqb.reshape(MLA_Q_RANK, MLA_HEADS * LANE).astype(BF)
    kvb = mla_w_kvb[l].reshape(MLA_KV_RANK, MLA_HEADS, MLA_NOPE + MLA_V)
    wk = jnp.pad(kvb[:, :, :MLA_NOPE], ((0, 0), (0, 0), (0, LANE - MLA_NOPE))).reshape(MLA_KV_RANK, MLA_HEADS * LANE)
    wv = kvb[:, :, MLA_NOPE:].reshape(MLA_KV_RANK, MLA_OUT)
    wkv = jnp.concatenate([wk, wv], axis=1).astype(BF)
    wrh, wrl = _split(w_router[l].T)
    return dict(win=win, wg=w_gate[l].astype(BF), wqb=wqb, wkv=wkv, wda=w_br_da[l].astype(BF),
                wmla=w_br_mla[l].astype(BF), wswa=w_br_swa[l].astype(BF), wo=w_o[l].astype(BF), wrh=wrh, wrl=wrl)


def kernel(x_prompt, x_sample, cache_da_k, cache_da_v, cache_mla_ckv, cache_mla_krope, cache_swa_k, cache_swa_v, c, c_ctx, w_ada, b_ada, norm1, norm2, w_in, da_lq1, da_lk1, da_lq2, da_lk2, da_subln, mla_q_norm, mla_w_qb, mla_kv_norm, mla_w_kvb, swa_sink, w_gate, w_br_da, w_br_mla, w_br_swa, w_o, w_router, w_ff1, w_ff3, w_ff2, final_norm):
    c_all = jnp.concatenate([c_ctx[None], c, jnp.zeros((SUB - 1 - DEC_BATCH, D_MODEL), F32)], axis=0)
    mod_all = _ada_call(c_all, w_ada, b_ada)[:, :1 + DEC_BATCH].reshape(DEPTH, 1 + DEC_BATCH, 6, D_MODEL)
    lam_all = _lam_call(da_lq1, da_lk1, da_lq2, da_lk2)
    tab = _rope_tables()

    xp = x_prompt.reshape(N_CTX, D_MODEL)
    xs = x_sample.reshape(N_LAT, D_MODEL)
    moe = None
    new = [[] for _ in range(6)]
    for l in range(DEPTH):
        w = _layer_weights(l, w_in, w_gate, mla_w_qb, mla_w_kvb, w_br_da, w_br_mla, w_br_swa, w_o, w_router)
        mod = mod_all[l]
        mod_prev = mod_all[l - 1] if l else None
        n1, n2 = norm1[l][None], norm2[l][None]
        subln, qn, kvn = da_subln[l][None], mla_q_norm[l][None], mla_kv_norm[l][None]
        lam = lam_all[l, :1]
        sink = swa_sink[l]

        xp, proj_c, gate_c = _pre_call(xp, moe, mod_prev, mod, n1, w["win"], w["wg"], latent=False)
        xs, proj_l, gate_l = _pre_call(xs, moe, mod_prev, mod, n1, w["win"], w["wg"], latent=True)

        oda_c, omla_c, oswa_c, ckv_c = _ctx_attn_call(l, lam, sink, proj_c, subln, qn, w["wqb"], kvn, w["wkv"])
        pc = proj_c.reshape(BATCH, SEQ, PROJ_COLS)
        new[0].append(pc[..., C_DAK:C_DAK + 512].reshape(BATCH, SEQ, DA_HEADS, 2 * DA_DH))
        new[1].append(pc[..., C_DAV:C_DAV + 512].reshape(BATCH, SEQ, DA_HEADS, 2 * DA_DH))
        new[2].append(ckv_c.reshape(BATCH, SEQ, MLA_KV_RANK))
        new[3].append(pc[..., C_MKR:C_MKR + MLA_ROPE])
        new[4].append(pc[..., C_SK:C_SK + LANE].reshape(BATCH, SEQ, SWA_KV_HEADS, SWA_DH))
        new[5].append(pc[..., C_SV:C_SV + LANE].reshape(BATCH, SEQ, SWA_KV_HEADS, SWA_DH))

        caches = (cache_da_k[:, l].reshape(DEC_BATCH, PAST_LEN, 512),
                  cache_da_v[:, l].reshape(DEC_BATCH, PAST_LEN, 512),
                  cache_mla_ckv[:, l],
                  jnp.pad(cache_mla_krope[:, l], ((0, 0), (0, 0), (0, LANE - MLA_ROPE))),
                  cache_swa_k[:, l].reshape(DEC_BATCH, PAST_LEN, LANE),
                  cache_swa_v[:, l].reshape(DEC_BATCH, PAST_LEN, LANE))
        daq, mlaq, swaq, dak, dav, mlak, mlav, swak, swav = _lat_prep_call(
            proj_l, tab, caches, qn, w["wqb"], kvn, w["wkv"])
        oda_l = _lat_da_call(l, lam, daq, dak, dav, subln).reshape(N_LAT, 512)
        omla_l = _lat_mla_call(mlaq, mlak, mlav).reshape(N_LAT, 512)
        oswa_l = _lat_swa_call(sink, swaq, swak, swav).reshape(N_LAT, 512)

        post_w = (w["wda"], w["wmla"], w["wswa"], w["wo"], w["wrh"], w["wrl"])
        xp, h2_c, lg_c = _post_call(xp, oda_c, omla_c, oswa_c, gate_c, mod, n2, *post_w, latent=False)
        xs, h2_l, lg_l = _post_call(xs, oda_l, omla_l, oswa_l, gate_l, mod, n2, *post_w, latent=True)

        idx_c, g_c, _ = _router_call(lg_c, CAP_CTX)
        idx_l, g_l, cnt_l = _router_call(lg_l, CAP_LAT)
        idx = jnp.concatenate([idx_c[:, :N_EXPERTS].T, idx_l[:, :N_EXPERTS].T], axis=1)
        gsel = jnp.concatenate([g_c[:, :N_EXPERTS].T, g_l[:, :N_EXPERTS].T], axis=1)
        n0 = cnt_l[:, 0].astype(I32)
        bnd = jnp.stack([jnp.zeros_like(n0), jnp.full_like(n0, CAP_CTX), CAP_CTX + n0,
                         jnp.full_like(n0, SLOTS)], axis=1)
        idx3 = idx.reshape(N_EXPERTS, 1, SLOTS)
        y = _ffn_call(idx3, h2_c.reshape(N_CTX, SUB, LANE), h2_l.reshape(N_LAT, SUB, LANE),
                      gsel.reshape(N_EXPERTS, SLOTS, 1), w_ff1[l], w_ff3[l], w_ff2[l])
        acc = _combine_call(idx3, bnd.reshape(N_EXPERTS, 1, N_HALVES + 1), y)
        moe = acc.reshape(N_HALVES * TOK_HALF * D_CHUNKS, LANE)

    mod_last = mod_all[DEPTH - 1]
    fn = final_norm[None]
    y_prompt = _final_call(xp, moe, mod_last, fn, latent=False).reshape(BATCH, SEQ, D_MODEL)
    y_sample = _final_call(xs, moe, mod_last, fn, latent=True).reshape(DEC_BATCH, DEC_SEQ, D_MODEL)
    return (y_prompt, y_sample) + tuple(jnp.stack(n, axis=1) for n in new)
```

```python
import functools
import math

import jax
import jax.numpy as jnp
from jax import lax
from jax.experimental import pallas as pl
from jax.experimental.pallas import tpu as pltpu

F32 = jnp.float32
BF = jnp.bfloat16
I32 = jnp.int32

D_MODEL = 1024
BATCH = 16
SEQ = 256
DEPTH = 2
DEC_BATCH = 2
DEC_SEQ = 4096
PAST_LEN = 512
GRID_W = 64
ROPE_BASE = 10000.0
EPS = 1e-6
DA_HEADS = 4
DA_DH = 64
DA_OUT = DA_HEADS * 2 * DA_DH
MLA_HEADS = 8
MLA_Q_RANK = 256
MLA_KV_RANK = 128
MLA_NOPE = 64
MLA_ROPE = 32
MLA_V = 64
MLA_OUT = MLA_HEADS * MLA_V
SWA_HEADS = 8
SWA_KV_HEADS = 2
SWA_DH = 64
W_BLOCK = 128
SWA_OUT = SWA_HEADS * SWA_DH
N_EXPERTS = 16
EXPERT_FF = 1024
CAPACITY_FACTOR = 2

LANE = 128
SUB = 8
HALF = 64
N_CTX = BATCH * SEQ
N_LAT = DEC_BATCH * DEC_SEQ
KEYS = PAST_LEN + DEC_SEQ
CAP_CTX = CAPACITY_FACTOR * N_CTX // N_EXPERTS
CAP_LAT = CAPACITY_FACTOR * N_LAT // N_EXPERTS
SLOTS = CAP_CTX + CAP_LAT
TOK_HALF = 4096
N_HALVES = (N_CTX + N_LAT) // TOK_HALF
D_CHUNKS = D_MODEL // LANE

C_DAQ, C_DAK, C_DAV = 0, 512, 1024
C_MQ, C_MKV, C_MKR = 1536, 1792, 1920
C_SQ, C_SK, C_SV = 2048, 2560, 2688
PROJ_COLS = 2816
GATE_COLS = 3 * D_MODEL
MLA_SCALE = (MLA_NOPE + MLA_ROPE) ** -0.5
HEAD_SCALE = DA_DH ** -0.5

TM_PRE = 256
TM_POST = 512
CHUNK = 512
TQ = 256
ROUTE_TILE = 128
ROUTE_LANES = 2048
FFN_ROWS = 512
FF_SPLIT = 2
COMBINE_GROUP = 8


def _cp(sem, vmem_mb):
    return pltpu.CompilerParams(dimension_semantics=sem, vmem_limit_bytes=vmem_mb * 1024 * 1024)


def _dot(a, b):
    return jnp.dot(a, b, preferred_element_type=F32)


def _dot_nt(a, b):
    return lax.dot_general(a, b, (((1,), (1,)), ((), ())), preferred_element_type=F32)


def _split(a):
    hi = a.astype(BF)
    lo = (a - hi.astype(F32)).astype(BF)
    return hi, lo


def _dot3(a, w):
    ah, al = _split(a)
    wh, wl = _split(w)
    return _dot(ah, wh) + _dot(ah, wl) + _dot(al, wh)


def _rms(x, g):
    return x * lax.rsqrt(jnp.mean(x * x, axis=-1, keepdims=True) + EPS) * g


def _lane(shape):
    return lax.broadcasted_iota(I32, shape, len(shape) - 1)


def _softmax_pv(segs, sink=None):
    m = None
    for s, _ in segs:
        ms = jnp.max(s, axis=-1, keepdims=True)
        m = ms if m is None else jnp.maximum(m, ms)
    if sink is not None:
        m = jnp.maximum(m, sink)
    l = None
    o = None
    for s, v in segs:
        e = jnp.exp(s - m)
        ls = jnp.sum(e, axis=-1, keepdims=True)
        os_ = _dot(e.astype(BF), v)
        l = ls if l is None else l + ls
        o = os_ if o is None else o + os_
    if sink is not None:
        l = l + jnp.exp(sink - m)
    return o / l


def _mask_halves(q):
    lo = _lane(q.shape) < HALF
    return jnp.where(lo, q, 0.0).astype(BF), jnp.where(lo, 0.0, q).astype(BF)


def _pair(o_a, o_b):
    return jnp.where(_lane(o_a.shape) < HALF, o_a, o_b)


def _dup(x):
    r = pltpu.roll(x, HALF, 1)
    lo = _lane(x.shape) < HALF
    return jnp.where(lo, x, r), jnp.where(lo, r, x)


def _rope(x, c, s):
    n = x.shape[-1]
    even = (_lane(x.shape) % 2) == 0
    sw = jnp.where(even, pltpu.roll(x, n - 1, 1), pltpu.roll(x, 1, 1))
    return x * c + sw * s


def _da_head(q, k, v, lam, subln, lam_init):
    q1, q2 = _mask_halves(q)
    o1 = _softmax_pv([(_dot_nt(q1, k), v)])
    o2 = _softmax_pv([(_dot_nt(q2, k), v)])
    return _rms(o1 - lam * o2, subln) * (1.0 - lam_init)


def _ada_kernel(c_ref, w_ref, b_ref, o_ref):
    c = c_ref[...]
    a = c * jax.nn.sigmoid(c)
    o_ref[...] = _dot3(a, w_ref[...]) + b_ref[...]


def _ada_call(c_all, w_ada, b_ada):
    tn = 1536
    return pl.pallas_call(
        _ada_kernel,
        grid=(DEPTH, 6 * D_MODEL // tn),
        in_specs=[pl.BlockSpec((SUB, D_MODEL), lambda l, j: (0, 0)),
                  pl.BlockSpec((None, D_MODEL, tn), lambda l, j: (l, 0, j)),
                  pl.BlockSpec((None, 1, tn), lambda l, j: (l, 0, j))],
        out_specs=pl.BlockSpec((None, SUB, tn), lambda l, j: (l, 0, j)),
        out_shape=jax.ShapeDtypeStruct((DEPTH, SUB, 6 * D_MODEL), F32),
        compiler_params=_cp(("parallel", "parallel"), 40),
        name="ada",
    )(c_all, w_ada, b_ada.reshape(DEPTH, 1, 6 * D_MODEL))


def _lam_kernel(q1, k1, q2, k2, o_ref):
    s1 = jnp.sum(q1[...] * k1[...], axis=-1, keepdims=True)
    s2 = jnp.sum(q2[...] * k2[...], axis=-1, keepdims=True)
    row = lax.broadcasted_iota(I32, (DEPTH, 1), 0)
    init = jnp.zeros((DEPTH, 1), F32)
    for l in range(DEPTH):
        init = jnp.where(row == l, _lam_init(l), init)
    o_ref[...] = jnp.broadcast_to(jnp.exp(s1) - jnp.exp(s2) + init, o_ref.shape)


def _lam_init(layer):
    return 0.8 - 0.6 * math.exp(-0.3 * layer)


def _lam_call(q1, k1, q2, k2):
    return pl.pallas_call(
        _lam_kernel,
        out_shape=jax.ShapeDtypeStruct((DEPTH, LANE), F32),
        name="lam",
    )(q1, k1, q2, k2)


def _moe_rows(moe_ref, tm):
    return jnp.concatenate([moe_ref[pl.ds(k, tm, stride=D_CHUNKS), :] for k in range(D_CHUNKS)], axis=1)


def _pre_body(x, mod_ref, n1_ref, win_ref, wg_ref, proj_ref, gate_ref):
    m = mod_ref[...]
    h = _rms(x, n1_ref[...]) * (1.0 + m[1:2]) + m[0:1]
    hb = h.astype(BF)
    proj_ref[...] = _dot(hb, win_ref[...])
    gate_ref[...] = jax.nn.sigmoid(_dot(hb, wg_ref[...])).astype(BF)


def _pre_first_kernel(x_ref, mod_ref, n1_ref, win_ref, wg_ref, proj_ref, gate_ref):
    _pre_body(x_ref[...], mod_ref, n1_ref, win_ref, wg_ref, proj_ref, gate_ref)


def _pre_next_kernel(x_ref, moe_ref, modp_ref, mod_ref, n1_ref, win_ref, wg_ref, xo_ref, proj_ref, gate_ref):
    x = x_ref[...] + modp_ref[...][5:6] * _moe_rows(moe_ref, TM_PRE)
    xo_ref[...] = x
    _pre_body(x, mod_ref, n1_ref, win_ref, wg_ref, proj_ref, gate_ref)


def _mod_spec(tm, latent):
    per = DEC_SEQ // tm
    if latent:
        return pl.BlockSpec((None, 6, D_MODEL), lambda i: (1 + i // per, 0, 0))
    return pl.BlockSpec((None, 6, D_MODEL), lambda i: (0, 0, 0))


def _const(shape):
    nd = len(shape)
    return pl.BlockSpec(shape, lambda *_: (0,) * nd)


def _pre_call(x, moe, mod_prev, mod, n1, win, wg, latent):
    n = x.shape[0]
    tm = TM_PRE
    row = pl.BlockSpec((tm, D_MODEL), lambda i: (i, 0))
    w_specs = [_mod_spec(tm, latent), _const((1, D_MODEL)), _const((D_MODEL, PROJ_COLS)),
               _const((D_MODEL, GATE_COLS))]
    outs = [jax.ShapeDtypeStruct((n, PROJ_COLS), F32), jax.ShapeDtypeStruct((n, GATE_COLS), BF)]
    out_specs = [pl.BlockSpec((tm, PROJ_COLS), lambda i: (i, 0)), pl.BlockSpec((tm, GATE_COLS), lambda i: (i, 0))]
    if moe is None:
        proj, gate = pl.pallas_call(
            _pre_first_kernel, grid=(n // tm,), in_specs=[row] + w_specs, out_specs=out_specs, out_shape=outs,
            compiler_params=_cp(("parallel",), 52), name="pre_first",
        )(x, mod, n1, win, wg)
        return x, proj, gate
    moe_off = (N_CTX // tm) if latent else 0
    moe_spec = pl.BlockSpec((tm * D_CHUNKS, LANE), lambda i: (i + moe_off, 0))
    xo, proj, gate = pl.pallas_call(
        _pre_next_kernel, grid=(n // tm,),
        in_specs=[row, moe_spec, _mod_spec(tm, latent)] + w_specs,
        out_specs=[row] + out_specs,
        out_shape=[jax.ShapeDtypeStruct((n, D_MODEL), F32)] + outs,
        compiler_params=_cp(("parallel",), 52), name="pre_next",
    )(x, moe, mod_prev, mod, n1, win, wg)
    return xo, proj, gate


def _ctx_attn_kernel(lam_init, lam_ref, sink_ref, p_ref, subln_ref, qn_ref, wqb_ref, kvn_ref, wkv_ref,
                     oda_ref, omla_ref, oswa_ref, ckv_ref):
    p = p_ref[...]
    lam = lam_ref[0]
    for h in range(DA_HEADS):
        blk = slice(h * LANE, (h + 1) * LANE)
        q = p[:, C_DAQ + h * LANE:C_DAQ + (h + 1) * LANE] * HEAD_SCALE
        k = p[:, C_DAK + h * LANE:C_DAK + (h + 1) * LANE].astype(BF)
        v = p[:, C_DAV + h * LANE:C_DAV + (h + 1) * LANE].astype(BF)
        oda_ref[:, blk] = _da_head(q, k, v, lam, subln_ref[...], lam_init).astype(BF)
    qn = _rms(p[:, C_MQ:C_MQ + MLA_Q_RANK], qn_ref[...]).astype(BF)
    cq = _dot(qn, wqb_ref[...])
    ckv = _rms(p[:, C_MKV:C_MKV + MLA_KV_RANK], kvn_ref[...])
    ckv_ref[...] = ckv
    kv = _dot(ckv.astype(BF), wkv_ref[...])
    kr_sh = pltpu.roll(p[:, C_MKR:C_MKR + LANE], HALF, 1)
    for m in range(MLA_HEADS // 2):
        vpair = kv[:, MLA_HEADS * LANE + m * LANE:MLA_HEADS * LANE + (m + 1) * LANE].astype(BF)
        outs = []
        for h in (2 * m, 2 * m + 1):
            q = (cq[:, h * LANE:(h + 1) * LANE] * MLA_SCALE).astype(BF)
            k = (kv[:, h * LANE:(h + 1) * LANE] + kr_sh).astype(BF)
            outs.append(_softmax_pv([(_dot_nt(q, k), vpair)]))
        omla_ref[:, m * LANE:(m + 1) * LANE] = _pair(outs[0], outs[1]).astype(BF)
    kd = _dup(p[:, C_SK:C_SK + LANE])
    vd = _dup(p[:, C_SV:C_SV + LANE])
    for m in range(SWA_HEADS // 2):
        g = (2 * m) // (SWA_HEADS // SWA_KV_HEADS)
        k = kd[g].astype(BF)
        v = vd[g].astype(BF)
        qa, qb = _mask_halves(p[:, C_SQ + m * LANE:C_SQ + (m + 1) * LANE] * HEAD_SCALE)
        oa = _softmax_pv([(_dot_nt(qa, k), v)], sink_ref[2 * m])
        ob = _softmax_pv([(_dot_nt(qb, k), v)], sink_ref[2 * m + 1])
        oswa_ref[:, m * LANE:(m + 1) * LANE] = _pair(oa, ob).astype(BF)


def _smem():
    return pl.BlockSpec(memory_space=pltpu.SMEM)


def _ctx_attn_call(layer, lam, sink, proj, subln, qn, wqb, kvn, wkv):
    row512 = pl.BlockSpec((SEQ, 512), lambda b: (b, 0))
    return pl.pallas_call(
        functools.partial(_ctx_attn_kernel, _lam_init(layer)),
        grid=(BATCH,),
        in_specs=[_smem(), _smem(), pl.BlockSpec((SEQ, PROJ_COLS), lambda b: (b, 0)),
                  _const((1, LANE)), _const((1, MLA_Q_RANK)), _const((MLA_Q_RANK, MLA_HEADS * LANE)),
                  _const((1, MLA_KV_RANK)), _const((MLA_KV_RANK, MLA_HEADS * LANE + MLA_OUT))],
        out_specs=[row512, row512, row512, pl.BlockSpec((SEQ, MLA_KV_RANK), lambda b: (b, 0))],
        out_shape=[jax.ShapeDtypeStruct((N_CTX, 512), BF)] * 3 + [jax.ShapeDtypeStruct((N_CTX, MLA_KV_RANK), F32)],
        compiler_params=_cp(("parallel",), 40), name="ctx_attn",
    )(lam, sink, proj, subln, qn, wqb, kvn, wkv)


def _lat_prep_kernel(p_ref, tab_ref, cdk_ref, cdv_ref, cckv_ref, ckr_ref, csk_ref, csv_ref,
                     qn_ref, wqb_ref, kvn_ref, wkv_ref,
                     daq_ref, mlaq_ref, swaq_ref, dak_ref, dav_ref, mlak_ref, mlav_ref, swak_ref, swav_ref):
    j = pl.program_id(1)

    def write_kv(dk, dv, ckv, kr_sh, sk, sv):
        dak_ref[...] = dk.astype(BF)
        dav_ref[...] = dv.astype(BF)
        kv = _dot(ckv.astype(BF), wkv_ref[...])
        for h in range(MLA_HEADS):
            mlak_ref[:, h * LANE:(h + 1) * LANE] = (kv[:, h * LANE:(h + 1) * LANE] + kr_sh).astype(BF)
        mlav_ref[...] = kv[:, MLA_HEADS * LANE:].astype(BF)
        k0, k1 = _dup(sk)
        v0, v1 = _dup(sv)
        swak_ref[:, 0:LANE] = k0.astype(BF)
        swak_ref[:, LANE:2 * LANE] = k1.astype(BF)
        swav_ref[:, 0:LANE] = v0.astype(BF)
        swav_ref[:, LANE:2 * LANE] = v1.astype(BF)

    @pl.when(j == 0)
    def _():
        write_kv(cdk_ref[...], cdv_ref[...], cckv_ref[...], pltpu.roll(ckr_ref[...], HALF, 1),
                 csk_ref[...], csv_ref[...])

    @pl.when(j > 0)
    def _():
        c64, s64 = tab_ref[0], tab_ref[1]
        cmq, smq = tab_ref[2], tab_ref[3]
        ckr, skr = tab_ref[4], tab_ref[5]

        def blk(c0, h):
            return p_ref[:, c0 + h * LANE:c0 + (h + 1) * LANE]

        for h in range(DA_HEADS):
            daq_ref[:, h * LANE:(h + 1) * LANE] = (_rope(blk(C_DAQ, h), c64, s64) * HEAD_SCALE).astype(BF)
            swaq_ref[:, h * LANE:(h + 1) * LANE] = (_rope(blk(C_SQ, h), c64, s64) * HEAD_SCALE).astype(BF)
        dk = jnp.concatenate([_rope(blk(C_DAK, h), c64, s64) for h in range(DA_HEADS)], axis=1)
        qn = _rms(p_ref[:, C_MQ:C_MQ + MLA_Q_RANK], qn_ref[...]).astype(BF)
        cq = _dot(qn, wqb_ref[...])
        for h in range(MLA_HEADS):
            mlaq_ref[:, h * LANE:(h + 1) * LANE] = (
                _rope(cq[:, h * LANE:(h + 1) * LANE], cmq, smq) * MLA_SCALE).astype(BF)
        ckv = _rms(p_ref[:, C_MKV:C_MKV + MLA_KV_RANK], kvn_ref[...])
        kr_sh = pltpu.roll(_rope(p_ref[:, C_MKR:C_MKR + LANE], ckr, skr), HALF, 1)
        sk = _rope(p_ref[:, C_SK:C_SK + LANE], c64, s64)
        write_kv(dk, p_ref[:, C_DAV:C_DAV + 512], ckv, kr_sh, sk, p_ref[:, C_SV:C_SV + LANE])


def _lat_prep_call(proj, tab, caches, qn, wqb, kvn, wkv):
    cdk, cdv, cckv, ckr, csk, csv = caches
    nj = 1 + DEC_SEQ // CHUNK

    def own(width):
        return pl.BlockSpec((None, CHUNK, width), lambda b, j: (b, jnp.maximum(j - 1, 0), 0))

    def cache(width):
        return pl.BlockSpec((None, PAST_LEN, width), lambda b, j: (b, 0, 0))

    def allk(width):
        return pl.BlockSpec((None, CHUNK, width), lambda b, j: (b, j, 0))

    def shp(rows, width):
        return jax.ShapeDtypeStruct((DEC_BATCH, rows, width), BF)

    return pl.pallas_call(
        _lat_prep_kernel,
        grid=(DEC_BATCH, nj),
        in_specs=[own(PROJ_COLS),
                  pl.BlockSpec((6, CHUNK, LANE), lambda b, j: (0, jnp.maximum(j - 1, 0), 0)),
                  cache(512), cache(512), cache(LANE), cache(LANE), cache(LANE), cache(LANE),
                  _const((1, MLA_Q_RANK)), _const((MLA_Q_RANK, MLA_HEADS * LANE)),
                  _const((1, MLA_KV_RANK)), _const((MLA_KV_RANK, MLA_HEADS * LANE + MLA_OUT))],
        out_specs=[own(512), own(MLA_HEADS * LANE), own(512),
                   allk(512), allk(512), allk(MLA_HEADS * LANE), allk(512), allk(2 * LANE), allk(2 * LANE)],
        out_shape=[shp(DEC_SEQ, 512), shp(DEC_SEQ, MLA_HEADS * LANE), shp(DEC_SEQ, 512),
                   shp(KEYS, 512), shp(KEYS, 512), shp(KEYS, MLA_HEADS * LANE), shp(KEYS, 512),
                   shp(KEYS, 2 * LANE), shp(KEYS, 2 * LANE)],
        compiler_params=_cp(("parallel", "arbitrary"), 48), name="lat_prep",
    )(proj.reshape(DEC_BATCH, DEC_SEQ, PROJ_COLS), tab, cdk, cdv, cckv, ckr, csk, csv, qn, wqb, kvn, wkv)


def _lat_da_kernel(lam_init, lam_ref, q_ref, k_ref, v_ref, subln_ref, o_ref):
    o_ref[...] = _da_head(q_ref[...].astype(F32), k_ref[...], v_ref[...], lam_ref[0], subln_ref[...],
                          lam_init).astype(BF)


def _lat_da_call(layer, lam, q, k, v, subln):
    return pl.pallas_call(
        functools.partial(_lat_da_kernel, _lam_init(layer)),
        grid=(DEC_BATCH, DA_HEADS, DEC_SEQ // TQ),
        in_specs=[_smem(),
                  pl.BlockSpec((None, TQ, LANE), lambda b, h, i: (b, i, h)),
                  pl.BlockSpec((None, KEYS, LANE), lambda b, h, i: (b, 0, h)),
                  pl.BlockSpec((None, KEYS, LANE), lambda b, h, i: (b, 0, h)),
                  _const((1, LANE))],
        out_specs=pl.BlockSpec((None, TQ, LANE), lambda b, h, i: (b, i, h)),
        out_shape=jax.ShapeDtypeStruct((DEC_BATCH, DEC_SEQ, 512), BF),
        compiler_params=_cp(("parallel", "parallel", "arbitrary"), 48), name="lat_da",
    )(lam, q, k, v, subln)


def _lat_mla_kernel(q_ref, k_ref, v_ref, o_ref):
    v = v_ref[...]
    oa = _softmax_pv([(_dot_nt(q_ref[:, 0:LANE], k_ref[:, 0:LANE]), v)])
    ob = _softmax_pv([(_dot_nt(q_ref[:, LANE:2 * LANE], k_ref[:, LANE:2 * LANE]), v)])
    o_ref[...] = _pair(oa, ob).astype(BF)


def _lat_mla_call(q, k, v):
    return pl.pallas_call(
        _lat_mla_kernel,
        grid=(DEC_BATCH, MLA_HEADS // 2, DEC_SEQ // TQ),
        in_specs=[pl.BlockSpec((None, TQ, 2 * LANE), lambda b, m, i: (b, i, m)),
                  pl.BlockSpec((None, KEYS, 2 * LANE), lambda b, m, i: (b, 0, m)),
                  pl.BlockSpec((None, KEYS, LANE), lambda b, m, i: (b, 0, m))],
        out_specs=pl.BlockSpec((None, TQ, LANE), lambda b, m, i: (b, i, m)),
        out_shape=jax.ShapeDtypeStruct((DEC_BATCH, DEC_SEQ, 512), BF),
        compiler_params=_cp(("parallel", "parallel", "arbitrary"), 48), name="lat_mla",
    )(q, k, v)


def _lat_swa_kernel(sink_ref, q_ref, kc_ref, kp_ref, k0_ref, kn_ref, vc_ref, vp_ref, v0_ref, vn_ref, o_ref):
    n = pl.program_id(1)
    nb = pl.num_programs(1)
    qi = lax.broadcasted_iota(I32, (W_BLOCK, W_BLOCK), 0)
    kj = lax.broadcasted_iota(I32, (W_BLOCK, W_BLOCK), 1)
    neg = -jnp.inf
    ok_prev = (kj >= qi) & (n > 0)
    ok_next = (kj <= qi) & (n < nb - 1)
    for m in range(SWA_HEADS // 2):
        g = (2 * m) // (SWA_HEADS // SWA_KV_HEADS)
        gs = slice(g * LANE, (g + 1) * LANE)
        outs = []
        for half, q in enumerate(_mask_halves(q_ref[:, m * LANE:(m + 1) * LANE].astype(F32))):
            segs = [(_dot_nt(q, kc_ref[:, gs]), vc_ref[:, gs]),
                    (jnp.where(ok_prev, _dot_nt(q, kp_ref[:, gs]), neg), vp_ref[:, gs]),
                    (_dot_nt(q, k0_ref[:, gs]), v0_ref[:, gs]),
                    (jnp.where(ok_next, _dot_nt(q, kn_ref[:, gs]), neg), vn_ref[:, gs])]
            outs.append(_softmax_pv(segs, sink_ref[2 * m + half]))
        o_ref[:, m * LANE:(m + 1) * LANE] = _pair(outs[0], outs[1]).astype(BF)


def _lat_swa_call(sink, q, k, v):
    nb = DEC_SEQ // W_BLOCK
    cb = PAST_LEN // W_BLOCK

    def ctx():
        return pl.BlockSpec((None, PAST_LEN, 2 * LANE), lambda b, n: (b, 0, 0))

    def loc(d):
        return pl.BlockSpec((None, W_BLOCK, 2 * LANE),
                            lambda b, n: (b, cb + jnp.clip(n + d, 0, nb - 1), 0))

    return pl.pallas_call(
        _lat_swa_kernel,
        grid=(DEC_BATCH, nb),
        in_specs=[_smem(), pl.BlockSpec((None, W_BLOCK, 512), lambda b, n: (b, n, 0)),
                  ctx(), loc(-1), loc(0), loc(1), ctx(), loc(-1), loc(0), loc(1)],
        out_specs=pl.BlockSpec((None, W_BLOCK, 512), lambda b, n: (b, n, 0)),
        out_shape=jax.ShapeDtypeStruct((DEC_BATCH, DEC_SEQ, 512), BF),
        compiler_params=_cp(("parallel", "arbitrary"), 40), name="lat_swa",
    )(sink, q, k, k, k, k, v, v, v, v)


def _rows_to_tiles(o_ref, val, tm):
    for k in range(D_CHUNKS):
        o_ref[pl.ds(k, tm, stride=D_CHUNKS), :] = val[:, k * LANE:(k + 1) * LANE]


def _post_kernel(x_ref, oda_ref, omla_ref, oswa_ref, gate_ref, mod_ref, n2_ref, wda_ref, wmla_ref, wswa_ref,
                 wo_ref, wrh_ref, wrl_ref, xo_ref, h2_ref, lg_ref):
    m = mod_ref[...]
    g = gate_ref[...].astype(F32)
    merged = (g[:, 0:D_MODEL] * _dot(oda_ref[...], wda_ref[...])
              + g[:, D_MODEL:2 * D_MODEL] * _dot(omla_ref[...], wmla_ref[...])
              + g[:, 2 * D_MODEL:] * _dot(oswa_ref[...], wswa_ref[...]))
    x = x_ref[...] + m[2:3] * _dot(merged.astype(BF), wo_ref[...])
    xo_ref[...] = x
    h2 = _rms(x, n2_ref[...]) * (1.0 + m[4:5]) + m[3:4]
    _rows_to_tiles(h2_ref, h2, TM_POST)
    hh, hl = _split(h2)
    lg_ref[...] = _dot_nt(wrh_ref[...], hh) + _dot_nt(wrl_ref[...], hh) + _dot_nt(wrh_ref[...], hl)


def _post_call(x, oda, omla, oswa, gate, mod, n2, wda, wmla, wswa, wo, wrh, wrl, latent):
    n = x.shape[0]
    tm = TM_POST
    row = pl.BlockSpec((tm, D_MODEL), lambda i: (i, 0))
    o512 = pl.BlockSpec((tm, 512), lambda i: (i, 0))
    return pl.pallas_call(
        _post_kernel, grid=(n // tm,),
        in_specs=[row, o512, o512, o512, pl.BlockSpec((tm, GATE_COLS), lambda i: (i, 0)),
                  _mod_spec(tm, latent), _const((1, D_MODEL)),
                  _const((512, D_MODEL)), _const((512, D_MODEL)), _const((512, D_MODEL)),
                  _const((D_MODEL, D_MODEL)), _const((N_EXPERTS, D_MODEL)), _const((N_EXPERTS, D_MODEL))],
        out_specs=[row, pl.BlockSpec((tm * D_CHUNKS, LANE), lambda i: (i, 0)),
                   pl.BlockSpec((N_EXPERTS, tm), lambda i: (0, i))],
        out_shape=[jax.ShapeDtypeStruct((n, D_MODEL), F32),
                   jax.ShapeDtypeStruct((n * D_CHUNKS, LANE), F32),
                   jax.ShapeDtypeStruct((N_EXPERTS, n), F32)],
        compiler_params=_cp(("parallel",), 48), name="post",
    )(x, oda, omla, oswa, gate, mod, n2, wda, wmla, wswa, wo, wrh, wrl)


def _router_kernel(cap, lg_ref, idx_ref, g_ref, cnt_ref, q_scr, a_scr):
    e_n, n = lg_ref.shape
    lg = lg_ref[...]
    ex = jnp.exp(lg - jnp.max(lg, axis=0, keepdims=True))
    aff = ex / jnp.sum(ex, axis=0, keepdims=True)
    a_scr[...] = aff
    bits = pltpu.bitcast(aff, I32)
    capf = float(cap)

    def count(mask):
        return jnp.sum(jnp.where(mask, 1.0, 0.0), axis=1, keepdims=True)

    def thr_bit(i, t):
        cand = t | jnp.left_shift(jnp.int32(1), 30 - i)
        return jnp.where(count(bits >= cand) >= capf, cand, t)

    thr = lax.fori_loop(0, 31, thr_bit, jnp.zeros((e_n, 1), I32))
    gt = bits > thr
    eq = bits == thr
    need = capf - count(gt)
    tok = lax.broadcasted_iota(I32, (e_n, n), 1)
    n_bits = n.bit_length()

    def tie_bit(i, bound):
        cand = bound | jnp.left_shift(jnp.int32(1), n_bits - 1 - i)
        ok = (cand <= n) & (count(eq & (tok < cand)) <= need)
        return jnp.where(ok, cand, bound)

    bound = lax.fori_loop(0, n_bits, tie_bit, jnp.zeros((e_n, 1), I32))
    sel = gt | (eq & (tok < bound))
    cnt_ref[...] = jnp.broadcast_to(count(sel & (tok < TOK_HALF)), cnt_ref.shape)
    blk = 2 * LANE
    upper = (lax.broadcasted_iota(I32, (blk, blk), 0) <= lax.broadcasted_iota(I32, (blk, blk), 1))
    upper = jnp.where(upper, 1.0, 0.0).astype(BF)
    carry = jnp.zeros((e_n, 1), F32)
    for c in range(n // blk):
        s_blk = sel[:, c * blk:(c + 1) * blk]
        rank = _dot(jnp.where(s_blk, 1.0, 0.0).astype(BF), upper) + carry
        q_scr[:, c * blk:(c + 1) * blk] = jnp.where(s_blk, rank, 0.0)
        carry = rank[:, blk - 1:blk]
    idx_ref[...] = jnp.zeros(idx_ref.shape, I32)
    g_ref[...] = jnp.zeros(g_ref.shape, F32)

    def per_expert(e, _):
        def per_tile(rt, _):
            off = pl.multiple_of(rt * ROUTE_TILE, ROUTE_TILE)
            slot = (off + 1 + lax.broadcasted_iota(I32, (ROUTE_TILE, 1), 0)).astype(F32)
            idx_v = jnp.zeros((ROUTE_TILE, 1), F32)
            g_v = jnp.zeros((ROUTE_TILE, 1), F32)
            for c in range(n // ROUTE_LANES):
                cs = slice(c * ROUTE_LANES, (c + 1) * ROUTE_LANES)
                hit = q_scr[pl.ds(e, 1), cs] == slot
                tok_f = (c * ROUTE_LANES + lax.broadcasted_iota(I32, (1, ROUTE_LANES), 1)).astype(F32)
                idx_v = idx_v + jnp.sum(jnp.where(hit, tok_f, 0.0), axis=1, keepdims=True)
                g_v = g_v + jnp.sum(jnp.where(hit, a_scr[pl.ds(e, 1), cs], 0.0), axis=1, keepdims=True)
            mine = _lane((ROUTE_TILE, LANE)) == e
            idx_ref[pl.ds(off, ROUTE_TILE), :] = jnp.where(mine, idx_v.astype(I32), idx_ref[pl.ds(off, ROUTE_TILE), :])
            g_ref[pl.ds(off, ROUTE_TILE), :] = jnp.where(mine, g_v, g_ref[pl.ds(off, ROUTE_TILE), :])
            return 0

        return lax.fori_loop(0, cap // ROUTE_TILE, per_tile, 0)

    lax.fori_loop(0, e_n, per_expert, 0)


def _router_call(logits_t, cap):
    n = logits_t.shape[1]
    return pl.pallas_call(
        functools.partial(_router_kernel, cap),
        out_shape=[jax.ShapeDtypeStruct((cap, LANE), I32), jax.ShapeDtypeStruct((cap, LANE), F32),
                   jax.ShapeDtypeStruct((N_EXPERTS, LANE), F32)],
        scratch_shapes=[pltpu.VMEM((N_EXPERTS, n), F32), pltpu.VMEM((N_EXPERTS, n), F32)],
        compiler_params=pltpu.CompilerParams(vmem_limit_bytes=40 * 1024 * 1024), name="router",
    )(logits_t)


def _row_copy(src_hbm, xe, sem, buf, tok, slot):
    dst = xe.at[buf, pl.ds(pl.multiple_of(slot * SUB, SUB), SUB)]
    return pltpu.make_async_copy(src_hbm.at[tok], dst, sem.at[buf])


def _ffn_kernel(idx_ref, idxn_ref, hc_hbm, hl_hbm, g_ref, w1_ref, w3_ref, w2_ref, yc_ref, yl_ref,
                xe, xb, w1b, w3b, w2b, sem):
    e = pl.program_id(0)
    f = pl.program_id(1)
    buf = e % 2

    @pl.when((e == 0) & (f == 0))
    def _():
        def start_ctx(r, _):
            _row_copy(hc_hbm, xe, sem, 0, idx_ref[0, r], r).start()
            return 0

        def start_lat(r, _):
            _row_copy(hl_hbm, xe, sem, 0, idx_ref[0, r], r).start()
            return 0

        def wait_ctx(r, _):
            _row_copy(hc_hbm, xe, sem, 0, idx_ref[0, r], r).wait()
            return 0

        def wait_lat(r, _):
            _row_copy(hl_hbm, xe, sem, 0, idx_ref[0, r], r).wait()
            return 0

        lax.fori_loop(0, CAP_CTX, start_ctx, 0)
        lax.fori_loop(CAP_CTX, SLOTS, start_lat, 0)
        lax.fori_loop(0, CAP_CTX, wait_ctx, 0)
        lax.fori_loop(CAP_CTX, SLOTS, wait_lat, 0)

    @pl.when(f == 0)
    def _():
        for k in range(D_CHUNKS):
            xb[:, k * LANE:(k + 1) * LANE] = xe[buf, pl.ds(k, SLOTS, stride=D_CHUNKS), :].astype(BF)

    w1b[...] = w1_ref[...].astype(BF)
    w3b[...] = w3_ref[...].astype(BF)
    w2b[...] = w2_ref[...].astype(BF)

    def next_rows():
        part_c, part_l = CAP_CTX // FF_SPLIT, CAP_LAT // FF_SPLIT
        for u in range(part_c):
            r = f * part_c + u
            yield _row_copy(hc_hbm, xe, sem, 1 - buf, idxn_ref[0, r], r)
        for u in range(part_l):
            r = CAP_CTX + f * part_l + u
            yield _row_copy(hl_hbm, xe, sem, 1 - buf, idxn_ref[0, r], r)

    n_tiles = SLOTS // FFN_ROWS
    for t in range(n_tiles):
        if t == 0:
            for cp in next_rows():
                cp.start()
        if t == n_tiles - 1:
            for cp in next_rows():
                cp.wait()
        rows = slice(t * FFN_ROWS, (t + 1) * FFN_ROWS)
        x = xb[rows, :]
        a = _dot(x, w1b[...])
        hid = (a * jax.nn.sigmoid(a) * _dot(x, w3b[...])).astype(BF)
        y = _dot(hid, w2b[...]) * g_ref[rows, :]
        y_ref, row0 = (yc_ref, t * FFN_ROWS) if t * FFN_ROWS < CAP_CTX else (yl_ref, t * FFN_ROWS - CAP_CTX)

        def tile(k):
            return pl.ds(row0 * D_CHUNKS + k, FFN_ROWS, stride=D_CHUNKS)

        @pl.when(f == 0)
        def _():
            for k in range(D_CHUNKS):
                y_ref[tile(k), :] = y[:, k * LANE:(k + 1) * LANE]

        @pl.when(f > 0)
        def _():
            for k in range(D_CHUNKS):
                y_ref[tile(k), :] = y_ref[tile(k), :] + y[:, k * LANE:(k + 1) * LANE]


def _ffn_call(layer, idx, h_ctx, h_lat, g, w1, w3, w2):
    ff = EXPERT_FF // FF_SPLIT
    wspec = pl.BlockSpec((None, None, D_MODEL, ff), lambda e, f: (layer, e, 0, f))
    last = N_EXPERTS - 1
    return pl.pallas_call(
        _ffn_kernel,
        grid=(N_EXPERTS, FF_SPLIT),
        in_specs=[pl.BlockSpec((None, 1, SLOTS), lambda e, f: (e, 0, 0), memory_space=pltpu.SMEM),
                  pl.BlockSpec((None, 1, SLOTS), lambda e, f: (jnp.minimum(e + 1, last), 0, 0),
                               memory_space=pltpu.SMEM),
                  pl.BlockSpec(memory_space=pl.ANY), pl.BlockSpec(memory_space=pl.ANY),
                  pl.BlockSpec((None, SLOTS, 1), lambda e, f: (e, 0, 0)),
                  wspec, wspec, pl.BlockSpec((None, None, ff, D_MODEL), lambda e, f: (layer, e, f, 0))],
        out_specs=[pl.BlockSpec((None, CAP_CTX * D_CHUNKS, LANE), lambda e, f: (e, 0, 0)),
                   pl.BlockSpec((None, CAP_LAT * D_CHUNKS, LANE), lambda e, f: (e, 0, 0))],
        out_shape=[jax.ShapeDtypeStruct((N_EXPERTS, CAP_CTX * D_CHUNKS, LANE), F32),
                   jax.ShapeDtypeStruct((N_EXPERTS, CAP_LAT * D_CHUNKS, LANE), F32)],
        scratch_shapes=[pltpu.VMEM((2, SLOTS * D_CHUNKS, LANE), F32), pltpu.VMEM((SLOTS, D_MODEL), BF),
                        pltpu.VMEM((D_MODEL, ff), BF), pltpu.VMEM((D_MODEL, ff), BF),
                        pltpu.VMEM((ff, D_MODEL), BF), pltpu.SemaphoreType.DMA((2,))],
        compiler_params=_cp(("arbitrary", "arbitrary"), 54), name="ffn",
    )(idx, idx, h_ctx, h_lat, g, w1, w3, w2)


def _combine_kernel(idx_ref, bnd_ref, yc_ref, yl_ref, acc_ref):
    h = pl.program_id(0)
    e = pl.program_id(1)

    @pl.when(e == 0)
    def _():
        acc_ref[...] = jnp.zeros(acc_ref.shape, F32)

    def add_rows(y_ref, slot0, lo, hi, base):
        def group(i, _):
            r0 = lo + i * COMBINE_GROUP
            toks = [idx_ref[0, r0 + u] - base for u in range(COMBINE_GROUP)]
            sums = [acc_ref[toks[u]] + y_ref[r0 - slot0 + u] for u in range(COMBINE_GROUP)]
            for u in range(COMBINE_GROUP):
                acc_ref[toks[u]] = sums[u]
            return 0

        def single(r, _):
            t = idx_ref[0, r] - base
            acc_ref[t] = acc_ref[t] + y_ref[r - slot0]
            return 0

        n_groups = (hi - lo) // COMBINE_GROUP
        lax.fori_loop(0, n_groups, group, 0)
        lax.fori_loop(lo + n_groups * COMBINE_GROUP, hi, single, 0)

    @pl.when(h == 0)
    def _():
        add_rows(yc_ref, 0, 0, CAP_CTX, 0)

    @pl.when(h > 0)
    def _():
        add_rows(yl_ref, CAP_CTX, bnd_ref[0, h], bnd_ref[0, h + 1], (h - 1) * TOK_HALF)


def _combine_call(idx, bnd, y_ctx, y_lat):
    last = N_EXPERTS - 1
    return pl.pallas_call(
        _combine_kernel,
        grid=(N_HALVES, N_EXPERTS),
        in_specs=[pl.BlockSpec((None, 1, SLOTS), lambda h, e: (e, 0, 0), memory_space=pltpu.SMEM),
                  pl.BlockSpec((None, 1, N_HALVES + 1), lambda h, e: (e, 0, 0), memory_space=pltpu.SMEM),
                  pl.BlockSpec((None, CAP_CTX, SUB, LANE), lambda h, e: (jnp.where(h == 0, e, last), 0, 0, 0)),
                  pl.BlockSpec((None, CAP_LAT, SUB, LANE), lambda h, e: (jnp.where(h == 0, 0, e), 0, 0, 0))],
        out_specs=pl.BlockSpec((None, TOK_HALF, SUB, LANE), lambda h, e: (h, 0, 0, 0)),
        out_shape=jax.ShapeDtypeStruct((N_HALVES, TOK_HALF, SUB, LANE), F32),
        compiler_params=_cp(("parallel", "arbitrary"), 52), name="combine",
    )(idx, bnd, y_ctx.reshape(N_EXPERTS, CAP_CTX, SUB, LANE), y_lat.reshape(N_EXPERTS, CAP_LAT, SUB, LANE))


def _final_kernel(x_ref, moe_ref, mod_ref, g_ref, o_ref):
    x = x_ref[...] + mod_ref[...][5:6] * _moe_rows(moe_ref, TM_PRE)
    o_ref[...] = _rms(x, g_ref[...])


def _final_call(x, moe, mod, g, latent):
    n = x.shape[0]
    tm = TM_PRE
    moe_off = (N_CTX // tm) if latent else 0
    row = pl.BlockSpec((tm, D_MODEL), lambda i: (i, 0))
    return pl.pallas_call(
        _final_kernel, grid=(n // tm,),
        in_specs=[row, pl.BlockSpec((tm * D_CHUNKS, LANE), lambda i: (i + moe_off, 0)),
                  _mod_spec(tm, latent), _const((1, D_MODEL))],
        out_specs=row, out_shape=jax.ShapeDtypeStruct((n, D_MODEL), F32),
        compiler_params=_cp(("parallel",), 32), name="final",
    )(x, moe, mod, g)


def _axial_tables(rot_dim):
    rows = DEC_SEQ // GRID_W
    row = jnp.repeat(jnp.arange(rows, dtype=F32), GRID_W)
    col = jnp.tile(jnp.arange(GRID_W, dtype=F32), rows)
    n_freq = rot_dim // 4
    inv = ROPE_BASE ** (-jnp.arange(n_freq, dtype=F32) / n_freq)
    ang = jnp.concatenate([row[:, None] * inv, col[:, None] * inv], axis=-1)
    sign = jnp.tile(jnp.array([-1.0, 1.0], F32), rot_dim // 2)
    return jnp.repeat(jnp.cos(ang), 2, axis=1), jnp.repeat(jnp.sin(ang), 2, axis=1) * sign


def _rope_tables():
    c64, s64 = _axial_tables(DA_DH)
    c32, s32 = _axial_tables(MLA_ROPE)
    one = lambda w: jnp.ones((DEC_SEQ, w), F32)
    zero = lambda w: jnp.zeros((DEC_SEQ, w), F32)
    pad = LANE - MLA_NOPE - MLA_ROPE
    return jnp.stack([
        jnp.tile(c64, (1, 2)), jnp.tile(s64, (1, 2)),
        jnp.concatenate([one(MLA_NOPE), c32, one(pad)], axis=1),
        jnp.concatenate([zero(MLA_NOPE), s32, zero(pad)], axis=1),
        jnp.concatenate([c32, one(LANE - MLA_ROPE)], axis=1),
        jnp.concatenate([s32, zero(LANE - MLA_ROPE)], axis=1)])


def _layer_weights(l, w_in, w_gate, mla_w_qb, mla_w_kvb, w_br_da, w_br_mla, w_br_swa, w_o, w_router):
    kr_end = C_MKR + MLA_ROPE
    win = jnp.concatenate([w_in[l][:, :kr_end], jnp.zeros((D_MODEL, LANE - MLA_ROPE), F32),
                           w_in[l][:, kr_end:]], axis=1).astype(BF)
    dk = MLA_NOPE + MLA_ROPE
    wqb = jnp.pad(mla_w_qb[l].reshape(MLA_Q_RANK, MLA_HEADS, dk), ((0, 0), (0, 0), (0, LANE - dk)))
    wqb = wqb.reshape(MLA_Q_RANK, MLA_HEADS * LANE).astype(BF)
    kvb = mla_w_kvb[l].reshape(MLA_KV_RANK, MLA_HEADS, MLA_NOPE + MLA_V)
    wk = jnp.pad(kvb[:, :, :MLA_NOPE], ((0, 0), (0, 0), (0, LANE - MLA_NOPE))).reshape(MLA_KV_RANK, MLA_HEADS * LANE)
    wv = kvb[:, :, MLA_NOPE:].reshape(MLA_KV_RANK, MLA_OUT)
    wkv = jnp.concatenate([wk, wv], axis=1).astype(BF)
    wrh, wrl = _split(w_router[l].T)
    return dict(win=win, wg=w_gate[l].astype(BF), wqb=wqb, wkv=wkv, wda=w_br_da[l].astype(BF),
                wmla=w_br_mla[l].astype(BF), wswa=w_br_swa[l].astype(BF), wo=w_o[l].astype(BF), wrh=wrh, wrl=wrl)


def kernel(x_prompt, x_sample, cache_da_k, cache_da_v, cache_mla_ckv, cache_mla_krope, cache_swa_k, cache_swa_v, c, c_ctx, w_ada, b_ada, norm1, norm2, w_in, da_lq1, da_lk1, da_lq2, da_lk2, da_subln, mla_q_norm, mla_w_qb, mla_kv_norm, mla_w_kvb, swa_sink, w_gate, w_br_da, w_br_mla, w_br_swa, w_o, w_router, w_ff1, w_ff3, w_ff2, final_norm):
    c_all = jnp.concatenate([c_ctx[None], c, jnp.zeros((SUB - 1 - DEC_BATCH, D_MODEL), F32)], axis=0)
    mod_all = _ada_call(c_all, w_ada, b_ada)[:, :1 + DEC_BATCH].reshape(DEPTH, 1 + DEC_BATCH, 6, D_MODEL)
    lam_all = _lam_call(da_lq1, da_lk1, da_lq2, da_lk2)
    tab = _rope_tables()

    xp = x_prompt.reshape(N_CTX, D_MODEL)
    xs = x_sample.reshape(N_LAT, D_MODEL)
    moe = None
    new = [[] for _ in range(6)]
    for l in range(DEPTH):
        w = _layer_weights(l, w_in, w_gate, mla_w_qb, mla_w_kvb, w_br_da, w_br_mla, w_br_swa, w_o, w_router)
        mod = mod_all[l]
        mod_prev = mod_all[l - 1] if l else None
        n1, n2 = norm1[l][None], norm2[l][None]
        subln, qn, kvn = da_subln[l][None], mla_q_norm[l][None], mla_kv_norm[l][None]
        lam = lam_all[l, :1]
        sink = swa_sink[l]

        xp, proj_c, gate_c = _pre_call(xp, moe, mod_prev, mod, n1, w["win"], w["wg"], latent=False)
        xs, proj_l, gate_l = _pre_call(xs, moe, mod_prev, mod, n1, w["win"], w["wg"], latent=True)

        oda_c, omla_c, oswa_c, ckv_c = _ctx_attn_call(l, lam, sink, proj_c, subln, qn, w["wqb"], kvn, w["wkv"])
        pc = proj_c.reshape(BATCH, SEQ, PROJ_COLS)
        new[0].append(pc[..., C_DAK:C_DAK + 512].reshape(BATCH, SEQ, DA_HEADS, 2 * DA_DH))
        new[1].append(pc[..., C_DAV:C_DAV + 512].reshape(BATCH, SEQ, DA_HEADS, 2 * DA_DH))
        new[2].append(ckv_c.reshape(BATCH, SEQ, MLA_KV_RANK))
        new[3].append(pc[..., C_MKR:C_MKR + MLA_ROPE])
        new[4].append(pc[..., C_SK:C_SK + LANE].reshape(BATCH, SEQ, SWA_KV_HEADS, SWA_DH))
        new[5].append(pc[..., C_SV:C_SV + LANE].reshape(BATCH, SEQ, SWA_KV_HEADS, SWA_DH))

        caches = (cache_da_k[:, l].reshape(DEC_BATCH, PAST_LEN, 512),
                  cache_da_v[:, l].reshape(DEC_BATCH, PAST_LEN, 512),
                  cache_mla_ckv[:, l],
                  jnp.pad(cache_mla_krope[:, l], ((0, 0), (0, 0), (0, LANE - MLA_ROPE))),
                  cache_swa_k[:, l].reshape(DEC_BATCH, PAST_LEN, LANE),
                  cache_swa_v[:, l].reshape(DEC_BATCH, PAST_LEN, LANE))
        daq, mlaq, swaq, dak, dav, mlak, mlav, swak, swav = _lat_prep_call(
            proj_l, tab, caches, qn, w["wqb"], kvn, w["wkv"])
        oda_l = _lat_da_call(l, lam, daq, dak, dav, subln).reshape(N_LAT, 512)
        omla_l = _lat_mla_call(mlaq, mlak, mlav).reshape(N_LAT, 512)
        oswa_l = _lat_swa_call(sink, swaq, swak, swav).reshape(N_LAT, 512)

        post_w = (w["wda"], w["wmla"], w["wswa"], w["wo"], w["wrh"], w["wrl"])
        xp, h2_c, lg_c = _post_call(xp, oda_c, omla_c, oswa_c, gate_c, mod, n2, *post_w, latent=False)
        xs, h2_l, lg_l = _post_call(xs, oda_l, omla_l, oswa_l, gate_l, mod, n2, *post_w, latent=True)

        idx_c, g_c, _ = _router_call(lg_c, CAP_CTX)
        idx_l, g_l, cnt_l = _router_call(lg_l, CAP_LAT)
        idx = jnp.concatenate([idx_c[:, :N_EXPERTS].T, idx_l[:, :N_EXPERTS].T], axis=1)
        gsel = jnp.concatenate([g_c[:, :N_EXPERTS].T, g_l[:, :N_EXPERTS].T], axis=1)
        n0 = cnt_l[:, 0].astype(I32)
        bnd = jnp.stack([jnp.zeros_like(n0), jnp.full_like(n0, CAP_CTX), CAP_CTX + n0,
                         jnp.full_like(n0, SLOTS)], axis=1)
        idx3 = idx.reshape(N_EXPERTS, 1, SLOTS)
        y_c, y_l = _ffn_call(l, idx3, h2_c.reshape(N_CTX, SUB, LANE), h2_l.reshape(N_LAT, SUB, LANE),
                             gsel.reshape(N_EXPERTS, SLOTS, 1), w_ff1, w_ff3, w_ff2)
        acc = _combine_call(idx3, bnd.reshape(N_EXPERTS, 1, N_HALVES + 1), y_c, y_l)
        moe = acc.reshape(N_HALVES * TOK_HALF * D_CHUNKS, LANE)

    mod_last = mod_all[DEPTH - 1]
    fn = final_norm[None]
    y_prompt = _final_call(xp, moe, mod_last, fn, latent=False).reshape(BATCH, SEQ, D_MODEL)
    y_sample = _final_call(xs, moe, mod_last, fn, latent=True).reshape(DEC_BATCH, DEC_SEQ, D_MODEL)
    return (y_prompt, y_sample) + tuple(jnp.stack(n, axis=1) for n in new)
```

```python
import functools
import math

import jax
import jax.numpy as jnp
from jax import lax
from jax.experimental import pallas as pl
from jax.experimental.pallas import tpu as pltpu

F32 = jnp.float32
BF = jnp.bfloat16
I32 = jnp.int32

D_MODEL = 1024
BATCH = 16
SEQ = 256
DEPTH = 2
DEC_BATCH = 2
DEC_SEQ = 4096
PAST_LEN = 512
GRID_W = 64
ROPE_BASE = 10000.0
EPS = 1e-6
DA_HEADS = 4
DA_DH = 64
DA_OUT = DA_HEADS * 2 * DA_DH
MLA_HEADS = 8
MLA_Q_RANK = 256
MLA_KV_RANK = 128
MLA_NOPE = 64
MLA_ROPE = 32
MLA_V = 64
MLA_OUT = MLA_HEADS * MLA_V
SWA_HEADS = 8
SWA_KV_HEADS = 2
SWA_DH = 64
W_BLOCK = 128
SWA_OUT = SWA_HEADS * SWA_DH
N_EXPERTS = 16
EXPERT_FF = 1024
CAPACITY_FACTOR = 2

LANE = 128
SUB = 8
HALF = 64
N_CTX = BATCH * SEQ
N_LAT = DEC_BATCH * DEC_SEQ
KEYS = PAST_LEN + DEC_SEQ
CAP_CTX = CAPACITY_FACTOR * N_CTX // N_EXPERTS
CAP_LAT = CAPACITY_FACTOR * N_LAT // N_EXPERTS
SLOTS = CAP_CTX + CAP_LAT
TOK_HALF = 4096
N_HALVES = (N_CTX + N_LAT) // TOK_HALF
D_CHUNKS = D_MODEL // LANE

C_DAQ, C_DAK, C_DAV = 0, 512, 1024
C_MQ, C_MKV, C_MKR = 1536, 1792, 1920
C_SQ, C_SK, C_SV = 2048, 2560, 2688
PROJ_COLS = 2816
GATE_COLS = 3 * D_MODEL
MLA_SCALE = (MLA_NOPE + MLA_ROPE) ** -0.5
HEAD_SCALE = DA_DH ** -0.5
LOG2E = math.log2(math.e)

TM_PRE = 256
TM_POST = 512
CHUNK = 512
TQ = 256
DA_STEP_HEADS = 4
MLA_STEP_PAIRS = 4
ROUTE_TILE = 128
ROUTE_CHUNK = 512
THRESHOLD_STEPS = 48
FFN_ROWS = 512
FF_SPLIT = 2
COMBINE_GROUP = 8


def _cp(sem, vmem_mb):
    return pltpu.CompilerParams(dimension_semantics=sem, vmem_limit_bytes=vmem_mb * 1024 * 1024)


def _dot(a, b):
    return jnp.dot(a, b, preferred_element_type=F32)


def _dot_nt(a, b):
    return lax.dot_general(a, b, (((1,), (1,)), ((), ())), preferred_element_type=F32)


def _split(a):
    hi = a.astype(BF)
    lo = (a - hi.astype(F32)).astype(BF)
    return hi, lo


def _dot3(a, w):
    ah, al = _split(a)
    wh, wl = _split(w)
    return _dot(ah, wh) + _dot(ah, wl) + _dot(al, wh)


def _rms(x, g):
    return x * lax.rsqrt(jnp.mean(x * x, axis=-1, keepdims=True) + EPS) * g


def _lane(shape):
    return lax.broadcasted_iota(I32, shape, len(shape) - 1)


def _softmax_pv(segs, sink=None):
    m = None
    for s, _ in segs:
        ms = jnp.max(s, axis=-1, keepdims=True)
        m = ms if m is None else jnp.maximum(m, ms)
    if sink is not None:
        m = jnp.maximum(m, sink)
    l = None
    o = None
    for s, v in segs:
        e = jnp.exp(s - m)
        ls = jnp.sum(e, axis=-1, keepdims=True)
        os_ = _dot(e.astype(BF), v)
        l = ls if l is None else l + ls
        o = os_ if o is None else o + os_
    if sink is not None:
        l = l + jnp.exp(sink - m)
    return o / l


def _mask_halves(q):
    lo = _lane(q.shape) < HALF
    return jnp.where(lo, q, 0.0).astype(BF), jnp.where(lo, 0.0, q).astype(BF)


def _pair(o_a, o_b):
    return jnp.where(_lane(o_a.shape) < HALF, o_a, o_b)


def _dup(x):
    r = pltpu.roll(x, HALF, 1)
    lo = _lane(x.shape) < HALF
    return jnp.where(lo, x, r), jnp.where(lo, r, x)


def _rope(x, c, s):
    n = x.shape[-1]
    even = (_lane(x.shape) % 2) == 0
    sw = jnp.where(even, pltpu.roll(x, n - 1, 1), pltpu.roll(x, 1, 1))
    return x * c + sw * s


def _da_head(q, k, v, lam, subln, lam_init):
    q1, q2 = _mask_halves(q)
    o1 = _softmax_pv([(_dot_nt(q1, k), v)])
    o2 = _softmax_pv([(_dot_nt(q2, k), v)])
    return _rms(o1 - lam * o2, subln) * (1.0 - lam_init)


def _ada_kernel(c_ref, w_ref, b_ref, o_ref):
    c = c_ref[...]
    a = c * jax.nn.sigmoid(c)
    o_ref[...] = _dot3(a, w_ref[...]) + b_ref[...]


def _ada_call(c_all, w_ada, b_ada):
    tn = 1536
    return pl.pallas_call(
        _ada_kernel,
        grid=(DEPTH, 6 * D_MODEL // tn),
        in_specs=[pl.BlockSpec((SUB, D_MODEL), lambda l, j: (0, 0)),
                  pl.BlockSpec((None, D_MODEL, tn), lambda l, j: (l, 0, j)),
                  pl.BlockSpec((None, 1, tn), lambda l, j: (l, 0, j))],
        out_specs=pl.BlockSpec((None, SUB, tn), lambda l, j: (l, 0, j)),
        out_shape=jax.ShapeDtypeStruct((DEPTH, SUB, 6 * D_MODEL), F32),
        compiler_params=_cp(("parallel", "parallel"), 40),
        name="ada",
    )(c_all, w_ada, b_ada.reshape(DEPTH, 1, 6 * D_MODEL))


def _lam_kernel(q1, k1, q2, k2, o_ref):
    s1 = jnp.sum(q1[...] * k1[...], axis=-1, keepdims=True)
    s2 = jnp.sum(q2[...] * k2[...], axis=-1, keepdims=True)
    row = lax.broadcasted_iota(I32, (DEPTH, 1), 0)
    init = jnp.zeros((DEPTH, 1), F32)
    for l in range(DEPTH):
        init = jnp.where(row == l, _lam_init(l), init)
    o_ref[...] = jnp.broadcast_to(jnp.exp(s1) - jnp.exp(s2) + init, o_ref.shape)


def _lam_init(layer):
    return 0.8 - 0.6 * math.exp(-0.3 * layer)


def _lam_call(q1, k1, q2, k2):
    return pl.pallas_call(
        _lam_kernel,
        out_shape=jax.ShapeDtypeStruct((DEPTH, LANE), F32),
        name="lam",
    )(q1, k1, q2, k2)


def _moe_rows(moe_ref, tm):
    return jnp.concatenate([moe_ref[pl.ds(k, tm, stride=D_CHUNKS), :] for k in range(D_CHUNKS)], axis=1)


def _pre_body(x, mod_ref, n1_ref, win_ref, wg_ref, proj_ref, gate_ref):
    m = mod_ref[...]
    h = _rms(x, n1_ref[...]) * (1.0 + m[1:2]) + m[0:1]
    hb = h.astype(BF)
    proj_ref[...] = _dot(hb, win_ref[...])
    gate_ref[...] = jax.nn.sigmoid(_dot(hb, wg_ref[...])).astype(BF)


def _pre_first_kernel(x_ref, mod_ref, n1_ref, win_ref, wg_ref, proj_ref, gate_ref):
    _pre_body(x_ref[...], mod_ref, n1_ref, win_ref, wg_ref, proj_ref, gate_ref)


def _pre_next_kernel(x_ref, moe_ref, modp_ref, mod_ref, n1_ref, win_ref, wg_ref, xo_ref, proj_ref, gate_ref):
    x = x_ref[...] + modp_ref[...][5:6] * _moe_rows(moe_ref, TM_PRE)
    xo_ref[...] = x
    _pre_body(x, mod_ref, n1_ref, win_ref, wg_ref, proj_ref, gate_ref)


def _mod_spec(tm, latent):
    per = DEC_SEQ // tm
    if latent:
        return pl.BlockSpec((None, 6, D_MODEL), lambda i: (1 + i // per, 0, 0))
    return pl.BlockSpec((None, 6, D_MODEL), lambda i: (0, 0, 0))


def _const(shape):
    nd = len(shape)
    return pl.BlockSpec(shape, lambda *_: (0,) * nd)


def _pre_call(x, moe, mod_prev, mod, n1, win, wg, latent):
    n = x.shape[0]
    tm = TM_PRE
    row = pl.BlockSpec((tm, D_MODEL), lambda i: (i, 0))
    w_specs = [_mod_spec(tm, latent), _const((1, D_MODEL)), _const((D_MODEL, PROJ_COLS)),
               _const((D_MODEL, GATE_COLS))]
    outs = [jax.ShapeDtypeStruct((n, PROJ_COLS), F32), jax.ShapeDtypeStruct((n, GATE_COLS), BF)]
    out_specs = [pl.BlockSpec((tm, PROJ_COLS), lambda i: (i, 0)), pl.BlockSpec((tm, GATE_COLS), lambda i: (i, 0))]
    if moe is None:
        proj, gate = pl.pallas_call(
            _pre_first_kernel, grid=(n // tm,), in_specs=[row] + w_specs, out_specs=out_specs, out_shape=outs,
            compiler_params=_cp(("parallel",), 52), name="pre_first",
        )(x, mod, n1, win, wg)
        return x, proj, gate
    moe_off = (N_CTX // tm) if latent else 0
    moe_spec = pl.BlockSpec((tm * D_CHUNKS, LANE), lambda i: (i + moe_off, 0))
    xo, proj, gate = pl.pallas_call(
        _pre_next_kernel, grid=(n // tm,),
        in_specs=[row, moe_spec, _mod_spec(tm, latent)] + w_specs,
        out_specs=[row] + out_specs,
        out_shape=[jax.ShapeDtypeStruct((n, D_MODEL), F32)] + outs,
        compiler_params=_cp(("parallel",), 52), name="pre_next",
    )(x, moe, mod_prev, mod, n1, win, wg)
    return xo, proj, gate


def _ctx_attn_kernel(lam_init, lam_ref, sink_ref, p_ref, subln_ref, qn_ref, wqb_ref, kvn_ref, wkv_ref,
                     oda_ref, omla_ref, oswa_ref, ckv_ref):
    p = p_ref[...]
    lam = lam_ref[0]
    for h in range(DA_HEADS):
        blk = slice(h * LANE, (h + 1) * LANE)
        q = p[:, C_DAQ + h * LANE:C_DAQ + (h + 1) * LANE] * HEAD_SCALE
        k = p[:, C_DAK + h * LANE:C_DAK + (h + 1) * LANE].astype(BF)
        v = p[:, C_DAV + h * LANE:C_DAV + (h + 1) * LANE].astype(BF)
        oda_ref[:, blk] = _da_head(q, k, v, lam, subln_ref[...], lam_init).astype(BF)
    qn = _rms(p[:, C_MQ:C_MQ + MLA_Q_RANK], qn_ref[...]).astype(BF)
    cq = _dot(qn, wqb_ref[...])
    ckv = _rms(p[:, C_MKV:C_MKV + MLA_KV_RANK], kvn_ref[...])
    ckv_ref[...] = ckv
    kv = _dot(ckv.astype(BF), wkv_ref[...])
    kr_sh = pltpu.roll(p[:, C_MKR:C_MKR + LANE], HALF, 1)
    for m in range(MLA_HEADS // 2):
        vpair = kv[:, MLA_HEADS * LANE + m * LANE:MLA_HEADS * LANE + (m + 1) * LANE].astype(BF)
        outs = []
        for h in (2 * m, 2 * m + 1):
            q = (cq[:, h * LANE:(h + 1) * LANE] * MLA_SCALE).astype(BF)
            k = (kv[:, h * LANE:(h + 1) * LANE] + kr_sh).astype(BF)
            outs.append(_softmax_pv([(_dot_nt(q, k), vpair)]))
        omla_ref[:, m * LANE:(m + 1) * LANE] = _pair(outs[0], outs[1]).astype(BF)
    kd = _dup(p[:, C_SK:C_SK + LANE])
    vd = _dup(p[:, C_SV:C_SV + LANE])
    for m in range(SWA_HEADS // 2):
        g = (2 * m) // (SWA_HEADS // SWA_KV_HEADS)
        k = kd[g].astype(BF)
        v = vd[g].astype(BF)
        qa, qb = _mask_halves(p[:, C_SQ + m * LANE:C_SQ + (m + 1) * LANE] * HEAD_SCALE)
        oa = _softmax_pv([(_dot_nt(qa, k), v)], sink_ref[2 * m])
        ob = _softmax_pv([(_dot_nt(qb, k), v)], sink_ref[2 * m + 1])
        oswa_ref[:, m * LANE:(m + 1) * LANE] = _pair(oa, ob).astype(BF)


def _smem():
    return pl.BlockSpec(memory_space=pltpu.SMEM)


def _ctx_attn_call(layer, lam, sink, proj, subln, qn, wqb, kvn, wkv):
    row512 = pl.BlockSpec((SEQ, 512), lambda b: (b, 0))
    return pl.pallas_call(
        functools.partial(_ctx_attn_kernel, _lam_init(layer)),
        grid=(BATCH,),
        in_specs=[_smem(), _smem(), pl.BlockSpec((SEQ, PROJ_COLS), lambda b: (b, 0)),
                  _const((1, LANE)), _const((1, MLA_Q_RANK)), _const((MLA_Q_RANK, MLA_HEADS * LANE)),
                  _const((1, MLA_KV_RANK)), _const((MLA_KV_RANK, MLA_HEADS * LANE + MLA_OUT))],
        out_specs=[row512, row512, row512, pl.BlockSpec((SEQ, MLA_KV_RANK), lambda b: (b, 0))],
        out_shape=[jax.ShapeDtypeStruct((N_CTX, 512), BF)] * 3 + [jax.ShapeDtypeStruct((N_CTX, MLA_KV_RANK), F32)],
        compiler_params=_cp(("parallel",), 40), name="ctx_attn",
    )(lam, sink, proj, subln, qn, wqb, kvn, wkv)


def _lat_prep_kernel(p_ref, tab_ref, cdk_ref, cdv_ref, cckv_ref, ckr_ref, csk_ref, csv_ref,
                     qn_ref, wqb_ref, kvn_ref, wkv_ref,
                     daq_ref, mlaq_ref, swaq_ref, dak_ref, dav_ref, mlak_ref, mlav_ref, swak_ref, swav_ref):
    j = pl.program_id(1)

    def write_kv(dk, dv, ckv, kr_sh, sk, sv):
        ones_col = jnp.where(_lane((CHUNK, LANE)) == 0, 1.0, 0.0).astype(BF)
        dak_ref[...] = dk.astype(BF)
        for h in range(DA_HEADS):
            dav_ref[:, 2 * h * LANE:(2 * h + 1) * LANE] = dv[:, h * LANE:(h + 1) * LANE].astype(BF)
            dav_ref[:, (2 * h + 1) * LANE:(2 * h + 2) * LANE] = ones_col
        kv = _dot(ckv.astype(BF), wkv_ref[...])
        for h in range(MLA_HEADS):
            mlak_ref[:, h * LANE:(h + 1) * LANE] = (kv[:, h * LANE:(h + 1) * LANE] + kr_sh).astype(BF)
        for p in range(MLA_HEADS // 2):
            mlav_ref[:, 2 * p * LANE:(2 * p + 1) * LANE] = kv[:, (MLA_HEADS + p) * LANE:(MLA_HEADS + p + 1) * LANE].astype(BF)
            mlav_ref[:, (2 * p + 1) * LANE:(2 * p + 2) * LANE] = ones_col
        k0, k1 = _dup(sk)
        v0, v1 = _dup(sv)
        swak_ref[:, 0:LANE] = k0.astype(BF)
        swak_ref[:, LANE:2 * LANE] = k1.astype(BF)
        swav_ref[:, 0:LANE] = v0.astype(BF)
        swav_ref[:, LANE:2 * LANE] = v1.astype(BF)

    @pl.when(j == 0)
    def _():
        write_kv(cdk_ref[...], cdv_ref[...], cckv_ref[...], pltpu.roll(ckr_ref[...], HALF, 1),
                 csk_ref[...], csv_ref[...])

    @pl.when(j > 0)
    def _():
        c64, s64 = tab_ref[0], tab_ref[1]
        cmq, smq = tab_ref[2], tab_ref[3]
        ckr, skr = tab_ref[4], tab_ref[5]

        def blk(c0, h):
            return p_ref[:, c0 + h * LANE:c0 + (h + 1) * LANE]

        for h in range(DA_HEADS):
            daq_ref[:, h * LANE:(h + 1) * LANE] = (_rope(blk(C_DAQ, h), c64, s64) * (HEAD_SCALE * LOG2E)).astype(BF)
            swaq_ref[:, h * LANE:(h + 1) * LANE] = (_rope(blk(C_SQ, h), c64, s64) * HEAD_SCALE).astype(BF)
        dk = jnp.concatenate([_rope(blk(C_DAK, h), c64, s64) for h in range(DA_HEADS)], axis=1)
        qn = _rms(p_ref[:, C_MQ:C_MQ + MLA_Q_RANK], qn_ref[...]).astype(BF)
        cq = _dot(qn, wqb_ref[...])
        for h in range(MLA_HEADS):
            mlaq_ref[:, h * LANE:(h + 1) * LANE] = (
                _rope(cq[:, h * LANE:(h + 1) * LANE], cmq, smq) * (MLA_SCALE * LOG2E)).astype(BF)
        ckv = _rms(p_ref[:, C_MKV:C_MKV + MLA_KV_RANK], kvn_ref[...])
        kr_sh = pltpu.roll(_rope(p_ref[:, C_MKR:C_MKR + LANE], ckr, skr), HALF, 1)
        sk = _rope(p_ref[:, C_SK:C_SK + LANE], c64, s64)
        write_kv(dk, p_ref[:, C_DAV:C_DAV + 512], ckv, kr_sh, sk, p_ref[:, C_SV:C_SV + LANE])


def _lat_prep_call(proj, tab, caches, qn, wqb, kvn, wkv):
    cdk, cdv, cckv, ckr, csk, csv = caches
    nj = 1 + DEC_SEQ // CHUNK

    def own(width):
        return pl.BlockSpec((None, CHUNK, width), lambda b, j: (b, jnp.maximum(j - 1, 0), 0))

    def cache(width):
        return pl.BlockSpec((None, PAST_LEN, width), lambda b, j: (b, 0, 0))

    def allk(width):
        return pl.BlockSpec((None, CHUNK, width), lambda b, j: (b, j, 0))

    def shp(rows, width):
        return jax.ShapeDtypeStruct((DEC_BATCH, rows, width), BF)

    return pl.pallas_call(
        _lat_prep_kernel,
        grid=(DEC_BATCH, nj),
        in_specs=[own(PROJ_COLS),
                  pl.BlockSpec((6, CHUNK, LANE), lambda b, j: (0, jnp.maximum(j - 1, 0), 0)),
                  cache(512), cache(512), cache(LANE), cache(LANE), cache(LANE), cache(LANE),
                  _const((1, MLA_Q_RANK)), _const((MLA_Q_RANK, MLA_HEADS * LANE)),
                  _const((1, MLA_KV_RANK)), _const((MLA_KV_RANK, MLA_HEADS * LANE + MLA_OUT))],
        out_specs=[own(512), own(MLA_HEADS * LANE), own(512),
                   allk(512), allk(1024), allk(MLA_HEADS * LANE), allk(1024), allk(2 * LANE), allk(2 * LANE)],
        out_shape=[shp(DEC_SEQ, 512), shp(DEC_SEQ, MLA_HEADS * LANE), shp(DEC_SEQ, 512),
                   shp(KEYS, 512), shp(KEYS, 1024), shp(KEYS, MLA_HEADS * LANE), shp(KEYS, 1024),
                   shp(KEYS, 2 * LANE), shp(KEYS, 2 * LANE)],
        compiler_params=_cp(("parallel", "arbitrary"), 48), name="lat_prep",
    )(proj.reshape(DEC_BATCH, DEC_SEQ, PROJ_COLS), tab, cdk, cdv, cckv, ckr, csk, csv, qn, wqb, kvn, wkv)


def _softmax2_pv(s, v_aug):
    e = jnp.exp2(s - jnp.max(s, axis=-1, keepdims=True)).astype(BF)
    o = _dot(e, v_aug)
    return o[:, 0:LANE] / o[:, LANE:LANE + 1]


def _lat_da_kernel(lam_init, lam_ref, q_ref, k_ref, v_ref, subln_ref, o_ref):
    for h in range(DA_STEP_HEADS):
        cols = slice(h * LANE, (h + 1) * LANE)
        q1, q2 = _mask_halves(q_ref[:, cols].astype(F32))
        k = k_ref[:, cols]
        v = v_ref[:, 2 * h * LANE:(2 * h + 2) * LANE]
        o = _softmax2_pv(_dot_nt(q1, k), v) - lam_ref[0] * _softmax2_pv(_dot_nt(q2, k), v)
        o_ref[:, cols] = (_rms(o, subln_ref[...]) * (1.0 - lam_init)).astype(BF)


def _lat_da_call(layer, lam, q, k, v, subln):
    return pl.pallas_call(
        functools.partial(_lat_da_kernel, _lam_init(layer)),
        grid=(DEC_BATCH, DA_HEADS // DA_STEP_HEADS, DEC_SEQ // TQ),
        in_specs=[_smem(),
                  pl.BlockSpec((None, TQ, DA_STEP_HEADS * LANE), lambda b, h, i: (b, i, h)),
                  pl.BlockSpec((None, KEYS, DA_STEP_HEADS * LANE), lambda b, h, i: (b, 0, h)),
                  pl.BlockSpec((None, KEYS, 2 * DA_STEP_HEADS * LANE), lambda b, h, i: (b, 0, h)),
                  _const((1, LANE))],
        out_specs=pl.BlockSpec((None, TQ, DA_STEP_HEADS * LANE), lambda b, h, i: (b, i, h)),
        out_shape=jax.ShapeDtypeStruct((DEC_BATCH, DEC_SEQ, 512), BF),
        compiler_params=_cp(("parallel", "parallel", "arbitrary"), 56), name="lat_da",
    )(lam, q, k, v, subln)


def _lat_mla_kernel(q_ref, k_ref, v_ref, o_ref):
    for p in range(MLA_STEP_PAIRS):
        v = v_ref[:, 2 * p * LANE:(2 * p + 2) * LANE]
        outs = []
        for h in (2 * p, 2 * p + 1):
            cols = slice(h * LANE, (h + 1) * LANE)
            outs.append(_softmax2_pv(_dot_nt(q_ref[:, cols], k_ref[:, cols]), v))
        o_ref[:, p * LANE:(p + 1) * LANE] = _pair(outs[0], outs[1]).astype(BF)


def _lat_mla_call(q, k, v):
    sp = MLA_STEP_PAIRS
    return pl.pallas_call(
        _lat_mla_kernel,
        grid=(DEC_BATCH, MLA_HEADS // (2 * sp), DEC_SEQ // TQ),
        in_specs=[pl.BlockSpec((None, TQ, 2 * sp * LANE), lambda b, m, i: (b, i, m)),
                  pl.BlockSpec((None, KEYS, 2 * sp * LANE), lambda b, m, i: (b, 0, m)),
                  pl.BlockSpec((None, KEYS, 2 * sp * LANE), lambda b, m, i: (b, 0, m))],
        out_specs=pl.BlockSpec((None, TQ, sp * LANE), lambda b, m, i: (b, i, m)),
        out_shape=jax.ShapeDtypeStruct((DEC_BATCH, DEC_SEQ, 512), BF),
        compiler_params=_cp(("parallel", "parallel", "arbitrary"), 56), name="lat_mla",
    )(q, k, v)


def _lat_swa_kernel(sink_ref, q_ref, kc_ref, kp_ref, k0_ref, kn_ref, vc_ref, vp_ref, v0_ref, vn_ref, o_ref):
    n = pl.program_id(1)
    nb = pl.num_programs(1)
    group = SWA_HEADS // SWA_KV_HEADS
    rows = group * W_BLOCK
    n_keys = PAST_LEN + 3 * W_BLOCK
    qi = lax.broadcasted_iota(I32, (rows, n_keys), 0) & (W_BLOCK - 1)
    loc = lax.broadcasted_iota(I32, (rows, n_keys), 1) - PAST_LEN
    ok = ((loc < 0) | ((loc >= W_BLOCK) & (loc < 2 * W_BLOCK))
          | ((loc >= 0) & (loc < W_BLOCK) & (loc >= qi) & (n > 0))
          | ((loc >= 2 * W_BLOCK) & (loc - 2 * W_BLOCK <= qi) & (n < nb - 1)))
    head_of_row = lax.broadcasted_iota(I32, (rows, 1), 0) // W_BLOCK
    for g in range(SWA_KV_HEADS):
        gs = slice(g * LANE, (g + 1) * LANE)
        q_parts = []
        sink = jnp.zeros((rows, 1), F32)
        for j in range(group):
            sink = jnp.where(head_of_row == j, sink_ref[g * group + j], sink)
        for p in range(group // 2):
            blk = g * (group // 2) + p
            q_parts.extend(_mask_halves(q_ref[:, blk * LANE:(blk + 1) * LANE].astype(F32)))
        q = jnp.concatenate(q_parts, axis=0)
        k = jnp.concatenate([kc_ref[:, gs], kp_ref[:, gs], k0_ref[:, gs], kn_ref[:, gs]], axis=0)
        v = jnp.concatenate([vc_ref[:, gs], vp_ref[:, gs], v0_ref[:, gs], vn_ref[:, gs]], axis=0)
        o = _softmax_pv([(jnp.where(ok, _dot_nt(q, k), -jnp.inf), v)], sink)
        for p in range(group // 2):
            blk = g * (group // 2) + p
            o_a = o[(2 * p) * W_BLOCK:(2 * p + 1) * W_BLOCK]
            o_b = o[(2 * p + 1) * W_BLOCK:(2 * p + 2) * W_BLOCK]
            o_ref[:, blk * LANE:(blk + 1) * LANE] = _pair(o_a, o_b).astype(BF)


def _lat_swa_call(sink, q, k, v):
    nb = DEC_SEQ // W_BLOCK
    cb = PAST_LEN // W_BLOCK

    def ctx():
        return pl.BlockSpec((None, PAST_LEN, 2 * LANE), lambda b, n: (b, 0, 0))

    def loc(d):
        return pl.BlockSpec((None, W_BLOCK, 2 * LANE),
                            lambda b, n: (b, cb + jnp.clip(n + d, 0, nb - 1), 0))

    return pl.pallas_call(
        _lat_swa_kernel,
        grid=(DEC_BATCH, nb),
        in_specs=[_smem(), pl.BlockSpec((None, W_BLOCK, 512), lambda b, n: (b, n, 0)),
                  ctx(), loc(-1), loc(0), loc(1), ctx(), loc(-1), loc(0), loc(1)],
        out_specs=pl.BlockSpec((None, W_BLOCK, 512), lambda b, n: (b, n, 0)),
        out_shape=jax.ShapeDtypeStruct((DEC_BATCH, DEC_SEQ, 512), BF),
        compiler_params=_cp(("parallel", "arbitrary"), 40), name="lat_swa",
    )(sink, q, k, k, k, k, v, v, v, v)


def _rows_to_tiles(o_ref, val, tm):
    for k in range(D_CHUNKS):
        o_ref[pl.ds(k, tm, stride=D_CHUNKS), :] = val[:, k * LANE:(k + 1) * LANE]


def _post_kernel(x_ref, oda_ref, omla_ref, oswa_ref, gate_ref, mod_ref, n2_ref, wda_ref, wmla_ref, wswa_ref,
                 wo_ref, wrh_ref, wrl_ref, xo_ref, h2_ref, lg_ref):
    m = mod_ref[...]
    g = gate_ref[...].astype(F32)
    merged = (g[:, 0:D_MODEL] * _dot(oda_ref[...], wda_ref[...])
              + g[:, D_MODEL:2 * D_MODEL] * _dot(omla_ref[...], wmla_ref[...])
              + g[:, 2 * D_MODEL:] * _dot(oswa_ref[...], wswa_ref[...]))
    x = x_ref[...] + m[2:3] * _dot(merged.astype(BF), wo_ref[...])
    xo_ref[...] = x
    h2 = _rms(x, n2_ref[...]) * (1.0 + m[4:5]) + m[3:4]
    _rows_to_tiles(h2_ref, h2, TM_POST)
    hh, hl = _split(h2)
    lg_ref[...] = _dot_nt(wrh_ref[...], hh) + _dot_nt(wrl_ref[...], hh) + _dot_nt(wrh_ref[...], hl)


def _post_call(x, oda, omla, oswa, gate, mod, n2, wda, wmla, wswa, wo, wrh, wrl, latent):
    n = x.shape[0]
    tm = TM_POST
    row = pl.BlockSpec((tm, D_MODEL), lambda i: (i, 0))
    o512 = pl.BlockSpec((tm, 512), lambda i: (i, 0))
    return pl.pallas_call(
        _post_kernel, grid=(n // tm,),
        in_specs=[row, o512, o512, o512, pl.BlockSpec((tm, GATE_COLS), lambda i: (i, 0)),
                  _mod_spec(tm, latent), _const((1, D_MODEL)),
                  _const((512, D_MODEL)), _const((512, D_MODEL)), _const((512, D_MODEL)),
                  _const((D_MODEL, D_MODEL)), _const((N_EXPERTS, D_MODEL)), _const((N_EXPERTS, D_MODEL))],
        out_specs=[row, pl.BlockSpec((tm * D_CHUNKS, LANE), lambda i: (i, 0)),
                   pl.BlockSpec((N_EXPERTS, tm), lambda i: (0, i))],
        out_shape=[jax.ShapeDtypeStruct((n, D_MODEL), F32),
                   jax.ShapeDtypeStruct((n * D_CHUNKS, LANE), F32),
                   jax.ShapeDtypeStruct((N_EXPERTS, n), F32)],
        compiler_params=_cp(("parallel",), 48), name="post",
    )(x, oda, omla, oswa, gate, mod, n2, wda, wmla, wswa, wo, wrh, wrl)


def _router_kernel(cap, lg_ref, q_ref, a_ref, lo_ref, hi_ref, cnt_ref):
    e_n, n = lg_ref.shape
    lg = lg_ref[...]
    shifted = lg - jnp.max(lg, axis=0, keepdims=True)
    ex = jnp.exp(shifted)
    den = jnp.sum(ex, axis=0, keepdims=True)
    aff = ex / den
    score = shifted - jnp.log(den)
    capf = float(cap)

    def count(mask):
        return jnp.sum(jnp.where(mask, 1.0, 0.0), axis=1, keepdims=True)

    def halve(_, bounds):
        lo, hi = bounds
        mid = 0.5 * (lo + hi)
        enough = count(score >= mid) >= capf
        return jnp.where(enough, mid, lo), jnp.where(enough, hi, mid)

    lo, hi = lax.fori_loop(0, THRESHOLD_STEPS, halve,
                           (jnp.min(score, axis=1, keepdims=True), jnp.ones((e_n, 1), F32)))
    gt = score >= hi
    eq = (score >= lo) & (score < hi)
    need = capf - count(gt)
    tok = lax.broadcasted_iota(I32, (e_n, n), 1)
    n_bits = n.bit_length()

    def tie_bit(i, bound):
        cand = bound | jnp.left_shift(jnp.int32(1), n_bits - 1 - i)
        ok = (cand <= n) & (count(eq & (tok < cand)) <= need)
        return jnp.where(ok, cand, bound)

    bound = lax.fori_loop(0, n_bits, tie_bit, jnp.zeros((e_n, 1), I32))
    sel = gt | (eq & (tok < bound))
    cnt_ref[...] = jnp.broadcast_to(count(sel & (tok < TOK_HALF)), cnt_ref.shape)
    blk = 2 * LANE
    per_chunk = ROUTE_CHUNK // blk
    n_chunks = n // ROUTE_CHUNK
    upper = (lax.broadcasted_iota(I32, (blk, blk), 0) <= lax.broadcasted_iota(I32, (blk, blk), 1))
    upper = jnp.where(upper, 1.0, 0.0).astype(BF)
    lane = _lane((e_n, LANE))
    carry = jnp.zeros((e_n, 1), F32)
    c_in = jnp.zeros((e_n, LANE), F32)
    c_ex = jnp.zeros((e_n, LANE), F32)
    for c in range(n // blk):
        chunk, part = divmod(c, per_chunk)
        if part == 0:
            c_ex = jnp.where(lane == chunk, carry, c_ex)
            a_ref[chunk] = aff[:, chunk * ROUTE_CHUNK:(chunk + 1) * ROUTE_CHUNK]
        s_blk = sel[:, c * blk:(c + 1) * blk]
        rank = _dot(jnp.where(s_blk, 1.0, 0.0).astype(BF), upper) + carry
        q_ref[chunk, :, part * blk:(part + 1) * blk] = jnp.where(s_blk, rank, 0.0)
        carry = rank[:, blk - 1:blk]
        if part == per_chunk - 1:
            c_in = jnp.where(lane == chunk, carry, c_in)
    valid = lane < n_chunks
    lo = jnp.zeros((e_n, LANE), I32)
    hi = jnp.zeros((e_n, LANE), I32)
    for rt in range(cap // ROUTE_TILE):
        lo_rt = count(valid & (c_in < float(rt * ROUTE_TILE + 1)))
        hi_rt = count(valid & (c_ex < float((rt + 1) * ROUTE_TILE)))
        lo = jnp.where(lane == rt, lo_rt.astype(I32), lo)
        hi = jnp.where(lane == rt, hi_rt.astype(I32), hi)
    lo_ref[...] = lo
    hi_ref[...] = hi


def _compact_kernel(lo_ref, hi_ref, q_ref, a_ref, idx_ref, g_ref):
    e = pl.program_id(0)
    n_tiles = idx_ref.shape[0]
    eye = lax.broadcasted_iota(I32, (ROUTE_TILE, LANE), 0) == lax.broadcasted_iota(I32, (ROUTE_TILE, LANE), 1)

    def per_tile(rt, _):
        slot = (rt * ROUTE_TILE + 1 + lax.broadcasted_iota(I32, (ROUTE_TILE, 1), 0)).astype(F32)

        def per_chunk(c, acc):
            hit = q_ref[c, pl.ds(e, 1), :] == slot
            tok = (c * ROUTE_CHUNK + lax.broadcasted_iota(I32, (1, ROUTE_CHUNK), 1)).astype(F32)
            return (acc[0] + jnp.sum(jnp.where(hit, tok, 0.0), axis=1, keepdims=True),
                    acc[1] + jnp.sum(jnp.where(hit, a_ref[c, pl.ds(e, 1), :], 0.0), axis=1, keepdims=True))

        zero = jnp.zeros((ROUTE_TILE, 1), F32)
        idx_v, g_v = lax.fori_loop(lo_ref[e, rt], hi_ref[e, rt], per_chunk, (zero, zero))
        idx_ref[pl.ds(rt, 1), :] = jnp.sum(jnp.where(eye, idx_v, 0.0), axis=0, keepdims=True).astype(I32)
        g_ref[pl.ds(rt, 1), :] = jnp.sum(jnp.where(eye, g_v, 0.0), axis=0, keepdims=True)
        return 0

    lax.fori_loop(0, n_tiles, per_tile, 0)


def _router_call(logits_t, cap):
    n = logits_t.shape[1]
    n_chunks = n // ROUTE_CHUNK
    n_tiles = cap // ROUTE_TILE
    chunked = jax.ShapeDtypeStruct((n_chunks, N_EXPERTS, ROUTE_CHUNK), F32)
    q, a, lo, hi, cnt = pl.pallas_call(
        functools.partial(_router_kernel, cap),
        out_shape=[chunked, chunked, jax.ShapeDtypeStruct((N_EXPERTS, LANE), I32),
                   jax.ShapeDtypeStruct((N_EXPERTS, LANE), I32), jax.ShapeDtypeStruct((N_EXPERTS, LANE), F32)],
        compiler_params=pltpu.CompilerParams(vmem_limit_bytes=40 * 1024 * 1024), name="router",
    )(logits_t)
    whole = pl.BlockSpec((n_chunks, N_EXPERTS, ROUTE_CHUNK), lambda e, *_: (0, 0, 0))
    slots = pl.BlockSpec((None, n_tiles, ROUTE_TILE), lambda e, *_: (e, 0, 0))
    idx, g = pl.pallas_call(
        _compact_kernel,
        grid_spec=pltpu.PrefetchScalarGridSpec(
            num_scalar_prefetch=2, grid=(N_EXPERTS,), in_specs=[whole, whole], out_specs=[slots, slots]),
        out_shape=[jax.ShapeDtypeStruct((N_EXPERTS, n_tiles, ROUTE_TILE), I32),
                   jax.ShapeDtypeStruct((N_EXPERTS, n_tiles, ROUTE_TILE), F32)],
        compiler_params=_cp(("parallel",), 32), name="compact",
    )(lo[:, :n_tiles], hi[:, :n_tiles], q, a)
    return idx.reshape(N_EXPERTS, cap), g.reshape(N_EXPERTS, cap), cnt


def _row_copy(src_hbm, xe, sem, buf, tok, slot):
    dst = xe.at[buf, pl.ds(pl.multiple_of(slot * SUB, SUB), SUB)]
    return pltpu.make_async_copy(src_hbm.at[tok], dst, sem.at[buf])


def _ffn_kernel(idx_ref, idxn_ref, hc_hbm, hl_hbm, g_ref, w1_ref, w3_ref, w2_ref, yc_ref, yl_ref,
                xe, xb, w1b, w3b, w2b, sem):
    e = pl.program_id(0)
    f = pl.program_id(1)
    buf = e % 2

    @pl.when((e == 0) & (f == 0))
    def _():
        def start_ctx(r, _):
            _row_copy(hc_hbm, xe, sem, 0, idx_ref[0, r], r).start()
            return 0

        def start_lat(r, _):
            _row_copy(hl_hbm, xe, sem, 0, idx_ref[0, r], r).start()
            return 0

        def wait_ctx(r, _):
            _row_copy(hc_hbm, xe, sem, 0, idx_ref[0, r], r).wait()
            return 0

        def wait_lat(r, _):
            _row_copy(hl_hbm, xe, sem, 0, idx_ref[0, r], r).wait()
            return 0

        lax.fori_loop(0, CAP_CTX, start_ctx, 0)
        lax.fori_loop(CAP_CTX, SLOTS, start_lat, 0)
        lax.fori_loop(0, CAP_CTX, wait_ctx, 0)
        lax.fori_loop(CAP_CTX, SLOTS, wait_lat, 0)

    @pl.when(f == 0)
    def _():
        for k in range(D_CHUNKS):
            xb[:, k * LANE:(k + 1) * LANE] = xe[buf, pl.ds(k, SLOTS, stride=D_CHUNKS), :].astype(BF)

    w1b[...] = w1_ref[...].astype(BF)
    w3b[...] = w3_ref[...].astype(BF)
    w2b[...] = w2_ref[...].astype(BF)

    def next_rows():
        part_c, part_l = CAP_CTX // FF_SPLIT, CAP_LAT // FF_SPLIT
        for u in range(part_c):
            r = f * part_c + u
            yield _row_copy(hc_hbm, xe, sem, 1 - buf, idxn_ref[0, r], r)
        for u in range(part_l):
            r = CAP_CTX + f * part_l + u
            yield _row_copy(hl_hbm, xe, sem, 1 - buf, idxn_ref[0, r], r)

    n_tiles = SLOTS // FFN_ROWS
    for t in range(n_tiles):
        if t == 0:
            for cp in next_rows():
                cp.start()
        if t == n_tiles - 1:
            for cp in next_rows():
                cp.wait()
        rows = slice(t * FFN_ROWS, (t + 1) * FFN_ROWS)
        x = xb[rows, :]
        a = _dot(x, w1b[...])
        hid = (a * jax.nn.sigmoid(a) * _dot(x, w3b[...])).astype(BF)
        y = _dot(hid, w2b[...]) * g_ref[rows, :]
        y_ref, row0 = (yc_ref, t * FFN_ROWS) if t * FFN_ROWS < CAP_CTX else (yl_ref, t * FFN_ROWS - CAP_CTX)

        def tile(k):
            return pl.ds(row0 * D_CHUNKS + k, FFN_ROWS, stride=D_CHUNKS)

        @pl.when(f == 0)
        def _():
            for k in range(D_CHUNKS):
                y_ref[tile(k), :] = y[:, k * LANE:(k + 1) * LANE]

        @pl.when(f > 0)
        def _():
            for k in range(D_CHUNKS):
                y_ref[tile(k), :] = y_ref[tile(k), :] + y[:, k * LANE:(k + 1) * LANE]


def _ffn_call(layer, idx, h_ctx, h_lat, g, w1, w3, w2):
    ff = EXPERT_FF // FF_SPLIT
    wspec = pl.BlockSpec((None, None, D_MODEL, ff), lambda e, f: (layer, e, 0, f))
    last = N_EXPERTS - 1
    return pl.pallas_call(
        _ffn_kernel,
        grid=(N_EXPERTS, FF_SPLIT),
        in_specs=[pl.BlockSpec((None, 1, SLOTS), lambda e, f: (e, 0, 0), memory_space=pltpu.SMEM),
                  pl.BlockSpec((None, 1, SLOTS), lambda e, f: (jnp.minimum(e + 1, last), 0, 0),
                               memory_space=pltpu.SMEM),
                  pl.BlockSpec(memory_space=pl.ANY), pl.BlockSpec(memory_space=pl.ANY),
                  pl.BlockSpec((None, SLOTS, 1), lambda e, f: (e, 0, 0)),
                  wspec, wspec, pl.BlockSpec((None, None, ff, D_MODEL), lambda e, f: (layer, e, f, 0))],
        out_specs=[pl.BlockSpec((None, CAP_CTX * D_CHUNKS, LANE), lambda e, f: (e, 0, 0)),
                   pl.BlockSpec((None, CAP_LAT * D_CHUNKS, LANE), lambda e, f: (e, 0, 0))],
        out_shape=[jax.ShapeDtypeStruct((N_EXPERTS, CAP_CTX * D_CHUNKS, LANE), F32),
                   jax.ShapeDtypeStruct((N_EXPERTS, CAP_LAT * D_CHUNKS, LANE), F32)],
        scratch_shapes=[pltpu.VMEM((2, SLOTS * D_CHUNKS, LANE), F32), pltpu.VMEM((SLOTS, D_MODEL), BF),
                        pltpu.VMEM((D_MODEL, ff), BF), pltpu.VMEM((D_MODEL, ff), BF),
                        pltpu.VMEM((ff, D_MODEL), BF), pltpu.SemaphoreType.DMA((2,))],
        compiler_params=_cp(("arbitrary", "arbitrary"), 54), name="ffn",
    )(idx, idx, h_ctx, h_lat, g, w1, w3, w2)


def _combine_kernel(idx_ref, bnd_ref, yc_ref, yl_ref, acc_ref):
    h = pl.program_id(0)
    e = pl.program_id(1)

    @pl.when(e == 0)
    def _():
        acc_ref[...] = jnp.zeros(acc_ref.shape, F32)

    def add_rows(y_ref, slot0, lo, hi, base):
        def group(i, _):
            r0 = lo + i * COMBINE_GROUP
            toks = [idx_ref[0, r0 + u] - base for u in range(COMBINE_GROUP)]
            sums = [acc_ref[toks[u]] + y_ref[r0 - slot0 + u] for u in range(COMBINE_GROUP)]
            for u in range(COMBINE_GROUP):
                acc_ref[toks[u]] = sums[u]
            return 0

        def single(r, _):
            t = idx_ref[0, r] - base
            acc_ref[t] = acc_ref[t] + y_ref[r - slot0]
            return 0

        n_groups = (hi - lo) // COMBINE_GROUP
        lax.fori_loop(0, n_groups, group, 0)
        lax.fori_loop(lo + n_groups * COMBINE_GROUP, hi, single, 0)

    @pl.when(h == 0)
    def _():
        add_rows(yc_ref, 0, 0, CAP_CTX, 0)

    @pl.when(h > 0)
    def _():
        add_rows(yl_ref, CAP_CTX, bnd_ref[0, h], bnd_ref[0, h + 1], (h - 1) * TOK_HALF)


def _combine_call(idx, bnd, y_ctx, y_lat):
    last = N_EXPERTS - 1
    return pl.pallas_call(
        _combine_kernel,
        grid=(N_HALVES, N_EXPERTS),
        in_specs=[pl.BlockSpec((None, 1, SLOTS), lambda h, e: (e, 0, 0), memory_space=pltpu.SMEM),
                  pl.BlockSpec((None, 1, N_HALVES + 1), lambda h, e: (e, 0, 0), memory_space=pltpu.SMEM),
                  pl.BlockSpec((None, CAP_CTX, SUB, LANE), lambda h, e: (jnp.where(h == 0, e, last), 0, 0, 0)),
                  pl.BlockSpec((None, CAP_LAT, SUB, LANE), lambda h, e: (jnp.where(h == 0, 0, e), 0, 0, 0))],
        out_specs=pl.BlockSpec((None, TOK_HALF, SUB, LANE), lambda h, e: (h, 0, 0, 0)),
        out_shape=jax.ShapeDtypeStruct((N_HALVES, TOK_HALF, SUB, LANE), F32),
        compiler_params=_cp(("parallel", "arbitrary"), 52), name="combine",
    )(idx, bnd, y_ctx.reshape(N_EXPERTS, CAP_CTX, SUB, LANE), y_lat.reshape(N_EXPERTS, CAP_LAT, SUB, LANE))


def _final_kernel(x_ref, moe_ref, mod_ref, g_ref, o_ref):
    x = x_ref[...] + mod_ref[...][5:6] * _moe_rows(moe_ref, TM_PRE)
    o_ref[...] = _rms(x, g_ref[...])


def _final_call(x, moe, mod, g, latent):
    n = x.shape[0]
    tm = TM_PRE
    moe_off = (N_CTX // tm) if latent else 0
    row = pl.BlockSpec((tm, D_MODEL), lambda i: (i, 0))
    return pl.pallas_call(
        _final_kernel, grid=(n // tm,),
        in_specs=[row, pl.BlockSpec((tm * D_CHUNKS, LANE), lambda i: (i + moe_off, 0)),
                  _mod_spec(tm, latent), _const((1, D_MODEL))],
        out_specs=row, out_shape=jax.ShapeDtypeStruct((n, D_MODEL), F32),
        compiler_params=_cp(("parallel",), 32), name="final",
    )(x, moe, mod, g)


def _axial_tables(rot_dim):
    rows = DEC_SEQ // GRID_W
    row = jnp.repeat(jnp.arange(rows, dtype=F32), GRID_W)
    col = jnp.tile(jnp.arange(GRID_W, dtype=F32), rows)
    n_freq = rot_dim // 4
    inv = ROPE_BASE ** (-jnp.arange(n_freq, dtype=F32) / n_freq)
    ang = jnp.concatenate([row[:, None] * inv, col[:, None] * inv], axis=-1)
    sign = jnp.tile(jnp.array([-1.0, 1.0], F32), rot_dim // 2)
    return jnp.repeat(jnp.cos(ang), 2, axis=1), jnp.repeat(jnp.sin(ang), 2, axis=1) * sign


def _rope_tables():
    c64, s64 = _axial_tables(DA_DH)
    c32, s32 = _axial_tables(MLA_ROPE)
    one = lambda w: jnp.ones((DEC_SEQ, w), F32)
    zero = lambda w: jnp.zeros((DEC_SEQ, w), F32)
    pad = LANE - MLA_NOPE - MLA_ROPE
    return jnp.stack([
        jnp.tile(c64, (1, 2)), jnp.tile(s64, (1, 2)),
        jnp.concatenate([one(MLA_NOPE), c32, one(pad)], axis=1),
        jnp.concatenate([zero(MLA_NOPE), s32, zero(pad)], axis=1),
        jnp.concatenate([c32, one(LANE - MLA_ROPE)], axis=1),
        jnp.concatenate([s32, zero(LANE - MLA_ROPE)], axis=1)])


def _layer_weights(l, w_in, w_gate, mla_w_qb, mla_w_kvb, w_br_da, w_br_mla, w_br_swa, w_o, w_router):
    kr_end = C_MKR + MLA_ROPE
    win = jnp.concatenate([w_in[l][:, :kr_end], jnp.zeros((D_MODEL, LANE - MLA_ROPE), F32),
                           w_in[l][:, kr_end:]], axis=1).astype(BF)
    dk = MLA_NOPE + MLA_ROPE
    wqb = jnp.pad(mla_w_qb[l].reshape(MLA_Q_RANK, MLA_HEADS, dk), ((0, 0), (0, 0), (0, LANE - dk)))
    wqb = wqb.reshape(MLA_Q_RANK, MLA_HEADS * LANE).astype(BF)
    kvb = mla_w_kvb[l].reshape(MLA_KV_RANK, MLA_HEADS, MLA_NOPE + MLA_V)
    wk = jnp.pad(kvb[:, :, :MLA_NOPE], ((0, 0), (0, 0), (0, LANE - MLA_NOPE))).reshape(MLA_KV_RANK, MLA_HEADS * LANE)
    wv = kvb[:, :, MLA_NOPE:].reshape(MLA_KV_RANK, MLA_OUT)
    wkv = jnp.concatenate([wk, wv], axis=1).astype(BF)
    wrh, wrl = _split(w_router[l].T)
    return dict(win=win, wg=w_gate[l].astype(BF), wqb=wqb, wkv=wkv, wda=w_br_da[l].astype(BF),
                wmla=w_br_mla[l].astype(BF), wswa=w_br_swa[l].astype(BF), wo=w_o[l].astype(BF), wrh=wrh, wrl=wrl)


def kernel(x_prompt, x_sample, cache_da_k, cache_da_v, cache_mla_ckv, cache_mla_krope, cache_swa_k, cache_swa_v, c, c_ctx, w_ada, b_ada, norm1, norm2, w_in, da_lq1, da_lk1, da_lq2, da_lk2, da_subln, mla_q_norm, mla_w_qb, mla_kv_norm, mla_w_kvb, swa_sink, w_gate, w_br_da, w_br_mla, w_br_swa, w_o, w_router, w_ff1, w_ff3, w_ff2, final_norm):
    c_all = jnp.concatenate([c_ctx[None], c, jnp.zeros((SUB - 1 - DEC_BATCH, D_MODEL), F32)], axis=0)
    mod_all = _ada_call(c_all, w_ada, b_ada)[:, :1 + DEC_BATCH].reshape(DEPTH, 1 + DEC_BATCH, 6, D_MODEL)
    lam_all = _lam_call(da_lq1, da_lk1, da_lq2, da_lk2)
    tab = _rope_tables()

    xp = x_prompt.reshape(N_CTX, D_MODEL)
    xs = x_sample.reshape(N_LAT, D_MODEL)
    moe = None
    new = [[] for _ in range(6)]
    for l in range(DEPTH):
        w = _layer_weights(l, w_in, w_gate, mla_w_qb, mla_w_kvb, w_br_da, w_br_mla, w_br_swa, w_o, w_router)
        mod = mod_all[l]
        mod_prev = mod_all[l - 1] if l else None
        n1, n2 = norm1[l][None], norm2[l][None]
        subln, qn, kvn = da_subln[l][None], mla_q_norm[l][None], mla_kv_norm[l][None]
        lam = lam_all[l, :1]
        sink = swa_sink[l]

        xp, proj_c, gate_c = _pre_call(xp, moe, mod_prev, mod, n1, w["win"], w["wg"], latent=False)
        xs, proj_l, gate_l = _pre_call(xs, moe, mod_prev, mod, n1, w["win"], w["wg"], latent=True)

        oda_c, omla_c, oswa_c, ckv_c = _ctx_attn_call(l, lam, sink, proj_c, subln, qn, w["wqb"], kvn, w["wkv"])
        pc = proj_c.reshape(BATCH, SEQ, PROJ_COLS)
        new[0].append(pc[..., C_DAK:C_DAK + 512].reshape(BATCH, SEQ, DA_HEADS, 2 * DA_DH))
        new[1].append(pc[..., C_DAV:C_DAV + 512].reshape(BATCH, SEQ, DA_HEADS, 2 * DA_DH))
        new[2].append(ckv_c.reshape(BATCH, SEQ, MLA_KV_RANK))
        new[3].append(pc[..., C_MKR:C_MKR + MLA_ROPE])
        new[4].append(pc[..., C_SK:C_SK + LANE].reshape(BATCH, SEQ, SWA_KV_HEADS, SWA_DH))
        new[5].append(pc[..., C_SV:C_SV + LANE].reshape(BATCH, SEQ, SWA_KV_HEADS, SWA_DH))

        caches = (cache_da_k[:, l].reshape(DEC_BATCH, PAST_LEN, 512),
                  cache_da_v[:, l].reshape(DEC_BATCH, PAST_LEN, 512),
                  cache_mla_ckv[:, l],
                  jnp.pad(cache_mla_krope[:, l], ((0, 0), (0, 0), (0, LANE - MLA_ROPE))),
                  cache_swa_k[:, l].reshape(DEC_BATCH, PAST_LEN, LANE),
                  cache_swa_v[:, l].reshape(DEC_BATCH, PAST_LEN, LANE))
        daq, mlaq, swaq, dak, dav, mlak, mlav, swak, swav = _lat_prep_call(
            proj_l, tab, caches, qn, w["wqb"], kvn, w["wkv"])
        oda_l = _lat_da_call(l, lam, daq, dak, dav, subln).reshape(N_LAT, 512)
        omla_l = _lat_mla_call(mlaq, mlak, mlav).reshape(N_LAT, 512)
        oswa_l = _lat_swa_call(sink, swaq, swak, swav).reshape(N_LAT, 512)

        post_w = (w["wda"], w["wmla"], w["wswa"], w["wo"], w["wrh"], w["wrl"])
        xp, h2_c, lg_c = _post_call(xp, oda_c, omla_c, oswa_c, gate_c, mod, n2, *post_w, latent=False)
        xs, h2_l, lg_l = _post_call(xs, oda_l, omla_l, oswa_l, gate_l, mod, n2, *post_w, latent=True)

        idx_c, g_c, _ = _router_call(lg_c, CAP_CTX)
        idx_l, g_l, cnt_l = _router_call(lg_l, CAP_LAT)
        idx = jnp.concatenate([idx_c, idx_l], axis=1)
        gsel = jnp.concatenate([g_c, g_l], axis=1)
        n0 = cnt_l[:, 0].astype(I32)
        bnd = jnp.stack([jnp.zeros_like(n0), jnp.full_like(n0, CAP_CTX), CAP_CTX + n0,
                         jnp.full_like(n0, SLOTS)], axis=1)
        idx3 = idx.reshape(N_EXPERTS, 1, SLOTS)
        y_c, y_l = _ffn_call(l, idx3, h2_c.reshape(N_CTX, SUB, LANE), h2_l.reshape(N_LAT, SUB, LANE),
                             gsel.reshape(N_EXPERTS, SLOTS, 1), w_ff1, w_ff3, w_ff2)
        acc = _combine_call(idx3, bnd.reshape(N_EXPERTS, 1, N_HALVES + 1), y_c, y_l)
        moe = acc.reshape(N_HALVES * TOK_HALF * D_CHUNKS, LANE)

    mod_last = mod_all[DEPTH - 1]
    fn = final_norm[None]
    y_prompt = _final_call(xp, moe, mod_last, fn, latent=False).reshape(BATCH, SEQ, D_MODEL)
    y_sample = _final_call(xs, moe, mod_last, fn, latent=True).reshape(DEC_BATCH, DEC_SEQ, D_MODEL)
    return (y_prompt, y_sample) + tuple(jnp.stack(n, axis=1) for n in new)
```

```python
import functools
import math

import jax
import jax.numpy as jnp
from jax import lax
from jax.experimental import pallas as pl
from jax.experimental.pallas import tpu as pltpu

F32 = jnp.float32
BF = jnp.bfloat16
I32 = jnp.int32

D_MODEL = 1024
BATCH = 16
SEQ = 256
DEPTH = 2
DEC_BATCH = 2
DEC_SEQ = 4096
PAST_LEN = 512
GRID_W = 64
ROPE_BASE = 10000.0
EPS = 1e-6
DA_HEADS = 4
DA_DH = 64
DA_OUT = DA_HEADS * 2 * DA_DH
MLA_HEADS = 8
MLA_Q_RANK = 256
MLA_KV_RANK = 128
MLA_NOPE = 64
MLA_ROPE = 32
MLA_V = 64
MLA_OUT = MLA_HEADS * MLA_V
SWA_HEADS = 8
SWA_KV_HEADS = 2
SWA_DH = 64
W_BLOCK = 128
SWA_OUT = SWA_HEADS * SWA_DH
N_EXPERTS = 16
EXPERT_FF = 1024
CAPACITY_FACTOR = 2

LANE = 128
SUB = 8
HALF = 64
N_CTX = BATCH * SEQ
N_LAT = DEC_BATCH * DEC_SEQ
KEYS = PAST_LEN + DEC_SEQ
CAP_CTX = CAPACITY_FACTOR * N_CTX // N_EXPERTS
CAP_LAT = CAPACITY_FACTOR * N_LAT // N_EXPERTS
SLOTS = CAP_CTX + CAP_LAT
TOK_HALF = 4096
N_HALVES = (N_CTX + N_LAT) // TOK_HALF
D_CHUNKS = D_MODEL // LANE

C_DAQ, C_DAK, C_DAV = 0, 512, 1024
C_MQ, C_MKV, C_MKR = 1536, 1792, 1920
C_SQ, C_SK, C_SV = 2048, 2560, 2688
PROJ_COLS = 2816
GATE_COLS = 3 * D_MODEL
MLA_SCALE = (MLA_NOPE + MLA_ROPE) ** -0.5
HEAD_SCALE = DA_DH ** -0.5
LOG2E = math.log2(math.e)

TM_PRE = 512
TM_POST = 512
CHUNK = 512
TQ = 256
DA_STEP_HEADS = 4
MLA_STEP_PAIRS = 4
ROUTE_TILE = 128
ROUTE_CHUNK = 512
THRESHOLD_STEPS = 48
FFN_ROWS = 512
FF_SPLIT = 2
COMBINE_GROUP = 8


def _cp(sem, vmem_mb):
    return pltpu.CompilerParams(dimension_semantics=sem, vmem_limit_bytes=vmem_mb * 1024 * 1024)


def _dot(a, b):
    return jnp.dot(a, b, preferred_element_type=F32)


def _dot_nt(a, b):
    return lax.dot_general(a, b, (((1,), (1,)), ((), ())), preferred_element_type=F32)


def _split(a):
    hi = a.astype(BF)
    lo = (a - hi.astype(F32)).astype(BF)
    return hi, lo


def _dot3(a, w):
    ah, al = _split(a)
    wh, wl = _split(w)
    return _dot(ah, wh) + _dot(ah, wl) + _dot(al, wh)


def _rms(x, g):
    return x * lax.rsqrt(jnp.mean(x * x, axis=-1, keepdims=True) + EPS) * g


def _lane(shape):
    return lax.broadcasted_iota(I32, shape, len(shape) - 1)


def _softmax_pv(segs, sink=None):
    m = None
    for s, _ in segs:
        ms = jnp.max(s, axis=-1, keepdims=True)
        m = ms if m is None else jnp.maximum(m, ms)
    if sink is not None:
        m = jnp.maximum(m, sink)
    l = None
    o = None
    for s, v in segs:
        e = jnp.exp(s - m)
        ls = jnp.sum(e, axis=-1, keepdims=True)
        os_ = _dot(e.astype(BF), v)
        l = ls if l is None else l + ls
        o = os_ if o is None else o + os_
    if sink is not None:
        l = l + jnp.exp(sink - m)
    return o / l


def _mask_halves(q):
    lo = _lane(q.shape) < HALF
    return jnp.where(lo, q, 0.0).astype(BF), jnp.where(lo, 0.0, q).astype(BF)


def _pair(o_a, o_b):
    return jnp.where(_lane(o_a.shape) < HALF, o_a, o_b)


def _dup(x):
    r = pltpu.roll(x, HALF, 1)
    lo = _lane(x.shape) < HALF
    return jnp.where(lo, x, r), jnp.where(lo, r, x)


def _rope(x, c, s):
    n = x.shape[-1]
    even = (_lane(x.shape) % 2) == 0
    sw = jnp.where(even, pltpu.roll(x, n - 1, 1), pltpu.roll(x, 1, 1))
    return x * c + sw * s


def _da_head(q, k, v, lam, subln, lam_init):
    q1, q2 = _mask_halves(q)
    o1 = _softmax_pv([(_dot_nt(q1, k), v)])
    o2 = _softmax_pv([(_dot_nt(q2, k), v)])
    return _rms(o1 - lam * o2, subln) * (1.0 - lam_init)


def _ada_kernel(c_ref, w_ref, b_ref, o_ref):
    c = c_ref[...]
    a = c * jax.nn.sigmoid(c)
    o_ref[...] = _dot3(a, w_ref[...]) + b_ref[...]


def _ada_call(c_all, w_ada, b_ada):
    tn = 1536
    return pl.pallas_call(
        _ada_kernel,
        grid=(DEPTH, 6 * D_MODEL // tn),
        in_specs=[pl.BlockSpec((SUB, D_MODEL), lambda l, j: (0, 0)),
                  pl.BlockSpec((None, D_MODEL, tn), lambda l, j: (l, 0, j)),
                  pl.BlockSpec((None, 1, tn), lambda l, j: (l, 0, j))],
        out_specs=pl.BlockSpec((None, SUB, tn), lambda l, j: (l, 0, j)),
        out_shape=jax.ShapeDtypeStruct((DEPTH, SUB, 6 * D_MODEL), F32),
        compiler_params=_cp(("parallel", "parallel"), 40),
        name="ada",
    )(c_all, w_ada, b_ada.reshape(DEPTH, 1, 6 * D_MODEL))


def _lam_kernel(q1, k1, q2, k2, o_ref):
    s1 = jnp.sum(q1[...] * k1[...], axis=-1, keepdims=True)
    s2 = jnp.sum(q2[...] * k2[...], axis=-1, keepdims=True)
    row = lax.broadcasted_iota(I32, (DEPTH, 1), 0)
    init = jnp.zeros((DEPTH, 1), F32)
    for l in range(DEPTH):
        init = jnp.where(row == l, _lam_init(l), init)
    o_ref[...] = jnp.broadcast_to(jnp.exp(s1) - jnp.exp(s2) + init, o_ref.shape)


def _lam_init(layer):
    return 0.8 - 0.6 * math.exp(-0.3 * layer)


def _lam_call(q1, k1, q2, k2):
    return pl.pallas_call(
        _lam_kernel,
        out_shape=jax.ShapeDtypeStruct((DEPTH, LANE), F32),
        name="lam",
    )(q1, k1, q2, k2)


def _moe_rows(moe_ref, tm):
    return jnp.concatenate([moe_ref[pl.ds(k, tm, stride=D_CHUNKS), :] for k in range(D_CHUNKS)], axis=1)


def _pre_body(x, mod_ref, n1_ref, win_ref, wg_ref, proj_ref, gate_ref):
    m = mod_ref[...]
    h = _rms(x, n1_ref[...]) * (1.0 + m[1:2]) + m[0:1]
    hb = h.astype(BF)
    proj_ref[...] = _dot(hb, win_ref[...])
    gate_ref[...] = jax.nn.sigmoid(_dot(hb, wg_ref[...])).astype(BF)


def _pre_first_kernel(x_ref, mod_ref, n1_ref, win_ref, wg_ref, proj_ref, gate_ref):
    _pre_body(x_ref[...], mod_ref, n1_ref, win_ref, wg_ref, proj_ref, gate_ref)


def _pre_next_kernel(x_ref, moe_ref, modp_ref, mod_ref, n1_ref, win_ref, wg_ref, xo_ref, proj_ref, gate_ref):
    x = x_ref[...] + modp_ref[...][5:6] * _moe_rows(moe_ref, TM_PRE)
    xo_ref[...] = x
    _pre_body(x, mod_ref, n1_ref, win_ref, wg_ref, proj_ref, gate_ref)


def _mod_spec(layer, tm, latent):
    per = DEC_SEQ // tm
    if latent:
        return pl.BlockSpec((None, None, 6, D_MODEL), lambda i: (layer, 1 + i // per, 0, 0))
    return pl.BlockSpec((None, None, 6, D_MODEL), lambda i: (layer, 0, 0, 0))


def _layer(layer, shape):
    nd = len(shape)
    return pl.BlockSpec((None,) + shape, lambda *_: (layer,) + (0,) * nd, pipeline_mode=pl.Buffered(1))


def _pre_call(layer, x, moe, mod, n1, win, wg, latent):
    n = x.shape[0]
    tm = TM_PRE
    row = pl.BlockSpec((tm, D_MODEL), lambda i: (i, 0))
    w_specs = [_mod_spec(layer, tm, latent), _layer(layer, (1, D_MODEL)), _layer(layer, (D_MODEL, PROJ_COLS)),
               _layer(layer, (D_MODEL, GATE_COLS))]
    outs = [jax.ShapeDtypeStruct((n, PROJ_COLS), F32), jax.ShapeDtypeStruct((n, GATE_COLS), BF)]
    out_specs = [pl.BlockSpec((tm, PROJ_COLS), lambda i: (i, 0)), pl.BlockSpec((tm, GATE_COLS), lambda i: (i, 0))]
    if moe is None:
        proj, gate = pl.pallas_call(
            _pre_first_kernel, grid=(n // tm,), in_specs=[row] + w_specs, out_specs=out_specs, out_shape=outs,
            compiler_params=_cp(("parallel",), 52), name="pre_first",
        )(x, mod, n1, win, wg)
        return x, proj, gate
    moe_off = (N_CTX // tm) if latent else 0
    moe_spec = pl.BlockSpec((tm * D_CHUNKS, LANE), lambda i: (i + moe_off, 0))
    xo, proj, gate = pl.pallas_call(
        _pre_next_kernel, grid=(n // tm,),
        in_specs=[row, moe_spec, _mod_spec(layer - 1, tm, latent)] + w_specs,
        out_specs=[row] + out_specs,
        out_shape=[jax.ShapeDtypeStruct((n, D_MODEL), F32)] + outs,
        compiler_params=_cp(("parallel",), 52), name="pre_next",
    )(x, moe, mod, mod, n1, win, wg)
    return xo, proj, gate


def _ctx_attn_kernel(lam_init, lam_ref, sink_ref, p_ref, subln_ref, qn_ref, wqb_ref, kvn_ref, wkv_ref,
                     oda_ref, omla_ref, oswa_ref, ckv_ref):
    p = p_ref[...]
    lam = lam_ref[0]
    for h in range(DA_HEADS):
        blk = slice(h * LANE, (h + 1) * LANE)
        q = p[:, C_DAQ + h * LANE:C_DAQ + (h + 1) * LANE] * HEAD_SCALE
        k = p[:, C_DAK + h * LANE:C_DAK + (h + 1) * LANE].astype(BF)
        v = p[:, C_DAV + h * LANE:C_DAV + (h + 1) * LANE].astype(BF)
        oda_ref[:, blk] = _da_head(q, k, v, lam, subln_ref[...], lam_init).astype(BF)
    qn = _rms(p[:, C_MQ:C_MQ + MLA_Q_RANK], qn_ref[...]).astype(BF)
    cq = _dot(qn, wqb_ref[...])
    ckv = _rms(p[:, C_MKV:C_MKV + MLA_KV_RANK], kvn_ref[...])
    ckv_ref[...] = ckv
    kv = _dot(ckv.astype(BF), wkv_ref[...])
    kr_sh = pltpu.roll(p[:, C_MKR:C_MKR + LANE], HALF, 1)
    for m in range(MLA_HEADS // 2):
        vpair = kv[:, MLA_HEADS * LANE + m * LANE:MLA_HEADS * LANE + (m + 1) * LANE].astype(BF)
        outs = []
        for h in (2 * m, 2 * m + 1):
            q = (cq[:, h * LANE:(h + 1) * LANE] * MLA_SCALE).astype(BF)
            k = (kv[:, h * LANE:(h + 1) * LANE] + kr_sh).astype(BF)
            outs.append(_softmax_pv([(_dot_nt(q, k), vpair)]))
        omla_ref[:, m * LANE:(m + 1) * LANE] = _pair(outs[0], outs[1]).astype(BF)
    kd = _dup(p[:, C_SK:C_SK + LANE])
    vd = _dup(p[:, C_SV:C_SV + LANE])
    for m in range(SWA_HEADS // 2):
        g = (2 * m) // (SWA_HEADS // SWA_KV_HEADS)
        k = kd[g].astype(BF)
        v = vd[g].astype(BF)
        qa, qb = _mask_halves(p[:, C_SQ + m * LANE:C_SQ + (m + 1) * LANE] * HEAD_SCALE)
        oa = _softmax_pv([(_dot_nt(qa, k), v)], sink_ref[2 * m])
        ob = _softmax_pv([(_dot_nt(qb, k), v)], sink_ref[2 * m + 1])
        oswa_ref[:, m * LANE:(m + 1) * LANE] = _pair(oa, ob).astype(BF)


def _smem():
    return pl.BlockSpec(memory_space=pltpu.SMEM)


def _ctx_attn_call(layer, lam, sink, proj, subln, qn, wqb, kvn, wkv):
    row512 = pl.BlockSpec((SEQ, 512), lambda b: (b, 0))
    return pl.pallas_call(
        functools.partial(_ctx_attn_kernel, _lam_init(layer)),
        grid=(BATCH,),
        in_specs=[_smem(), _smem(), pl.BlockSpec((SEQ, PROJ_COLS), lambda b: (b, 0)),
                  _layer(layer, (1, LANE)), _layer(layer, (1, MLA_Q_RANK)),
                  _layer(layer, (MLA_Q_RANK, MLA_HEADS * LANE)), _layer(layer, (1, MLA_KV_RANK)),
                  _layer(layer, (MLA_KV_RANK, MLA_HEADS * LANE + MLA_OUT))],
        out_specs=[row512, row512, row512, pl.BlockSpec((SEQ, MLA_KV_RANK), lambda b: (b, 0))],
        out_shape=[jax.ShapeDtypeStruct((N_CTX, 512), BF)] * 3 + [jax.ShapeDtypeStruct((N_CTX, MLA_KV_RANK), F32)],
        compiler_params=_cp(("parallel",), 40), name="ctx_attn",
    )(lam, sink, proj, subln, qn, wqb, kvn, wkv)


def _lat_prep_kernel(p_ref, tab_ref, cdk_ref, cdv_ref, cckv_ref, ckr_ref, csk_ref, csv_ref,
                     qn_ref, wqb_ref, kvn_ref, wkv_ref,
                     daq_ref, mlaq_ref, swaq_ref, dak_ref, dav_ref, mlak_ref, mlav_ref, swak_ref, swav_ref):
    j = pl.program_id(1)

    def write_kv(dk, dv, ckv, kr_sh, sk, sv):
        ones_col = jnp.where(_lane((CHUNK, LANE)) == 0, 1.0, 0.0).astype(BF)
        dak_ref[...] = dk.astype(BF)
        for h in range(DA_HEADS):
            dav_ref[:, 2 * h * LANE:(2 * h + 1) * LANE] = dv[:, h * LANE:(h + 1) * LANE].astype(BF)
            dav_ref[:, (2 * h + 1) * LANE:(2 * h + 2) * LANE] = ones_col
        kv = _dot(ckv.astype(BF), wkv_ref[...])
        for h in range(MLA_HEADS):
            mlak_ref[:, h * LANE:(h + 1) * LANE] = (kv[:, h * LANE:(h + 1) * LANE] + kr_sh).astype(BF)
        for p in range(MLA_HEADS // 2):
            mlav_ref[:, 2 * p * LANE:(2 * p + 1) * LANE] = kv[:, (MLA_HEADS + p) * LANE:(MLA_HEADS + p + 1) * LANE].astype(BF)
            mlav_ref[:, (2 * p + 1) * LANE:(2 * p + 2) * LANE] = ones_col
        k0, k1 = _dup(sk)
        v0, v1 = _dup(sv)
        swak_ref[:, 0:LANE] = k0.astype(BF)
        swak_ref[:, LANE:2 * LANE] = k1.astype(BF)
        swav_ref[:, 0:LANE] = v0.astype(BF)
        swav_ref[:, LANE:2 * LANE] = v1.astype(BF)

    @pl.when(j == 0)
    def _():
        write_kv(cdk_ref[...], cdv_ref[...], cckv_ref[...], pltpu.roll(ckr_ref[...], HALF, 1),
                 csk_ref[...], csv_ref[...])

    @pl.when(j > 0)
    def _():
        c64, s64 = tab_ref[0], tab_ref[1]
        cmq, smq = tab_ref[2], tab_ref[3]
        ckr, skr = tab_ref[4], tab_ref[5]

        def blk(c0, h):
            return p_ref[:, c0 + h * LANE:c0 + (h + 1) * LANE]

        for h in range(DA_HEADS):
            daq_ref[:, h * LANE:(h + 1) * LANE] = (_rope(blk(C_DAQ, h), c64, s64) * (HEAD_SCALE * LOG2E)).astype(BF)
            swaq_ref[:, h * LANE:(h + 1) * LANE] = (_rope(blk(C_SQ, h), c64, s64) * HEAD_SCALE).astype(BF)
        dk = jnp.concatenate([_rope(blk(C_DAK, h), c64, s64) for h in range(DA_HEADS)], axis=1)
        qn = _rms(p_ref[:, C_MQ:C_MQ + MLA_Q_RANK], qn_ref[...]).astype(BF)
        cq = _dot(qn, wqb_ref[...])
        for h in range(MLA_HEADS):
            mlaq_ref[:, h * LANE:(h + 1) * LANE] = (
                _rope(cq[:, h * LANE:(h + 1) * LANE], cmq, smq) * (MLA_SCALE * LOG2E)).astype(BF)
        ckv = _rms(p_ref[:, C_MKV:C_MKV + MLA_KV_RANK], kvn_ref[...])
        kr_sh = pltpu.roll(_rope(p_ref[:, C_MKR:C_MKR + LANE], ckr, skr), HALF, 1)
        sk = _rope(p_ref[:, C_SK:C_SK + LANE], c64, s64)
        write_kv(dk, p_ref[:, C_DAV:C_DAV + 512], ckv, kr_sh, sk, p_ref[:, C_SV:C_SV + LANE])


def _lat_prep_call(layer, proj, tab, caches, qn, wqb, kvn, wkv):
    cdk, cdv, cckv, ckr, csk, csv = caches
    nj = 1 + DEC_SEQ // CHUNK

    def own(width):
        return pl.BlockSpec((None, CHUNK, width), lambda b, j: (b, jnp.maximum(j - 1, 0), 0))

    def cache(width):
        return pl.BlockSpec((None, None, PAST_LEN, width), lambda b, j: (b, layer, 0, 0))

    def allk(width):
        return pl.BlockSpec((None, CHUNK, width), lambda b, j: (b, j, 0))

    def shp(rows, width):
        return jax.ShapeDtypeStruct((DEC_BATCH, rows, width), BF)

    return pl.pallas_call(
        _lat_prep_kernel,
        grid=(DEC_BATCH, nj),
        in_specs=[own(PROJ_COLS),
                  pl.BlockSpec((6, CHUNK, LANE), lambda b, j: (0, jnp.maximum(j - 1, 0), 0)),
                  cache(512), cache(512), cache(LANE), cache(LANE), cache(LANE), cache(LANE),
                  _layer(layer, (1, MLA_Q_RANK)), _layer(layer, (MLA_Q_RANK, MLA_HEADS * LANE)),
                  _layer(layer, (1, MLA_KV_RANK)), _layer(layer, (MLA_KV_RANK, MLA_HEADS * LANE + MLA_OUT))],
        out_specs=[own(512), own(MLA_HEADS * LANE), own(512),
                   allk(512), allk(1024), allk(MLA_HEADS * LANE), allk(1024), allk(2 * LANE), allk(2 * LANE)],
        out_shape=[shp(DEC_SEQ, 512), shp(DEC_SEQ, MLA_HEADS * LANE), shp(DEC_SEQ, 512),
                   shp(KEYS, 512), shp(KEYS, 1024), shp(KEYS, MLA_HEADS * LANE), shp(KEYS, 1024),
                   shp(KEYS, 2 * LANE), shp(KEYS, 2 * LANE)],
        compiler_params=_cp(("parallel", "arbitrary"), 48), name="lat_prep",
    )(proj.reshape(DEC_BATCH, DEC_SEQ, PROJ_COLS), tab, cdk, cdv, cckv, ckr, csk, csv, qn, wqb, kvn, wkv)


def _softmax2_pv(s, v_aug):
    e = jnp.exp2(s - jnp.max(s, axis=-1, keepdims=True)).astype(BF)
    o = _dot(e, v_aug)
    return o[:, 0:LANE] / o[:, LANE:LANE + 1]


def _lat_da_kernel(lam_init, lam_ref, q_ref, k_ref, v_ref, subln_ref, o_ref):
    for h in range(DA_STEP_HEADS):
        cols = slice(h * LANE, (h + 1) * LANE)
        q1, q2 = _mask_halves(q_ref[:, cols].astype(F32))
        k = k_ref[:, cols]
        v = v_ref[:, 2 * h * LANE:(2 * h + 2) * LANE]
        o = _softmax2_pv(_dot_nt(q1, k), v) - lam_ref[0] * _softmax2_pv(_dot_nt(q2, k), v)
        o_ref[:, cols] = (_rms(o, subln_ref[...]) * (1.0 - lam_init)).astype(BF)


def _lat_da_call(layer, lam, q, k, v, subln):
    return pl.pallas_call(
        functools.partial(_lat_da_kernel, _lam_init(layer)),
        grid=(DEC_BATCH, DA_HEADS // DA_STEP_HEADS, DEC_SEQ // TQ),
        in_specs=[_smem(),
                  pl.BlockSpec((None, TQ, DA_STEP_HEADS * LANE), lambda b, h, i: (b, i, h)),
                  pl.BlockSpec((None, KEYS, DA_STEP_HEADS * LANE), lambda b, h, i: (b, 0, h)),
                  pl.BlockSpec((None, KEYS, 2 * DA_STEP_HEADS * LANE), lambda b, h, i: (b, 0, h)),
                  _layer(layer, (1, LANE))],
        out_specs=pl.BlockSpec((None, TQ, DA_STEP_HEADS * LANE), lambda b, h, i: (b, i, h)),
        out_shape=jax.ShapeDtypeStruct((DEC_BATCH, DEC_SEQ, 512), BF),
        compiler_params=_cp(("parallel", "parallel", "arbitrary"), 56), name="lat_da",
    )(lam, q, k, v, subln)


def _lat_mla_kernel(q_ref, k_ref, v_ref, o_ref):
    for p in range(MLA_STEP_PAIRS):
        v = v_ref[:, 2 * p * LANE:(2 * p + 2) * LANE]
        outs = []
        for h in (2 * p, 2 * p + 1):
            cols = slice(h * LANE, (h + 1) * LANE)
            outs.append(_softmax2_pv(_dot_nt(q_ref[:, cols], k_ref[:, cols]), v))
        o_ref[:, p * LANE:(p + 1) * LANE] = _pair(outs[0], outs[1]).astype(BF)


def _lat_mla_call(q, k, v):
    sp = MLA_STEP_PAIRS
    return pl.pallas_call(
        _lat_mla_kernel,
        grid=(DEC_BATCH, MLA_HEADS // (2 * sp), DEC_SEQ // TQ),
        in_specs=[pl.BlockSpec((None, TQ, 2 * sp * LANE), lambda b, m, i: (b, i, m)),
                  pl.BlockSpec((None, KEYS, 2 * sp * LANE), lambda b, m, i: (b, 0, m)),
                  pl.BlockSpec((None, KEYS, 2 * sp * LANE), lambda b, m, i: (b, 0, m))],
        out_specs=pl.BlockSpec((None, TQ, sp * LANE), lambda b, m, i: (b, i, m)),
        out_shape=jax.ShapeDtypeStruct((DEC_BATCH, DEC_SEQ, 512), BF),
        compiler_params=_cp(("parallel", "parallel", "arbitrary"), 56), name="lat_mla",
    )(q, k, v)


def _lat_swa_kernel(sink_ref, q_ref, kc_ref, kp_ref, k0_ref, kn_ref, vc_ref, vp_ref, v0_ref, vn_ref, o_ref):
    n = pl.program_id(1)
    nb = pl.num_programs(1)
    group = SWA_HEADS // SWA_KV_HEADS
    rows = group * W_BLOCK
    n_keys = PAST_LEN + 3 * W_BLOCK
    qi = lax.broadcasted_iota(I32, (rows, n_keys), 0) & (W_BLOCK - 1)
    loc = lax.broadcasted_iota(I32, (rows, n_keys), 1) - PAST_LEN
    ok = ((loc < 0) | ((loc >= W_BLOCK) & (loc < 2 * W_BLOCK))
          | ((loc >= 0) & (loc < W_BLOCK) & (loc >= qi) & (n > 0))
          | ((loc >= 2 * W_BLOCK) & (loc - 2 * W_BLOCK <= qi) & (n < nb - 1)))
    head_of_row = lax.broadcasted_iota(I32, (rows, 1), 0) // W_BLOCK
    for g in range(SWA_KV_HEADS):
        gs = slice(g * LANE, (g + 1) * LANE)
        q_parts = []
        sink = jnp.zeros((rows, 1), F32)
        for j in range(group):
            sink = jnp.where(head_of_row == j, sink_ref[g * group + j], sink)
        for p in range(group // 2):
            blk = g * (group // 2) + p
            q_parts.extend(_mask_halves(q_ref[:, blk * LANE:(blk + 1) * LANE].astype(F32)))
        q = jnp.concatenate(q_parts, axis=0)
        k = jnp.concatenate([kc_ref[:, gs], kp_ref[:, gs], k0_ref[:, gs], kn_ref[:, gs]], axis=0)
        v = jnp.concatenate([vc_ref[:, gs], vp_ref[:, gs], v0_ref[:, gs], vn_ref[:, gs]], axis=0)
        o = _softmax_pv([(jnp.where(ok, _dot_nt(q, k), -jnp.inf), v)], sink)
        for p in range(group // 2):
            blk = g * (group // 2) + p
            o_a = o[(2 * p) * W_BLOCK:(2 * p + 1) * W_BLOCK]
            o_b = o[(2 * p + 1) * W_BLOCK:(2 * p + 2) * W_BLOCK]
            o_ref[:, blk * LANE:(blk + 1) * LANE] = _pair(o_a, o_b).astype(BF)


def _lat_swa_call(sink, q, k, v):
    nb = DEC_SEQ // W_BLOCK
    cb = PAST_LEN // W_BLOCK

    def ctx():
        return pl.BlockSpec((None, PAST_LEN, 2 * LANE), lambda b, n: (b, 0, 0))

    def loc(d):
        return pl.BlockSpec((None, W_BLOCK, 2 * LANE),
                            lambda b, n: (b, cb + jnp.clip(n + d, 0, nb - 1), 0))

    return pl.pallas_call(
        _lat_swa_kernel,
        grid=(DEC_BATCH, nb),
        in_specs=[_smem(), pl.BlockSpec((None, W_BLOCK, 512), lambda b, n: (b, n, 0)),
                  ctx(), loc(-1), loc(0), loc(1), ctx(), loc(-1), loc(0), loc(1)],
        out_specs=pl.BlockSpec((None, W_BLOCK, 512), lambda b, n: (b, n, 0)),
        out_shape=jax.ShapeDtypeStruct((DEC_BATCH, DEC_SEQ, 512), BF),
        compiler_params=_cp(("parallel", "arbitrary"), 40), name="lat_swa",
    )(sink, q, k, k, k, k, v, v, v, v)


def _rows_to_tiles(o_ref, val, tm):
    for k in range(D_CHUNKS):
        o_ref[pl.ds(k, tm, stride=D_CHUNKS), :] = val[:, k * LANE:(k + 1) * LANE]


def _post_kernel(x_ref, oda_ref, omla_ref, oswa_ref, gate_ref, mod_ref, n2_ref, wda_ref, wmla_ref, wswa_ref,
                 wo_ref, wrh_ref, wrl_ref, xo_ref, h2_ref, lg_ref):
    m = mod_ref[...]
    g = gate_ref[...].astype(F32)
    merged = (g[:, 0:D_MODEL] * _dot(oda_ref[...], wda_ref[...])
              + g[:, D_MODEL:2 * D_MODEL] * _dot(omla_ref[...], wmla_ref[...])
              + g[:, 2 * D_MODEL:] * _dot(oswa_ref[...], wswa_ref[...]))
    x = x_ref[...] + m[2:3] * _dot(merged.astype(BF), wo_ref[...])
    xo_ref[...] = x
    h2 = _rms(x, n2_ref[...]) * (1.0 + m[4:5]) + m[3:4]
    _rows_to_tiles(h2_ref, h2, TM_POST)
    hh, hl = _split(h2)
    lg_ref[...] = _dot_nt(wrh_ref[...], hh) + _dot_nt(wrl_ref[...], hh) + _dot_nt(wrh_ref[...], hl)


def _post_call(layer, x, oda, omla, oswa, gate, mod, n2, wda, wmla, wswa, wo, wrh, wrl, latent):
    n = x.shape[0]
    tm = TM_POST
    row = pl.BlockSpec((tm, D_MODEL), lambda i: (i, 0))
    o512 = pl.BlockSpec((tm, 512), lambda i: (i, 0))
    return pl.pallas_call(
        _post_kernel, grid=(n // tm,),
        in_specs=[row, o512, o512, o512, pl.BlockSpec((tm, GATE_COLS), lambda i: (i, 0)),
                  _mod_spec(layer, tm, latent), _layer(layer, (1, D_MODEL)),
                  _layer(layer, (512, D_MODEL)), _layer(layer, (512, D_MODEL)), _layer(layer, (512, D_MODEL)),
                  _layer(layer, (D_MODEL, D_MODEL)), _layer(layer, (N_EXPERTS, D_MODEL)),
                  _layer(layer, (N_EXPERTS, D_MODEL))],
        out_specs=[row, pl.BlockSpec((tm * D_CHUNKS, LANE), lambda i: (i, 0)),
                   pl.BlockSpec((N_EXPERTS, tm), lambda i: (0, i))],
        out_shape=[jax.ShapeDtypeStruct((n, D_MODEL), F32),
                   jax.ShapeDtypeStruct((n * D_CHUNKS, LANE), F32),
                   jax.ShapeDtypeStruct((N_EXPERTS, n), F32)],
        compiler_params=_cp(("parallel",), 48), name="post",
    )(x, oda, omla, oswa, gate, mod, n2, wda, wmla, wswa, wo, wrh, wrl)


def _router_kernel(cap, lg_ref, q_ref, a_ref, lo_ref, hi_ref, cnt_ref):
    e_n, n = lg_ref.shape
    lg = lg_ref[...]
    shifted = lg - jnp.max(lg, axis=0, keepdims=True)
    ex = jnp.exp(shifted)
    den = jnp.sum(ex, axis=0, keepdims=True)
    aff = ex / den
    score = shifted - jnp.log(den)
    capf = float(cap)

    def count(mask):
        return jnp.sum(jnp.where(mask, 1.0, 0.0), axis=1, keepdims=True)

    def halve(_, bounds):
        lo, hi = bounds
        mid = 0.5 * (lo + hi)
        enough = count(score >= mid) >= capf
        return jnp.where(enough, mid, lo), jnp.where(enough, hi, mid)

    lo, hi = lax.fori_loop(0, THRESHOLD_STEPS, halve,
                           (jnp.min(score, axis=1, keepdims=True), jnp.ones((e_n, 1), F32)))
    gt = score >= hi
    eq = (score >= lo) & (score < hi)
    need = capf - count(gt)
    tok = lax.broadcasted_iota(I32, (e_n, n), 1)
    n_bits = n.bit_length()

    def tie_bit(i, bound):
        cand = bound | jnp.left_shift(jnp.int32(1), n_bits - 1 - i)
        ok = (cand <= n) & (count(eq & (tok < cand)) <= need)
        return jnp.where(ok, cand, bound)

    bound = lax.fori_loop(0, n_bits, tie_bit, jnp.zeros((e_n, 1), I32))
    sel = gt | (eq & (tok < bound))
    cnt_ref[...] = jnp.broadcast_to(count(sel & (tok < TOK_HALF)), cnt_ref.shape)
    blk = 2 * LANE
    per_chunk = ROUTE_CHUNK // blk
    n_chunks = n // ROUTE_CHUNK
    upper = (lax.broadcasted_iota(I32, (blk, blk), 0) <= lax.broadcasted_iota(I32, (blk, blk), 1))
    upper = jnp.where(upper, 1.0, 0.0).astype(BF)
    lane = _lane((e_n, LANE))
    carry = jnp.zeros((e_n, 1), F32)
    c_in = jnp.zeros((e_n, LANE), F32)
    c_ex = jnp.zeros((e_n, LANE), F32)
    for c in range(n // blk):
        chunk, part = divmod(c, per_chunk)
        if part == 0:
            c_ex = jnp.where(lane == chunk, carry, c_ex)
            a_ref[chunk] = aff[:, chunk * ROUTE_CHUNK:(chunk + 1) * ROUTE_CHUNK]
        s_blk = sel[:, c * blk:(c + 1) * blk]
        rank = _dot(jnp.where(s_blk, 1.0, 0.0).astype(BF), upper) + carry
        q_ref[chunk, :, part * blk:(part + 1) * blk] = jnp.where(s_blk, rank, 0.0)
        carry = rank[:, blk - 1:blk]
        if part == per_chunk - 1:
            c_in = jnp.where(lane == chunk, carry, c_in)
    valid = lane < n_chunks
    lo = jnp.zeros((e_n, LANE), I32)
    hi = jnp.zeros((e_n, LANE), I32)
    for rt in range(cap // ROUTE_TILE):
        lo_rt = count(valid & (c_in < float(rt * ROUTE_TILE + 1)))
        hi_rt = count(valid & (c_ex < float((rt + 1) * ROUTE_TILE)))
        lo = jnp.where(lane == rt, lo_rt.astype(I32), lo)
        hi = jnp.where(lane == rt, hi_rt.astype(I32), hi)
    lo_ref[...] = lo
    hi_ref[...] = hi


def _compact_kernel(lo_ref, hi_ref, q_ref, a_ref, idx_ref, g_ref):
    e = pl.program_id(0)
    n_tiles = idx_ref.shape[0]
    eye = lax.broadcasted_iota(I32, (ROUTE_TILE, LANE), 0) == lax.broadcasted_iota(I32, (ROUTE_TILE, LANE), 1)

    def per_tile(rt, _):
        slot = (rt * ROUTE_TILE + 1 + lax.broadcasted_iota(I32, (ROUTE_TILE, 1), 0)).astype(F32)

        def per_chunk(c, acc):
            hit = q_ref[c, pl.ds(e, 1), :] == slot
            tok = (c * ROUTE_CHUNK + lax.broadcasted_iota(I32, (1, ROUTE_CHUNK), 1)).astype(F32)
            return (acc[0] + jnp.sum(jnp.where(hit, tok, 0.0), axis=1, keepdims=True),
                    acc[1] + jnp.sum(jnp.where(hit, a_ref[c, pl.ds(e, 1), :], 0.0), axis=1, keepdims=True))

        zero = jnp.zeros((ROUTE_TILE, 1), F32)
        idx_v, g_v = lax.fori_loop(lo_ref[e, rt], hi_ref[e, rt], per_chunk, (zero, zero))
        idx_ref[pl.ds(rt, 1), :] = jnp.sum(jnp.where(eye, idx_v, 0.0), axis=0, keepdims=True).astype(I32)
        g_ref[pl.ds(rt, 1), :] = jnp.sum(jnp.where(eye, g_v, 0.0), axis=0, keepdims=True)
        return 0

    lax.fori_loop(0, n_tiles, per_tile, 0)


def _router_call(logits_t, cap):
    n = logits_t.shape[1]
    n_chunks = n // ROUTE_CHUNK
    n_tiles = cap // ROUTE_TILE
    chunked = jax.ShapeDtypeStruct((n_chunks, N_EXPERTS, ROUTE_CHUNK), F32)
    q, a, lo, hi, cnt = pl.pallas_call(
        functools.partial(_router_kernel, cap),
        out_shape=[chunked, chunked, jax.ShapeDtypeStruct((N_EXPERTS, LANE), I32),
                   jax.ShapeDtypeStruct((N_EXPERTS, LANE), I32), jax.ShapeDtypeStruct((N_EXPERTS, LANE), F32)],
        compiler_params=pltpu.CompilerParams(vmem_limit_bytes=40 * 1024 * 1024), name="router",
    )(logits_t)
    whole = pl.BlockSpec((n_chunks, N_EXPERTS, ROUTE_CHUNK), lambda e, *_: (0, 0, 0))
    slots = pl.BlockSpec((None, n_tiles, ROUTE_TILE), lambda e, *_: (e, 0, 0))
    idx, g = pl.pallas_call(
        _compact_kernel,
        grid_spec=pltpu.PrefetchScalarGridSpec(
            num_scalar_prefetch=2, grid=(N_EXPERTS,), in_specs=[whole, whole], out_specs=[slots, slots]),
        out_shape=[jax.ShapeDtypeStruct((N_EXPERTS, n_tiles, ROUTE_TILE), I32),
                   jax.ShapeDtypeStruct((N_EXPERTS, n_tiles, ROUTE_TILE), F32)],
        compiler_params=_cp(("parallel",), 32), name="compact",
    )(lo[:, :n_tiles], hi[:, :n_tiles], q, a)
    return idx.reshape(N_EXPERTS, cap), g.reshape(N_EXPERTS, cap), cnt


def _row_copy(src_hbm, xe, sem, buf, tok, slot):
    dst = xe.at[buf, pl.ds(pl.multiple_of(slot * SUB, SUB), SUB)]
    return pltpu.make_async_copy(src_hbm.at[tok], dst, sem.at[buf])


def _ffn_kernel(idx_ref, idxn_ref, hc_hbm, hl_hbm, w1_ref, w3_ref, w2_ref, yc_ref, yl_ref,
                xe, xb, w1b, w3b, w2b, sem):
    e = pl.program_id(0)
    f = pl.program_id(1)
    buf = e % 2

    @pl.when((e == 0) & (f == 0))
    def _():
        def start_ctx(r, _):
            _row_copy(hc_hbm, xe, sem, 0, idx_ref[0, r], r).start()
            return 0

        def start_lat(r, _):
            _row_copy(hl_hbm, xe, sem, 0, idx_ref[0, r], r).start()
            return 0

        def wait_ctx(r, _):
            _row_copy(hc_hbm, xe, sem, 0, idx_ref[0, r], r).wait()
            return 0

        def wait_lat(r, _):
            _row_copy(hl_hbm, xe, sem, 0, idx_ref[0, r], r).wait()
            return 0

        lax.fori_loop(0, CAP_CTX, start_ctx, 0)
        lax.fori_loop(CAP_CTX, SLOTS, start_lat, 0)
        lax.fori_loop(0, CAP_CTX, wait_ctx, 0)
        lax.fori_loop(CAP_CTX, SLOTS, wait_lat, 0)

    @pl.when(f == 0)
    def _():
        for k in range(D_CHUNKS):
            xb[:, k * LANE:(k + 1) * LANE] = xe[buf, pl.ds(k, SLOTS, stride=D_CHUNKS), :].astype(BF)

    w1b[...] = w1_ref[...].astype(BF)
    w3b[...] = w3_ref[...].astype(BF)
    w2b[...] = w2_ref[...].astype(BF)

    part_c, part_l = CAP_CTX // FF_SPLIT, CAP_LAT // FF_SPLIT
    next_rows = ([(hc_hbm, f * part_c + u) for u in range(part_c)]
                 + [(hl_hbm, CAP_CTX + f * part_l + u) for u in range(part_l)])

    def next_copy(src, r):
        return _row_copy(src, xe, sem, 1 - buf, idxn_ref[0, r], r)

    n_tiles = SLOTS // FFN_ROWS
    for t in range(n_tiles):
        if t < n_tiles - 1:
            for src, r in next_rows[t::n_tiles - 1]:
                next_copy(src, r).start()
        else:
            for src, r in next_rows:
                next_copy(src, r).wait()
        rows = slice(t * FFN_ROWS, (t + 1) * FFN_ROWS)
        x = xb[rows, :]
        a = _dot(x, w1b[...])
        hid = (a * jax.nn.sigmoid(a) * _dot(x, w3b[...])).astype(BF)
        y = _dot(hid, w2b[...])
        y_ref, row0 = (yc_ref, t * FFN_ROWS) if t * FFN_ROWS < CAP_CTX else (yl_ref, t * FFN_ROWS - CAP_CTX)

        def tile(k):
            return pl.ds(row0 * D_CHUNKS + k, FFN_ROWS, stride=D_CHUNKS)

        @pl.when(f == 0)
        def _():
            for k in range(D_CHUNKS):
                y_ref[tile(k), :] = y[:, k * LANE:(k + 1) * LANE]

        @pl.when(f > 0)
        def _():
            for k in range(D_CHUNKS):
                y_ref[tile(k), :] = y_ref[tile(k), :] + y[:, k * LANE:(k + 1) * LANE]


def _ffn_call(layer, idx, h_ctx, h_lat, w1, w3, w2):
    ff = EXPERT_FF // FF_SPLIT
    wspec = pl.BlockSpec((None, None, D_MODEL, ff), lambda e, f: (layer, e, 0, f))
    last = N_EXPERTS - 1
    return pl.pallas_call(
        _ffn_kernel,
        grid=(N_EXPERTS, FF_SPLIT),
        in_specs=[pl.BlockSpec((None, 1, SLOTS), lambda e, f: (e, 0, 0), memory_space=pltpu.SMEM),
                  pl.BlockSpec((None, 1, SLOTS), lambda e, f: (jnp.minimum(e + 1, last), 0, 0),
                               memory_space=pltpu.SMEM),
                  pl.BlockSpec(memory_space=pl.ANY), pl.BlockSpec(memory_space=pl.ANY),
                  wspec, wspec, pl.BlockSpec((None, None, ff, D_MODEL), lambda e, f: (layer, e, f, 0))],
        out_specs=[pl.BlockSpec((None, CAP_CTX * D_CHUNKS, LANE), lambda e, f: (e, 0, 0)),
                   pl.BlockSpec((None, CAP_LAT * D_CHUNKS, LANE), lambda e, f: (e, 0, 0))],
        out_shape=[jax.ShapeDtypeStruct((N_EXPERTS, CAP_CTX * D_CHUNKS, LANE), F32),
                   jax.ShapeDtypeStruct((N_EXPERTS, CAP_LAT * D_CHUNKS, LANE), F32)],
        scratch_shapes=[pltpu.VMEM((2, SLOTS * D_CHUNKS, LANE), F32), pltpu.VMEM((SLOTS, D_MODEL), BF),
                        pltpu.VMEM((D_MODEL, ff), BF), pltpu.VMEM((D_MODEL, ff), BF),
                        pltpu.VMEM((ff, D_MODEL), BF), pltpu.SemaphoreType.DMA((2,))],
        compiler_params=_cp(("arbitrary", "arbitrary"), 54), name="ffn",
    )(idx, idx, h_ctx, h_lat, w1, w3, w2)


def _combine_kernel(idx_ref, g_ref, bnd_ref, yc_ref, yl_ref, acc_ref):
    h = pl.program_id(0)
    e = pl.program_id(1)

    @pl.when(e == 0)
    def _():
        acc_ref[...] = jnp.zeros(acc_ref.shape, F32)

    def add_rows(y_ref, slot0, lo, hi, base):
        def group(i, _):
            r0 = lo + i * COMBINE_GROUP
            toks = [idx_ref[0, r0 + u] - base for u in range(COMBINE_GROUP)]
            sums = [acc_ref[toks[u]] + y_ref[r0 - slot0 + u] * g_ref[0, r0 + u] for u in range(COMBINE_GROUP)]
            for u in range(COMBINE_GROUP):
                acc_ref[toks[u]] = sums[u]
            return 0

        def single(r, _):
            t = idx_ref[0, r] - base
            acc_ref[t] = acc_ref[t] + y_ref[r - slot0] * g_ref[0, r]
            return 0

        n_groups = (hi - lo) // COMBINE_GROUP
        lax.fori_loop(0, n_groups, group, 0)
        lax.fori_loop(lo + n_groups * COMBINE_GROUP, hi, single, 0)

    @pl.when(h == 0)
    def _():
        add_rows(yc_ref, 0, 0, CAP_CTX, 0)

    @pl.when(h > 0)
    def _():
        add_rows(yl_ref, CAP_CTX, bnd_ref[0, h], bnd_ref[0, h + 1], (h - 1) * TOK_HALF)


def _combine_call(idx, g, bnd, y_ctx, y_lat):
    last = N_EXPERTS - 1
    return pl.pallas_call(
        _combine_kernel,
        grid=(N_HALVES, N_EXPERTS),
        in_specs=[pl.BlockSpec((None, 1, SLOTS), lambda h, e: (e, 0, 0), memory_space=pltpu.SMEM),
                  pl.BlockSpec((None, 1, SLOTS), lambda h, e: (e, 0, 0), memory_space=pltpu.SMEM),
                  pl.BlockSpec((None, 1, N_HALVES + 1), lambda h, e: (e, 0, 0), memory_space=pltpu.SMEM),
                  pl.BlockSpec((None, CAP_CTX, SUB, LANE), lambda h, e: (jnp.where(h == 0, e, last), 0, 0, 0)),
                  pl.BlockSpec((None, CAP_LAT, SUB, LANE), lambda h, e: (jnp.where(h == 0, 0, e), 0, 0, 0))],
        out_specs=pl.BlockSpec((None, TOK_HALF, SUB, LANE), lambda h, e: (h, 0, 0, 0)),
        out_shape=jax.ShapeDtypeStruct((N_HALVES, TOK_HALF, SUB, LANE), F32),
        compiler_params=_cp(("parallel", "arbitrary"), 52), name="combine",
    )(idx, g, bnd, y_ctx.reshape(N_EXPERTS, CAP_CTX, SUB, LANE), y_lat.reshape(N_EXPERTS, CAP_LAT, SUB, LANE))


def _final_kernel(x_ref, moe_ref, mod_ref, g_ref, o_ref):
    x = x_ref[...] + mod_ref[...][5:6] * _moe_rows(moe_ref, TM_PRE)
    o_ref[...] = _rms(x, g_ref[...])


def _final_call(x, moe, mod, g, latent):
    n = x.shape[0]
    tm = TM_PRE
    moe_off = (N_CTX // tm) if latent else 0
    row = pl.BlockSpec((tm, D_MODEL), lambda i: (i, 0))
    return pl.pallas_call(
        _final_kernel, grid=(n // tm,),
        in_specs=[row, pl.BlockSpec((tm * D_CHUNKS, LANE), lambda i: (i + moe_off, 0)),
                  _mod_spec(DEPTH - 1, tm, latent), _layer(0, (1, D_MODEL))],
        out_specs=row, out_shape=jax.ShapeDtypeStruct((n, D_MODEL), F32),
        compiler_params=_cp(("parallel",), 32), name="final",
    )(x, moe, mod, g)


def _axial_tables(rot_dim):
    rows = DEC_SEQ // GRID_W
    row = jnp.repeat(jnp.arange(rows, dtype=F32), GRID_W)
    col = jnp.tile(jnp.arange(GRID_W, dtype=F32), rows)
    n_freq = rot_dim // 4
    inv = ROPE_BASE ** (-jnp.arange(n_freq, dtype=F32) / n_freq)
    ang = jnp.concatenate([row[:, None] * inv, col[:, None] * inv], axis=-1)
    sign = jnp.tile(jnp.array([-1.0, 1.0], F32), rot_dim // 2)
    return jnp.repeat(jnp.cos(ang), 2, axis=1), jnp.repeat(jnp.sin(ang), 2, axis=1) * sign


def _rope_tables():
    c64, s64 = _axial_tables(DA_DH)
    c32, s32 = _axial_tables(MLA_ROPE)
    one = lambda w: jnp.ones((DEC_SEQ, w), F32)
    zero = lambda w: jnp.zeros((DEC_SEQ, w), F32)
    pad = LANE - MLA_NOPE - MLA_ROPE
    return jnp.stack([
        jnp.tile(c64, (1, 2)), jnp.tile(s64, (1, 2)),
        jnp.concatenate([one(MLA_NOPE), c32, one(pad)], axis=1),
        jnp.concatenate([zero(MLA_NOPE), s32, zero(pad)], axis=1),
        jnp.concatenate([c32, one(LANE - MLA_ROPE)], axis=1),
        jnp.concatenate([s32, zero(LANE - MLA_ROPE)], axis=1)])


def _prep_weights(w_in, w_gate, mla_w_qb, mla_w_kvb, w_br_da, w_br_mla, w_br_swa, w_o, w_router):
    kr_end = C_MKR + MLA_ROPE
    win = jnp.concatenate([w_in[:, :, :kr_end].astype(BF), jnp.zeros((DEPTH, D_MODEL, LANE - MLA_ROPE), BF),
                           w_in[:, :, kr_end:].astype(BF)], axis=2)
    dk = MLA_NOPE + MLA_ROPE
    wqb = jnp.pad(mla_w_qb.reshape(DEPTH, MLA_Q_RANK, MLA_HEADS, dk), ((0, 0), (0, 0), (0, 0), (0, LANE - dk)))
    wqb = wqb.reshape(DEPTH, MLA_Q_RANK, MLA_HEADS * LANE).astype(BF)
    kvb = mla_w_kvb.reshape(DEPTH, MLA_KV_RANK, MLA_HEADS, MLA_NOPE + MLA_V)
    wk = jnp.pad(kvb[..., :MLA_NOPE], ((0, 0), (0, 0), (0, 0), (0, LANE - MLA_NOPE)))
    wk = wk.reshape(DEPTH, MLA_KV_RANK, MLA_HEADS * LANE)
    wv = kvb[..., MLA_NOPE:].reshape(DEPTH, MLA_KV_RANK, MLA_OUT)
    wkv = jnp.concatenate([wk, wv], axis=2).astype(BF)
    wrh, wrl = _split(jnp.swapaxes(w_router, 1, 2))
    return dict(win=win, wg=w_gate.astype(BF), wqb=wqb, wkv=wkv, wda=w_br_da.astype(BF),
                wmla=w_br_mla.astype(BF), wswa=w_br_swa.astype(BF), wo=w_o.astype(BF), wrh=wrh, wrl=wrl)


def kernel(x_prompt, x_sample, cache_da_k, cache_da_v, cache_mla_ckv, cache_mla_krope, cache_swa_k, cache_swa_v, c, c_ctx, w_ada, b_ada, norm1, norm2, w_in, da_lq1, da_lk1, da_lq2, da_lk2, da_subln, mla_q_norm, mla_w_qb, mla_kv_norm, mla_w_kvb, swa_sink, w_gate, w_br_da, w_br_mla, w_br_swa, w_o, w_router, w_ff1, w_ff3, w_ff2, final_norm):
    c_all = jnp.concatenate([c_ctx[None], c, jnp.zeros((SUB - 1 - DEC_BATCH, D_MODEL), F32)], axis=0)
    mod = _ada_call(c_all, w_ada, b_ada)[:, :1 + DEC_BATCH].reshape(DEPTH, 1 + DEC_BATCH, 6, D_MODEL)
    lam_all = _lam_call(da_lq1, da_lk1, da_lq2, da_lk2)
    tab = _rope_tables()
    w = _prep_weights(w_in, w_gate, mla_w_qb, mla_w_kvb, w_br_da, w_br_mla, w_br_swa, w_o, w_router)
    n1, n2 = norm1[:, None], norm2[:, None]
    subln, qn, kvn = da_subln[:, None], mla_q_norm[:, None], mla_kv_norm[:, None]
    caches = (cache_da_k.reshape(DEC_BATCH, DEPTH, PAST_LEN, 512),
              cache_da_v.reshape(DEC_BATCH, DEPTH, PAST_LEN, 512),
              cache_mla_ckv,
              jnp.pad(cache_mla_krope, ((0, 0), (0, 0), (0, 0), (0, LANE - MLA_ROPE))),
              cache_swa_k.reshape(DEC_BATCH, DEPTH, PAST_LEN, LANE),
              cache_swa_v.reshape(DEC_BATCH, DEPTH, PAST_LEN, LANE))

    xp = x_prompt.reshape(N_CTX, D_MODEL)
    xs = x_sample.reshape(N_LAT, D_MODEL)
    moe = None
    new = [[] for _ in range(6)]
    for l in range(DEPTH):
        lam = lam_all[l, :1]
        sink = swa_sink[l]

        xp, proj_c, gate_c = _pre_call(l, xp, moe, mod, n1, w["win"], w["wg"], latent=False)
        xs, proj_l, gate_l = _pre_call(l, xs, moe, mod, n1, w["win"], w["wg"], latent=True)

        oda_c, omla_c, oswa_c, ckv_c = _ctx_attn_call(l, lam, sink, proj_c, subln, qn, w["wqb"], kvn, w["wkv"])
        pc = proj_c.reshape(BATCH, SEQ, PROJ_COLS)
        new[0].append(pc[..., C_DAK:C_DAK + 512].reshape(BATCH, SEQ, DA_HEADS, 2 * DA_DH))
        new[1].append(pc[..., C_DAV:C_DAV + 512].reshape(BATCH, SEQ, DA_HEADS, 2 * DA_DH))
        new[2].append(ckv_c.reshape(BATCH, SEQ, MLA_KV_RANK))
        new[3].append(pc[..., C_MKR:C_MKR + MLA_ROPE])
        new[4].append(pc[..., C_SK:C_SK + LANE].reshape(BATCH, SEQ, SWA_KV_HEADS, SWA_DH))
        new[5].append(pc[..., C_SV:C_SV + LANE].reshape(BATCH, SEQ, SWA_KV_HEADS, SWA_DH))

        daq, mlaq, swaq, dak, dav, mlak, mlav, swak, swav = _lat_prep_call(
            l, proj_l, tab, caches, qn, w["wqb"], kvn, w["wkv"])
        oda_l = _lat_da_call(l, lam, daq, dak, dav, subln).reshape(N_LAT, 512)
        omla_l = _lat_mla_call(mlaq, mlak, mlav).reshape(N_LAT, 512)
        oswa_l = _lat_swa_call(sink, swaq, swak, swav).reshape(N_LAT, 512)

        post_w = (w["wda"], w["wmla"], w["wswa"], w["wo"], w["wrh"], w["wrl"])
        xp, h2_c, lg_c = _post_call(l, xp, oda_c, omla_c, oswa_c, gate_c, mod, n2, *post_w, latent=False)
        xs, h2_l, lg_l = _post_call(l, xs, oda_l, omla_l, oswa_l, gate_l, mod, n2, *post_w, latent=True)

        idx_c, g_c, _ = _router_call(lg_c, CAP_CTX)
        idx_l, g_l, cnt_l = _router_call(lg_l, CAP_LAT)
        idx = jnp.concatenate([idx_c, idx_l], axis=1).reshape(N_EXPERTS, 1, SLOTS)
        gsel = jnp.concatenate([g_c, g_l], axis=1).reshape(N_EXPERTS, 1, SLOTS)
        n0 = cnt_l[:, 0].astype(I32)
        bnd = jnp.stack([jnp.zeros_like(n0), jnp.full_like(n0, CAP_CTX), CAP_CTX + n0,
                         jnp.full_like(n0, SLOTS)], axis=1)
        y_c, y_l = _ffn_call(l, idx, h2_c.reshape(N_CTX, SUB, LANE), h2_l.reshape(N_LAT, SUB, LANE),
                             w_ff1, w_ff3, w_ff2)
        acc = _combine_call(idx, gsel, bnd.reshape(N_EXPERTS, 1, N_HALVES + 1), y_c, y_l)
        moe = acc.reshape(N_HALVES * TOK_HALF * D_CHUNKS, LANE)

    fn = final_norm[None, None]
    y_prompt = _final_call(xp, moe, mod, fn, latent=False).reshape(BATCH, SEQ, D_MODEL)
    y_sample = _final_call(xs, moe, mod, fn, latent=True).reshape(DEC_BATCH, DEC_SEQ, D_MODEL)
    return (y_prompt, y_sample) + tuple(jnp.stack(n, axis=1) for n in new)
```

```python
import functools
import math

import jax
import jax.numpy as jnp
from jax import lax
from jax.experimental import pallas as pl
from jax.experimental.pallas import tpu as pltpu

F32 = jnp.float32
BF = jnp.bfloat16
I32 = jnp.int32

D_MODEL = 1024
BATCH = 16
SEQ = 256
DEPTH = 2
DEC_BATCH = 2
DEC_SEQ = 4096
PAST_LEN = 512
GRID_W = 64
ROPE_BASE = 10000.0
EPS = 1e-6
DA_HEADS = 4
DA_DH = 64
DA_OUT = DA_HEADS * 2 * DA_DH
MLA_HEADS = 8
MLA_Q_RANK = 256
MLA_KV_RANK = 128
MLA_NOPE = 64
MLA_ROPE = 32
MLA_V = 64
MLA_OUT = MLA_HEADS * MLA_V
SWA_HEADS = 8
SWA_KV_HEADS = 2
SWA_DH = 64
W_BLOCK = 128
SWA_OUT = SWA_HEADS * SWA_DH
N_EXPERTS = 16
EXPERT_FF = 1024
CAPACITY_FACTOR = 2

LANE = 128
SUB = 8
HALF = 64
N_CTX = BATCH * SEQ
N_LAT = DEC_BATCH * DEC_SEQ
KEYS = PAST_LEN + DEC_SEQ
CAP_CTX = CAPACITY_FACTOR * N_CTX // N_EXPERTS
CAP_LAT = CAPACITY_FACTOR * N_LAT // N_EXPERTS
SLOTS = CAP_CTX + CAP_LAT
TOK_HALF = 4096
N_HALVES = (N_CTX + N_LAT) // TOK_HALF
D_CHUNKS = D_MODEL // LANE

C_DAQ, C_DAK, C_DAV = 0, 512, 1024
C_MQ, C_MKV, C_MKR = 1536, 1792, 1920
C_SQ, C_SK, C_SV = 2048, 2560, 2688
PROJ_COLS = 2816
GATE_COLS = 3 * D_MODEL
MLA_SCALE = (MLA_NOPE + MLA_ROPE) ** -0.5
HEAD_SCALE = DA_DH ** -0.5
LOG2E = math.log2(math.e)

TM_PRE = 256
TM_FINAL = 512
TM_POST = 512
CHUNK = 512
TQ = 256
DA_STEP_HEADS = 4
MLA_STEP_PAIRS = 4
ROUTE_TILE = 128
ROUTE_CHUNK = 512
THRESHOLD_STEPS = 48
FFN_ROWS = 512
FF_SPLIT = 2
COMBINE_GROUP = 8


def _cp(sem, vmem_mb):
    return pltpu.CompilerParams(dimension_semantics=sem, vmem_limit_bytes=vmem_mb * 1024 * 1024)


def _dot(a, b):
    return jnp.dot(a, b, preferred_element_type=F32)


def _dot_nt(a, b):
    return lax.dot_general(a, b, (((1,), (1,)), ((), ())), preferred_element_type=F32)


def _split(a):
    hi = a.astype(BF)
    lo = (a - hi.astype(F32)).astype(BF)
    return hi, lo


def _dot3(a, w):
    ah, al = _split(a)
    wh, wl = _split(w)
    return _dot(ah, wh) + _dot(ah, wl) + _dot(al, wh)


def _rms(x, g):
    return x * lax.rsqrt(jnp.mean(x * x, axis=-1, keepdims=True) + EPS) * g


def _lane(shape):
    return lax.broadcasted_iota(I32, shape, len(shape) - 1)


def _softmax_pv(segs, sink=None):
    m = None
    for s, _ in segs:
        ms = jnp.max(s, axis=-1, keepdims=True)
        m = ms if m is None else jnp.maximum(m, ms)
    if sink is not None:
        m = jnp.maximum(m, sink)
    l = None
    o = None
    for s, v in segs:
        e = jnp.exp(s - m)
        ls = jnp.sum(e, axis=-1, keepdims=True)
        os_ = _dot(e.astype(BF), v)
        l = ls if l is None else l + ls
        o = os_ if o is None else o + os_
    if sink is not None:
        l = l + jnp.exp(sink - m)
    return o / l


def _mask_halves(q):
    lo = _lane(q.shape) < HALF
    return jnp.where(lo, q, 0.0).astype(BF), jnp.where(lo, 0.0, q).astype(BF)


def _pair(o_a, o_b):
    return jnp.where(_lane(o_a.shape) < HALF, o_a, o_b)


def _dup(x):
    r = pltpu.roll(x, HALF, 1)
    lo = _lane(x.shape) < HALF
    return jnp.where(lo, x, r), jnp.where(lo, r, x)


def _rope(x, c, s):
    n = x.shape[-1]
    even = (_lane(x.shape) % 2) == 0
    sw = jnp.where(even, pltpu.roll(x, n - 1, 1), pltpu.roll(x, 1, 1))
    return x * c + sw * s


def _da_head(q, k, v, lam, subln, lam_init):
    q1, q2 = _mask_halves(q)
    o1 = _softmax_pv([(_dot_nt(q1, k), v)])
    o2 = _softmax_pv([(_dot_nt(q2, k), v)])
    return _rms(o1 - lam * o2, subln) * (1.0 - lam_init)


def _ada_kernel(c_ref, w_ref, b_ref, o_ref):
    c = c_ref[...]
    a = c * jax.nn.sigmoid(c)
    o_ref[...] = _dot3(a, w_ref[...]) + b_ref[...]


def _ada_call(c_all, w_ada, b_ada):
    tn = 1536
    return pl.pallas_call(
        _ada_kernel,
        grid=(DEPTH, 6 * D_MODEL // tn),
        in_specs=[pl.BlockSpec((SUB, D_MODEL), lambda l, j: (0, 0)),
                  pl.BlockSpec((None, D_MODEL, tn), lambda l, j: (l, 0, j)),
                  pl.BlockSpec((None, 1, tn), lambda l, j: (l, 0, j))],
        out_specs=pl.BlockSpec((None, SUB, tn), lambda l, j: (l, 0, j)),
        out_shape=jax.ShapeDtypeStruct((DEPTH, SUB, 6 * D_MODEL), F32),
        compiler_params=_cp(("parallel", "parallel"), 40),
        name="ada",
    )(c_all, w_ada, b_ada.reshape(DEPTH, 1, 6 * D_MODEL))


def _lam_kernel(q1, k1, q2, k2, o_ref):
    s1 = jnp.sum(q1[...] * k1[...], axis=-1, keepdims=True)
    s2 = jnp.sum(q2[...] * k2[...], axis=-1, keepdims=True)
    row = lax.broadcasted_iota(I32, (DEPTH, 1), 0)
    init = jnp.zeros((DEPTH, 1), F32)
    for l in range(DEPTH):
        init = jnp.where(row == l, _lam_init(l), init)
    o_ref[...] = jnp.broadcast_to(jnp.exp(s1) - jnp.exp(s2) + init, o_ref.shape)


def _lam_init(layer):
    return 0.8 - 0.6 * math.exp(-0.3 * layer)


def _lam_call(q1, k1, q2, k2):
    return pl.pallas_call(
        _lam_kernel,
        out_shape=jax.ShapeDtypeStruct((DEPTH, LANE), F32),
        name="lam",
    )(q1, k1, q2, k2)


def _moe_rows(moe_ref, tm):
    return jnp.concatenate([moe_ref[pl.ds(k, tm, stride=D_CHUNKS), :] for k in range(D_CHUNKS)], axis=1)


def _pre_body(x, mod_ref, n1_ref, win_ref, wg_ref, proj_ref, gate_ref):
    m = mod_ref[...]
    h = _rms(x, n1_ref[...]) * (1.0 + m[1:2]) + m[0:1]
    hb = h.astype(BF)
    proj_ref[...] = _dot(hb, win_ref[...]).astype(proj_ref.dtype)
    gate_ref[...] = jax.nn.sigmoid(_dot(hb, wg_ref[...])).astype(BF)


def _pre_first_kernel(x_ref, mod_ref, n1_ref, win_ref, wg_ref, proj_ref, gate_ref):
    _pre_body(x_ref[...], mod_ref, n1_ref, win_ref, wg_ref, proj_ref, gate_ref)


def _pre_next_kernel(x_ref, moe_ref, modp_ref, mod_ref, n1_ref, win_ref, wg_ref, xo_ref, proj_ref, gate_ref):
    x = x_ref[...] + modp_ref[...][5:6] * _moe_rows(moe_ref, TM_PRE)
    xo_ref[...] = x
    _pre_body(x, mod_ref, n1_ref, win_ref, wg_ref, proj_ref, gate_ref)


def _mod_spec(layer, tm, latent):
    per = DEC_SEQ // tm
    if latent:
        return pl.BlockSpec((None, None, 6, D_MODEL), lambda i: (layer, 1 + i // per, 0, 0))
    return pl.BlockSpec((None, None, 6, D_MODEL), lambda i: (layer, 0, 0, 0))


def _layer(layer, shape):
    nd = len(shape)
    return pl.BlockSpec((None,) + shape, lambda *_: (layer,) + (0,) * nd, pipeline_mode=pl.Buffered(1))


def _pre_call(layer, x, moe, mod, n1, win, wg, latent):
    n = x.shape[0]
    tm = TM_PRE
    row = pl.BlockSpec((tm, D_MODEL), lambda i: (i, 0))
    w_specs = [_mod_spec(layer, tm, latent), _layer(layer, (1, D_MODEL)), _layer(layer, (D_MODEL, PROJ_COLS)),
               _layer(layer, (D_MODEL, GATE_COLS))]
    outs = [jax.ShapeDtypeStruct((n, PROJ_COLS), BF if latent else F32), jax.ShapeDtypeStruct((n, GATE_COLS), BF)]
    out_specs = [pl.BlockSpec((tm, PROJ_COLS), lambda i: (i, 0)), pl.BlockSpec((tm, GATE_COLS), lambda i: (i, 0))]
    if moe is None:
        proj, gate = pl.pallas_call(
            _pre_first_kernel, grid=(n // tm,), in_specs=[row] + w_specs, out_specs=out_specs, out_shape=outs,
            compiler_params=_cp(("parallel",), 52), name="pre_first",
        )(x, mod, n1, win, wg)
        return x, proj, gate
    moe_off = (N_CTX // tm) if latent else 0
    moe_spec = pl.BlockSpec((tm * D_CHUNKS, LANE), lambda i: (i + moe_off, 0))
    xo, proj, gate = pl.pallas_call(
        _pre_next_kernel, grid=(n // tm,),
        in_specs=[row, moe_spec, _mod_spec(layer - 1, tm, latent)] + w_specs,
        out_specs=[row] + out_specs,
        out_shape=[jax.ShapeDtypeStruct((n, D_MODEL), F32)] + outs,
        compiler_params=_cp(("parallel",), 52), name="pre_next",
    )(x, moe, mod, mod, n1, win, wg)
    return xo, proj, gate


def _ctx_attn_kernel(lam_init, lam_ref, sink_ref, p_ref, subln_ref, qn_ref, wqb_ref, kvn_ref, wkv_ref,
                     oda_ref, omla_ref, oswa_ref, ckv_ref):
    p = p_ref[...]
    lam = lam_ref[0]
    for h in range(DA_HEADS):
        blk = slice(h * LANE, (h + 1) * LANE)
        q = p[:, C_DAQ + h * LANE:C_DAQ + (h + 1) * LANE] * HEAD_SCALE
        k = p[:, C_DAK + h * LANE:C_DAK + (h + 1) * LANE].astype(BF)
        v = p[:, C_DAV + h * LANE:C_DAV + (h + 1) * LANE].astype(BF)
        oda_ref[:, blk] = _da_head(q, k, v, lam, subln_ref[...], lam_init).astype(BF)
    qn = _rms(p[:, C_MQ:C_MQ + MLA_Q_RANK], qn_ref[...]).astype(BF)
    cq = _dot(qn, wqb_ref[...])
    ckv = _rms(p[:, C_MKV:C_MKV + MLA_KV_RANK], kvn_ref[...])
    ckv_ref[...] = ckv
    kv = _dot(ckv.astype(BF), wkv_ref[...])
    kr_sh = pltpu.roll(p[:, C_MKR:C_MKR + LANE], HALF, 1)
    for m in range(MLA_HEADS // 2):
        vpair = kv[:, MLA_HEADS * LANE + m * LANE:MLA_HEADS * LANE + (m + 1) * LANE].astype(BF)
        outs = []
        for h in (2 * m, 2 * m + 1):
            q = (cq[:, h * LANE:(h + 1) * LANE] * MLA_SCALE).astype(BF)
            k = (kv[:, h * LANE:(h + 1) * LANE] + kr_sh).astype(BF)
            outs.append(_softmax_pv([(_dot_nt(q, k), vpair)]))
        omla_ref[:, m * LANE:(m + 1) * LANE] = _pair(outs[0], outs[1]).astype(BF)
    kd = _dup(p[:, C_SK:C_SK + LANE])
    vd = _dup(p[:, C_SV:C_SV + LANE])
    for m in range(SWA_HEADS // 2):
        g = (2 * m) // (SWA_HEADS // SWA_KV_HEADS)
        k = kd[g].astype(BF)
        v = vd[g].astype(BF)
        qa, qb = _mask_halves(p[:, C_SQ + m * LANE:C_SQ + (m + 1) * LANE] * HEAD_SCALE)
        oa = _softmax_pv([(_dot_nt(qa, k), v)], sink_ref[2 * m])
        ob = _softmax_pv([(_dot_nt(qb, k), v)], sink_ref[2 * m + 1])
        oswa_ref[:, m * LANE:(m + 1) * LANE] = _pair(oa, ob).astype(BF)


def _smem():
    return pl.BlockSpec(memory_space=pltpu.SMEM)


def _ctx_attn_call(layer, lam, sink, proj, subln, qn, wqb, kvn, wkv):
    row512 = pl.BlockSpec((SEQ, 512), lambda b: (b, 0))
    return pl.pallas_call(
        functools.partial(_ctx_attn_kernel, _lam_init(layer)),
        grid=(BATCH,),
        in_specs=[_smem(), _smem(), pl.BlockSpec((SEQ, PROJ_COLS), lambda b: (b, 0)),
                  _layer(layer, (1, LANE)), _layer(layer, (1, MLA_Q_RANK)),
                  _layer(layer, (MLA_Q_RANK, MLA_HEADS * LANE)), _layer(layer, (1, MLA_KV_RANK)),
                  _layer(layer, (MLA_KV_RANK, MLA_HEADS * LANE + MLA_OUT))],
        out_specs=[row512, row512, row512, pl.BlockSpec((SEQ, MLA_KV_RANK), lambda b: (b, 0))],
        out_shape=[jax.ShapeDtypeStruct((N_CTX, 512), BF)] * 3 + [jax.ShapeDtypeStruct((N_CTX, MLA_KV_RANK), F32)],
        compiler_params=_cp(("parallel",), 40), name="ctx_attn",
    )(lam, sink, proj, subln, qn, wqb, kvn, wkv)


def _lat_prep_kernel(p_ref, tab_ref, cdk_ref, cdv_ref, cckv_ref, ckr_ref, csk_ref, csv_ref,
                     qn_ref, wqb_ref, kvn_ref, wkv_ref,
                     daq_ref, mlaq_ref, swaq_ref, dak_ref, dav_ref, mlak_ref, mlav_ref, swak_ref, swav_ref):
    j = pl.program_id(1)

    def write_kv(dk, dv, ckv, kr_sh, sk, sv):
        ones_col = jnp.where(_lane((CHUNK, LANE)) == 0, 1.0, 0.0).astype(BF)
        dak_ref[...] = dk.astype(BF)
        for h in range(DA_HEADS):
            dav_ref[:, 2 * h * LANE:(2 * h + 1) * LANE] = dv[:, h * LANE:(h + 1) * LANE].astype(BF)
            dav_ref[:, (2 * h + 1) * LANE:(2 * h + 2) * LANE] = ones_col
        kv = _dot(ckv.astype(BF), wkv_ref[...])
        for h in range(MLA_HEADS):
            mlak_ref[:, h * LANE:(h + 1) * LANE] = (kv[:, h * LANE:(h + 1) * LANE] + kr_sh).astype(BF)
        for p in range(MLA_HEADS // 2):
            mlav_ref[:, 2 * p * LANE:(2 * p + 1) * LANE] = kv[:, (MLA_HEADS + p) * LANE:(MLA_HEADS + p + 1) * LANE].astype(BF)
            mlav_ref[:, (2 * p + 1) * LANE:(2 * p + 2) * LANE] = ones_col
        k0, k1 = _dup(sk)
        v0, v1 = _dup(sv)
        swak_ref[:, 0:LANE] = k0.astype(BF)
        swak_ref[:, LANE:2 * LANE] = k1.astype(BF)
        swav_ref[:, 0:LANE] = v0.astype(BF)
        swav_ref[:, LANE:2 * LANE] = ones_col
        swav_ref[:, 2 * LANE:3 * LANE] = v1.astype(BF)
        swav_ref[:, 3 * LANE:4 * LANE] = ones_col

    @pl.when(j == 0)
    def _():
        write_kv(cdk_ref[...], cdv_ref[...], cckv_ref[...], pltpu.roll(ckr_ref[...], HALF, 1),
                 csk_ref[...], csv_ref[...])

    @pl.when(j > 0)
    def _():
        c64, s64 = tab_ref[0], tab_ref[1]
        cmq, smq = tab_ref[2], tab_ref[3]
        ckr, skr = tab_ref[4], tab_ref[5]

        def cols(c0, width):
            return p_ref[:, c0:c0 + width].astype(F32)

        def blk(c0, h):
            return cols(c0 + h * LANE, LANE)

        for h in range(DA_HEADS):
            daq_ref[:, h * LANE:(h + 1) * LANE] = (_rope(blk(C_DAQ, h), c64, s64) * (HEAD_SCALE * LOG2E)).astype(BF)
            swaq_ref[:, h * LANE:(h + 1) * LANE] = (_rope(blk(C_SQ, h), c64, s64) * (HEAD_SCALE * LOG2E)).astype(BF)
        dk = jnp.concatenate([_rope(blk(C_DAK, h), c64, s64) for h in range(DA_HEADS)], axis=1)
        qn = _rms(cols(C_MQ, MLA_Q_RANK), qn_ref[...]).astype(BF)
        cq = _dot(qn, wqb_ref[...])
        for h in range(MLA_HEADS):
            mlaq_ref[:, h * LANE:(h + 1) * LANE] = (
                _rope(cq[:, h * LANE:(h + 1) * LANE], cmq, smq) * (MLA_SCALE * LOG2E)).astype(BF)
        ckv = _rms(cols(C_MKV, MLA_KV_RANK), kvn_ref[...])
        kr_sh = pltpu.roll(_rope(cols(C_MKR, LANE), ckr, skr), HALF, 1)
        sk = _rope(cols(C_SK, LANE), c64, s64)
        write_kv(dk, cols(C_DAV, 512), ckv, kr_sh, sk, cols(C_SV, LANE))


def _lat_prep_call(layer, proj, tab, caches, qn, wqb, kvn, wkv):
    cdk, cdv, cckv, ckr, csk, csv = caches
    nj = 1 + DEC_SEQ // CHUNK

    def own(width):
        return pl.BlockSpec((None, CHUNK, width), lambda b, j: (b, jnp.maximum(j - 1, 0), 0))

    def cache(width):
        return pl.BlockSpec((None, None, PAST_LEN, width), lambda b, j: (b, layer, 0, 0))

    def allk(width):
        return pl.BlockSpec((None, CHUNK, width), lambda b, j: (b, j, 0))

    def shp(rows, width):
        return jax.ShapeDtypeStruct((DEC_BATCH, rows, width), BF)

    return pl.pallas_call(
        _lat_prep_kernel,
        grid=(DEC_BATCH, nj),
        in_specs=[own(PROJ_COLS),
                  pl.BlockSpec((6, CHUNK, LANE), lambda b, j: (0, jnp.maximum(j - 1, 0), 0)),
                  cache(512), cache(512), cache(LANE), cache(LANE), cache(LANE), cache(LANE),
                  _layer(layer, (1, MLA_Q_RANK)), _layer(layer, (MLA_Q_RANK, MLA_HEADS * LANE)),
                  _layer(layer, (1, MLA_KV_RANK)), _layer(layer, (MLA_KV_RANK, MLA_HEADS * LANE + MLA_OUT))],
        out_specs=[own(512), own(MLA_HEADS * LANE), own(512),
                   allk(512), allk(1024), allk(MLA_HEADS * LANE), allk(1024), allk(2 * LANE), allk(4 * LANE)],
        out_shape=[shp(DEC_SEQ, 512), shp(DEC_SEQ, MLA_HEADS * LANE), shp(DEC_SEQ, 512),
                   shp(KEYS, 512), shp(KEYS, 1024), shp(KEYS, MLA_HEADS * LANE), shp(KEYS, 1024),
                   shp(KEYS, 2 * LANE), shp(KEYS, 4 * LANE)],
        compiler_params=_cp(("parallel", "arbitrary"), 48), name="lat_prep",
    )(proj.reshape(DEC_BATCH, DEC_SEQ, PROJ_COLS), tab, cdk, cdv, cckv, ckr, csk, csv, qn, wqb, kvn, wkv)


def _softmax2_pv(s, v_aug):
    e = jnp.exp2(s - jnp.max(s, axis=-1, keepdims=True)).astype(BF)
    o = _dot(e, v_aug)
    return o[:, 0:LANE] / o[:, LANE:LANE + 1]


def _lat_da_kernel(lam_init, lam_ref, q_ref, k_ref, v_ref, subln_ref, o_ref):
    for h in range(DA_STEP_HEADS):
        cols = slice(h * LANE, (h + 1) * LANE)
        q1, q2 = _mask_halves(q_ref[:, cols].astype(F32))
        k = k_ref[:, cols]
        v = v_ref[:, 2 * h * LANE:(2 * h + 2) * LANE]
        o = _softmax2_pv(_dot_nt(q1, k), v) - lam_ref[0] * _softmax2_pv(_dot_nt(q2, k), v)
        o_ref[:, cols] = (_rms(o, subln_ref[...]) * (1.0 - lam_init)).astype(BF)


def _lat_da_call(layer, lam, q, k, v, subln):
    return pl.pallas_call(
        functools.partial(_lat_da_kernel, _lam_init(layer)),
        grid=(DEC_BATCH, DA_HEADS // DA_STEP_HEADS, DEC_SEQ // TQ),
        in_specs=[_smem(),
                  pl.BlockSpec((None, TQ, DA_STEP_HEADS * LANE), lambda b, h, i: (b, i, h)),
                  pl.BlockSpec((None, KEYS, DA_STEP_HEADS * LANE), lambda b, h, i: (b, 0, h)),
                  pl.BlockSpec((None, KEYS, 2 * DA_STEP_HEADS * LANE), lambda b, h, i: (b, 0, h)),
                  _layer(layer, (1, LANE))],
        out_specs=pl.BlockSpec((None, TQ, DA_STEP_HEADS * LANE), lambda b, h, i: (b, i, h)),
        out_shape=jax.ShapeDtypeStruct((DEC_BATCH, DEC_SEQ, 512), BF),
        compiler_params=_cp(("parallel", "parallel", "arbitrary"), 56), name="lat_da",
    )(lam, q, k, v, subln)


def _lat_mla_kernel(q_ref, k_ref, v_ref, o_ref):
    for p in range(MLA_STEP_PAIRS):
        v = v_ref[:, 2 * p * LANE:(2 * p + 2) * LANE]
        outs = []
        for h in (2 * p, 2 * p + 1):
            cols = slice(h * LANE, (h + 1) * LANE)
            outs.append(_softmax2_pv(_dot_nt(q_ref[:, cols], k_ref[:, cols]), v))
        o_ref[:, p * LANE:(p + 1) * LANE] = _pair(outs[0], outs[1]).astype(BF)


def _lat_mla_call(q, k, v):
    sp = MLA_STEP_PAIRS
    return pl.pallas_call(
        _lat_mla_kernel,
        grid=(DEC_BATCH, MLA_HEADS // (2 * sp), DEC_SEQ // TQ),
        in_specs=[pl.BlockSpec((None, TQ, 2 * sp * LANE), lambda b, m, i: (b, i, m)),
                  pl.BlockSpec((None, KEYS, 2 * sp * LANE), lambda b, m, i: (b, 0, m)),
                  pl.BlockSpec((None, KEYS, 2 * sp * LANE), lambda b, m, i: (b, 0, m))],
        out_specs=pl.BlockSpec((None, TQ, sp * LANE), lambda b, m, i: (b, i, m)),
        out_shape=jax.ShapeDtypeStruct((DEC_BATCH, DEC_SEQ, 512), BF),
        compiler_params=_cp(("parallel", "parallel", "arbitrary"), 56), name="lat_mla",
    )(q, k, v)


def _swa_bias():
    rows = (SWA_HEADS // SWA_KV_HEADS) * W_BLOCK
    qi = (jnp.arange(rows) % W_BLOCK)[:, None]
    loc = (jnp.arange(PAST_LEN + 3 * W_BLOCK) - PAST_LEN)[None, :]
    always = (loc < 0) | ((loc >= W_BLOCK) & (loc < 2 * W_BLOCK))
    prev = (loc >= 0) & (loc < W_BLOCK) & (loc >= qi)
    nxt = (loc >= 2 * W_BLOCK) & (loc - 2 * W_BLOCK <= qi)
    ok = jnp.stack([always | nxt, always | prev | nxt, always | prev])
    return jnp.where(ok, 0.0, -jnp.inf).astype(F32)


def _lat_swa_kernel(sink_ref, bias_ref, q_ref, kc_ref, kp_ref, k0_ref, kn_ref, vc_ref, vp_ref, v0_ref, vn_ref,
                    o_ref):
    group = SWA_HEADS // SWA_KV_HEADS
    rows = group * W_BLOCK
    bias = bias_ref[...]
    head_of_row = lax.broadcasted_iota(I32, (rows, 1), 0) // W_BLOCK
    for g in range(SWA_KV_HEADS):
        gs = slice(g * LANE, (g + 1) * LANE)
        vs = slice(2 * g * LANE, (2 * g + 2) * LANE)
        q_parts = []
        sink = jnp.zeros((rows, 1), F32)
        for j in range(group):
            sink = jnp.where(head_of_row == j, sink_ref[g * group + j] * LOG2E, sink)
        for p in range(group // 2):
            blk = g * (group // 2) + p
            q_parts.extend(_mask_halves(q_ref[:, blk * LANE:(blk + 1) * LANE].astype(F32)))
        q = jnp.concatenate(q_parts, axis=0)
        k = jnp.concatenate([kc_ref[:, gs], kp_ref[:, gs], k0_ref[:, gs], kn_ref[:, gs]], axis=0)
        v = jnp.concatenate([vc_ref[:, vs], vp_ref[:, vs], v0_ref[:, vs], vn_ref[:, vs]], axis=0)
        s = _dot_nt(q, k) + bias
        m = jnp.maximum(jnp.max(s, axis=-1, keepdims=True), sink)
        ov = _dot(jnp.exp2(s - m).astype(BF), v)
        o = ov[:, 0:LANE] / (ov[:, LANE:LANE + 1] + jnp.exp2(sink - m))
        for p in range(group // 2):
            blk = g * (group // 2) + p
            o_a = o[(2 * p) * W_BLOCK:(2 * p + 1) * W_BLOCK]
            o_b = o[(2 * p + 1) * W_BLOCK:(2 * p + 2) * W_BLOCK]
            o_ref[:, blk * LANE:(blk + 1) * LANE] = _pair(o_a, o_b).astype(BF)


def _lat_swa_call(sink, bias, q, k, v):
    nb = DEC_SEQ // W_BLOCK
    cb = PAST_LEN // W_BLOCK

    def ctx(width):
        return pl.BlockSpec((None, PAST_LEN, width), lambda b, n: (b, 0, 0))

    def loc(d, width):
        return pl.BlockSpec((None, W_BLOCK, width),
                            lambda b, n: (b, cb + jnp.clip(n + d, 0, nb - 1), 0))

    kw, vw = 2 * LANE, 4 * LANE
    return pl.pallas_call(
        _lat_swa_kernel,
        grid=(DEC_BATCH, nb),
        in_specs=[_smem(),
                  pl.BlockSpec((None,) + bias.shape[1:],
                               lambda b, n: (jnp.where(n == 0, 0, jnp.where(n == nb - 1, 2, 1)), 0, 0)),
                  pl.BlockSpec((None, W_BLOCK, 512), lambda b, n: (b, n, 0)),
                  ctx(kw), loc(-1, kw), loc(0, kw), loc(1, kw), ctx(vw), loc(-1, vw), loc(0, vw), loc(1, vw)],
        out_specs=pl.BlockSpec((None, W_BLOCK, 512), lambda b, n: (b, n, 0)),
        out_shape=jax.ShapeDtypeStruct((DEC_BATCH, DEC_SEQ, 512), BF),
        compiler_params=_cp(("parallel", "arbitrary"), 40), name="lat_swa",
    )(sink, bias, q, k, k, k, k, v, v, v, v)


def _rows_to_tiles(o_ref, val, tm):
    for k in range(D_CHUNKS):
        o_ref[pl.ds(k, tm, stride=D_CHUNKS), :] = val[:, k * LANE:(k + 1) * LANE]


def _post_kernel(x_ref, oda_ref, omla_ref, oswa_ref, gate_ref, mod_ref, n2_ref, wda_ref, wmla_ref, wswa_ref,
                 wo_ref, wrh_ref, wrl_ref, xo_ref, h2_ref, lg_ref):
    m = mod_ref[...]
    g = gate_ref[...].astype(F32)
    merged = (g[:, 0:D_MODEL] * _dot(oda_ref[...], wda_ref[...])
              + g[:, D_MODEL:2 * D_MODEL] * _dot(omla_ref[...], wmla_ref[...])
              + g[:, 2 * D_MODEL:] * _dot(oswa_ref[...], wswa_ref[...]))
    x = x_ref[...] + m[2:3] * _dot(merged.astype(BF), wo_ref[...])
    xo_ref[...] = x
    h2 = _rms(x, n2_ref[...]) * (1.0 + m[4:5]) + m[3:4]
    _rows_to_tiles(h2_ref, h2, TM_POST)
    hh, hl = _split(h2)
    lg_ref[...] = _dot_nt(wrh_ref[...], hh) + _dot_nt(wrl_ref[...], hh) + _dot_nt(wrh_ref[...], hl)


def _post_call(layer, x, oda, omla, oswa, gate, mod, n2, wda, wmla, wswa, wo, wrh, wrl, latent):
    n = x.shape[0]
    tm = TM_POST
    row = pl.BlockSpec((tm, D_MODEL), lambda i: (i, 0))
    o512 = pl.BlockSpec((tm, 512), lambda i: (i, 0))
    return pl.pallas_call(
        _post_kernel, grid=(n // tm,),
        in_specs=[row, o512, o512, o512, pl.BlockSpec((tm, GATE_COLS), lambda i: (i, 0)),
                  _mod_spec(layer, tm, latent), _layer(layer, (1, D_MODEL)),
                  _layer(layer, (512, D_MODEL)), _layer(layer, (512, D_MODEL)), _layer(layer, (512, D_MODEL)),
                  _layer(layer, (D_MODEL, D_MODEL)), _layer(layer, (N_EXPERTS, D_MODEL)),
                  _layer(layer, (N_EXPERTS, D_MODEL))],
        out_specs=[row, pl.BlockSpec((tm * D_CHUNKS, LANE), lambda i: (i, 0)),
                   pl.BlockSpec((N_EXPERTS, tm), lambda i: (0, i))],
        out_shape=[jax.ShapeDtypeStruct((n, D_MODEL), F32),
                   jax.ShapeDtypeStruct((n * D_CHUNKS, LANE), F32),
                   jax.ShapeDtypeStruct((N_EXPERTS, n), F32)],
        compiler_params=_cp(("parallel",), 48), name="post",
    )(x, oda, omla, oswa, gate, mod, n2, wda, wmla, wswa, wo, wrh, wrl)


def _router_kernel(cap, lg_ref, q_ref, a_ref, lo_ref, hi_ref, cnt_ref):
    e_n, n = lg_ref.shape
    lg = lg_ref[...]
    shifted = lg - jnp.max(lg, axis=0, keepdims=True)
    ex = jnp.exp(shifted)
    den = jnp.sum(ex, axis=0, keepdims=True)
    aff = ex / den
    score = shifted - jnp.log(den)
    capf = float(cap)

    def count(mask):
        return jnp.sum(jnp.where(mask, 1.0, 0.0), axis=1, keepdims=True)

    def halve(_, bounds):
        lo, hi = bounds
        mid = 0.5 * (lo + hi)
        enough = count(score >= mid) >= capf
        return jnp.where(enough, mid, lo), jnp.where(enough, hi, mid)

    lo, hi = lax.fori_loop(0, THRESHOLD_STEPS, halve,
                           (jnp.min(score, axis=1, keepdims=True), jnp.ones((e_n, 1), F32)))
    gt = score >= hi
    eq = (score >= lo) & (score < hi)
    need = capf - count(gt)
    tok = lax.broadcasted_iota(I32, (e_n, n), 1)
    n_bits = n.bit_length()

    def tie_bit(i, bound):
        cand = bound | jnp.left_shift(jnp.int32(1), n_bits - 1 - i)
        ok = (cand <= n) & (count(eq & (tok < cand)) <= need)
        return jnp.where(ok, cand, bound)

    bound = lax.fori_loop(0, n_bits, tie_bit, jnp.zeros((e_n, 1), I32))
    sel = gt | (eq & (tok < bound))
    cnt_ref[...] = jnp.broadcast_to(count(sel & (tok < TOK_HALF)), cnt_ref.shape)
    blk = 2 * LANE
    per_chunk = ROUTE_CHUNK // blk
    n_chunks = n // ROUTE_CHUNK
    upper = (lax.broadcasted_iota(I32, (blk, blk), 0) <= lax.broadcasted_iota(I32, (blk, blk), 1))
    upper = jnp.where(upper, 1.0, 0.0).astype(BF)
    lane = _lane((e_n, LANE))
    carry = jnp.zeros((e_n, 1), F32)
    c_in = jnp.zeros((e_n, LANE), F32)
    c_ex = jnp.zeros((e_n, LANE), F32)
    for c in range(n // blk):
        chunk, part = divmod(c, per_chunk)
        if part == 0:
            c_ex = jnp.where(lane == chunk, carry, c_ex)
            a_ref[chunk] = aff[:, chunk * ROUTE_CHUNK:(chunk + 1) * ROUTE_CHUNK]
        s_blk = sel[:, c * blk:(c + 1) * blk]
        rank = _dot(jnp.where(s_blk, 1.0, 0.0).astype(BF), upper) + carry
        q_ref[chunk, :, part * blk:(part + 1) * blk] = jnp.where(s_blk, rank, 0.0)
        carry = rank[:, blk - 1:blk]
        if part == per_chunk - 1:
            c_in = jnp.where(lane == chunk, carry, c_in)
    valid = lane < n_chunks
    lo = jnp.zeros((e_n, LANE), I32)
    hi = jnp.zeros((e_n, LANE), I32)
    for rt in range(cap // ROUTE_TILE):
        lo_rt = count(valid & (c_in < float(rt * ROUTE_TILE + 1)))
        hi_rt = count(valid & (c_ex < float((rt + 1) * ROUTE_TILE)))
        lo = jnp.where(lane == rt, lo_rt.astype(I32), lo)
        hi = jnp.where(lane == rt, hi_rt.astype(I32), hi)
    lo_ref[...] = lo
    hi_ref[...] = hi


def _compact_kernel(lo_ref, hi_ref, q_ref, a_ref, idx_ref, g_ref):
    e = pl.program_id(0)
    n_tiles = idx_ref.shape[0]
    eye = lax.broadcasted_iota(I32, (ROUTE_TILE, LANE), 0) == lax.broadcasted_iota(I32, (ROUTE_TILE, LANE), 1)

    def per_tile(rt, _):
        slot = (rt * ROUTE_TILE + 1 + lax.broadcasted_iota(I32, (ROUTE_TILE, 1), 0)).astype(F32)

        def per_chunk(c, acc):
            hit = q_ref[c, pl.ds(e, 1), :] == slot
            tok = (c * ROUTE_CHUNK + lax.broadcasted_iota(I32, (1, ROUTE_CHUNK), 1)).astype(F32)
            return (acc[0] + jnp.sum(jnp.where(hit, tok, 0.0), axis=1, keepdims=True),
                    acc[1] + jnp.sum(jnp.where(hit, a_ref[c, pl.ds(e, 1), :], 0.0), axis=1, keepdims=True))

        zero = jnp.zeros((ROUTE_TILE, 1), F32)
        idx_v, g_v = lax.fori_loop(lo_ref[e, rt], hi_ref[e, rt], per_chunk, (zero, zero))
        idx_ref[pl.ds(rt, 1), :] = jnp.sum(jnp.where(eye, idx_v, 0.0), axis=0, keepdims=True).astype(I32)
        g_ref[pl.ds(rt, 1), :] = jnp.sum(jnp.where(eye, g_v, 0.0), axis=0, keepdims=True)
        return 0

    lax.fori_loop(0, n_tiles, per_tile, 0)


def _router_call(logits_t, cap):
    n = logits_t.shape[1]
    n_chunks = n // ROUTE_CHUNK
    n_tiles = cap // ROUTE_TILE
    chunked = jax.ShapeDtypeStruct((n_chunks, N_EXPERTS, ROUTE_CHUNK), F32)
    q, a, lo, hi, cnt = pl.pallas_call(
        functools.partial(_router_kernel, cap),
        out_shape=[chunked, chunked, jax.ShapeDtypeStruct((N_EXPERTS, LANE), I32),
                   jax.ShapeDtypeStruct((N_EXPERTS, LANE), I32), jax.ShapeDtypeStruct((N_EXPERTS, LANE), F32)],
        compiler_params=pltpu.CompilerParams(vmem_limit_bytes=40 * 1024 * 1024), name="router",
    )(logits_t)
    whole = pl.BlockSpec((n_chunks, N_EXPERTS, ROUTE_CHUNK), lambda e, *_: (0, 0, 0))
    slots = pl.BlockSpec((None, n_tiles, ROUTE_TILE), lambda e, *_: (e, 0, 0))
    idx, g = pl.pallas_call(
        _compact_kernel,
        grid_spec=pltpu.PrefetchScalarGridSpec(
            num_scalar_prefetch=2, grid=(N_EXPERTS,), in_specs=[whole, whole], out_specs=[slots, slots]),
        out_shape=[jax.ShapeDtypeStruct((N_EXPERTS, n_tiles, ROUTE_TILE), I32),
                   jax.ShapeDtypeStruct((N_EXPERTS, n_tiles, ROUTE_TILE), F32)],
        compiler_params=_cp(("parallel",), 32), name="compact",
    )(lo[:, :n_tiles], hi[:, :n_tiles], q, a)
    return idx.reshape(N_EXPERTS, cap), g.reshape(N_EXPERTS, cap), cnt


def _row_copy(src_hbm, xe, sem, buf, tok, slot):
    dst = xe.at[buf, pl.ds(pl.multiple_of(slot * SUB, SUB), SUB)]
    return pltpu.make_async_copy(src_hbm.at[tok], dst, sem.at[buf])


def _ffn_kernel(idx_ref, idxn_ref, hc_hbm, hl_hbm, w1_ref, w3_ref, w2_ref, yc_ref, yl_ref,
                xe, xb, w1b, w3b, w2b, sem):
    e = pl.program_id(0)
    f = pl.program_id(1)
    buf = e % 2

    @pl.when((e == 0) & (f == 0))
    def _():
        def start_ctx(r, _):
            _row_copy(hc_hbm, xe, sem, 0, idx_ref[0, r], r).start()
            return 0

        def start_lat(r, _):
            _row_copy(hl_hbm, xe, sem, 0, idx_ref[0, r], r).start()
            return 0

        def wait_ctx(r, _):
            _row_copy(hc_hbm, xe, sem, 0, idx_ref[0, r], r).wait()
            return 0

        def wait_lat(r, _):
            _row_copy(hl_hbm, xe, sem, 0, idx_ref[0, r], r).wait()
            return 0

        lax.fori_loop(0, CAP_CTX, start_ctx, 0)
        lax.fori_loop(CAP_CTX, SLOTS, start_lat, 0)
        lax.fori_loop(0, CAP_CTX, wait_ctx, 0)
        lax.fori_loop(CAP_CTX, SLOTS, wait_lat, 0)

    @pl.when(f == 0)
    def _():
        for k in range(D_CHUNKS):
            xb[:, k * LANE:(k + 1) * LANE] = xe[buf, pl.ds(k, SLOTS, stride=D_CHUNKS), :].astype(BF)

    w1b[...] = w1_ref[...].astype(BF)
    w3b[...] = w3_ref[...].astype(BF)
    w2b[...] = w2_ref[...].astype(BF)

    part_c, part_l = CAP_CTX // FF_SPLIT, CAP_LAT // FF_SPLIT
    next_rows = ([(hc_hbm, f * part_c + u) for u in range(part_c)]
                 + [(hl_hbm, CAP_CTX + f * part_l + u) for u in range(part_l)])

    def next_copy(src, r):
        return _row_copy(src, xe, sem, 1 - buf, idxn_ref[0, r], r)

    n_tiles = SLOTS // FFN_ROWS
    for t in range(n_tiles):
        if t < n_tiles - 1:
            for src, r in next_rows[t::n_tiles - 1]:
                next_copy(src, r).start()
        else:
            for src, r in next_rows:
                next_copy(src, r).wait()
        rows = slice(t * FFN_ROWS, (t + 1) * FFN_ROWS)
        x = xb[rows, :]
        a = _dot(x, w1b[...])
        hid = (a * jax.nn.sigmoid(a) * _dot(x, w3b[...])).astype(BF)
        y = _dot(hid, w2b[...])
        y_ref, row0 = (yc_ref, t * FFN_ROWS) if t * FFN_ROWS < CAP_CTX else (yl_ref, t * FFN_ROWS - CAP_CTX)

        def tile(k):
            return pl.ds(row0 * D_CHUNKS + k, FFN_ROWS, stride=D_CHUNKS)

        @pl.when(f == 0)
        def _():
            for k in range(D_CHUNKS):
                y_ref[tile(k), :] = y[:, k * LANE:(k + 1) * LANE]

        @pl.when(f > 0)
        def _():
            for k in range(D_CHUNKS):
                y_ref[tile(k), :] = y_ref[tile(k), :] + y[:, k * LANE:(k + 1) * LANE]


def _ffn_call(layer, idx, h_ctx, h_lat, w1, w3, w2):
    ff = EXPERT_FF // FF_SPLIT
    wspec = pl.BlockSpec((None, None, D_MODEL, ff), lambda e, f: (layer, e, 0, f))
    last = N_EXPERTS - 1
    return pl.pallas_call(
        _ffn_kernel,
        grid=(N_EXPERTS, FF_SPLIT),
        in_specs=[pl.BlockSpec((None, 1, SLOTS), lambda e, f: (e, 0, 0), memory_space=pltpu.SMEM),
                  pl.BlockSpec((None, 1, SLOTS), lambda e, f: (jnp.minimum(e + 1, last), 0, 0),
                               memory_space=pltpu.SMEM),
                  pl.BlockSpec(memory_space=pl.ANY), pl.BlockSpec(memory_space=pl.ANY),
                  wspec, wspec, pl.BlockSpec((None, None, ff, D_MODEL), lambda e, f: (layer, e, f, 0))],
        out_specs=[pl.BlockSpec((None, CAP_CTX * D_CHUNKS, LANE), lambda e, f: (e, 0, 0)),
                   pl.BlockSpec((None, CAP_LAT * D_CHUNKS, LANE), lambda e, f: (e, 0, 0))],
        out_shape=[jax.ShapeDtypeStruct((N_EXPERTS, CAP_CTX * D_CHUNKS, LANE), F32),
                   jax.ShapeDtypeStruct((N_EXPERTS, CAP_LAT * D_CHUNKS, LANE), F32)],
        scratch_shapes=[pltpu.VMEM((2, SLOTS * D_CHUNKS, LANE), F32), pltpu.VMEM((SLOTS, D_MODEL), BF),
                        pltpu.VMEM((D_MODEL, ff), BF), pltpu.VMEM((D_MODEL, ff), BF),
                        pltpu.VMEM((ff, D_MODEL), BF), pltpu.SemaphoreType.DMA((2,))],
        compiler_params=_cp(("arbitrary", "arbitrary"), 54), name="ffn",
    )(idx, idx, h_ctx, h_lat, w1, w3, w2)


def _combine_kernel(idx_ref, g_ref, bnd_ref, yc_ref, yl_ref, acc_ref):
    h = pl.program_id(0)
    e = pl.program_id(1)

    @pl.when(e == 0)
    def _():
        acc_ref[...] = jnp.zeros(acc_ref.shape, F32)

    def add_rows(y_ref, slot0, lo, hi, base):
        def group(i, _):
            r0 = lo + i * COMBINE_GROUP
            toks = [idx_ref[0, r0 + u] - base for u in range(COMBINE_GROUP)]
            sums = [acc_ref[toks[u]] + y_ref[r0 - slot0 + u] * g_ref[0, r0 + u] for u in range(COMBINE_GROUP)]
            for u in range(COMBINE_GROUP):
                acc_ref[toks[u]] = sums[u]
            return 0

        def single(r, _):
            t = idx_ref[0, r] - base
            acc_ref[t] = acc_ref[t] + y_ref[r - slot0] * g_ref[0, r]
            return 0

        n_groups = (hi - lo) // COMBINE_GROUP
        lax.fori_loop(0, n_groups, group, 0)
        lax.fori_loop(lo + n_groups * COMBINE_GROUP, hi, single, 0)

    @pl.when(h == 0)
    def _():
        add_rows(yc_ref, 0, 0, CAP_CTX, 0)

    @pl.when(h > 0)
    def _():
        add_rows(yl_ref, CAP_CTX, bnd_ref[0, h], bnd_ref[0, h + 1], (h - 1) * TOK_HALF)


def _combine_call(idx, g, bnd, y_ctx, y_lat):
    last = N_EXPERTS - 1
    return pl.pallas_call(
        _combine_kernel,
        grid=(N_HALVES, N_EXPERTS),
        in_specs=[pl.BlockSpec((None, 1, SLOTS), lambda h, e: (e, 0, 0), memory_space=pltpu.SMEM),
                  pl.BlockSpec((None, 1, SLOTS), lambda h, e: (e, 0, 0), memory_space=pltpu.SMEM),
                  pl.BlockSpec((None, 1, N_HALVES + 1), lambda h, e: (e, 0, 0), memory_space=pltpu.SMEM),
                  pl.BlockSpec((None, CAP_CTX, SUB, LANE), lambda h, e: (jnp.where(h == 0, e, last), 0, 0, 0)),
                  pl.BlockSpec((None, CAP_LAT, SUB, LANE), lambda h, e: (jnp.where(h == 0, 0, e), 0, 0, 0))],
        out_specs=pl.BlockSpec((None, TOK_HALF, SUB, LANE), lambda h, e: (h, 0, 0, 0)),
        out_shape=jax.ShapeDtypeStruct((N_HALVES, TOK_HALF, SUB, LANE), F32),
        compiler_params=_cp(("parallel", "arbitrary"), 52), name="combine",
    )(idx, g, bnd, y_ctx.reshape(N_EXPERTS, CAP_CTX, SUB, LANE), y_lat.reshape(N_EXPERTS, CAP_LAT, SUB, LANE))


def _final_kernel(x_ref, moe_ref, mod_ref, g_ref, o_ref):
    x = x_ref[...] + mod_ref[...][5:6] * _moe_rows(moe_ref, TM_FINAL)
    o_ref[...] = _rms(x, g_ref[...])


def _final_call(x, moe, mod, g, latent):
    n = x.shape[0]
    tm = TM_FINAL
    moe_off = (N_CTX // tm) if latent else 0
    row = pl.BlockSpec((tm, D_MODEL), lambda i: (i, 0))
    return pl.pallas_call(
        _final_kernel, grid=(n // tm,),
        in_specs=[row, pl.BlockSpec((tm * D_CHUNKS, LANE), lambda i: (i + moe_off, 0)),
                  _mod_spec(DEPTH - 1, tm, latent), _layer(0, (1, D_MODEL))],
        out_specs=row, out_shape=jax.ShapeDtypeStruct((n, D_MODEL), F32),
        compiler_params=_cp(("parallel",), 32), name="final",
    )(x, moe, mod, g)


def _axial_tables(rot_dim):
    rows = DEC_SEQ // GRID_W
    row = jnp.repeat(jnp.arange(rows, dtype=F32), GRID_W)
    col = jnp.tile(jnp.arange(GRID_W, dtype=F32), rows)
    n_freq = rot_dim // 4
    inv = ROPE_BASE ** (-jnp.arange(n_freq, dtype=F32) / n_freq)
    ang = jnp.concatenate([row[:, None] * inv, col[:, None] * inv], axis=-1)
    sign = jnp.tile(jnp.array([-1.0, 1.0], F32), rot_dim // 2)
    return jnp.repeat(jnp.cos(ang), 2, axis=1), jnp.repeat(jnp.sin(ang), 2, axis=1) * sign


def _rope_tables():
    c64, s64 = _axial_tables(DA_DH)
    c32, s32 = _axial_tables(MLA_ROPE)
    one = lambda w: jnp.ones((DEC_SEQ, w), F32)
    zero = lambda w: jnp.zeros((DEC_SEQ, w), F32)
    pad = LANE - MLA_NOPE - MLA_ROPE
    return jnp.stack([
        jnp.tile(c64, (1, 2)), jnp.tile(s64, (1, 2)),
        jnp.concatenate([one(MLA_NOPE), c32, one(pad)], axis=1),
        jnp.concatenate([zero(MLA_NOPE), s32, zero(pad)], axis=1),
        jnp.concatenate([c32, one(LANE - MLA_ROPE)], axis=1),
        jnp.concatenate([s32, zero(LANE - MLA_ROPE)], axis=1)])


def _prep_weights(w_in, w_gate, mla_w_qb, mla_w_kvb, w_br_da, w_br_mla, w_br_swa, w_o, w_router):
    kr_end = C_MKR + MLA_ROPE
    win = jnp.concatenate([w_in[:, :, :kr_end].astype(BF), jnp.zeros((DEPTH, D_MODEL, LANE - MLA_ROPE), BF),
                           w_in[:, :, kr_end:].astype(BF)], axis=2)
    dk = MLA_NOPE + MLA_ROPE
    wqb = jnp.pad(mla_w_qb.reshape(DEPTH, MLA_Q_RANK, MLA_HEADS, dk), ((0, 0), (0, 0), (0, 0), (0, LANE - dk)))
    wqb = wqb.reshape(DEPTH, MLA_Q_RANK, MLA_HEADS * LANE).astype(BF)
    kvb = mla_w_kvb.reshape(DEPTH, MLA_KV_RANK, MLA_HEADS, MLA_NOPE + MLA_V)
    wk = jnp.pad(kvb[..., :MLA_NOPE], ((0, 0), (0, 0), (0, 0), (0, LANE - MLA_NOPE)))
    wk = wk.reshape(DEPTH, MLA_KV_RANK, MLA_HEADS * LANE)
    wv = kvb[..., MLA_NOPE:].reshape(DEPTH, MLA_KV_RANK, MLA_OUT)
    wkv = jnp.concatenate([wk, wv], axis=2).astype(BF)
    wrh, wrl = _split(jnp.swapaxes(w_router, 1, 2))
    return dict(win=win, wg=w_gate.astype(BF), wqb=wqb, wkv=wkv, wda=w_br_da.astype(BF),
                wmla=w_br_mla.astype(BF), wswa=w_br_swa.astype(BF), wo=w_o.astype(BF), wrh=wrh, wrl=wrl)


def kernel(x_prompt, x_sample, cache_da_k, cache_da_v, cache_mla_ckv, cache_mla_krope, cache_swa_k, cache_swa_v, c, c_ctx, w_ada, b_ada, norm1, norm2, w_in, da_lq1, da_lk1, da_lq2, da_lk2, da_subln, mla_q_norm, mla_w_qb, mla_kv_norm, mla_w_kvb, swa_sink, w_gate, w_br_da, w_br_mla, w_br_swa, w_o, w_router, w_ff1, w_ff3, w_ff2, final_norm):
    c_all = jnp.concatenate([c_ctx[None], c, jnp.zeros((SUB - 1 - DEC_BATCH, D_MODEL), F32)], axis=0)
    mod = _ada_call(c_all, w_ada, b_ada)[:, :1 + DEC_BATCH].reshape(DEPTH, 1 + DEC_BATCH, 6, D_MODEL)
    lam_all = _lam_call(da_lq1, da_lk1, da_lq2, da_lk2)
    tab = _rope_tables()
    swa_bias = _swa_bias()
    w = _prep_weights(w_in, w_gate, mla_w_qb, mla_w_kvb, w_br_da, w_br_mla, w_br_swa, w_o, w_router)
    n1, n2 = norm1[:, None], norm2[:, None]
    subln, qn, kvn = da_subln[:, None], mla_q_norm[:, None], mla_kv_norm[:, None]
    caches = (cache_da_k.reshape(DEC_BATCH, DEPTH, PAST_LEN, 512),
              cache_da_v.reshape(DEC_BATCH, DEPTH, PAST_LEN, 512),
              cache_mla_ckv,
              jnp.pad(cache_mla_krope, ((0, 0), (0, 0), (0, 0), (0, LANE - MLA_ROPE))),
              cache_swa_k.reshape(DEC_BATCH, DEPTH, PAST_LEN, LANE),
              cache_swa_v.reshape(DEC_BATCH, DEPTH, PAST_LEN, LANE))

    xp = x_prompt.reshape(N_CTX, D_MODEL)
    xs = x_sample.reshape(N_LAT, D_MODEL)
    moe = None
    new = [[] for _ in range(6)]
    for l in range(DEPTH):
        lam = lam_all[l, :1]
        sink = swa_sink[l]

        xp, proj_c, gate_c = _pre_call(l, xp, moe, mod, n1, w["win"], w["wg"], latent=False)
        xs, proj_l, gate_l = _pre_call(l, xs, moe, mod, n1, w["win"], w["wg"], latent=True)

        oda_c, omla_c, oswa_c, ckv_c = _ctx_attn_call(l, lam, sink, proj_c, subln, qn, w["wqb"], kvn, w["wkv"])
        pc = proj_c.reshape(BATCH, SEQ, PROJ_COLS)
        new[0].append(pc[..., C_DAK:C_DAK + 512].reshape(BATCH, SEQ, DA_HEADS, 2 * DA_DH))
        new[1].append(pc[..., C_DAV:C_DAV + 512].reshape(BATCH, SEQ, DA_HEADS, 2 * DA_DH))
        new[2].append(ckv_c.reshape(BATCH, SEQ, MLA_KV_RANK))
        new[3].append(pc[..., C_MKR:C_MKR + MLA_ROPE])
        new[4].append(pc[..., C_SK:C_SK + LANE].reshape(BATCH, SEQ, SWA_KV_HEADS, SWA_DH))
        new[5].append(pc[..., C_SV:C_SV + LANE].reshape(BATCH, SEQ, SWA_KV_HEADS, SWA_DH))

        daq, mlaq, swaq, dak, dav, mlak, mlav, swak, swav = _lat_prep_call(
            l, proj_l, tab, caches, qn, w["wqb"], kvn, w["wkv"])
        oda_l = _lat_da_call(l, lam, daq, dak, dav, subln).reshape(N_LAT, 512)
        omla_l = _lat_mla_call(mlaq, mlak, mlav).reshape(N_LAT, 512)
        oswa_l = _lat_swa_call(sink, swa_bias, swaq, swak, swav).reshape(N_LAT, 512)

        post_w = (w["wda"], w["wmla"], w["wswa"], w["wo"], w["wrh"], w["wrl"])
        xp, h2_c, lg_c = _post_call(l, xp, oda_c, omla_c, oswa_c, gate_c, mod, n2, *post_w, latent=False)
        xs, h2_l, lg_l = _post_call(l, xs, oda_l, omla_l, oswa_l, gate_l, mod, n2, *post_w, latent=True)

        idx_c, g_c, _ = _router_call(lg_c, CAP_CTX)
        idx_l, g_l, cnt_l = _router_call(lg_l, CAP_LAT)
        idx = jnp.concatenate([idx_c, idx_l], axis=1).reshape(N_EXPERTS, 1, SLOTS)
        gsel = jnp.concatenate([g_c, g_l], axis=1).reshape(N_EXPERTS, 1, SLOTS)
        n0 = cnt_l[:, 0].astype(I32)
        bnd = jnp.stack([jnp.zeros_like(n0), jnp.full_like(n0, CAP_CTX), CAP_CTX + n0,
                         jnp.full_like(n0, SLOTS)], axis=1)
        y_c, y_l = _ffn_call(l, idx, h2_c.reshape(N_CTX, SUB, LANE), h2_l.reshape(N_LAT, SUB, LANE),
                             w_ff1, w_ff3, w_ff2)
        acc = _combine_call(idx, gsel, bnd.reshape(N_EXPERTS, 1, N_HALVES + 1), y_c, y_l)
        moe = acc.reshape(N_HALVES * TOK_HALF * D_CHUNKS, LANE)

    fn = final_norm[None, None]
    y_prompt = _final_call(xp, moe, mod, fn, latent=False).reshape(BATCH, SEQ, D_MODEL)
    y_sample = _final_call(xs, moe, mod, fn, latent=True).reshape(DEC_BATCH, DEC_SEQ, D_MODEL)
    return (y_prompt, y_sample) + tuple(jnp.stack(n, axis=1) for n in new)
```

```python
import functools
import math

import jax
import jax.numpy as jnp
from jax import lax
from jax.experimental import pallas as pl
from jax.experimental.pallas import tpu as pltpu

F32 = jnp.float32
BF = jnp.bfloat16
I32 = jnp.int32

D_MODEL = 1024
BATCH = 16
SEQ = 256
DEPTH = 2
DEC_BATCH = 2
DEC_SEQ = 4096
PAST_LEN = 512
GRID_W = 64
ROPE_BASE = 10000.0
EPS = 1e-6
DA_HEADS = 4
DA_DH = 64
DA_OUT = DA_HEADS * 2 * DA_DH
MLA_HEADS = 8
MLA_Q_RANK = 256
MLA_KV_RANK = 128
MLA_NOPE = 64
MLA_ROPE = 32
MLA_V = 64
MLA_OUT = MLA_HEADS * MLA_V
SWA_HEADS = 8
SWA_KV_HEADS = 2
SWA_DH = 64
W_BLOCK = 128
SWA_OUT = SWA_HEADS * SWA_DH
N_EXPERTS = 16
EXPERT_FF = 1024
CAPACITY_FACTOR = 2

LANE = 128
SUB = 8
HALF = 64
N_CTX = BATCH * SEQ
N_LAT = DEC_BATCH * DEC_SEQ
KEYS = PAST_LEN + DEC_SEQ
CAP_CTX = CAPACITY_FACTOR * N_CTX // N_EXPERTS
CAP_LAT = CAPACITY_FACTOR * N_LAT // N_EXPERTS
SLOTS = CAP_CTX + CAP_LAT
TOK_HALF = 4096
N_HALVES = (N_CTX + N_LAT) // TOK_HALF
D_CHUNKS = D_MODEL // LANE

C_DAQ, C_DAK, C_DAV = 0, 512, 1024
C_MQ, C_MKV, C_MKR = 1536, 1792, 1920
C_SQ, C_SK, C_SV = 2048, 2560, 2688
PROJ_COLS = 2816
GATE_COLS = 3 * D_MODEL
MLA_SCALE = (MLA_NOPE + MLA_ROPE) ** -0.5
HEAD_SCALE = DA_DH ** -0.5
LOG2E = math.log2(math.e)

TM_PRE = 256
TM_FINAL = 512
TM_POST = 512
CHUNK = 512
TQ = 256
DA_STEP_HEADS = 4
MLA_STEP_PAIRS = 4
ROUTE_TILE = 128
ROUTE_CHUNK = 512
THRESHOLD_STEPS = 48
FFN_ROWS = 512
FF_SPLIT = 2
COMBINE_GROUP = 8


def _cp(sem, vmem_mb):
    return pltpu.CompilerParams(dimension_semantics=sem, vmem_limit_bytes=vmem_mb * 1024 * 1024)


def _dot(a, b):
    return jnp.dot(a, b, preferred_element_type=F32)


def _dot_nt(a, b):
    return lax.dot_general(a, b, (((1,), (1,)), ((), ())), preferred_element_type=F32)


def _split(a):
    hi = a.astype(BF)
    lo = (a - hi.astype(F32)).astype(BF)
    return hi, lo


def _dot3(a, w):
    ah, al = _split(a)
    wh, wl = _split(w)
    return _dot(ah, wh) + _dot(ah, wl) + _dot(al, wh)


def _rms(x, g):
    return x * lax.rsqrt(jnp.mean(x * x, axis=-1, keepdims=True) + EPS) * g


def _lane(shape):
    return lax.broadcasted_iota(I32, shape, len(shape) - 1)


def _softmax_pv(segs, sink=None):
    m = None
    for s, _ in segs:
        ms = jnp.max(s, axis=-1, keepdims=True)
        m = ms if m is None else jnp.maximum(m, ms)
    if sink is not None:
        m = jnp.maximum(m, sink)
    l = None
    o = None
    for s, v in segs:
        e = jnp.exp(s - m)
        ls = jnp.sum(e, axis=-1, keepdims=True)
        os_ = _dot(e.astype(BF), v)
        l = ls if l is None else l + ls
        o = os_ if o is None else o + os_
    if sink is not None:
        l = l + jnp.exp(sink - m)
    return o / l


def _mask_halves(q):
    lo = _lane(q.shape) < HALF
    return jnp.where(lo, q, 0.0).astype(BF), jnp.where(lo, 0.0, q).astype(BF)


def _pair(o_a, o_b):
    return jnp.where(_lane(o_a.shape) < HALF, o_a, o_b)


def _dup(x):
    r = pltpu.roll(x, HALF, 1)
    lo = _lane(x.shape) < HALF
    return jnp.where(lo, x, r), jnp.where(lo, r, x)


def _rope(x, c, s):
    n = x.shape[-1]
    even = (_lane(x.shape) % 2) == 0
    sw = jnp.where(even, pltpu.roll(x, n - 1, 1), pltpu.roll(x, 1, 1))
    return x * c + sw * s


def _da_head(q, k, v, lam, subln, lam_init):
    q1, q2 = _mask_halves(q)
    o1 = _softmax_pv([(_dot_nt(q1, k), v)])
    o2 = _softmax_pv([(_dot_nt(q2, k), v)])
    return _rms(o1 - lam * o2, subln) * (1.0 - lam_init)


def _ada_kernel(c_ref, w_ref, b_ref, o_ref):
    c = c_ref[...]
    a = c * jax.nn.sigmoid(c)
    o_ref[...] = _dot3(a, w_ref[...]) + b_ref[...]


def _ada_call(c_all, w_ada, b_ada):
    tn = 1536
    return pl.pallas_call(
        _ada_kernel,
        grid=(DEPTH, 6 * D_MODEL // tn),
        in_specs=[pl.BlockSpec((SUB, D_MODEL), lambda l, j: (0, 0)),
                  pl.BlockSpec((None, D_MODEL, tn), lambda l, j: (l, 0, j)),
                  pl.BlockSpec((None, 1, tn), lambda l, j: (l, 0, j))],
        out_specs=pl.BlockSpec((None, SUB, tn), lambda l, j: (l, 0, j)),
        out_shape=jax.ShapeDtypeStruct((DEPTH, SUB, 6 * D_MODEL), F32),
        compiler_params=_cp(("parallel", "parallel"), 40),
        name="ada",
    )(c_all, w_ada, b_ada.reshape(DEPTH, 1, 6 * D_MODEL))


def _lam_kernel(q1, k1, q2, k2, o_ref):
    s1 = jnp.sum(q1[...] * k1[...], axis=-1, keepdims=True)
    s2 = jnp.sum(q2[...] * k2[...], axis=-1, keepdims=True)
    row = lax.broadcasted_iota(I32, (DEPTH, 1), 0)
    init = jnp.zeros((DEPTH, 1), F32)
    for l in range(DEPTH):
        init = jnp.where(row == l, _lam_init(l), init)
    o_ref[...] = jnp.broadcast_to(jnp.exp(s1) - jnp.exp(s2) + init, o_ref.shape)


def _lam_init(layer):
    return 0.8 - 0.6 * math.exp(-0.3 * layer)


def _lam_call(q1, k1, q2, k2):
    return pl.pallas_call(
        _lam_kernel,
        out_shape=jax.ShapeDtypeStruct((DEPTH, LANE), F32),
        name="lam",
    )(q1, k1, q2, k2)


def _moe_rows(moe_ref, tm):
    return jnp.concatenate([moe_ref[pl.ds(k, tm, stride=D_CHUNKS), :] for k in range(D_CHUNKS)], axis=1)


def _pre_body(x, mod_ref, n1_ref, win_ref, wg_ref, proj_ref, gate_ref):
    m = mod_ref[...]
    h = _rms(x, n1_ref[...]) * (1.0 + m[1:2]) + m[0:1]
    hb = h.astype(BF)
    proj_ref[...] = _dot(hb, win_ref[...]).astype(proj_ref.dtype)
    gate_ref[...] = jax.nn.sigmoid(_dot(hb, wg_ref[...])).astype(BF)


def _pre_first_kernel(x_ref, mod_ref, n1_ref, win_ref, wg_ref, proj_ref, gate_ref):
    _pre_body(x_ref[...], mod_ref, n1_ref, win_ref, wg_ref, proj_ref, gate_ref)


def _pre_next_kernel(x_ref, moe_ref, modp_ref, mod_ref, n1_ref, win_ref, wg_ref, xo_ref, proj_ref, gate_ref):
    x = x_ref[...] + modp_ref[...][5:6] * _moe_rows(moe_ref, TM_PRE)
    xo_ref[...] = x
    _pre_body(x, mod_ref, n1_ref, win_ref, wg_ref, proj_ref, gate_ref)


def _mod_spec(layer, tm, latent):
    per = DEC_SEQ // tm
    if latent:
        return pl.BlockSpec((None, None, 6, D_MODEL), lambda i: (layer, 1 + i // per, 0, 0))
    return pl.BlockSpec((None, None, 6, D_MODEL), lambda i: (layer, 0, 0, 0))


def _layer(layer, shape):
    nd = len(shape)
    return pl.BlockSpec((None,) + shape, lambda *_: (layer,) + (0,) * nd, pipeline_mode=pl.Buffered(1))


def _pre_call(layer, x, moe, mod, n1, win, wg, latent):
    n = x.shape[0]
    tm = TM_PRE
    row = pl.BlockSpec((tm, D_MODEL), lambda i: (i, 0))
    w_specs = [_mod_spec(layer, tm, latent), _layer(layer, (1, D_MODEL)), _layer(layer, (D_MODEL, PROJ_COLS)),
               _layer(layer, (D_MODEL, GATE_COLS))]
    outs = [jax.ShapeDtypeStruct((n, PROJ_COLS), BF if latent else F32), jax.ShapeDtypeStruct((n, GATE_COLS), BF)]
    out_specs = [pl.BlockSpec((tm, PROJ_COLS), lambda i: (i, 0)), pl.BlockSpec((tm, GATE_COLS), lambda i: (i, 0))]
    if moe is None:
        proj, gate = pl.pallas_call(
            _pre_first_kernel, grid=(n // tm,), in_specs=[row] + w_specs, out_specs=out_specs, out_shape=outs,
            compiler_params=_cp(("parallel",), 52), name="pre_first",
        )(x, mod, n1, win, wg)
        return x, proj, gate
    moe_off = (N_CTX // tm) if latent else 0
    moe_spec = pl.BlockSpec((tm * D_CHUNKS, LANE), lambda i: (i + moe_off, 0))
    xo, proj, gate = pl.pallas_call(
        _pre_next_kernel, grid=(n // tm,),
        in_specs=[row, moe_spec, _mod_spec(layer - 1, tm, latent)] + w_specs,
        out_specs=[row] + out_specs,
        out_shape=[jax.ShapeDtypeStruct((n, D_MODEL), F32)] + outs,
        compiler_params=_cp(("parallel",), 52), name="pre_next",
    )(x, moe, mod, mod, n1, win, wg)
    return xo, proj, gate


def _ctx_attn_kernel(lam_init, lam_ref, sink_ref, p_ref, subln_ref, qn_ref, wqb_ref, kvn_ref, wkv_ref,
                     oda_ref, omla_ref, oswa_ref, ckv_ref):
    p = p_ref[...]
    lam = lam_ref[0]
    for h in range(DA_HEADS):
        blk = slice(h * LANE, (h + 1) * LANE)
        q = p[:, C_DAQ + h * LANE:C_DAQ + (h + 1) * LANE] * HEAD_SCALE
        k = p[:, C_DAK + h * LANE:C_DAK + (h + 1) * LANE].astype(BF)
        v = p[:, C_DAV + h * LANE:C_DAV + (h + 1) * LANE].astype(BF)
        oda_ref[:, blk] = _da_head(q, k, v, lam, subln_ref[...], lam_init).astype(BF)
    qn = _rms(p[:, C_MQ:C_MQ + MLA_Q_RANK], qn_ref[...]).astype(BF)
    cq = _dot(qn, wqb_ref[...])
    ckv = _rms(p[:, C_MKV:C_MKV + MLA_KV_RANK], kvn_ref[...])
    ckv_ref[...] = ckv
    kv = _dot(ckv.astype(BF), wkv_ref[...])
    kr_sh = pltpu.roll(p[:, C_MKR:C_MKR + LANE], HALF, 1)
    for m in range(MLA_HEADS // 2):
        vpair = kv[:, MLA_HEADS * LANE + m * LANE:MLA_HEADS * LANE + (m + 1) * LANE].astype(BF)
        outs = []
        for h in (2 * m, 2 * m + 1):
            q = (cq[:, h * LANE:(h + 1) * LANE] * MLA_SCALE).astype(BF)
            k = (kv[:, h * LANE:(h + 1) * LANE] + kr_sh).astype(BF)
            outs.append(_softmax_pv([(_dot_nt(q, k), vpair)]))
        omla_ref[:, m * LANE:(m + 1) * LANE] = _pair(outs[0], outs[1]).astype(BF)
    kd = _dup(p[:, C_SK:C_SK + LANE])
    vd = _dup(p[:, C_SV:C_SV + LANE])
    for m in range(SWA_HEADS // 2):
        g = (2 * m) // (SWA_HEADS // SWA_KV_HEADS)
        k = kd[g].astype(BF)
        v = vd[g].astype(BF)
        qa, qb = _mask_halves(p[:, C_SQ + m * LANE:C_SQ + (m + 1) * LANE] * HEAD_SCALE)
        oa = _softmax_pv([(_dot_nt(qa, k), v)], sink_ref[2 * m])
        ob = _softmax_pv([(_dot_nt(qb, k), v)], sink_ref[2 * m + 1])
        oswa_ref[:, m * LANE:(m + 1) * LANE] = _pair(oa, ob).astype(BF)


def _smem():
    return pl.BlockSpec(memory_space=pltpu.SMEM)


def _ctx_attn_call(layer, lam, sink, proj, subln, qn, wqb, kvn, wkv):
    row512 = pl.BlockSpec((SEQ, 512), lambda b: (b, 0))
    return pl.pallas_call(
        functools.partial(_ctx_attn_kernel, _lam_init(layer)),
        grid=(BATCH,),
        in_specs=[_smem(), _smem(), pl.BlockSpec((SEQ, PROJ_COLS), lambda b: (b, 0)),
                  _layer(layer, (1, LANE)), _layer(layer, (1, MLA_Q_RANK)),
                  _layer(layer, (MLA_Q_RANK, MLA_HEADS * LANE)), _layer(layer, (1, MLA_KV_RANK)),
                  _layer(layer, (MLA_KV_RANK, MLA_HEADS * LANE + MLA_OUT))],
        out_specs=[row512, row512, row512, pl.BlockSpec((SEQ, MLA_KV_RANK), lambda b: (b, 0))],
        out_shape=[jax.ShapeDtypeStruct((N_CTX, 512), BF)] * 3 + [jax.ShapeDtypeStruct((N_CTX, MLA_KV_RANK), F32)],
        compiler_params=_cp(("parallel",), 40), name="ctx_attn",
    )(lam, sink, proj, subln, qn, wqb, kvn, wkv)


def _lat_prep_kernel(p_ref, tab_ref, cdk_ref, cdv_ref, cckv_ref, ckr_ref, csk_ref, csv_ref,
                     qn_ref, wqb_ref, kvn_ref, wkv_ref,
                     daq_ref, mlaq_ref, swaq_ref, dak_ref, dav_ref, mlak_ref, mlav_ref, swak_ref, swav_ref):
    j = pl.program_id(1)

    def write_kv(dk, dv, ckv, kr_sh, sk, sv):
        ones_col = jnp.where(_lane((CHUNK, LANE)) == 0, 1.0, 0.0).astype(BF)
        dak_ref[...] = dk.astype(BF)
        for h in range(DA_HEADS):
            dav_ref[:, 2 * h * LANE:(2 * h + 1) * LANE] = dv[:, h * LANE:(h + 1) * LANE].astype(BF)
            dav_ref[:, (2 * h + 1) * LANE:(2 * h + 2) * LANE] = ones_col
        kv = _dot(ckv.astype(BF), wkv_ref[...])
        for h in range(MLA_HEADS):
            mlak_ref[:, h * LANE:(h + 1) * LANE] = (kv[:, h * LANE:(h + 1) * LANE] + kr_sh).astype(BF)
        for p in range(MLA_HEADS // 2):
            mlav_ref[:, 2 * p * LANE:(2 * p + 1) * LANE] = kv[:, (MLA_HEADS + p) * LANE:(MLA_HEADS + p + 1) * LANE].astype(BF)
            mlav_ref[:, (2 * p + 1) * LANE:(2 * p + 2) * LANE] = ones_col
        k0, k1 = _dup(sk)
        v0, v1 = _dup(sv)
        swak_ref[:, 0:LANE] = k0.astype(BF)
        swak_ref[:, LANE:2 * LANE] = k1.astype(BF)
        swav_ref[:, 0:LANE] = v0.astype(BF)
        swav_ref[:, LANE:2 * LANE] = ones_col
        swav_ref[:, 2 * LANE:3 * LANE] = v1.astype(BF)
        swav_ref[:, 3 * LANE:4 * LANE] = ones_col

    @pl.when(j == 0)
    def _():
        write_kv(cdk_ref[...], cdv_ref[...], cckv_ref[...], pltpu.roll(ckr_ref[...], HALF, 1),
                 csk_ref[...], csv_ref[...])

    @pl.when(j > 0)
    def _():
        c64, s64 = tab_ref[0], tab_ref[1]
        cmq, smq = tab_ref[2], tab_ref[3]
        ckr, skr = tab_ref[4], tab_ref[5]

        def cols(c0, width):
            return p_ref[:, c0:c0 + width].astype(F32)

        def blk(c0, h):
            return cols(c0 + h * LANE, LANE)

        for h in range(DA_HEADS):
            daq_ref[:, h * LANE:(h + 1) * LANE] = (_rope(blk(C_DAQ, h), c64, s64) * (HEAD_SCALE * LOG2E)).astype(BF)
            swaq_ref[:, h * LANE:(h + 1) * LANE] = (_rope(blk(C_SQ, h), c64, s64) * (HEAD_SCALE * LOG2E)).astype(BF)
        dk = jnp.concatenate([_rope(blk(C_DAK, h), c64, s64) for h in range(DA_HEADS)], axis=1)
        qn = _rms(cols(C_MQ, MLA_Q_RANK), qn_ref[...]).astype(BF)
        cq = _dot(qn, wqb_ref[...])
        for h in range(MLA_HEADS):
            mlaq_ref[:, h * LANE:(h + 1) * LANE] = (
                _rope(cq[:, h * LANE:(h + 1) * LANE], cmq, smq) * (MLA_SCALE * LOG2E)).astype(BF)
        ckv = _rms(cols(C_MKV, MLA_KV_RANK), kvn_ref[...])
        kr_sh = pltpu.roll(_rope(cols(C_MKR, LANE), ckr, skr), HALF, 1)
        sk = _rope(cols(C_SK, LANE), c64, s64)
        write_kv(dk, cols(C_DAV, 512), ckv, kr_sh, sk, cols(C_SV, LANE))


def _lat_prep_call(layer, proj, tab, caches, qn, wqb, kvn, wkv):
    cdk, cdv, cckv, ckr, csk, csv = caches
    nj = 1 + DEC_SEQ // CHUNK

    def own(width):
        return pl.BlockSpec((None, CHUNK, width), lambda b, j: (b, jnp.maximum(j - 1, 0), 0))

    def cache(width):
        return pl.BlockSpec((None, None, PAST_LEN, width), lambda b, j: (b, layer, 0, 0))

    def allk(width):
        return pl.BlockSpec((None, CHUNK, width), lambda b, j: (b, j, 0))

    def shp(rows, width):
        return jax.ShapeDtypeStruct((DEC_BATCH, rows, width), BF)

    return pl.pallas_call(
        _lat_prep_kernel,
        grid=(DEC_BATCH, nj),
        in_specs=[own(PROJ_COLS),
                  pl.BlockSpec((6, CHUNK, LANE), lambda b, j: (0, jnp.maximum(j - 1, 0), 0)),
                  cache(512), cache(512), cache(LANE), cache(LANE), cache(LANE), cache(LANE),
                  _layer(layer, (1, MLA_Q_RANK)), _layer(layer, (MLA_Q_RANK, MLA_HEADS * LANE)),
                  _layer(layer, (1, MLA_KV_RANK)), _layer(layer, (MLA_KV_RANK, MLA_HEADS * LANE + MLA_OUT))],
        out_specs=[own(512), own(MLA_HEADS * LANE), own(512),
                   allk(512), allk(1024), allk(MLA_HEADS * LANE), allk(1024), allk(2 * LANE), allk(4 * LANE)],
        out_shape=[shp(DEC_SEQ, 512), shp(DEC_SEQ, MLA_HEADS * LANE), shp(DEC_SEQ, 512),
                   shp(KEYS, 512), shp(KEYS, 1024), shp(KEYS, MLA_HEADS * LANE), shp(KEYS, 1024),
                   shp(KEYS, 2 * LANE), shp(KEYS, 4 * LANE)],
        compiler_params=_cp(("parallel", "arbitrary"), 48), name="lat_prep",
    )(proj.reshape(DEC_BATCH, DEC_SEQ, PROJ_COLS), tab, cdk, cdv, cckv, ckr, csk, csv, qn, wqb, kvn, wkv)


def _softmax2_pv(s, v_aug):
    e = jnp.exp2(s - jnp.max(s, axis=-1, keepdims=True)).astype(BF)
    o = _dot(e, v_aug)
    return o[:, 0:LANE] / o[:, LANE:LANE + 1]


def _lat_da_kernel(lam_init, lam_ref, q_ref, k_ref, v_ref, subln_ref, o_ref):
    for h in range(DA_STEP_HEADS):
        cols = slice(h * LANE, (h + 1) * LANE)
        q1, q2 = _mask_halves(q_ref[:, cols].astype(F32))
        k = k_ref[:, cols]
        v = v_ref[:, 2 * h * LANE:(2 * h + 2) * LANE]
        o = _softmax2_pv(_dot_nt(q1, k), v) - lam_ref[0] * _softmax2_pv(_dot_nt(q2, k), v)
        o_ref[:, cols] = (_rms(o, subln_ref[...]) * (1.0 - lam_init)).astype(BF)


def _lat_da_call(layer, lam, q, k, v, subln):
    return pl.pallas_call(
        functools.partial(_lat_da_kernel, _lam_init(layer)),
        grid=(DEC_BATCH, DA_HEADS // DA_STEP_HEADS, DEC_SEQ // TQ),
        in_specs=[_smem(),
                  pl.BlockSpec((None, TQ, DA_STEP_HEADS * LANE), lambda b, h, i: (b, i, h)),
                  pl.BlockSpec((None, KEYS, DA_STEP_HEADS * LANE), lambda b, h, i: (b, 0, h)),
                  pl.BlockSpec((None, KEYS, 2 * DA_STEP_HEADS * LANE), lambda b, h, i: (b, 0, h)),
                  _layer(layer, (1, LANE))],
        out_specs=pl.BlockSpec((None, TQ, DA_STEP_HEADS * LANE), lambda b, h, i: (b, i, h)),
        out_shape=jax.ShapeDtypeStruct((DEC_BATCH, DEC_SEQ, 512), BF),
        compiler_params=_cp(("parallel", "parallel", "arbitrary"), 56), name="lat_da",
    )(lam, q, k, v, subln)


def _lat_mla_kernel(q_ref, k_ref, v_ref, o_ref):
    for p in range(MLA_STEP_PAIRS):
        v = v_ref[:, 2 * p * LANE:(2 * p + 2) * LANE]
        outs = []
        for h in (2 * p, 2 * p + 1):
            cols = slice(h * LANE, (h + 1) * LANE)
            outs.append(_softmax2_pv(_dot_nt(q_ref[:, cols], k_ref[:, cols]), v))
        o_ref[:, p * LANE:(p + 1) * LANE] = _pair(outs[0], outs[1]).astype(BF)


def _lat_mla_call(q, k, v):
    sp = MLA_STEP_PAIRS
    return pl.pallas_call(
        _lat_mla_kernel,
        grid=(DEC_BATCH, MLA_HEADS // (2 * sp), DEC_SEQ // TQ),
        in_specs=[pl.BlockSpec((None, TQ, 2 * sp * LANE), lambda b, m, i: (b, i, m)),
                  pl.BlockSpec((None, KEYS, 2 * sp * LANE), lambda b, m, i: (b, 0, m)),
                  pl.BlockSpec((None, KEYS, 2 * sp * LANE), lambda b, m, i: (b, 0, m))],
        out_specs=pl.BlockSpec((None, TQ, sp * LANE), lambda b, m, i: (b, i, m)),
        out_shape=jax.ShapeDtypeStruct((DEC_BATCH, DEC_SEQ, 512), BF),
        compiler_params=_cp(("parallel", "parallel", "arbitrary"), 56), name="lat_mla",
    )(q, k, v)


def _swa_bias():
    rows = (SWA_HEADS // SWA_KV_HEADS) * W_BLOCK
    qi = (jnp.arange(rows) % W_BLOCK)[:, None]
    loc = (jnp.arange(PAST_LEN + 3 * W_BLOCK) - PAST_LEN)[None, :]
    always = (loc < 0) | ((loc >= W_BLOCK) & (loc < 2 * W_BLOCK))
    prev = (loc >= 0) & (loc < W_BLOCK) & (loc >= qi)
    nxt = (loc >= 2 * W_BLOCK) & (loc - 2 * W_BLOCK <= qi)
    ok = jnp.stack([always | nxt, always | prev | nxt, always | prev])
    return jnp.where(ok, 0.0, -jnp.inf).astype(F32)


def _lat_swa_kernel(sink_ref, bias_ref, q_ref, kc_ref, kp_ref, k0_ref, kn_ref, vc_ref, vp_ref, v0_ref, vn_ref,
                    o_ref):
    group = SWA_HEADS // SWA_KV_HEADS
    rows = group * W_BLOCK
    bias = bias_ref[...]
    head_of_row = lax.broadcasted_iota(I32, (rows, 1), 0) // W_BLOCK
    for g in range(SWA_KV_HEADS):
        gs = slice(g * LANE, (g + 1) * LANE)
        vs = slice(2 * g * LANE, (2 * g + 2) * LANE)
        q_parts = []
        sink = jnp.zeros((rows, 1), F32)
        for j in range(group):
            sink = jnp.where(head_of_row == j, sink_ref[g * group + j] * LOG2E, sink)
        for p in range(group // 2):
            blk = g * (group // 2) + p
            q_parts.extend(_mask_halves(q_ref[:, blk * LANE:(blk + 1) * LANE].astype(F32)))
        q = jnp.concatenate(q_parts, axis=0)
        k = jnp.concatenate([kc_ref[:, gs], kp_ref[:, gs], k0_ref[:, gs], kn_ref[:, gs]], axis=0)
        v = jnp.concatenate([vc_ref[:, vs], vp_ref[:, vs], v0_ref[:, vs], vn_ref[:, vs]], axis=0)
        s = _dot_nt(q, k) + bias
        m = jnp.maximum(jnp.max(s, axis=-1, keepdims=True), sink)
        ov = _dot(jnp.exp2(s - m).astype(BF), v)
        o = ov[:, 0:LANE] / (ov[:, LANE:LANE + 1] + jnp.exp2(sink - m))
        for p in range(group // 2):
            blk = g * (group // 2) + p
            o_a = o[(2 * p) * W_BLOCK:(2 * p + 1) * W_BLOCK]
            o_b = o[(2 * p + 1) * W_BLOCK:(2 * p + 2) * W_BLOCK]
            o_ref[:, blk * LANE:(blk + 1) * LANE] = _pair(o_a, o_b).astype(BF)


def _lat_swa_call(sink, bias, q, k, v):
    nb = DEC_SEQ // W_BLOCK
    cb = PAST_LEN // W_BLOCK

    def ctx(width):
        return pl.BlockSpec((None, PAST_LEN, width), lambda b, n: (b, 0, 0))

    def loc(d, width):
        return pl.BlockSpec((None, W_BLOCK, width),
                            lambda b, n: (b, cb + jnp.clip(n + d, 0, nb - 1), 0))

    kw, vw = 2 * LANE, 4 * LANE
    return pl.pallas_call(
        _lat_swa_kernel,
        grid=(DEC_BATCH, nb),
        in_specs=[_smem(),
                  pl.BlockSpec((None,) + bias.shape[1:],
                               lambda b, n: (jnp.where(n == 0, 0, jnp.where(n == nb - 1, 2, 1)), 0, 0)),
                  pl.BlockSpec((None, W_BLOCK, 512), lambda b, n: (b, n, 0)),
                  ctx(kw), loc(-1, kw), loc(0, kw), loc(1, kw), ctx(vw), loc(-1, vw), loc(0, vw), loc(1, vw)],
        out_specs=pl.BlockSpec((None, W_BLOCK, 512), lambda b, n: (b, n, 0)),
        out_shape=jax.ShapeDtypeStruct((DEC_BATCH, DEC_SEQ, 512), BF),
        compiler_params=_cp(("parallel", "arbitrary"), 40), name="lat_swa",
    )(sink, bias, q, k, k, k, k, v, v, v, v)


def _rows_to_tiles(o_ref, val, tm):
    for k in range(D_CHUNKS):
        o_ref[pl.ds(k, tm, stride=D_CHUNKS), :] = val[:, k * LANE:(k + 1) * LANE]


def _post_kernel(x_ref, oda_ref, omla_ref, oswa_ref, gate_ref, mod_ref, n2_ref, wda_ref, wmla_ref, wswa_ref,
                 wo_ref, wrh_ref, wrl_ref, xo_ref, h2_ref, lg_ref):
    m = mod_ref[...]
    g = gate_ref[...].astype(F32)
    merged = (g[:, 0:D_MODEL] * _dot(oda_ref[...], wda_ref[...])
              + g[:, D_MODEL:2 * D_MODEL] * _dot(omla_ref[...], wmla_ref[...])
              + g[:, 2 * D_MODEL:] * _dot(oswa_ref[...], wswa_ref[...]))
    x = x_ref[...] + m[2:3] * _dot(merged.astype(BF), wo_ref[...])
    xo_ref[...] = x
    h2 = _rms(x, n2_ref[...]) * (1.0 + m[4:5]) + m[3:4]
    _rows_to_tiles(h2_ref, h2, TM_POST)
    hh, hl = _split(h2)
    lg_ref[...] = _dot_nt(wrh_ref[...], hh) + _dot_nt(wrl_ref[...], hh) + _dot_nt(wrh_ref[...], hl)


def _post_call(layer, x, oda, omla, oswa, gate, mod, n2, wda, wmla, wswa, wo, wrh, wrl, latent):
    n = x.shape[0]
    tm = TM_POST
    row = pl.BlockSpec((tm, D_MODEL), lambda i: (i, 0))
    o512 = pl.BlockSpec((tm, 512), lambda i: (i, 0))
    return pl.pallas_call(
        _post_kernel, grid=(n // tm,),
        in_specs=[row, o512, o512, o512, pl.BlockSpec((tm, GATE_COLS), lambda i: (i, 0)),
                  _mod_spec(layer, tm, latent), _layer(layer, (1, D_MODEL)),
                  _layer(layer, (512, D_MODEL)), _layer(layer, (512, D_MODEL)), _layer(layer, (512, D_MODEL)),
                  _layer(layer, (D_MODEL, D_MODEL)), _layer(layer, (N_EXPERTS, D_MODEL)),
                  _layer(layer, (N_EXPERTS, D_MODEL))],
        out_specs=[row, pl.BlockSpec((tm * D_CHUNKS, LANE), lambda i: (i, 0)),
                   pl.BlockSpec((N_EXPERTS, tm), lambda i: (0, i))],
        out_shape=[jax.ShapeDtypeStruct((n, D_MODEL), F32),
                   jax.ShapeDtypeStruct((n * D_CHUNKS, LANE), F32),
                   jax.ShapeDtypeStruct((N_EXPERTS, n), F32)],
        compiler_params=_cp(("parallel",), 48), name="post",
    )(x, oda, omla, oswa, gate, mod, n2, wda, wmla, wswa, wo, wrh, wrl)


def _router_kernel(cap, lg_ref, q_ref, a_ref, lo_ref, hi_ref, cnt_ref):
    e_n, n = lg_ref.shape
    lg = lg_ref[...]
    shifted = lg - jnp.max(lg, axis=0, keepdims=True)
    ex = jnp.exp(shifted)
    den = jnp.sum(ex, axis=0, keepdims=True)
    aff = ex / den
    score = shifted - jnp.log(den)
    capf = float(cap)

    def count(mask):
        return jnp.sum(jnp.where(mask, 1.0, 0.0), axis=1, keepdims=True)

    def halve(_, bounds):
        lo, hi = bounds
        mid = 0.5 * (lo + hi)
        enough = count(score >= mid) >= capf
        return jnp.where(enough, mid, lo), jnp.where(enough, hi, mid)

    lo, hi = lax.fori_loop(0, THRESHOLD_STEPS, halve,
                           (jnp.min(score, axis=1, keepdims=True), jnp.ones((e_n, 1), F32)))
    gt = score >= hi
    eq = (score >= lo) & (score < hi)
    need = capf - count(gt)
    tok = lax.broadcasted_iota(I32, (e_n, n), 1)
    n_bits = n.bit_length()

    def tie_bit(i, bound):
        cand = bound | jnp.left_shift(jnp.int32(1), n_bits - 1 - i)
        ok = (cand <= n) & (count(eq & (tok < cand)) <= need)
        return jnp.where(ok, cand, bound)

    bound = lax.fori_loop(0, n_bits, tie_bit, jnp.zeros((e_n, 1), I32))
    sel = gt | (eq & (tok < bound))
    cnt_ref[...] = jnp.broadcast_to(count(sel & (tok < TOK_HALF)), cnt_ref.shape)
    blk = 2 * LANE
    per_chunk = ROUTE_CHUNK // blk
    n_chunks = n // ROUTE_CHUNK
    upper = (lax.broadcasted_iota(I32, (blk, blk), 0) <= lax.broadcasted_iota(I32, (blk, blk), 1))
    upper = jnp.where(upper, 1.0, 0.0).astype(BF)
    lane = _lane((e_n, LANE))
    carry = jnp.zeros((e_n, 1), F32)
    c_in = jnp.zeros((e_n, LANE), F32)
    c_ex = jnp.zeros((e_n, LANE), F32)
    for c in range(n // blk):
        chunk, part = divmod(c, per_chunk)
        if part == 0:
            c_ex = jnp.where(lane == chunk, carry, c_ex)
            a_ref[chunk] = aff[:, chunk * ROUTE_CHUNK:(chunk + 1) * ROUTE_CHUNK]
        s_blk = sel[:, c * blk:(c + 1) * blk]
        rank = _dot(jnp.where(s_blk, 1.0, 0.0).astype(BF), upper) + carry
        q_ref[chunk, :, part * blk:(part + 1) * blk] = jnp.where(s_blk, rank, 0.0)
        carry = rank[:, blk - 1:blk]
        if part == per_chunk - 1:
            c_in = jnp.where(lane == chunk, carry, c_in)
    valid = lane < n_chunks
    lo = jnp.zeros((e_n, LANE), I32)
    hi = jnp.zeros((e_n, LANE), I32)
    for rt in range(cap // ROUTE_TILE):
        lo_rt = count(valid & (c_in < float(rt * ROUTE_TILE + 1)))
        hi_rt = count(valid & (c_ex < float((rt + 1) * ROUTE_TILE)))
        lo = jnp.where(lane == rt, lo_rt.astype(I32), lo)
        hi = jnp.where(lane == rt, hi_rt.astype(I32), hi)
    lo_ref[...] = lo
    hi_ref[...] = hi


def _compact_kernel(lo_ref, hi_ref, q_ref, a_ref, idx_ref, g_ref):
    e = pl.program_id(0)
    n_tiles = idx_ref.shape[0]
    eye = lax.broadcasted_iota(I32, (ROUTE_TILE, LANE), 0) == lax.broadcasted_iota(I32, (ROUTE_TILE, LANE), 1)

    def per_tile(rt, _):
        slot = (rt * ROUTE_TILE + 1 + lax.broadcasted_iota(I32, (ROUTE_TILE, 1), 0)).astype(F32)

        def per_chunk(c, acc):
            hit = q_ref[c, pl.ds(e, 1), :] == slot
            tok = (c * ROUTE_CHUNK + lax.broadcasted_iota(I32, (1, ROUTE_CHUNK), 1)).astype(F32)
            return (acc[0] + jnp.sum(jnp.where(hit, tok, 0.0), axis=1, keepdims=True),
                    acc[1] + jnp.sum(jnp.where(hit, a_ref[c, pl.ds(e, 1), :], 0.0), axis=1, keepdims=True))

        zero = jnp.zeros((ROUTE_TILE, 1), F32)
        idx_v, g_v = lax.fori_loop(lo_ref[e, rt], hi_ref[e, rt], per_chunk, (zero, zero))
        idx_ref[pl.ds(rt, 1), :] = jnp.sum(jnp.where(eye, idx_v, 0.0), axis=0, keepdims=True).astype(I32)
        g_ref[pl.ds(rt, 1), :] = jnp.sum(jnp.where(eye, g_v, 0.0), axis=0, keepdims=True)
        return 0

    lax.fori_loop(0, n_tiles, per_tile, 0)


def _router_call(logits_t, cap):
    n = logits_t.shape[1]
    n_chunks = n // ROUTE_CHUNK
    n_tiles = cap // ROUTE_TILE
    chunked = jax.ShapeDtypeStruct((n_chunks, N_EXPERTS, ROUTE_CHUNK), F32)
    q, a, lo, hi, cnt = pl.pallas_call(
        functools.partial(_router_kernel, cap),
        out_shape=[chunked, chunked, jax.ShapeDtypeStruct((N_EXPERTS, LANE), I32),
                   jax.ShapeDtypeStruct((N_EXPERTS, LANE), I32), jax.ShapeDtypeStruct((N_EXPERTS, LANE), F32)],
        compiler_params=pltpu.CompilerParams(vmem_limit_bytes=40 * 1024 * 1024), name="router",
    )(logits_t)
    whole = pl.BlockSpec((n_chunks, N_EXPERTS, ROUTE_CHUNK), lambda e, *_: (0, 0, 0))
    slots = pl.BlockSpec((None, n_tiles, ROUTE_TILE), lambda e, *_: (e, 0, 0))
    idx, g = pl.pallas_call(
        _compact_kernel,
        grid_spec=pltpu.PrefetchScalarGridSpec(
            num_scalar_prefetch=2, grid=(N_EXPERTS,), in_specs=[whole, whole], out_specs=[slots, slots]),
        out_shape=[jax.ShapeDtypeStruct((N_EXPERTS, n_tiles, ROUTE_TILE), I32),
                   jax.ShapeDtypeStruct((N_EXPERTS, n_tiles, ROUTE_TILE), F32)],
        compiler_params=_cp(("parallel",), 32), name="compact",
    )(lo[:, :n_tiles], hi[:, :n_tiles], q, a)
    return idx.reshape(N_EXPERTS, cap), g.reshape(N_EXPERTS, cap), cnt


def _row_copy(src_hbm, xe, sem, buf, tok, slot):
    dst = xe.at[buf, pl.ds(pl.multiple_of(slot * SUB, SUB), SUB)]
    return pltpu.make_async_copy(src_hbm.at[tok], dst, sem.at[buf])


def _rows_wait(rows_hbm, xe, sem, buf, slot, n):
    dst = xe.at[buf, pl.ds(pl.multiple_of(slot * SUB, SUB), n * SUB)]
    return pltpu.make_async_copy(rows_hbm.at[pl.ds(0, n * SUB)], dst, sem.at[buf])


def _ffn_kernel(idx_ref, idxn_ref, hc_hbm, hl_hbm, rows_hbm, w1_ref, w3_ref, w2_ref, yc_ref, yl_ref,
                xe, xb, w1b, w3b, w2b, sem):
    e = pl.program_id(0)
    f = pl.program_id(1)
    buf = e % 2

    @pl.when((e == 0) & (f == 0))
    def _():
        def start_ctx(r, _):
            _row_copy(hc_hbm, xe, sem, 0, idx_ref[0, r], r).start()
            return 0

        def start_lat(r, _):
            _row_copy(hl_hbm, xe, sem, 0, idx_ref[0, r], r).start()
            return 0

        lax.fori_loop(0, CAP_CTX, start_ctx, 0)
        lax.fori_loop(CAP_CTX, SLOTS, start_lat, 0)
        _rows_wait(rows_hbm, xe, sem, 0, 0, SLOTS).wait()

    @pl.when(f == 0)
    def _():
        for k in range(D_CHUNKS):
            xb[:, k * LANE:(k + 1) * LANE] = xe[buf, pl.ds(k, SLOTS, stride=D_CHUNKS), :].astype(BF)

    w1b[...] = w1_ref[...].astype(BF)
    w3b[...] = w3_ref[...].astype(BF)
    w2b[...] = w2_ref[...].astype(BF)

    part_c, part_l = CAP_CTX // FF_SPLIT, CAP_LAT // FF_SPLIT
    next_rows = ([(hc_hbm, f * part_c + u) for u in range(part_c)]
                 + [(hl_hbm, CAP_CTX + f * part_l + u) for u in range(part_l)])

    def next_copy(src, r):
        return _row_copy(src, xe, sem, 1 - buf, idxn_ref[0, r], r)

    n_tiles = SLOTS // FFN_ROWS
    for t in range(n_tiles):
        if t < n_tiles - 1:
            for src, r in next_rows[t::n_tiles - 1]:
                next_copy(src, r).start()
        rows = slice(t * FFN_ROWS, (t + 1) * FFN_ROWS)
        x = xb[rows, :]
        a = _dot(x, w1b[...])
        hid = (a * jax.nn.sigmoid(a) * _dot(x, w3b[...])).astype(BF)
        y = _dot(hid, w2b[...])
        lo, hi = t * FFN_ROWS, (t + 1) * FFN_ROWS
        segs = []
        if lo < CAP_CTX:
            segs.append((yc_ref, lo, 0, min(hi, CAP_CTX) - lo))
        if hi > CAP_CTX:
            first = max(lo, CAP_CTX)
            segs.append((yl_ref, first - CAP_CTX, first - lo, hi - first))

        def pieces():
            for y_ref, row0, y0, n in segs:
                for k in range(D_CHUNKS):
                    yield y_ref, pl.ds(row0 * D_CHUNKS + k, n, stride=D_CHUNKS), y[y0:y0 + n, k * LANE:(k + 1) * LANE]

        @pl.when(f == 0)
        def _():
            for y_ref, dst, val in pieces():
                y_ref[dst, :] = val

        @pl.when(f > 0)
        def _():
            for y_ref, dst, val in pieces():
                y_ref[dst, :] = y_ref[dst, :] + val

    _rows_wait(rows_hbm, xe, sem, 1 - buf, f * part_c, part_c).wait()
    _rows_wait(rows_hbm, xe, sem, 1 - buf, CAP_CTX + f * part_l, part_l).wait()


def _ffn_call(layer, idx, h_ctx, h_lat, w1, w3, w2):
    ff = EXPERT_FF // FF_SPLIT
    wspec = pl.BlockSpec((None, None, D_MODEL, ff), lambda e, f: (layer, e, 0, f))
    last = N_EXPERTS - 1
    return pl.pallas_call(
        _ffn_kernel,
        grid=(N_EXPERTS, FF_SPLIT),
        in_specs=[pl.BlockSpec((None, 1, SLOTS), lambda e, f: (e, 0, 0), memory_space=pltpu.SMEM),
                  pl.BlockSpec((None, 1, SLOTS), lambda e, f: (jnp.minimum(e + 1, last), 0, 0),
                               memory_space=pltpu.SMEM),
                  pl.BlockSpec(memory_space=pl.ANY), pl.BlockSpec(memory_space=pl.ANY),
                  pl.BlockSpec(memory_space=pl.ANY), wspec, wspec, pl.BlockSpec((None, None, ff, D_MODEL), lambda e, f: (layer, e, f, 0))],
        out_specs=[pl.BlockSpec((None, CAP_CTX * D_CHUNKS, LANE), lambda e, f: (e, 0, 0)),
                   pl.BlockSpec((None, CAP_LAT * D_CHUNKS, LANE), lambda e, f: (e, 0, 0))],
        out_shape=[jax.ShapeDtypeStruct((N_EXPERTS, CAP_CTX * D_CHUNKS, LANE), F32),
                   jax.ShapeDtypeStruct((N_EXPERTS, CAP_LAT * D_CHUNKS, LANE), F32)],
        scratch_shapes=[pltpu.VMEM((2, SLOTS * D_CHUNKS, LANE), F32), pltpu.VMEM((SLOTS, D_MODEL), BF),
                        pltpu.VMEM((D_MODEL, ff), BF), pltpu.VMEM((D_MODEL, ff), BF),
                        pltpu.VMEM((ff, D_MODEL), BF), pltpu.SemaphoreType.DMA((2,))],
        compiler_params=_cp(("arbitrary", "arbitrary"), 54), name="ffn",
    )(idx, idx, h_ctx, h_lat, h_lat.reshape(N_LAT * SUB, LANE), w1, w3, w2)


def _combine_kernel(idx_ref, g_ref, bnd_ref, yc_ref, yl_ref, acc_ref):
    h = pl.program_id(0)
    e = pl.program_id(1)

    @pl.when(e == 0)
    def _():
        acc_ref[...] = jnp.zeros(acc_ref.shape, F32)

    def add_rows(y_ref, slot0, lo, hi, base):
        def group(i, _):
            r0 = lo + i * COMBINE_GROUP
            toks = [idx_ref[0, r0 + u] - base for u in range(COMBINE_GROUP)]
            sums = [acc_ref[toks[u]] + y_ref[r0 - slot0 + u] * g_ref[0, r0 + u] for u in range(COMBINE_GROUP)]
            for u in range(COMBINE_GROUP):
                acc_ref[toks[u]] = sums[u]
            return 0

        def single(r, _):
            t = idx_ref[0, r] - base
            acc_ref[t] = acc_ref[t] + y_ref[r - slot0] * g_ref[0, r]
            return 0

        n_groups = (hi - lo) // COMBINE_GROUP
        lax.fori_loop(0, n_groups, group, 0)
        lax.fori_loop(lo + n_groups * COMBINE_GROUP, hi, single, 0)

    @pl.when(h == 0)
    def _():
        add_rows(yc_ref, 0, 0, CAP_CTX, 0)

    @pl.when(h > 0)
    def _():
        add_rows(yl_ref, CAP_CTX, bnd_ref[0, h], bnd_ref[0, h + 1], (h - 1) * TOK_HALF)


def _combine_call(idx, g, bnd, y_ctx, y_lat):
    last = N_EXPERTS - 1
    return pl.pallas_call(
        _combine_kernel,
        grid=(N_HALVES, N_EXPERTS),
        in_specs=[pl.BlockSpec((None, 1, SLOTS), lambda h, e: (e, 0, 0), memory_space=pltpu.SMEM),
                  pl.BlockSpec((None, 1, SLOTS), lambda h, e: (e, 0, 0), memory_space=pltpu.SMEM),
                  pl.BlockSpec((None, 1, N_HALVES + 1), lambda h, e: (e, 0, 0), memory_space=pltpu.SMEM),
                  pl.BlockSpec((None, CAP_CTX, SUB, LANE), lambda h, e: (jnp.where(h == 0, e, last), 0, 0, 0)),
                  pl.BlockSpec((None, CAP_LAT, SUB, LANE), lambda h, e: (jnp.where(h == 0, 0, e), 0, 0, 0))],
        out_specs=pl.BlockSpec((None, TOK_HALF, SUB, LANE), lambda h, e: (h, 0, 0, 0)),
        out_shape=jax.ShapeDtypeStruct((N_HALVES, TOK_HALF, SUB, LANE), F32),
        compiler_params=_cp(("parallel", "arbitrary"), 52), name="combine",
    )(idx, g, bnd, y_ctx.reshape(N_EXPERTS, CAP_CTX, SUB, LANE), y_lat.reshape(N_EXPERTS, CAP_LAT, SUB, LANE))


def _final_kernel(x_ref, moe_ref, mod_ref, g_ref, o_ref):
    x = x_ref[...] + mod_ref[...][5:6] * _moe_rows(moe_ref, TM_FINAL)
    o_ref[...] = _rms(x, g_ref[...])


def _final_call(x, moe, mod, g, latent):
    n = x.shape[0]
    tm = TM_FINAL
    moe_off = (N_CTX // tm) if latent else 0
    row = pl.BlockSpec((tm, D_MODEL), lambda i: (i, 0))
    return pl.pallas_call(
        _final_kernel, grid=(n // tm,),
        in_specs=[row, pl.BlockSpec((tm * D_CHUNKS, LANE), lambda i: (i + moe_off, 0)),
                  _mod_spec(DEPTH - 1, tm, latent), _layer(0, (1, D_MODEL))],
        out_specs=row, out_shape=jax.ShapeDtypeStruct((n, D_MODEL), F32),
        compiler_params=_cp(("parallel",), 32), name="final",
    )(x, moe, mod, g)


def _axial_tables(rot_dim):
    rows = DEC_SEQ // GRID_W
    row = jnp.repeat(jnp.arange(rows, dtype=F32), GRID_W)
    col = jnp.tile(jnp.arange(GRID_W, dtype=F32), rows)
    n_freq = rot_dim // 4
    inv = ROPE_BASE ** (-jnp.arange(n_freq, dtype=F32) / n_freq)
    ang = jnp.concatenate([row[:, None] * inv, col[:, None] * inv], axis=-1)
    sign = jnp.tile(jnp.array([-1.0, 1.0], F32), rot_dim // 2)
    return jnp.repeat(jnp.cos(ang), 2, axis=1), jnp.repeat(jnp.sin(ang), 2, axis=1) * sign


def _rope_tables():
    c64, s64 = _axial_tables(DA_DH)
    c32, s32 = _axial_tables(MLA_ROPE)
    one = lambda w: jnp.ones((DEC_SEQ, w), F32)
    zero = lambda w: jnp.zeros((DEC_SEQ, w), F32)
    pad = LANE - MLA_NOPE - MLA_ROPE
    return jnp.stack([
        jnp.tile(c64, (1, 2)), jnp.tile(s64, (1, 2)),
        jnp.concatenate([one(MLA_NOPE), c32, one(pad)], axis=1),
        jnp.concatenate([zero(MLA_NOPE), s32, zero(pad)], axis=1),
        jnp.concatenate([c32, one(LANE - MLA_ROPE)], axis=1),
        jnp.concatenate([s32, zero(LANE - MLA_ROPE)], axis=1)])


def _prep_weights(w_in, w_gate, mla_w_qb, mla_w_kvb, w_br_da, w_br_mla, w_br_swa, w_o, w_router):
    kr_end = C_MKR + MLA_ROPE
    win = jnp.concatenate([w_in[:, :, :kr_end].astype(BF), jnp.zeros((DEPTH, D_MODEL, LANE - MLA_ROPE), BF),
                           w_in[:, :, kr_end:].astype(BF)], axis=2)
    dk = MLA_NOPE + MLA_ROPE
    wqb = jnp.pad(mla_w_qb.reshape(DEPTH, MLA_Q_RANK, MLA_HEADS, dk), ((0, 0), (0, 0), (0, 0), (0, LANE - dk)))
    wqb = wqb.reshape(DEPTH, MLA_Q_RANK, MLA_HEADS * LANE).astype(BF)
    kvb = mla_w_kvb.reshape(DEPTH, MLA_KV_RANK, MLA_HEADS, MLA_NOPE + MLA_V)
    wk = jnp.pad(kvb[..., :MLA_NOPE], ((0, 0), (0, 0), (0, 0), (0, LANE - MLA_NOPE)))
    wk = wk.reshape(DEPTH, MLA_KV_RANK, MLA_HEADS * LANE)
    wv = kvb[..., MLA_NOPE:].reshape(DEPTH, MLA_KV_RANK, MLA_OUT)
    wkv = jnp.concatenate([wk, wv], axis=2).astype(BF)
    wrh, wrl = _split(jnp.swapaxes(w_router, 1, 2))
    return dict(win=win, wg=w_gate.astype(BF), wqb=wqb, wkv=wkv, wda=w_br_da.astype(BF),
                wmla=w_br_mla.astype(BF), wswa=w_br_swa.astype(BF), wo=w_o.astype(BF), wrh=wrh, wrl=wrl)


def kernel(x_prompt, x_sample, cache_da_k, cache_da_v, cache_mla_ckv, cache_mla_krope, cache_swa_k, cache_swa_v, c, c_ctx, w_ada, b_ada, norm1, norm2, w_in, da_lq1, da_lk1, da_lq2, da_lk2, da_subln, mla_q_norm, mla_w_qb, mla_kv_norm, mla_w_kvb, swa_sink, w_gate, w_br_da, w_br_mla, w_br_swa, w_o, w_router, w_ff1, w_ff3, w_ff2, final_norm):
    c_all = jnp.concatenate([c_ctx[None], c, jnp.zeros((SUB - 1 - DEC_BATCH, D_MODEL), F32)], axis=0)
    mod = _ada_call(c_all, w_ada, b_ada)[:, :1 + DEC_BATCH].reshape(DEPTH, 1 + DEC_BATCH, 6, D_MODEL)
    lam_all = _lam_call(da_lq1, da_lk1, da_lq2, da_lk2)
    tab = _rope_tables()
    swa_bias = _swa_bias()
    w = _prep_weights(w_in, w_gate, mla_w_qb, mla_w_kvb, w_br_da, w_br_mla, w_br_swa, w_o, w_router)
    n1, n2 = norm1[:, None], norm2[:, None]
    subln, qn, kvn = da_subln[:, None], mla_q_norm[:, None], mla_kv_norm[:, None]
    caches = (cache_da_k.reshape(DEC_BATCH, DEPTH, PAST_LEN, 512),
              cache_da_v.reshape(DEC_BATCH, DEPTH, PAST_LEN, 512),
              cache_mla_ckv,
              jnp.pad(cache_mla_krope, ((0, 0), (0, 0), (0, 0), (0, LANE - MLA_ROPE))),
              cache_swa_k.reshape(DEC_BATCH, DEPTH, PAST_LEN, LANE),
              cache_swa_v.reshape(DEC_BATCH, DEPTH, PAST_LEN, LANE))

    xp = x_prompt.reshape(N_CTX, D_MODEL)
    xs = x_sample.reshape(N_LAT, D_MODEL)
    moe = None
    new = [[] for _ in range(6)]
    for l in range(DEPTH):
        lam = lam_all[l, :1]
        sink = swa_sink[l]

        xp, proj_c, gate_c = _pre_call(l, xp, moe, mod, n1, w["win"], w["wg"], latent=False)
        xs, proj_l, gate_l = _pre_call(l, xs, moe, mod, n1, w["win"], w["wg"], latent=True)

        oda_c, omla_c, oswa_c, ckv_c = _ctx_attn_call(l, lam, sink, proj_c, subln, qn, w["wqb"], kvn, w["wkv"])
        pc = proj_c.reshape(BATCH, SEQ, PROJ_COLS)
        new[0].append(pc[..., C_DAK:C_DAK + 512].reshape(BATCH, SEQ, DA_HEADS, 2 * DA_DH))
        new[1].append(pc[..., C_DAV:C_DAV + 512].reshape(BATCH, SEQ, DA_HEADS, 2 * DA_DH))
        new[2].append(ckv_c.reshape(BATCH, SEQ, MLA_KV_RANK))
        new[3].append(pc[..., C_MKR:C_MKR + MLA_ROPE])
        new[4].append(pc[..., C_SK:C_SK + LANE].reshape(BATCH, SEQ, SWA_KV_HEADS, SWA_DH))
        new[5].append(pc[..., C_SV:C_SV + LANE].reshape(BATCH, SEQ, SWA_KV_HEADS, SWA_DH))

        daq, mlaq, swaq, dak, dav, mlak, mlav, swak, swav = _lat_prep_call(
            l, proj_l, tab, caches, qn, w["wqb"], kvn, w["wkv"])
        oda_l = _lat_da_call(l, lam, daq, dak, dav, subln).reshape(N_LAT, 512)
        omla_l = _lat_mla_call(mlaq, mlak, mlav).reshape(N_LAT, 512)
        oswa_l = _lat_swa_call(sink, swa_bias, swaq, swak, swav).reshape(N_LAT, 512)

        post_w = (w["wda"], w["wmla"], w["wswa"], w["wo"], w["wrh"], w["wrl"])
        xp, h2_c, lg_c = _post_call(l, xp, oda_c, omla_c, oswa_c, gate_c, mod, n2, *post_w, latent=False)
        xs, h2_l, lg_l = _post_call(l, xs, oda_l, omla_l, oswa_l, gate_l, mod, n2, *post_w, latent=True)

        idx_c, g_c, _ = _router_call(lg_c, CAP_CTX)
        idx_l, g_l, cnt_l = _router_call(lg_l, CAP_LAT)
        idx = jnp.concatenate([idx_c, idx_l], axis=1).reshape(N_EXPERTS, 1, SLOTS)
        gsel = jnp.concatenate([g_c, g_l], axis=1).reshape(N_EXPERTS, 1, SLOTS)
        n0 = cnt_l[:, 0].astype(I32)
        bnd = jnp.stack([jnp.zeros_like(n0), jnp.full_like(n0, CAP_CTX), CAP_CTX + n0,
                         jnp.full_like(n0, SLOTS)], axis=1)
        y_c, y_l = _ffn_call(l, idx, h2_c.reshape(N_CTX, SUB, LANE), h2_l.reshape(N_LAT, SUB, LANE),
                             w_ff1, w_ff3, w_ff2)
        acc = _combine_call(idx, gsel, bnd.reshape(N_EXPERTS, 1, N_HALVES + 1), y_c, y_l)
        moe = acc.reshape(N_HALVES * TOK_HALF * D_CHUNKS, LANE)

    fn = final_norm[None, None]
    y_prompt = _final_call(xp, moe, mod, fn, latent=False).reshape(BATCH, SEQ, D_MODEL)
    y_sample = _final_call(xs, moe, mod, fn, latent=True).reshape(DEC_BATCH, DEC_SEQ, D_MODEL)
    return (y_prompt, y_sample) + tuple(jnp.stack(n, axis=1) for n in new)
```

```python
import functools
import math

import jax
import jax.numpy as jnp
from jax import lax
from jax.experimental import pallas as pl
from jax.experimental.pallas import tpu as pltpu

F32 = jnp.float32
BF = jnp.bfloat16
I32 = jnp.int32

D_MODEL = 1024
BATCH = 16
SEQ = 256
DEPTH = 2
DEC_BATCH = 2
DEC_SEQ = 4096
PAST_LEN = 512
GRID_W = 64
ROPE_BASE = 10000.0
EPS = 1e-6
DA_HEADS = 4
DA_DH = 64
DA_OUT = DA_HEADS * 2 * DA_DH
MLA_HEADS = 8
MLA_Q_RANK = 256
MLA_KV_RANK = 128
MLA_NOPE = 64
MLA_ROPE = 32
MLA_V = 64
MLA_OUT = MLA_HEADS * MLA_V
SWA_HEADS = 8
SWA_KV_HEADS = 2
SWA_DH = 64
W_BLOCK = 128
SWA_OUT = SWA_HEADS * SWA_DH
N_EXPERTS = 16
EXPERT_FF = 1024
CAPACITY_FACTOR = 2

LANE = 128
SUB = 8
HALF = 64
N_CTX = BATCH * SEQ
N_LAT = DEC_BATCH * DEC_SEQ
KEYS = PAST_LEN + DEC_SEQ
CAP_CTX = CAPACITY_FACTOR * N_CTX // N_EXPERTS
CAP_LAT = CAPACITY_FACTOR * N_LAT // N_EXPERTS
SLOTS = CAP_CTX + CAP_LAT
TOK_HALF = 4096
N_HALVES = (N_CTX + N_LAT) // TOK_HALF
D_CHUNKS = D_MODEL // LANE

C_DAQ, C_DAK, C_DAV = 0, 512, 1024
C_MQ, C_MKV, C_MKR = 1536, 1792, 1920
C_SQ, C_SK, C_SV = 2048, 2560, 2688
PROJ_COLS = 2816
GATE_COLS = 3 * D_MODEL
MLA_SCALE = (MLA_NOPE + MLA_ROPE) ** -0.5
HEAD_SCALE = DA_DH ** -0.5
LOG2E = math.log2(math.e)

TM_PRE = 256
TM_FINAL = 512
TM_POST = 512
CHUNK = 512
TQ = 256
DA_STEP_HEADS = 4
MLA_STEP_PAIRS = 4
ROUTE_TILE = 128
ROUTE_CHUNK = 512
THRESHOLD_STEPS = 48
FFN_ROWS = 512
FF_SPLIT = 2
COMBINE_GROUP = 8


def _cp(sem, vmem_mb):
    return pltpu.CompilerParams(dimension_semantics=sem, vmem_limit_bytes=vmem_mb * 1024 * 1024)


def _dot(a, b):
    return jnp.dot(a, b, preferred_element_type=F32)


def _dot_nt(a, b):
    return lax.dot_general(a, b, (((1,), (1,)), ((), ())), preferred_element_type=F32)


def _split(a):
    hi = a.astype(BF)
    lo = (a - hi.astype(F32)).astype(BF)
    return hi, lo


def _dot3(a, w):
    ah, al = _split(a)
    wh, wl = _split(w)
    return _dot(ah, wh) + _dot(ah, wl) + _dot(al, wh)


def _rms(x, g):
    return x * lax.rsqrt(jnp.mean(x * x, axis=-1, keepdims=True) + EPS) * g


def _lane(shape):
    return lax.broadcasted_iota(I32, shape, len(shape) - 1)


def _softmax_pv(segs, sink=None):
    m = None
    for s, _ in segs:
        ms = jnp.max(s, axis=-1, keepdims=True)
        m = ms if m is None else jnp.maximum(m, ms)
    if sink is not None:
        m = jnp.maximum(m, sink)
    l = None
    o = None
    for s, v in segs:
        e = jnp.exp(s - m)
        ls = jnp.sum(e, axis=-1, keepdims=True)
        os_ = _dot(e.astype(BF), v)
        l = ls if l is None else l + ls
        o = os_ if o is None else o + os_
    if sink is not None:
        l = l + jnp.exp(sink - m)
    return o / l


def _mask_halves(q):
    lo = _lane(q.shape) < HALF
    return jnp.where(lo, q, 0.0).astype(BF), jnp.where(lo, 0.0, q).astype(BF)


def _pair(o_a, o_b):
    return jnp.where(_lane(o_a.shape) < HALF, o_a, o_b)


def _dup(x):
    r = pltpu.roll(x, HALF, 1)
    lo = _lane(x.shape) < HALF
    return jnp.where(lo, x, r), jnp.where(lo, r, x)


def _rope(x, c, s):
    n = x.shape[-1]
    even = (_lane(x.shape) % 2) == 0
    sw = jnp.where(even, pltpu.roll(x, n - 1, 1), pltpu.roll(x, 1, 1))
    return x * c + sw * s


def _da_head(q, k, v, lam, subln, lam_init):
    q1, q2 = _mask_halves(q)
    o1 = _softmax_pv([(_dot_nt(q1, k), v)])
    o2 = _softmax_pv([(_dot_nt(q2, k), v)])
    return _rms(o1 - lam * o2, subln) * (1.0 - lam_init)


def _ada_kernel(c_ref, w_ref, b_ref, o_ref):
    c = c_ref[...]
    a = c * jax.nn.sigmoid(c)
    o_ref[...] = _dot3(a, w_ref[...]) + b_ref[...]


def _ada_call(c_all, w_ada, b_ada):
    tn = 1536
    return pl.pallas_call(
        _ada_kernel,
        grid=(DEPTH, 6 * D_MODEL // tn),
        in_specs=[pl.BlockSpec((SUB, D_MODEL), lambda l, j: (0, 0)),
                  pl.BlockSpec((None, D_MODEL, tn), lambda l, j: (l, 0, j)),
                  pl.BlockSpec((None, 1, tn), lambda l, j: (l, 0, j))],
        out_specs=pl.BlockSpec((None, SUB, tn), lambda l, j: (l, 0, j)),
        out_shape=jax.ShapeDtypeStruct((DEPTH, SUB, 6 * D_MODEL), F32),
        compiler_params=_cp(("parallel", "parallel"), 40),
        name="ada",
    )(c_all, w_ada, b_ada.reshape(DEPTH, 1, 6 * D_MODEL))


def _lam_kernel(q1, k1, q2, k2, o_ref):
    s1 = jnp.sum(q1[...] * k1[...], axis=-1, keepdims=True)
    s2 = jnp.sum(q2[...] * k2[...], axis=-1, keepdims=True)
    row = lax.broadcasted_iota(I32, (DEPTH, 1), 0)
    init = jnp.zeros((DEPTH, 1), F32)
    for l in range(DEPTH):
        init = jnp.where(row == l, _lam_init(l), init)
    o_ref[...] = jnp.broadcast_to(jnp.exp(s1) - jnp.exp(s2) + init, o_ref.shape)


def _lam_init(layer):
    return 0.8 - 0.6 * math.exp(-0.3 * layer)


def _lam_call(q1, k1, q2, k2):
    return pl.pallas_call(
        _lam_kernel,
        out_shape=jax.ShapeDtypeStruct((DEPTH, LANE), F32),
        name="lam",
    )(q1, k1, q2, k2)


def _moe_rows(moe_ref, tm):
    return jnp.concatenate([moe_ref[pl.ds(k, tm, stride=D_CHUNKS), :] for k in range(D_CHUNKS)], axis=1)


def _pre_body(x, mod_ref, n1_ref, win_ref, wg_ref, proj_ref, gate_ref):
    m = mod_ref[...]
    h = _rms(x, n1_ref[...]) * (1.0 + m[1:2]) + m[0:1]
    hb = h.astype(BF)
    proj_ref[...] = _dot(hb, win_ref[...]).astype(proj_ref.dtype)
    gate_ref[...] = jax.nn.sigmoid(_dot(hb, wg_ref[...])).astype(BF)


def _pre_first_kernel(x_ref, mod_ref, n1_ref, win_ref, wg_ref, proj_ref, gate_ref):
    _pre_body(x_ref[...], mod_ref, n1_ref, win_ref, wg_ref, proj_ref, gate_ref)


def _pre_next_kernel(x_ref, moe_ref, modp_ref, mod_ref, n1_ref, win_ref, wg_ref, xo_ref, proj_ref, gate_ref):
    x = x_ref[...] + modp_ref[...][5:6] * _moe_rows(moe_ref, TM_PRE)
    xo_ref[...] = x
    _pre_body(x, mod_ref, n1_ref, win_ref, wg_ref, proj_ref, gate_ref)


def _mod_spec(layer, tm, latent):
    per = DEC_SEQ // tm
    if latent:
        return pl.BlockSpec((None, None, 6, D_MODEL), lambda i: (layer, 1 + i // per, 0, 0))
    return pl.BlockSpec((None, None, 6, D_MODEL), lambda i: (layer, 0, 0, 0))


def _layer(layer, shape):
    nd = len(shape)
    return pl.BlockSpec((None,) + shape, lambda *_: (layer,) + (0,) * nd, pipeline_mode=pl.Buffered(1))


def _pre_call(layer, x, moe, mod, n1, win, wg, latent):
    n = x.shape[0]
    tm = TM_PRE
    row = pl.BlockSpec((tm, D_MODEL), lambda i: (i, 0))
    w_specs = [_mod_spec(layer, tm, latent), _layer(layer, (1, D_MODEL)), _layer(layer, (D_MODEL, PROJ_COLS)),
               _layer(layer, (D_MODEL, GATE_COLS))]
    outs = [jax.ShapeDtypeStruct((n, PROJ_COLS), BF if latent else F32), jax.ShapeDtypeStruct((n, GATE_COLS), BF)]
    out_specs = [pl.BlockSpec((tm, PROJ_COLS), lambda i: (i, 0)), pl.BlockSpec((tm, GATE_COLS), lambda i: (i, 0))]
    if moe is None:
        proj, gate = pl.pallas_call(
            _pre_first_kernel, grid=(n // tm,), in_specs=[row] + w_specs, out_specs=out_specs, out_shape=outs,
            compiler_params=_cp(("parallel",), 52), name="pre_first",
        )(x, mod, n1, win, wg)
        return x, proj, gate
    moe_off = (N_CTX // tm) if latent else 0
    moe_spec = pl.BlockSpec((tm * D_CHUNKS, LANE), lambda i: (i + moe_off, 0))
    xo, proj, gate = pl.pallas_call(
        _pre_next_kernel, grid=(n // tm,),
        in_specs=[row, moe_spec, _mod_spec(layer - 1, tm, latent)] + w_specs,
        out_specs=[row] + out_specs,
        out_shape=[jax.ShapeDtypeStruct((n, D_MODEL), F32)] + outs,
        compiler_params=_cp(("parallel",), 52), name="pre_next",
    )(x, moe, mod, mod, n1, win, wg)
    return xo, proj, gate


def _ctx_attn_kernel(lam_init, lam_ref, sink_ref, p_ref, subln_ref, qn_ref, wqb_ref, kvn_ref, wkv_ref,
                     oda_ref, omla_ref, oswa_ref, ckv_ref, kc_ref, vc_ref, kr_ref, sk_ref, sv_ref):
    p = p_ref[...]
    lam = lam_ref[0]
    for h in range(DA_HEADS):
        kc_ref[pl.ds(h, SEQ, stride=DA_HEADS), :] = p[:, C_DAK + h * LANE:C_DAK + (h + 1) * LANE]
        vc_ref[pl.ds(h, SEQ, stride=DA_HEADS), :] = p[:, C_DAV + h * LANE:C_DAV + (h + 1) * LANE]
    kr_ref[...] = p[:, C_MKR:C_MKR + MLA_ROPE]
    sk_ref[...] = p[:, C_SK:C_SK + LANE]
    sv_ref[...] = p[:, C_SV:C_SV + LANE]
    for h in range(DA_HEADS):
        blk = slice(h * LANE, (h + 1) * LANE)
        q = p[:, C_DAQ + h * LANE:C_DAQ + (h + 1) * LANE] * HEAD_SCALE
        k = p[:, C_DAK + h * LANE:C_DAK + (h + 1) * LANE].astype(BF)
        v = p[:, C_DAV + h * LANE:C_DAV + (h + 1) * LANE].astype(BF)
        oda_ref[:, blk] = _da_head(q, k, v, lam, subln_ref[...], lam_init).astype(BF)
    qn = _rms(p[:, C_MQ:C_MQ + MLA_Q_RANK], qn_ref[...]).astype(BF)
    cq = _dot(qn, wqb_ref[...])
    ckv = _rms(p[:, C_MKV:C_MKV + MLA_KV_RANK], kvn_ref[...])
    ckv_ref[...] = ckv
    kv = _dot(ckv.astype(BF), wkv_ref[...])
    kr_sh = pltpu.roll(p[:, C_MKR:C_MKR + LANE], HALF, 1)
    for m in range(MLA_HEADS // 2):
        vpair = kv[:, MLA_HEADS * LANE + m * LANE:MLA_HEADS * LANE + (m + 1) * LANE].astype(BF)
        outs = []
        for h in (2 * m, 2 * m + 1):
            q = (cq[:, h * LANE:(h + 1) * LANE] * MLA_SCALE).astype(BF)
            k = (kv[:, h * LANE:(h + 1) * LANE] + kr_sh).astype(BF)
            outs.append(_softmax_pv([(_dot_nt(q, k), vpair)]))
        omla_ref[:, m * LANE:(m + 1) * LANE] = _pair(outs[0], outs[1]).astype(BF)
    kd = _dup(p[:, C_SK:C_SK + LANE])
    vd = _dup(p[:, C_SV:C_SV + LANE])
    for m in range(SWA_HEADS // 2):
        g = (2 * m) // (SWA_HEADS // SWA_KV_HEADS)
        k = kd[g].astype(BF)
        v = vd[g].astype(BF)
        qa, qb = _mask_halves(p[:, C_SQ + m * LANE:C_SQ + (m + 1) * LANE] * HEAD_SCALE)
        oa = _softmax_pv([(_dot_nt(qa, k), v)], sink_ref[2 * m])
        ob = _softmax_pv([(_dot_nt(qb, k), v)], sink_ref[2 * m + 1])
        oswa_ref[:, m * LANE:(m + 1) * LANE] = _pair(oa, ob).astype(BF)


def _smem():
    return pl.BlockSpec(memory_space=pltpu.SMEM)


def _ctx_attn_call(layer, lam, sink, proj, subln, qn, wqb, kvn, wkv):
    row512 = pl.BlockSpec((SEQ, 512), lambda b: (b, 0))
    row128 = pl.BlockSpec((SEQ, LANE), lambda b: (b, 0))
    heads = pl.BlockSpec((SEQ * DA_HEADS, LANE), lambda b: (b, 0))
    return pl.pallas_call(
        functools.partial(_ctx_attn_kernel, _lam_init(layer)),
        grid=(BATCH,),
        in_specs=[_smem(), _smem(), pl.BlockSpec((SEQ, PROJ_COLS), lambda b: (b, 0)),
                  _layer(layer, (1, LANE)), _layer(layer, (1, MLA_Q_RANK)),
                  _layer(layer, (MLA_Q_RANK, MLA_HEADS * LANE)), _layer(layer, (1, MLA_KV_RANK)),
                  _layer(layer, (MLA_KV_RANK, MLA_HEADS * LANE + MLA_OUT))],
        out_specs=[row512, row512, row512, row128, heads, heads,
                   pl.BlockSpec((SEQ, MLA_ROPE), lambda b: (b, 0)), row128, row128],
        out_shape=[jax.ShapeDtypeStruct((N_CTX, 512), BF)] * 3
        + [jax.ShapeDtypeStruct((N_CTX, LANE), F32)]
        + [jax.ShapeDtypeStruct((N_CTX * DA_HEADS, LANE), F32)] * 2
        + [jax.ShapeDtypeStruct((N_CTX, MLA_ROPE), F32)] + [jax.ShapeDtypeStruct((N_CTX, LANE), F32)] * 2,
        compiler_params=_cp(("parallel",), 40), name="ctx_attn",
    )(lam, sink, proj, subln, qn, wqb, kvn, wkv)


def _lat_prep_kernel(p_ref, tab_ref, cdk_ref, cdv_ref, cckv_ref, ckr_ref, csk_ref, csv_ref,
                     qn_ref, wqb_ref, kvn_ref, wkv_ref,
                     daq_ref, mlaq_ref, swaq_ref, dak_ref, dav_ref, mlak_ref, mlav_ref, swak_ref, swav_ref):
    j = pl.program_id(1)

    def write_kv(dk, dv, ckv, kr_sh, sk, sv):
        ones_col = jnp.where(_lane((CHUNK, LANE)) == 0, 1.0, 0.0).astype(BF)
        dak_ref[...] = dk.astype(BF)
        for h in range(DA_HEADS):
            dav_ref[:, 2 * h * LANE:(2 * h + 1) * LANE] = dv[:, h * LANE:(h + 1) * LANE].astype(BF)
            dav_ref[:, (2 * h + 1) * LANE:(2 * h + 2) * LANE] = ones_col
        kv = _dot(ckv.astype(BF), wkv_ref[...])
        for h in range(MLA_HEADS):
            mlak_ref[:, h * LANE:(h + 1) * LANE] = (kv[:, h * LANE:(h + 1) * LANE] + kr_sh).astype(BF)
        for p in range(MLA_HEADS // 2):
            mlav_ref[:, 2 * p * LANE:(2 * p + 1) * LANE] = kv[:, (MLA_HEADS + p) * LANE:(MLA_HEADS + p + 1) * LANE].astype(BF)
            mlav_ref[:, (2 * p + 1) * LANE:(2 * p + 2) * LANE] = ones_col
        k0, k1 = _dup(sk)
        v0, v1 = _dup(sv)
        swak_ref[:, 0:LANE] = k0.astype(BF)
        swak_ref[:, LANE:2 * LANE] = k1.astype(BF)
        swav_ref[:, 0:LANE] = v0.astype(BF)
        swav_ref[:, LANE:2 * LANE] = ones_col
        swav_ref[:, 2 * LANE:3 * LANE] = v1.astype(BF)
        swav_ref[:, 3 * LANE:4 * LANE] = ones_col

    @pl.when(j == 0)
    def _():
        write_kv(cdk_ref[...], cdv_ref[...], cckv_ref[...], pltpu.roll(ckr_ref[...], HALF, 1),
                 csk_ref[...], csv_ref[...])

    @pl.when(j > 0)
    def _():
        c64, s64 = tab_ref[0], tab_ref[1]
        cmq, smq = tab_ref[2], tab_ref[3]
        ckr, skr = tab_ref[4], tab_ref[5]

        def cols(c0, width):
            return p_ref[:, c0:c0 + width].astype(F32)

        def blk(c0, h):
            return cols(c0 + h * LANE, LANE)

        for h in range(DA_HEADS):
            daq_ref[:, h * LANE:(h + 1) * LANE] = (_rope(blk(C_DAQ, h), c64, s64) * (HEAD_SCALE * LOG2E)).astype(BF)
            swaq_ref[:, h * LANE:(h + 1) * LANE] = (_rope(blk(C_SQ, h), c64, s64) * (HEAD_SCALE * LOG2E)).astype(BF)
        dk = jnp.concatenate([_rope(blk(C_DAK, h), c64, s64) for h in range(DA_HEADS)], axis=1)
        qn = _rms(cols(C_MQ, MLA_Q_RANK), qn_ref[...]).astype(BF)
        cq = _dot(qn, wqb_ref[...])
        for h in range(MLA_HEADS):
            mlaq_ref[:, h * LANE:(h + 1) * LANE] = (
                _rope(cq[:, h * LANE:(h + 1) * LANE], cmq, smq) * (MLA_SCALE * LOG2E)).astype(BF)
        ckv = _rms(cols(C_MKV, MLA_KV_RANK), kvn_ref[...])
        kr_sh = pltpu.roll(_rope(cols(C_MKR, LANE), ckr, skr), HALF, 1)
        sk = _rope(cols(C_SK, LANE), c64, s64)
        write_kv(dk, cols(C_DAV, 512), ckv, kr_sh, sk, cols(C_SV, LANE))


def _lat_prep_call(layer, proj, tab, caches, qn, wqb, kvn, wkv):
    cdk, cdv, cckv, ckr, csk, csv = caches
    nj = 1 + DEC_SEQ // CHUNK

    def own(width):
        return pl.BlockSpec((None, CHUNK, width), lambda b, j: (b, jnp.maximum(j - 1, 0), 0))

    def cache(width):
        return pl.BlockSpec((None, None, PAST_LEN, width), lambda b, j: (b, layer, 0, 0))

    def allk(width):
        return pl.BlockSpec((None, CHUNK, width), lambda b, j: (b, j, 0))

    def shp(rows, width):
        return jax.ShapeDtypeStruct((DEC_BATCH, rows, width), BF)

    return pl.pallas_call(
        _lat_prep_kernel,
        grid=(DEC_BATCH, nj),
        in_specs=[own(PROJ_COLS),
                  pl.BlockSpec((6, CHUNK, LANE), lambda b, j: (0, jnp.maximum(j - 1, 0), 0)),
                  cache(512), cache(512), cache(LANE), cache(LANE), cache(LANE), cache(LANE),
                  _layer(layer, (1, MLA_Q_RANK)), _layer(layer, (MLA_Q_RANK, MLA_HEADS * LANE)),
                  _layer(layer, (1, MLA_KV_RANK)), _layer(layer, (MLA_KV_RANK, MLA_HEADS * LANE + MLA_OUT))],
        out_specs=[own(512), own(MLA_HEADS * LANE), own(512),
                   allk(512), allk(1024), allk(MLA_HEADS * LANE), allk(1024), allk(2 * LANE), allk(4 * LANE)],
        out_shape=[shp(DEC_SEQ, 512), shp(DEC_SEQ, MLA_HEADS * LANE), shp(DEC_SEQ, 512),
                   shp(KEYS, 512), shp(KEYS, 1024), shp(KEYS, MLA_HEADS * LANE), shp(KEYS, 1024),
                   shp(KEYS, 2 * LANE), shp(KEYS, 4 * LANE)],
        compiler_params=_cp(("parallel", "arbitrary"), 48), name="lat_prep",
    )(proj.reshape(DEC_BATCH, DEC_SEQ, PROJ_COLS), tab, cdk, cdv, cckv, ckr, csk, csv, qn, wqb, kvn, wkv)


def _softmax2_pv(s, v_aug):
    e = jnp.exp2(s - jnp.max(s, axis=-1, keepdims=True)).astype(BF)
    o = _dot(e, v_aug)
    return o[:, 0:LANE] / o[:, LANE:LANE + 1]


def _lat_da_kernel(lam_init, lam_ref, q_ref, k_ref, v_ref, subln_ref, o_ref):
    for h in range(DA_STEP_HEADS):
        cols = slice(h * LANE, (h + 1) * LANE)
        q1, q2 = _mask_halves(q_ref[:, cols].astype(F32))
        k = k_ref[:, cols]
        v = v_ref[:, 2 * h * LANE:(2 * h + 2) * LANE]
        o = _softmax2_pv(_dot_nt(q1, k), v) - lam_ref[0] * _softmax2_pv(_dot_nt(q2, k), v)
        o_ref[:, cols] = (_rms(o, subln_ref[...]) * (1.0 - lam_init)).astype(BF)


def _lat_da_call(layer, lam, q, k, v, subln):
    return pl.pallas_call(
        functools.partial(_lat_da_kernel, _lam_init(layer)),
        grid=(DEC_BATCH, DA_HEADS // DA_STEP_HEADS, DEC_SEQ // TQ),
        in_specs=[_smem(),
                  pl.BlockSpec((None, TQ, DA_STEP_HEADS * LANE), lambda b, h, i: (b, i, h)),
                  pl.BlockSpec((None, KEYS, DA_STEP_HEADS * LANE), lambda b, h, i: (b, 0, h)),
                  pl.BlockSpec((None, KEYS, 2 * DA_STEP_HEADS * LANE), lambda b, h, i: (b, 0, h)),
                  _layer(layer, (1, LANE))],
        out_specs=pl.BlockSpec((None, TQ, DA_STEP_HEADS * LANE), lambda b, h, i: (b, i, h)),
        out_shape=jax.ShapeDtypeStruct((DEC_BATCH, DEC_SEQ, 512), BF),
        compiler_params=_cp(("parallel", "parallel", "arbitrary"), 56), name="lat_da",
    )(lam, q, k, v, subln)


def _lat_mla_kernel(q_ref, k_ref, v_ref, o_ref):
    for p in range(MLA_STEP_PAIRS):
        v = v_ref[:, 2 * p * LANE:(2 * p + 2) * LANE]
        outs = []
        for h in (2 * p, 2 * p + 1):
            cols = slice(h * LANE, (h + 1) * LANE)
            outs.append(_softmax2_pv(_dot_nt(q_ref[:, cols], k_ref[:, cols]), v))
        o_ref[:, p * LANE:(p + 1) * LANE] = _pair(outs[0], outs[1]).astype(BF)


def _lat_mla_call(q, k, v):
    sp = MLA_STEP_PAIRS
    return pl.pallas_call(
        _lat_mla_kernel,
        grid=(DEC_BATCH, MLA_HEADS // (2 * sp), DEC_SEQ // TQ),
        in_specs=[pl.BlockSpec((None, TQ, 2 * sp * LANE), lambda b, m, i: (b, i, m)),
                  pl.BlockSpec((None, KEYS, 2 * sp * LANE), lambda b, m, i: (b, 0, m)),
                  pl.BlockSpec((None, KEYS, 2 * sp * LANE), lambda b, m, i: (b, 0, m))],
        out_specs=pl.BlockSpec((None, TQ, sp * LANE), lambda b, m, i: (b, i, m)),
        out_shape=jax.ShapeDtypeStruct((DEC_BATCH, DEC_SEQ, 512), BF),
        compiler_params=_cp(("parallel", "parallel", "arbitrary"), 56), name="lat_mla",
    )(q, k, v)


def _swa_bias():
    rows = (SWA_HEADS // SWA_KV_HEADS) * W_BLOCK
    qi = (jnp.arange(rows) % W_BLOCK)[:, None]
    loc = (jnp.arange(PAST_LEN + 3 * W_BLOCK) - PAST_LEN)[None, :]
    always = (loc < 0) | ((loc >= W_BLOCK) & (loc < 2 * W_BLOCK))
    prev = (loc >= 0) & (loc < W_BLOCK) & (loc >= qi)
    nxt = (loc >= 2 * W_BLOCK) & (loc - 2 * W_BLOCK <= qi)
    ok = jnp.stack([always | nxt, always | prev | nxt, always | prev])
    return jnp.where(ok, 0.0, -jnp.inf).astype(F32)


def _lat_swa_kernel(sink_ref, bias_ref, q_ref, kc_ref, kp_ref, k0_ref, kn_ref, vc_ref, vp_ref, v0_ref, vn_ref,
                    o_ref):
    group = SWA_HEADS // SWA_KV_HEADS
    rows = group * W_BLOCK
    bias = bias_ref[...]
    head_of_row = lax.broadcasted_iota(I32, (rows, 1), 0) // W_BLOCK
    for g in range(SWA_KV_HEADS):
        gs = slice(g * LANE, (g + 1) * LANE)
        vs = slice(2 * g * LANE, (2 * g + 2) * LANE)
        q_parts = []
        sink = jnp.zeros((rows, 1), F32)
        for j in range(group):
            sink = jnp.where(head_of_row == j, sink_ref[g * group + j] * LOG2E, sink)
        for p in range(group // 2):
            blk = g * (group // 2) + p
            q_parts.extend(_mask_halves(q_ref[:, blk * LANE:(blk + 1) * LANE].astype(F32)))
        q = jnp.concatenate(q_parts, axis=0)
        k = jnp.concatenate([kc_ref[:, gs], kp_ref[:, gs], k0_ref[:, gs], kn_ref[:, gs]], axis=0)
        v = jnp.concatenate([vc_ref[:, vs], vp_ref[:, vs], v0_ref[:, vs], vn_ref[:, vs]], axis=0)
        s = _dot_nt(q, k) + bias
        m = jnp.maximum(jnp.max(s, axis=-1, keepdims=True), sink)
        ov = _dot(jnp.exp2(s - m).astype(BF), v)
        o = ov[:, 0:LANE] / (ov[:, LANE:LANE + 1] + jnp.exp2(sink - m))
        for p in range(group // 2):
            blk = g * (group // 2) + p
            o_a = o[(2 * p) * W_BLOCK:(2 * p + 1) * W_BLOCK]
            o_b = o[(2 * p + 1) * W_BLOCK:(2 * p + 2) * W_BLOCK]
            o_ref[:, blk * LANE:(blk + 1) * LANE] = _pair(o_a, o_b).astype(BF)


def _lat_swa_call(sink, bias, q, k, v):
    nb = DEC_SEQ // W_BLOCK
    cb = PAST_LEN // W_BLOCK

    def ctx(width):
        return pl.BlockSpec((None, PAST_LEN, width), lambda b, n: (b, 0, 0))

    def loc(d, width):
        return pl.BlockSpec((None, W_BLOCK, width),
                            lambda b, n: (b, cb + jnp.clip(n + d, 0, nb - 1), 0))

    kw, vw = 2 * LANE, 4 * LANE
    return pl.pallas_call(
        _lat_swa_kernel,
        grid=(DEC_BATCH, nb),
        in_specs=[_smem(),
                  pl.BlockSpec((None,) + bias.shape[1:],
                               lambda b, n: (jnp.where(n == 0, 0, jnp.where(n == nb - 1, 2, 1)), 0, 0)),
                  pl.BlockSpec((None, W_BLOCK, 512), lambda b, n: (b, n, 0)),
                  ctx(kw), loc(-1, kw), loc(0, kw), loc(1, kw), ctx(vw), loc(-1, vw), loc(0, vw), loc(1, vw)],
        out_specs=pl.BlockSpec((None, W_BLOCK, 512), lambda b, n: (b, n, 0)),
        out_shape=jax.ShapeDtypeStruct((DEC_BATCH, DEC_SEQ, 512), BF),
        compiler_params=_cp(("parallel", "arbitrary"), 40), name="lat_swa",
    )(sink, bias, q, k, k, k, k, v, v, v, v)


def _rows_to_tiles(o_ref, val, tm):
    for k in range(D_CHUNKS):
        o_ref[pl.ds(k, tm, stride=D_CHUNKS), :] = val[:, k * LANE:(k + 1) * LANE]


def _post_kernel(x_ref, oda_ref, omla_ref, oswa_ref, gate_ref, mod_ref, n2_ref, wda_ref, wmla_ref, wswa_ref,
                 wo_ref, wrh_ref, wrl_ref, xo_ref, h2_ref, lg_ref):
    m = mod_ref[...]
    g = gate_ref[...].astype(F32)
    merged = (g[:, 0:D_MODEL] * _dot(oda_ref[...], wda_ref[...])
              + g[:, D_MODEL:2 * D_MODEL] * _dot(omla_ref[...], wmla_ref[...])
              + g[:, 2 * D_MODEL:] * _dot(oswa_ref[...], wswa_ref[...]))
    x = x_ref[...] + m[2:3] * _dot(merged.astype(BF), wo_ref[...])
    xo_ref[...] = x
    h2 = _rms(x, n2_ref[...]) * (1.0 + m[4:5]) + m[3:4]
    _rows_to_tiles(h2_ref, h2, TM_POST)
    hh, hl = _split(h2)
    lg_ref[...] = _dot_nt(wrh_ref[...], hh) + _dot_nt(wrl_ref[...], hh) + _dot_nt(wrh_ref[...], hl)


def _post_call(layer, x, oda, omla, oswa, gate, mod, n2, wda, wmla, wswa, wo, wrh, wrl, latent):
    n = x.shape[0]
    tm = TM_POST
    row = pl.BlockSpec((tm, D_MODEL), lambda i: (i, 0))
    o512 = pl.BlockSpec((tm, 512), lambda i: (i, 0))
    return pl.pallas_call(
        _post_kernel, grid=(n // tm,),
        in_specs=[row, o512, o512, o512, pl.BlockSpec((tm, GATE_COLS), lambda i: (i, 0)),
                  _mod_spec(layer, tm, latent), _layer(layer, (1, D_MODEL)),
                  _layer(layer, (512, D_MODEL)), _layer(layer, (512, D_MODEL)), _layer(layer, (512, D_MODEL)),
                  _layer(layer, (D_MODEL, D_MODEL)), _layer(layer, (N_EXPERTS, D_MODEL)),
                  _layer(layer, (N_EXPERTS, D_MODEL))],
        out_specs=[row, pl.BlockSpec((tm * D_CHUNKS, LANE), lambda i: (i, 0)),
                   pl.BlockSpec((N_EXPERTS, tm), lambda i: (0, i))],
        out_shape=[jax.ShapeDtypeStruct((n, D_MODEL), F32),
                   jax.ShapeDtypeStruct((n * D_CHUNKS, LANE), F32),
                   jax.ShapeDtypeStruct((N_EXPERTS, n), F32)],
        compiler_params=_cp(("parallel",), 48), name="post",
    )(x, oda, omla, oswa, gate, mod, n2, wda, wmla, wswa, wo, wrh, wrl)


def _router_kernel(cap, lg_ref, q_ref, a_ref, lo_ref, hi_ref, cnt_ref):
    e_n, n = lg_ref.shape
    lg = lg_ref[...]
    shifted = lg - jnp.max(lg, axis=0, keepdims=True)
    ex = jnp.exp(shifted)
    den = jnp.sum(ex, axis=0, keepdims=True)
    aff = ex / den
    score = shifted - jnp.log(den)
    capf = float(cap)

    def count(mask):
        return jnp.sum(jnp.where(mask, 1.0, 0.0), axis=1, keepdims=True)

    def halve(_, bounds):
        lo, hi = bounds
        mid = 0.5 * (lo + hi)
        enough = count(score >= mid) >= capf
        return jnp.where(enough, mid, lo), jnp.where(enough, hi, mid)

    lo, hi = lax.fori_loop(0, THRESHOLD_STEPS, halve,
                           (jnp.min(score, axis=1, keepdims=True), jnp.ones((e_n, 1), F32)))
    gt = score >= hi
    eq = (score >= lo) & (score < hi)
    need = capf - count(gt)
    tok = lax.broadcasted_iota(I32, (e_n, n), 1)
    n_bits = n.bit_length()

    def tie_bit(i, bound):
        cand = bound | jnp.left_shift(jnp.int32(1), n_bits - 1 - i)
        ok = (cand <= n) & (count(eq & (tok < cand)) <= need)
        return jnp.where(ok, cand, bound)

    bound = lax.fori_loop(0, n_bits, tie_bit, jnp.zeros((e_n, 1), I32))
    sel = gt | (eq & (tok < bound))
    cnt_ref[...] = jnp.broadcast_to(count(sel & (tok < TOK_HALF)), cnt_ref.shape)
    blk = 2 * LANE
    per_chunk = ROUTE_CHUNK // blk
    n_chunks = n // ROUTE_CHUNK
    upper = (lax.broadcasted_iota(I32, (blk, blk), 0) <= lax.broadcasted_iota(I32, (blk, blk), 1))
    upper = jnp.where(upper, 1.0, 0.0).astype(BF)
    lane = _lane((e_n, LANE))
    carry = jnp.zeros((e_n, 1), F32)
    c_in = jnp.zeros((e_n, LANE), F32)
    c_ex = jnp.zeros((e_n, LANE), F32)
    for c in range(n // blk):
        chunk, part = divmod(c, per_chunk)
        if part == 0:
            c_ex = jnp.where(lane == chunk, carry, c_ex)
            a_ref[chunk] = aff[:, chunk * ROUTE_CHUNK:(chunk + 1) * ROUTE_CHUNK]
        s_blk = sel[:, c * blk:(c + 1) * blk]
        rank = _dot(jnp.where(s_blk, 1.0, 0.0).astype(BF), upper) + carry
        q_ref[chunk, :, part * blk:(part + 1) * blk] = jnp.where(s_blk, rank, 0.0)
        carry = rank[:, blk - 1:blk]
        if part == per_chunk - 1:
            c_in = jnp.where(lane == chunk, carry, c_in)
    valid = lane < n_chunks
    lo = jnp.zeros((e_n, LANE), I32)
    hi = jnp.zeros((e_n, LANE), I32)
    for rt in range(cap // ROUTE_TILE):
        lo_rt = count(valid & (c_in < float(rt * ROUTE_TILE + 1)))
        hi_rt = count(valid & (c_ex < float((rt + 1) * ROUTE_TILE)))
        lo = jnp.where(lane == rt, lo_rt.astype(I32), lo)
        hi = jnp.where(lane == rt, hi_rt.astype(I32), hi)
    lo_ref[...] = lo
    hi_ref[...] = hi


def _compact_kernel(lo_ref, hi_ref, q_ref, a_ref, idx_ref, g_ref):
    e = pl.program_id(0)
    n_tiles = idx_ref.shape[0]
    eye = lax.broadcasted_iota(I32, (ROUTE_TILE, LANE), 0) == lax.broadcasted_iota(I32, (ROUTE_TILE, LANE), 1)

    def per_tile(rt, _):
        slot = (rt * ROUTE_TILE + 1 + lax.broadcasted_iota(I32, (ROUTE_TILE, 1), 0)).astype(F32)

        def per_chunk(c, acc):
            idx_acc, g_acc = acc
            q_row = q_ref[c, pl.ds(e, 1), :]
            a_row = a_ref[c, pl.ds(e, 1), :]
            for part in range(ROUTE_CHUNK // LANE):
                lanes = slice(part * LANE, (part + 1) * LANE)
                hit = q_row[:, lanes] == slot
                tok = (c * ROUTE_CHUNK + part * LANE + lax.broadcasted_iota(I32, (1, LANE), 1)).astype(F32)
                idx_acc = idx_acc + jnp.where(hit, tok, 0.0)
                g_acc = g_acc + jnp.where(hit, a_row[:, lanes], 0.0)
            return idx_acc, g_acc

        zero = jnp.zeros((ROUTE_TILE, LANE), F32)
        idx_acc, g_acc = lax.fori_loop(lo_ref[e, rt], hi_ref[e, rt], per_chunk, (zero, zero))
        idx_v = jnp.sum(idx_acc, axis=1, keepdims=True)
        g_v = jnp.sum(g_acc, axis=1, keepdims=True)
        idx_ref[pl.ds(rt, 1), :] = jnp.sum(jnp.where(eye, idx_v, 0.0), axis=0, keepdims=True).astype(I32)
        g_ref[pl.ds(rt, 1), :] = jnp.sum(jnp.where(eye, g_v, 0.0), axis=0, keepdims=True)
        return 0

    lax.fori_loop(0, n_tiles, per_tile, 0)


def _router_call(logits_t, cap):
    n = logits_t.shape[1]
    n_chunks = n // ROUTE_CHUNK
    n_tiles = cap // ROUTE_TILE
    chunked = jax.ShapeDtypeStruct((n_chunks, N_EXPERTS, ROUTE_CHUNK), F32)
    q, a, lo, hi, cnt = pl.pallas_call(
        functools.partial(_router_kernel, cap),
        out_shape=[chunked, chunked, jax.ShapeDtypeStruct((N_EXPERTS, LANE), I32),
                   jax.ShapeDtypeStruct((N_EXPERTS, LANE), I32), jax.ShapeDtypeStruct((N_EXPERTS, LANE), F32)],
        compiler_params=pltpu.CompilerParams(vmem_limit_bytes=40 * 1024 * 1024), name="router",
    )(logits_t)
    whole = pl.BlockSpec((n_chunks, N_EXPERTS, ROUTE_CHUNK), lambda e, *_: (0, 0, 0))
    slots = pl.BlockSpec((None, n_tiles, ROUTE_TILE), lambda e, *_: (e, 0, 0))
    idx, g = pl.pallas_call(
        _compact_kernel,
        grid_spec=pltpu.PrefetchScalarGridSpec(
            num_scalar_prefetch=2, grid=(N_EXPERTS,), in_specs=[whole, whole], out_specs=[slots, slots]),
        out_shape=[jax.ShapeDtypeStruct((N_EXPERTS, n_tiles, ROUTE_TILE), I32),
                   jax.ShapeDtypeStruct((N_EXPERTS, n_tiles, ROUTE_TILE), F32)],
        compiler_params=_cp(("parallel",), 32), name="compact",
    )(lo[:, :n_tiles], hi[:, :n_tiles], q, a)
    return idx.reshape(N_EXPERTS, cap), g.reshape(N_EXPERTS, cap), cnt


def _row_copy(src_hbm, xe, sem, buf, tok, slot):
    dst = xe.at[buf, pl.ds(pl.multiple_of(slot * SUB, SUB), SUB)]
    return pltpu.make_async_copy(src_hbm.at[tok], dst, sem.at[buf])


def _rows_wait(rows_hbm, xe, sem, buf, slot, n):
    dst = xe.at[buf, pl.ds(pl.multiple_of(slot * SUB, SUB), n * SUB)]
    return pltpu.make_async_copy(rows_hbm.at[pl.ds(0, n * SUB)], dst, sem.at[buf])


def _ffn_kernel(idx_ref, idxn_ref, hc_hbm, hl_hbm, rows_hbm, w1_ref, w3_ref, w2_ref, yc_ref, yl_ref,
                xe, xb, w1b, w3b, w2b, sem):
    e = pl.program_id(0)
    f = pl.program_id(1)
    buf = e % 2

    @pl.when((e == 0) & (f == 0))
    def _():
        def start_ctx(r, _):
            _row_copy(hc_hbm, xe, sem, 0, idx_ref[0, r], r).start()
            return 0

        def start_lat(r, _):
            _row_copy(hl_hbm, xe, sem, 0, idx_ref[0, r], r).start()
            return 0

        lax.fori_loop(0, CAP_CTX, start_ctx, 0)
        lax.fori_loop(CAP_CTX, SLOTS, start_lat, 0)
        _rows_wait(rows_hbm, xe, sem, 0, 0, SLOTS).wait()

    @pl.when(f == 0)
    def _():
        for k in range(D_CHUNKS):
            xb[:, k * LANE:(k + 1) * LANE] = xe[buf, pl.ds(k, SLOTS, stride=D_CHUNKS), :].astype(BF)

    w1b[...] = w1_ref[...].astype(BF)
    w3b[...] = w3_ref[...].astype(BF)
    w2b[...] = w2_ref[...].astype(BF)

    part_c, part_l = CAP_CTX // FF_SPLIT, CAP_LAT // FF_SPLIT
    next_rows = ([(hc_hbm, f * part_c + u) for u in range(part_c)]
                 + [(hl_hbm, CAP_CTX + f * part_l + u) for u in range(part_l)])

    def next_copy(src, r):
        return _row_copy(src, xe, sem, 1 - buf, idxn_ref[0, r], r)

    n_tiles = SLOTS // FFN_ROWS
    for t in range(n_tiles):
        if t < n_tiles - 1:
            for src, r in next_rows[t::n_tiles - 1]:
                next_copy(src, r).start()
        rows = slice(t * FFN_ROWS, (t + 1) * FFN_ROWS)
        x = xb[rows, :]
        a = _dot(x, w1b[...])
        hid = (a * jax.nn.sigmoid(a) * _dot(x, w3b[...])).astype(BF)
        y = _dot(hid, w2b[...])
        lo, hi = t * FFN_ROWS, (t + 1) * FFN_ROWS
        segs = []
        if lo < CAP_CTX:
            segs.append((yc_ref, lo, 0, min(hi, CAP_CTX) - lo))
        if hi > CAP_CTX:
            first = max(lo, CAP_CTX)
            segs.append((yl_ref, first - CAP_CTX, first - lo, hi - first))

        def pieces():
            for y_ref, row0, y0, n in segs:
                for k in range(D_CHUNKS):
                    yield y_ref, pl.ds(row0 * D_CHUNKS + k, n, stride=D_CHUNKS), y[y0:y0 + n, k * LANE:(k + 1) * LANE]

        @pl.when(f == 0)
        def _():
            for y_ref, dst, val in pieces():
                y_ref[dst, :] = val

        @pl.when(f > 0)
        def _():
            for y_ref, dst, val in pieces():
                y_ref[dst, :] = y_ref[dst, :] + val

    _rows_wait(rows_hbm, xe, sem, 1 - buf, f * part_c, part_c).wait()
    _rows_wait(rows_hbm, xe, sem, 1 - buf, CAP_CTX + f * part_l, part_l).wait()


def _ffn_call(layer, idx, h_ctx, h_lat, w1, w3, w2):
    ff = EXPERT_FF // FF_SPLIT
    wspec = pl.BlockSpec((None, None, D_MODEL, ff), lambda e, f: (layer, e, 0, f))
    last = N_EXPERTS - 1
    return pl.pallas_call(
        _ffn_kernel,
        grid=(N_EXPERTS, FF_SPLIT),
        in_specs=[pl.BlockSpec((None, 1, SLOTS), lambda e, f: (e, 0, 0), memory_space=pltpu.SMEM),
                  pl.BlockSpec((None, 1, SLOTS), lambda e, f: (jnp.minimum(e + 1, last), 0, 0),
                               memory_space=pltpu.SMEM),
                  pl.BlockSpec(memory_space=pl.ANY), pl.BlockSpec(memory_space=pl.ANY),
                  pl.BlockSpec(memory_space=pl.ANY), wspec, wspec, pl.BlockSpec((None, None, ff, D_MODEL), lambda e, f: (layer, e, f, 0))],
        out_specs=[pl.BlockSpec((None, CAP_CTX * D_CHUNKS, LANE), lambda e, f: (e, 0, 0)),
                   pl.BlockSpec((None, CAP_LAT * D_CHUNKS, LANE), lambda e, f: (e, 0, 0))],
        out_shape=[jax.ShapeDtypeStruct((N_EXPERTS, CAP_CTX * D_CHUNKS, LANE), F32),
                   jax.ShapeDtypeStruct((N_EXPERTS, CAP_LAT * D_CHUNKS, LANE), F32)],
        scratch_shapes=[pltpu.VMEM((2, SLOTS * D_CHUNKS, LANE), F32), pltpu.VMEM((SLOTS, D_MODEL), BF),
                        pltpu.VMEM((D_MODEL, ff), BF), pltpu.VMEM((D_MODEL, ff), BF),
                        pltpu.VMEM((ff, D_MODEL), BF), pltpu.SemaphoreType.DMA((2,))],
        compiler_params=_cp(("arbitrary", "arbitrary"), 54), name="ffn",
    )(idx, idx, h_ctx, h_lat, h_lat.reshape(N_LAT * SUB, LANE), w1, w3, w2)


def _combine_kernel(idx_ref, g_ref, bnd_ref, yc_ref, yl_ref, acc_ref):
    h = pl.program_id(0)
    e = pl.program_id(1)

    @pl.when(e == 0)
    def _():
        acc_ref[...] = jnp.zeros(acc_ref.shape, F32)

    def add_rows(y_ref, slot0, lo, hi, base):
        def group(i, _):
            r0 = lo + i * COMBINE_GROUP
            toks = [idx_ref[0, r0 + u] - base for u in range(COMBINE_GROUP)]
            sums = [acc_ref[toks[u]] + y_ref[r0 - slot0 + u] * g_ref[0, r0 + u] for u in range(COMBINE_GROUP)]
            for u in range(COMBINE_GROUP):
                acc_ref[toks[u]] = sums[u]
            return 0

        def single(r, _):
            t = idx_ref[0, r] - base
            acc_ref[t] = acc_ref[t] + y_ref[r - slot0] * g_ref[0, r]
            return 0

        n_groups = (hi - lo) // COMBINE_GROUP
        lax.fori_loop(0, n_groups, group, 0)
        lax.fori_loop(lo + n_groups * COMBINE_GROUP, hi, single, 0)

    @pl.when(h == 0)
    def _():
        add_rows(yc_ref, 0, 0, CAP_CTX, 0)

    @pl.when(h > 0)
    def _():
        add_rows(yl_ref, CAP_CTX, bnd_ref[0, h], bnd_ref[0, h + 1], (h - 1) * TOK_HALF)


def _combine_call(idx, g, bnd, y_ctx, y_lat):
    last = N_EXPERTS - 1
    return pl.pallas_call(
        _combine_kernel,
        grid=(N_HALVES, N_EXPERTS),
        in_specs=[pl.BlockSpec((None, 1, SLOTS), lambda h, e: (e, 0, 0), memory_space=pltpu.SMEM),
                  pl.BlockSpec((None, 1, SLOTS), lambda h, e: (e, 0, 0), memory_space=pltpu.SMEM),
                  pl.BlockSpec((None, 1, N_HALVES + 1), lambda h, e: (e, 0, 0), memory_space=pltpu.SMEM),
                  pl.BlockSpec((None, CAP_CTX, SUB, LANE), lambda h, e: (jnp.where(h == 0, e, last), 0, 0, 0)),
                  pl.BlockSpec((None, CAP_LAT, SUB, LANE), lambda h, e: (jnp.where(h == 0, 0, e), 0, 0, 0))],
        out_specs=pl.BlockSpec((None, TOK_HALF, SUB, LANE), lambda h, e: (h, 0, 0, 0)),
        out_shape=jax.ShapeDtypeStruct((N_HALVES, TOK_HALF, SUB, LANE), F32),
        compiler_params=_cp(("parallel", "arbitrary"), 52), name="combine",
    )(idx, g, bnd, y_ctx.reshape(N_EXPERTS, CAP_CTX, SUB, LANE), y_lat.reshape(N_EXPERTS, CAP_LAT, SUB, LANE))


def _final_kernel(x_ref, moe_ref, mod_ref, g_ref, o_ref):
    x = x_ref[...] + mod_ref[...][5:6] * _moe_rows(moe_ref, TM_FINAL)
    o_ref[...] = _rms(x, g_ref[...])


def _final_call(x, moe, mod, g, latent):
    n = x.shape[0]
    tm = TM_FINAL
    moe_off = (N_CTX // tm) if latent else 0
    row = pl.BlockSpec((tm, D_MODEL), lambda i: (i, 0))
    return pl.pallas_call(
        _final_kernel, grid=(n // tm,),
        in_specs=[row, pl.BlockSpec((tm * D_CHUNKS, LANE), lambda i: (i + moe_off, 0)),
                  _mod_spec(DEPTH - 1, tm, latent), _layer(0, (1, D_MODEL))],
        out_specs=row, out_shape=jax.ShapeDtypeStruct((n, D_MODEL), F32),
        compiler_params=_cp(("parallel",), 32), name="final",
    )(x, moe, mod, g)


def _axial_tables(rot_dim):
    rows = DEC_SEQ // GRID_W
    row = jnp.repeat(jnp.arange(rows, dtype=F32), GRID_W)
    col = jnp.tile(jnp.arange(GRID_W, dtype=F32), rows)
    n_freq = rot_dim // 4
    inv = ROPE_BASE ** (-jnp.arange(n_freq, dtype=F32) / n_freq)
    ang = jnp.concatenate([row[:, None] * inv, col[:, None] * inv], axis=-1)
    sign = jnp.tile(jnp.array([-1.0, 1.0], F32), rot_dim // 2)
    return jnp.repeat(jnp.cos(ang), 2, axis=1), jnp.repeat(jnp.sin(ang), 2, axis=1) * sign


def _rope_tables():
    c64, s64 = _axial_tables(DA_DH)
    c32, s32 = _axial_tables(MLA_ROPE)
    one = lambda w: jnp.ones((DEC_SEQ, w), F32)
    zero = lambda w: jnp.zeros((DEC_SEQ, w), F32)
    pad = LANE - MLA_NOPE - MLA_ROPE
    return jnp.stack([
        jnp.tile(c64, (1, 2)), jnp.tile(s64, (1, 2)),
        jnp.concatenate([one(MLA_NOPE), c32, one(pad)], axis=1),
        jnp.concatenate([zero(MLA_NOPE), s32, zero(pad)], axis=1),
        jnp.concatenate([c32, one(LANE - MLA_ROPE)], axis=1),
        jnp.concatenate([s32, zero(LANE - MLA_ROPE)], axis=1)])


def _prep_weights(w_in, w_gate, mla_w_qb, mla_w_kvb, w_br_da, w_br_mla, w_br_swa, w_o, w_router):
    kr_end = C_MKR + MLA_ROPE
    win = jnp.concatenate([w_in[:, :, :kr_end].astype(BF), jnp.zeros((DEPTH, D_MODEL, LANE - MLA_ROPE), BF),
                           w_in[:, :, kr_end:].astype(BF)], axis=2)
    dk = MLA_NOPE + MLA_ROPE
    wqb = jnp.pad(mla_w_qb.reshape(DEPTH, MLA_Q_RANK, MLA_HEADS, dk), ((0, 0), (0, 0), (0, 0), (0, LANE - dk)))
    wqb = wqb.reshape(DEPTH, MLA_Q_RANK, MLA_HEADS * LANE).astype(BF)
    kvb = mla_w_kvb.reshape(DEPTH, MLA_KV_RANK, MLA_HEADS, MLA_NOPE + MLA_V)
    wk = jnp.pad(kvb[..., :MLA_NOPE], ((0, 0), (0, 0), (0, 0), (0, LANE - MLA_NOPE)))
    wk = wk.reshape(DEPTH, MLA_KV_RANK, MLA_HEADS * LANE)
    wv = kvb[..., MLA_NOPE:].reshape(DEPTH, MLA_KV_RANK, MLA_OUT)
    wkv = jnp.concatenate([wk, wv], axis=2).astype(BF)
    wrh, wrl = _split(jnp.swapaxes(w_router, 1, 2))
    return dict(win=win, wg=w_gate.astype(BF), wqb=wqb, wkv=wkv, wda=w_br_da.astype(BF),
                wmla=w_br_mla.astype(BF), wswa=w_br_swa.astype(BF), wo=w_o.astype(BF), wrh=wrh, wrl=wrl)


def kernel(x_prompt, x_sample, cache_da_k, cache_da_v, cache_mla_ckv, cache_mla_krope, cache_swa_k, cache_swa_v, c, c_ctx, w_ada, b_ada, norm1, norm2, w_in, da_lq1, da_lk1, da_lq2, da_lk2, da_subln, mla_q_norm, mla_w_qb, mla_kv_norm, mla_w_kvb, swa_sink, w_gate, w_br_da, w_br_mla, w_br_swa, w_o, w_router, w_ff1, w_ff3, w_ff2, final_norm):
    c_all = jnp.concatenate([c_ctx[None], c, jnp.zeros((SUB - 1 - DEC_BATCH, D_MODEL), F32)], axis=0)
    mod = _ada_call(c_all, w_ada, b_ada)[:, :1 + DEC_BATCH].reshape(DEPTH, 1 + DEC_BATCH, 6, D_MODEL)
    lam_all = _lam_call(da_lq1, da_lk1, da_lq2, da_lk2)
    tab = _rope_tables()
    swa_bias = _swa_bias()
    w = _prep_weights(w_in, w_gate, mla_w_qb, mla_w_kvb, w_br_da, w_br_mla, w_br_swa, w_o, w_router)
    n1, n2 = norm1[:, None], norm2[:, None]
    subln, qn, kvn = da_subln[:, None], mla_q_norm[:, None], mla_kv_norm[:, None]
    caches = (cache_da_k.reshape(DEC_BATCH, DEPTH, PAST_LEN, 512),
              cache_da_v.reshape(DEC_BATCH, DEPTH, PAST_LEN, 512),
              cache_mla_ckv,
              jnp.pad(cache_mla_krope, ((0, 0), (0, 0), (0, 0), (0, LANE - MLA_ROPE))),
              cache_swa_k.reshape(DEC_BATCH, DEPTH, PAST_LEN, LANE),
              cache_swa_v.reshape(DEC_BATCH, DEPTH, PAST_LEN, LANE))

    xp = x_prompt.reshape(N_CTX, D_MODEL)
    xs = x_sample.reshape(N_LAT, D_MODEL)
    moe = None
    new = [[] for _ in range(6)]
    for l in range(DEPTH):
        lam = lam_all[l, :1]
        sink = swa_sink[l]

        xp, proj_c, gate_c = _pre_call(l, xp, moe, mod, n1, w["win"], w["wg"], latent=False)
        xs, proj_l, gate_l = _pre_call(l, xs, moe, mod, n1, w["win"], w["wg"], latent=True)

        oda_c, omla_c, oswa_c, ckv_c, k_c, v_c, kr_c, sk_c, sv_c = _ctx_attn_call(
            l, lam, sink, proj_c, subln, qn, w["wqb"], kvn, w["wkv"])
        new[0].append(k_c.reshape(BATCH, SEQ, DA_HEADS, 2 * DA_DH))
        new[1].append(v_c.reshape(BATCH, SEQ, DA_HEADS, 2 * DA_DH))
        new[2].append(ckv_c.reshape(BATCH, SEQ, MLA_KV_RANK))
        new[3].append(kr_c.reshape(BATCH, SEQ, MLA_ROPE))
        new[4].append(sk_c.reshape(BATCH, SEQ, SWA_KV_HEADS, SWA_DH))
        new[5].append(sv_c.reshape(BATCH, SEQ, SWA_KV_HEADS, SWA_DH))

        daq, mlaq, swaq, dak, dav, mlak, mlav, swak, swav = _lat_prep_call(
            l, proj_l, tab, caches, qn, w["wqb"], kvn, w["wkv"])
        oda_l = _lat_da_call(l, lam, daq, dak, dav, subln).reshape(N_LAT, 512)
        omla_l = _lat_mla_call(mlaq, mlak, mlav).reshape(N_LAT, 512)
        oswa_l = _lat_swa_call(sink, swa_bias, swaq, swak, swav).reshape(N_LAT, 512)

        post_w = (w["wda"], w["wmla"], w["wswa"], w["wo"], w["wrh"], w["wrl"])
        xp, h2_c, lg_c = _post_call(l, xp, oda_c, omla_c, oswa_c, gate_c, mod, n2, *post_w, latent=False)
        xs, h2_l, lg_l = _post_call(l, xs, oda_l, omla_l, oswa_l, gate_l, mod, n2, *post_w, latent=True)

        idx_c, g_c, _ = _router_call(lg_c, CAP_CTX)
        idx_l, g_l, cnt_l = _router_call(lg_l, CAP_LAT)
        idx = jnp.concatenate([idx_c, idx_l], axis=1).reshape(N_EXPERTS, 1, SLOTS)
        gsel = jnp.concatenate([g_c, g_l], axis=1).reshape(N_EXPERTS, 1, SLOTS)
        n0 = cnt_l[:, 0].astype(I32)
        bnd = jnp.stack([jnp.zeros_like(n0), jnp.full_like(n0, CAP_CTX), CAP_CTX + n0,
                         jnp.full_like(n0, SLOTS)], axis=1)
        y_c, y_l = _ffn_call(l, idx, h2_c.reshape(N_CTX, SUB, LANE), h2_l.reshape(N_LAT, SUB, LANE),
                             w_ff1, w_ff3, w_ff2)
        acc = _combine_call(idx, gsel, bnd.reshape(N_EXPERTS, 1, N_HALVES + 1), y_c, y_l)
        moe = acc.reshape(N_HALVES * TOK_HALF * D_CHUNKS, LANE)

    fn = final_norm[None, None]
    y_prompt = _final_call(xp, moe, mod, fn, latent=False).reshape(BATCH, SEQ, D_MODEL)
    y_sample = _final_call(xs, moe, mod, fn, latent=True).reshape(DEC_BATCH, DEC_SEQ, D_MODEL)
    return (y_prompt, y_sample) + tuple(jnp.stack(n, axis=1) for n in new)
```

```python
import functools
import math

import jax
import jax.numpy as jnp
from jax import lax
from jax.experimental import pallas as pl
from jax.experimental.pallas import tpu as pltpu

F32 = jnp.float32
BF = jnp.bfloat16
I32 = jnp.int32

D_MODEL = 1024
BATCH = 16
SEQ = 256
DEPTH = 2
DEC_BATCH = 2
DEC_SEQ = 4096
PAST_LEN = 512
GRID_W = 64
ROPE_BASE = 10000.0
EPS = 1e-6
DA_HEADS = 4
DA_DH = 64
DA_OUT = DA_HEADS * 2 * DA_DH
MLA_HEADS = 8
MLA_Q_RANK = 256
MLA_KV_RANK = 128
MLA_NOPE = 64
MLA_ROPE = 32
MLA_V = 64
MLA_OUT = MLA_HEADS * MLA_V
SWA_HEADS = 8
SWA_KV_HEADS = 2
SWA_DH = 64
W_BLOCK = 128
SWA_OUT = SWA_HEADS * SWA_DH
N_EXPERTS = 16
EXPERT_FF = 1024
CAPACITY_FACTOR = 2

LANE = 128
SUB = 8
HALF = 64
N_CTX = BATCH * SEQ
N_LAT = DEC_BATCH * DEC_SEQ
KEYS = PAST_LEN + DEC_SEQ
CAP_CTX = CAPACITY_FACTOR * N_CTX // N_EXPERTS
CAP_LAT = CAPACITY_FACTOR * N_LAT // N_EXPERTS
SLOTS = CAP_CTX + CAP_LAT
TOK_HALF = 4096
N_HALVES = (N_CTX + N_LAT) // TOK_HALF
D_CHUNKS = D_MODEL // LANE

C_DAQ, C_DAK, C_DAV = 0, 512, 1024
C_MQ, C_MKV, C_MKR = 1536, 1792, 1920
C_SQ, C_SK, C_SV = 2048, 2560, 2688
PROJ_COLS = 2816
GATE_COLS = 3 * D_MODEL
MLA_SCALE = (MLA_NOPE + MLA_ROPE) ** -0.5
HEAD_SCALE = DA_DH ** -0.5
LOG2E = math.log2(math.e)

TM_PRE = 256
TM_FINAL = 512
TM_POST = 512
CHUNK = 512
TQ = 256
DA_STEP_HEADS = 4
MLA_STEP_PAIRS = 4
ROUTE_TILE = 128
ROUTE_CHUNK = 512
THRESHOLD_STEPS = 48
FFN_ROWS = 512
FF_SPLIT = 2
COMBINE_GROUP = 8


def _cp(sem, vmem_mb):
    return pltpu.CompilerParams(dimension_semantics=sem, vmem_limit_bytes=vmem_mb * 1024 * 1024)


def _dot(a, b):
    return jnp.dot(a, b, preferred_element_type=F32)


def _dot_nt(a, b):
    return lax.dot_general(a, b, (((1,), (1,)), ((), ())), preferred_element_type=F32)


def _split(a):
    hi = a.astype(BF)
    lo = (a - hi.astype(F32)).astype(BF)
    return hi, lo


def _dot3(a, w):
    ah, al = _split(a)
    wh, wl = _split(w)
    return _dot(ah, wh) + _dot(ah, wl) + _dot(al, wh)


def _rms(x, g):
    return x * lax.rsqrt(jnp.mean(x * x, axis=-1, keepdims=True) + EPS) * g


def _lane(shape):
    return lax.broadcasted_iota(I32, shape, len(shape) - 1)


def _softmax_pv(segs, sink=None):
    m = None
    for s, _ in segs:
        ms = jnp.max(s, axis=-1, keepdims=True)
        m = ms if m is None else jnp.maximum(m, ms)
    if sink is not None:
        m = jnp.maximum(m, sink)
    l = None
    o = None
    for s, v in segs:
        e = jnp.exp(s - m)
        ls = jnp.sum(e, axis=-1, keepdims=True)
        os_ = _dot(e.astype(BF), v)
        l = ls if l is None else l + ls
        o = os_ if o is None else o + os_
    if sink is not None:
        l = l + jnp.exp(sink - m)
    return o / l


def _mask_halves(q):
    lo = _lane(q.shape) < HALF
    return jnp.where(lo, q, 0.0).astype(BF), jnp.where(lo, 0.0, q).astype(BF)


def _pair(o_a, o_b):
    return jnp.where(_lane(o_a.shape) < HALF, o_a, o_b)


def _dup(x):
    r = pltpu.roll(x, HALF, 1)
    lo = _lane(x.shape) < HALF
    return jnp.where(lo, x, r), jnp.where(lo, r, x)


def _rope(x, c, s):
    n = x.shape[-1]
    even = (_lane(x.shape) % 2) == 0
    sw = jnp.where(even, pltpu.roll(x, n - 1, 1), pltpu.roll(x, 1, 1))
    return x * c + sw * s


def _da_head(q, k, v, lam, subln, lam_init):
    q1, q2 = _mask_halves(q)
    o1 = _softmax_pv([(_dot_nt(q1, k), v)])
    o2 = _softmax_pv([(_dot_nt(q2, k), v)])
    return _rms(o1 - lam * o2, subln) * (1.0 - lam_init)


def _ada_kernel(c_ref, w_ref, b_ref, o_ref):
    c = c_ref[...]
    a = c * jax.nn.sigmoid(c)
    o_ref[...] = _dot3(a, w_ref[...]) + b_ref[...]


def _ada_call(c_all, w_ada, b_ada):
    tn = 1536
    return pl.pallas_call(
        _ada_kernel,
        grid=(DEPTH, 6 * D_MODEL // tn),
        in_specs=[pl.BlockSpec((SUB, D_MODEL), lambda l, j: (0, 0)),
                  pl.BlockSpec((None, D_MODEL, tn), lambda l, j: (l, 0, j)),
                  pl.BlockSpec((None, 1, tn), lambda l, j: (l, 0, j))],
        out_specs=pl.BlockSpec((None, SUB, tn), lambda l, j: (l, 0, j)),
        out_shape=jax.ShapeDtypeStruct((DEPTH, SUB, 6 * D_MODEL), F32),
        compiler_params=_cp(("parallel", "parallel"), 40),
        name="ada",
    )(c_all, w_ada, b_ada.reshape(DEPTH, 1, 6 * D_MODEL))


def _lam_kernel(q1, k1, q2, k2, o_ref):
    s1 = jnp.sum(q1[...] * k1[...], axis=-1, keepdims=True)
    s2 = jnp.sum(q2[...] * k2[...], axis=-1, keepdims=True)
    row = lax.broadcasted_iota(I32, (DEPTH, 1), 0)
    init = jnp.zeros((DEPTH, 1), F32)
    for l in range(DEPTH):
        init = jnp.where(row == l, _lam_init(l), init)
    o_ref[...] = jnp.broadcast_to(jnp.exp(s1) - jnp.exp(s2) + init, o_ref.shape)


def _lam_init(layer):
    return 0.8 - 0.6 * math.exp(-0.3 * layer)


def _lam_call(q1, k1, q2, k2):
    return pl.pallas_call(
        _lam_kernel,
        out_shape=jax.ShapeDtypeStruct((DEPTH, LANE), F32),
        name="lam",
    )(q1, k1, q2, k2)


def _moe_rows(moe_ref, tm):
    return jnp.concatenate([moe_ref[pl.ds(k, tm, stride=D_CHUNKS), :] for k in range(D_CHUNKS)], axis=1)


def _pre_body(x, mod_ref, n1_ref, win_ref, wg_ref, proj_ref, gate_ref):
    m = mod_ref[...]
    h = _rms(x, n1_ref[...]) * (1.0 + m[1:2]) + m[0:1]
    hb = h.astype(BF)
    proj_ref[...] = _dot(hb, win_ref[...]).astype(proj_ref.dtype)
    gate_ref[...] = jax.nn.sigmoid(_dot(hb, wg_ref[...])).astype(BF)


def _pre_first_kernel(x_ref, mod_ref, n1_ref, win_ref, wg_ref, proj_ref, gate_ref):
    _pre_body(x_ref[...], mod_ref, n1_ref, win_ref, wg_ref, proj_ref, gate_ref)


def _pre_next_kernel(x_ref, moe_ref, modp_ref, mod_ref, n1_ref, win_ref, wg_ref, xo_ref, proj_ref, gate_ref):
    x = x_ref[...] + modp_ref[...][5:6] * _moe_rows(moe_ref, TM_PRE)
    xo_ref[...] = x
    _pre_body(x, mod_ref, n1_ref, win_ref, wg_ref, proj_ref, gate_ref)


def _mod_spec(layer, tm, latent):
    per = DEC_SEQ // tm
    if latent:
        return pl.BlockSpec((None, None, 6, D_MODEL), lambda i: (layer, 1 + i // per, 0, 0))
    return pl.BlockSpec((None, None, 6, D_MODEL), lambda i: (layer, 0, 0, 0))


def _layer(layer, shape):
    nd = len(shape)
    return pl.BlockSpec((None,) + shape, lambda *_: (layer,) + (0,) * nd, pipeline_mode=pl.Buffered(1))


def _pre_call(layer, x, moe, mod, n1, win, wg, latent):
    n = x.shape[0]
    tm = TM_PRE
    row = pl.BlockSpec((tm, D_MODEL), lambda i: (i, 0))
    w_specs = [_mod_spec(layer, tm, latent), _layer(layer, (1, D_MODEL)), _layer(layer, (D_MODEL, PROJ_COLS)),
               _layer(layer, (D_MODEL, GATE_COLS))]
    outs = [jax.ShapeDtypeStruct((n, PROJ_COLS), BF if latent else F32), jax.ShapeDtypeStruct((n, GATE_COLS), BF)]
    out_specs = [pl.BlockSpec((tm, PROJ_COLS), lambda i: (i, 0)), pl.BlockSpec((tm, GATE_COLS), lambda i: (i, 0))]
    if moe is None:
        proj, gate = pl.pallas_call(
            _pre_first_kernel, grid=(n // tm,), in_specs=[row] + w_specs, out_specs=out_specs, out_shape=outs,
            compiler_params=_cp(("parallel",), 52), name="pre_first",
        )(x, mod, n1, win, wg)
        return x, proj, gate
    moe_off = (N_CTX // tm) if latent else 0
    moe_spec = pl.BlockSpec((tm * D_CHUNKS, LANE), lambda i: (i + moe_off, 0))
    xo, proj, gate = pl.pallas_call(
        _pre_next_kernel, grid=(n // tm,),
        in_specs=[row, moe_spec, _mod_spec(layer - 1, tm, latent)] + w_specs,
        out_specs=[row] + out_specs,
        out_shape=[jax.ShapeDtypeStruct((n, D_MODEL), F32)] + outs,
        compiler_params=_cp(("parallel",), 52), name="pre_next",
    )(x, moe, mod, mod, n1, win, wg)
    return xo, proj, gate


def _ctx_attn_kernel(lam_init, lam_ref, sink_ref, p_ref, subln_ref, qn_ref, wqb_ref, kvn_ref, wkv_ref,
                     oda_ref, omla_ref, oswa_ref, ckv_ref, kc_ref, vc_ref, kr_ref, sk_ref, sv_ref):
    p = p_ref[...]
    lam = lam_ref[0]
    for h in range(DA_HEADS):
        kc_ref[pl.ds(h, SEQ, stride=DA_HEADS), :] = p[:, C_DAK + h * LANE:C_DAK + (h + 1) * LANE]
        vc_ref[pl.ds(h, SEQ, stride=DA_HEADS), :] = p[:, C_DAV + h * LANE:C_DAV + (h + 1) * LANE]
    kr_ref[...] = p[:, C_MKR:C_MKR + MLA_ROPE]
    sk_ref[...] = p[:, C_SK:C_SK + LANE]
    sv_ref[...] = p[:, C_SV:C_SV + LANE]
    for h in range(DA_HEADS):
        blk = slice(h * LANE, (h + 1) * LANE)
        q = p[:, C_DAQ + h * LANE:C_DAQ + (h + 1) * LANE] * HEAD_SCALE
        k = p[:, C_DAK + h * LANE:C_DAK + (h + 1) * LANE].astype(BF)
        v = p[:, C_DAV + h * LANE:C_DAV + (h + 1) * LANE].astype(BF)
        oda_ref[:, blk] = _da_head(q, k, v, lam, subln_ref[...], lam_init).astype(BF)
    qn = _rms(p[:, C_MQ:C_MQ + MLA_Q_RANK], qn_ref[...]).astype(BF)
    cq = _dot(qn, wqb_ref[...])
    ckv = _rms(p[:, C_MKV:C_MKV + MLA_KV_RANK], kvn_ref[...])
    ckv_ref[...] = ckv
    kv = _dot(ckv.astype(BF), wkv_ref[...])
    kr_sh = pltpu.roll(p[:, C_MKR:C_MKR + LANE], HALF, 1)
    for m in range(MLA_HEADS // 2):
        vpair = kv[:, MLA_HEADS * LANE + m * LANE:MLA_HEADS * LANE + (m + 1) * LANE].astype(BF)
        outs = []
        for h in (2 * m, 2 * m + 1):
            q = (cq[:, h * LANE:(h + 1) * LANE] * MLA_SCALE).astype(BF)
            k = (kv[:, h * LANE:(h + 1) * LANE] + kr_sh).astype(BF)
            outs.append(_softmax_pv([(_dot_nt(q, k), vpair)]))
        omla_ref[:, m * LANE:(m + 1) * LANE] = _pair(outs[0], outs[1]).astype(BF)
    kd = _dup(p[:, C_SK:C_SK + LANE])
    vd = _dup(p[:, C_SV:C_SV + LANE])
    for m in range(SWA_HEADS // 2):
        g = (2 * m) // (SWA_HEADS // SWA_KV_HEADS)
        k = kd[g].astype(BF)
        v = vd[g].astype(BF)
        qa, qb = _mask_halves(p[:, C_SQ + m * LANE:C_SQ + (m + 1) * LANE] * HEAD_SCALE)
        oa = _softmax_pv([(_dot_nt(qa, k), v)], sink_ref[2 * m])
        ob = _softmax_pv([(_dot_nt(qb, k), v)], sink_ref[2 * m + 1])
        oswa_ref[:, m * LANE:(m + 1) * LANE] = _pair(oa, ob).astype(BF)


def _smem():
    return pl.BlockSpec(memory_space=pltpu.SMEM)


def _ctx_attn_call(layer, lam, sink, proj, subln, qn, wqb, kvn, wkv):
    row512 = pl.BlockSpec((SEQ, 512), lambda b: (b, 0))
    row128 = pl.BlockSpec((SEQ, LANE), lambda b: (b, 0))
    heads = pl.BlockSpec((SEQ * DA_HEADS, LANE), lambda b: (b, 0))
    return pl.pallas_call(
        functools.partial(_ctx_attn_kernel, _lam_init(layer)),
        grid=(BATCH,),
        in_specs=[_smem(), _smem(), pl.BlockSpec((SEQ, PROJ_COLS), lambda b: (b, 0)),
                  _layer(layer, (1, LANE)), _layer(layer, (1, MLA_Q_RANK)),
                  _layer(layer, (MLA_Q_RANK, MLA_HEADS * LANE)), _layer(layer, (1, MLA_KV_RANK)),
                  _layer(layer, (MLA_KV_RANK, MLA_HEADS * LANE + MLA_OUT))],
        out_specs=[row512, row512, row512, row128, heads, heads,
                   pl.BlockSpec((SEQ, MLA_ROPE), lambda b: (b, 0)), row128, row128],
        out_shape=[jax.ShapeDtypeStruct((N_CTX, 512), BF)] * 3
        + [jax.ShapeDtypeStruct((N_CTX, LANE), F32)]
        + [jax.ShapeDtypeStruct((N_CTX * DA_HEADS, LANE), F32)] * 2
        + [jax.ShapeDtypeStruct((N_CTX, MLA_ROPE), F32)] + [jax.ShapeDtypeStruct((N_CTX, LANE), F32)] * 2,
        compiler_params=_cp(("parallel",), 40), name="ctx_attn",
    )(lam, sink, proj, subln, qn, wqb, kvn, wkv)


def _lat_prep_kernel(p_ref, tab_ref, cdk_ref, cdv_ref, cckv_ref, ckr_ref, csk_ref, csv_ref,
                     qn_ref, wqb_ref, kvn_ref, wkv_ref,
                     daq_ref, mlaq_ref, swaq_ref, dak_ref, dav_ref, mlak_ref, mlav_ref, swak_ref, swav_ref):
    j = pl.program_id(1)

    def write_kv(dk, dv, ckv, kr_sh, sk, sv):
        ones_col = jnp.where(_lane((CHUNK, LANE)) == 0, 1.0, 0.0).astype(BF)
        dak_ref[...] = dk.astype(BF)
        for h in range(DA_HEADS):
            dav_ref[:, 2 * h * LANE:(2 * h + 1) * LANE] = dv[:, h * LANE:(h + 1) * LANE].astype(BF)
            dav_ref[:, (2 * h + 1) * LANE:(2 * h + 2) * LANE] = ones_col
        kv = _dot(ckv.astype(BF), wkv_ref[...])
        for h in range(MLA_HEADS):
            mlak_ref[:, h * LANE:(h + 1) * LANE] = (kv[:, h * LANE:(h + 1) * LANE] + kr_sh).astype(BF)
        for p in range(MLA_HEADS // 2):
            mlav_ref[:, 2 * p * LANE:(2 * p + 1) * LANE] = kv[:, (MLA_HEADS + p) * LANE:(MLA_HEADS + p + 1) * LANE].astype(BF)
            mlav_ref[:, (2 * p + 1) * LANE:(2 * p + 2) * LANE] = ones_col
        k0, k1 = _dup(sk)
        v0, v1 = _dup(sv)
        swak_ref[:, 0:LANE] = k0.astype(BF)
        swak_ref[:, LANE:2 * LANE] = k1.astype(BF)
        swav_ref[:, 0:LANE] = v0.astype(BF)
        swav_ref[:, LANE:2 * LANE] = ones_col
        swav_ref[:, 2 * LANE:3 * LANE] = v1.astype(BF)
        swav_ref[:, 3 * LANE:4 * LANE] = ones_col

    @pl.when(j == 0)
    def _():
        write_kv(cdk_ref[...], cdv_ref[...], cckv_ref[...], pltpu.roll(ckr_ref[...], HALF, 1),
                 csk_ref[...], csv_ref[...])

    @pl.when(j > 0)
    def _():
        c64, s64 = tab_ref[0], tab_ref[1]
        cmq, smq = tab_ref[2], tab_ref[3]
        ckr, skr = tab_ref[4], tab_ref[5]

        def cols(c0, width):
            return p_ref[:, c0:c0 + width].astype(F32)

        def blk(c0, h):
            return cols(c0 + h * LANE, LANE)

        for h in range(DA_HEADS):
            daq_ref[:, h * LANE:(h + 1) * LANE] = (_rope(blk(C_DAQ, h), c64, s64) * (HEAD_SCALE * LOG2E)).astype(BF)
            swaq_ref[:, h * LANE:(h + 1) * LANE] = (_rope(blk(C_SQ, h), c64, s64) * (HEAD_SCALE * LOG2E)).astype(BF)
        dk = jnp.concatenate([_rope(blk(C_DAK, h), c64, s64) for h in range(DA_HEADS)], axis=1)
        qn = _rms(cols(C_MQ, MLA_Q_RANK), qn_ref[...]).astype(BF)
        cq = _dot(qn, wqb_ref[...])
        for h in range(MLA_HEADS):
            mlaq_ref[:, h * LANE:(h + 1) * LANE] = (
                _rope(cq[:, h * LANE:(h + 1) * LANE], cmq, smq) * (MLA_SCALE * LOG2E)).astype(BF)
        ckv = _rms(cols(C_MKV, MLA_KV_RANK), kvn_ref[...])
        kr_sh = pltpu.roll(_rope(cols(C_MKR, LANE), ckr, skr), HALF, 1)
        sk = _rope(cols(C_SK, LANE), c64, s64)
        write_kv(dk, cols(C_DAV, 512), ckv, kr_sh, sk, cols(C_SV, LANE))


def _lat_prep_call(layer, proj, tab, caches, qn, wqb, kvn, wkv):
    cdk, cdv, cckv, ckr, csk, csv = caches
    nj = 1 + DEC_SEQ // CHUNK

    def own(width):
        return pl.BlockSpec((None, CHUNK, width), lambda b, j: (b, jnp.maximum(j - 1, 0), 0))

    def cache(width):
        return pl.BlockSpec((None, None, PAST_LEN, width), lambda b, j: (b, layer, 0, 0))

    def allk(width):
        return pl.BlockSpec((None, CHUNK, width), lambda b, j: (b, j, 0))

    def shp(rows, width):
        return jax.ShapeDtypeStruct((DEC_BATCH, rows, width), BF)

    return pl.pallas_call(
        _lat_prep_kernel,
        grid=(DEC_BATCH, nj),
        in_specs=[own(PROJ_COLS),
                  pl.BlockSpec((6, CHUNK, LANE), lambda b, j: (0, jnp.maximum(j - 1, 0), 0)),
                  cache(512), cache(512), cache(LANE), cache(LANE), cache(LANE), cache(LANE),
                  _layer(layer, (1, MLA_Q_RANK)), _layer(layer, (MLA_Q_RANK, MLA_HEADS * LANE)),
                  _layer(layer, (1, MLA_KV_RANK)), _layer(layer, (MLA_KV_RANK, MLA_HEADS * LANE + MLA_OUT))],
        out_specs=[own(512), own(MLA_HEADS * LANE), own(512),
                   allk(512), allk(1024), allk(MLA_HEADS * LANE), allk(1024), allk(2 * LANE), allk(4 * LANE)],
        out_shape=[shp(DEC_SEQ, 512), shp(DEC_SEQ, MLA_HEADS * LANE), shp(DEC_SEQ, 512),
                   shp(KEYS, 512), shp(KEYS, 1024), shp(KEYS, MLA_HEADS * LANE), shp(KEYS, 1024),
                   shp(KEYS, 2 * LANE), shp(KEYS, 4 * LANE)],
        compiler_params=_cp(("parallel", "arbitrary"), 48), name="lat_prep",
    )(proj.reshape(DEC_BATCH, DEC_SEQ, PROJ_COLS), tab, cdk, cdv, cckv, ckr, csk, csv, qn, wqb, kvn, wkv)


def _softmax2_pv(s, v_aug):
    e = jnp.exp2(s - jnp.max(s, axis=-1, keepdims=True)).astype(BF)
    o = _dot(e, v_aug)
    return o[:, 0:LANE] / o[:, LANE:LANE + 1]


def _lat_da_kernel(lam_init, lam_ref, q_ref, k_ref, v_ref, subln_ref, o_ref):
    for h in range(DA_STEP_HEADS):
        cols = slice(h * LANE, (h + 1) * LANE)
        q1, q2 = _mask_halves(q_ref[:, cols].astype(F32))
        k = k_ref[:, cols]
        v = v_ref[:, 2 * h * LANE:(2 * h + 2) * LANE]
        o = _softmax2_pv(_dot_nt(q1, k), v) - lam_ref[0] * _softmax2_pv(_dot_nt(q2, k), v)
        o_ref[:, cols] = (_rms(o, subln_ref[...]) * (1.0 - lam_init)).astype(BF)


def _lat_da_call(layer, lam, q, k, v, subln):
    return pl.pallas_call(
        functools.partial(_lat_da_kernel, _lam_init(layer)),
        grid=(DEC_BATCH, DA_HEADS // DA_STEP_HEADS, DEC_SEQ // TQ),
        in_specs=[_smem(),
                  pl.BlockSpec((None, TQ, DA_STEP_HEADS * LANE), lambda b, h, i: (b, i, h)),
                  pl.BlockSpec((None, KEYS, DA_STEP_HEADS * LANE), lambda b, h, i: (b, 0, h)),
                  pl.BlockSpec((None, KEYS, 2 * DA_STEP_HEADS * LANE), lambda b, h, i: (b, 0, h)),
                  _layer(layer, (1, LANE))],
        out_specs=pl.BlockSpec((None, TQ, DA_STEP_HEADS * LANE), lambda b, h, i: (b, i, h)),
        out_shape=jax.ShapeDtypeStruct((DEC_BATCH, DEC_SEQ, 512), BF),
        compiler_params=_cp(("parallel", "parallel", "arbitrary"), 56), name="lat_da",
    )(lam, q, k, v, subln)


def _lat_mla_kernel(q_ref, k_ref, v_ref, o_ref):
    for p in range(MLA_STEP_PAIRS):
        v = v_ref[:, 2 * p * LANE:(2 * p + 2) * LANE]
        outs = []
        for h in (2 * p, 2 * p + 1):
            cols = slice(h * LANE, (h + 1) * LANE)
            outs.append(_softmax2_pv(_dot_nt(q_ref[:, cols], k_ref[:, cols]), v))
        o_ref[:, p * LANE:(p + 1) * LANE] = _pair(outs[0], outs[1]).astype(BF)


def _lat_mla_call(q, k, v):
    sp = MLA_STEP_PAIRS
    return pl.pallas_call(
        _lat_mla_kernel,
        grid=(DEC_BATCH, MLA_HEADS // (2 * sp), DEC_SEQ // TQ),
        in_specs=[pl.BlockSpec((None, TQ, 2 * sp * LANE), lambda b, m, i: (b, i, m)),
                  pl.BlockSpec((None, KEYS, 2 * sp * LANE), lambda b, m, i: (b, 0, m)),
                  pl.BlockSpec((None, KEYS, 2 * sp * LANE), lambda b, m, i: (b, 0, m))],
        out_specs=pl.BlockSpec((None, TQ, sp * LANE), lambda b, m, i: (b, i, m)),
        out_shape=jax.ShapeDtypeStruct((DEC_BATCH, DEC_SEQ, 512), BF),
        compiler_params=_cp(("parallel", "parallel", "arbitrary"), 56), name="lat_mla",
    )(q, k, v)


def _swa_bias():
    rows = (SWA_HEADS // SWA_KV_HEADS) * W_BLOCK
    qi = (jnp.arange(rows) % W_BLOCK)[:, None]
    loc = (jnp.arange(PAST_LEN + 3 * W_BLOCK) - PAST_LEN)[None, :]
    always = (loc < 0) | ((loc >= W_BLOCK) & (loc < 2 * W_BLOCK))
    prev = (loc >= 0) & (loc < W_BLOCK) & (loc >= qi)
    nxt = (loc >= 2 * W_BLOCK) & (loc - 2 * W_BLOCK <= qi)
    ok = jnp.stack([always | nxt, always | prev | nxt, always | prev])
    return jnp.where(ok, 0.0, -jnp.inf).astype(F32)


def _lat_swa_kernel(sink_ref, bias_a_ref, bias_b_ref, q_ref, kc_ref, kp_ref, ka_ref, kb_ref, kn_ref,
                    vc_ref, vp_ref, va_ref, vb_ref, vn_ref, o_ref):
    group = SWA_HEADS // SWA_KV_HEADS
    rows = group * W_BLOCK
    head_of_row = lax.broadcasted_iota(I32, (rows, 1), 0) // W_BLOCK
    windows = ((bias_a_ref, (kc_ref, kp_ref, ka_ref, kb_ref), (vc_ref, vp_ref, va_ref, vb_ref)),
               (bias_b_ref, (kc_ref, ka_ref, kb_ref, kn_ref), (vc_ref, va_ref, vb_ref, vn_ref)))
    for sub, (bias_ref, k_refs, v_refs) in enumerate(windows):
        q_rows = slice(sub * W_BLOCK, (sub + 1) * W_BLOCK)
        for g in range(SWA_KV_HEADS):
            gs = slice(g * LANE, (g + 1) * LANE)
            vs = slice(2 * g * LANE, (2 * g + 2) * LANE)
            q_parts = []
            sink = jnp.zeros((rows, 1), F32)
            for j in range(group):
                sink = jnp.where(head_of_row == j, sink_ref[g * group + j] * LOG2E, sink)
            for p in range(group // 2):
                blk = g * (group // 2) + p
                q_parts.extend(_mask_halves(q_ref[q_rows, blk * LANE:(blk + 1) * LANE].astype(F32)))
            q = jnp.concatenate(q_parts, axis=0)
            k = jnp.concatenate([r[:, gs] for r in k_refs], axis=0)
            v = jnp.concatenate([r[:, vs] for r in v_refs], axis=0)
            s = _dot_nt(q, k) + bias_ref[...]
            m = jnp.maximum(jnp.max(s, axis=-1, keepdims=True), sink)
            ov = _dot(jnp.exp2(s - m).astype(BF), v)
            o = ov[:, 0:LANE] / (ov[:, LANE:LANE + 1] + jnp.exp2(sink - m))
            for p in range(group // 2):
                blk = g * (group // 2) + p
                o_a = o[(2 * p) * W_BLOCK:(2 * p + 1) * W_BLOCK]
                o_b = o[(2 * p + 1) * W_BLOCK:(2 * p + 2) * W_BLOCK]
                o_ref[q_rows, blk * LANE:(blk + 1) * LANE] = _pair(o_a, o_b).astype(BF)


def _lat_swa_call(sink, bias, q, k, v):
    nb = DEC_SEQ // W_BLOCK
    cb = PAST_LEN // W_BLOCK
    steps = nb // 2

    def ctx(width):
        return pl.BlockSpec((None, PAST_LEN, width), lambda b, n: (b, 0, 0))

    def loc(d, width):
        return pl.BlockSpec((None, W_BLOCK, width),
                            lambda b, n: (b, cb + jnp.clip(2 * n + d, 0, nb - 1), 0))

    def bias_spec(pick):
        return pl.BlockSpec((None,) + bias.shape[1:], lambda b, n: (pick(n), 0, 0))

    kw, vw = 2 * LANE, 4 * LANE
    return pl.pallas_call(
        _lat_swa_kernel,
        grid=(DEC_BATCH, steps),
        in_specs=[_smem(),
                  bias_spec(lambda n: jnp.where(n == 0, 0, 1)), bias_spec(lambda n: jnp.where(n == steps - 1, 2, 1)),
                  pl.BlockSpec((None, 2 * W_BLOCK, 512), lambda b, n: (b, n, 0)),
                  ctx(kw), loc(-1, kw), loc(0, kw), loc(1, kw), loc(2, kw),
                  ctx(vw), loc(-1, vw), loc(0, vw), loc(1, vw), loc(2, vw)],
        out_specs=pl.BlockSpec((None, 2 * W_BLOCK, 512), lambda b, n: (b, n, 0)),
        out_shape=jax.ShapeDtypeStruct((DEC_BATCH, DEC_SEQ, 512), BF),
        compiler_params=_cp(("parallel", "arbitrary"), 40), name="lat_swa",
    )(sink, bias, bias, q, k, k, k, k, k, v, v, v, v, v)


def _rows_to_tiles(o_ref, val, tm):
    for k in range(D_CHUNKS):
        o_ref[pl.ds(k, tm, stride=D_CHUNKS), :] = val[:, k * LANE:(k + 1) * LANE]


def _post_kernel(x_ref, oda_ref, omla_ref, oswa_ref, gate_ref, mod_ref, n2_ref, wda_ref, wmla_ref, wswa_ref,
                 wo_ref, wrh_ref, wrl_ref, xo_ref, h2_ref, lg_ref):
    m = mod_ref[...]
    g = gate_ref[...].astype(F32)
    merged = (g[:, 0:D_MODEL] * _dot(oda_ref[...], wda_ref[...])
              + g[:, D_MODEL:2 * D_MODEL] * _dot(omla_ref[...], wmla_ref[...])
              + g[:, 2 * D_MODEL:] * _dot(oswa_ref[...], wswa_ref[...]))
    x = x_ref[...] + m[2:3] * _dot(merged.astype(BF), wo_ref[...])
    xo_ref[...] = x
    h2 = _rms(x, n2_ref[...]) * (1.0 + m[4:5]) + m[3:4]
    _rows_to_tiles(h2_ref, h2, TM_POST)
    hh, hl = _split(h2)
    lg_ref[...] = _dot_nt(wrh_ref[...], hh) + _dot_nt(wrl_ref[...], hh) + _dot_nt(wrh_ref[...], hl)


def _post_call(layer, x, oda, omla, oswa, gate, mod, n2, wda, wmla, wswa, wo, wrh, wrl, latent):
    n = x.shape[0]
    tm = TM_POST
    row = pl.BlockSpec((tm, D_MODEL), lambda i: (i, 0))
    o512 = pl.BlockSpec((tm, 512), lambda i: (i, 0))
    return pl.pallas_call(
        _post_kernel, grid=(n // tm,),
        in_specs=[row, o512, o512, o512, pl.BlockSpec((tm, GATE_COLS), lambda i: (i, 0)),
                  _mod_spec(layer, tm, latent), _layer(layer, (1, D_MODEL)),
                  _layer(layer, (512, D_MODEL)), _layer(layer, (512, D_MODEL)), _layer(layer, (512, D_MODEL)),
                  _layer(layer, (D_MODEL, D_MODEL)), _layer(layer, (N_EXPERTS, D_MODEL)),
                  _layer(layer, (N_EXPERTS, D_MODEL))],
        out_specs=[row, pl.BlockSpec((tm * D_CHUNKS, LANE), lambda i: (i, 0)),
                   pl.BlockSpec((N_EXPERTS, tm), lambda i: (0, i))],
        out_shape=[jax.ShapeDtypeStruct((n, D_MODEL), F32),
                   jax.ShapeDtypeStruct((n * D_CHUNKS, LANE), F32),
                   jax.ShapeDtypeStruct((N_EXPERTS, n), F32)],
        compiler_params=_cp(("parallel",), 48), name="post",
    )(x, oda, omla, oswa, gate, mod, n2, wda, wmla, wswa, wo, wrh, wrl)


def _router_kernel(cap, lg_ref, q_ref, a_ref, lo_ref, hi_ref, cnt_ref):
    e_n, n = lg_ref.shape
    lg = lg_ref[...]
    shifted = lg - jnp.max(lg, axis=0, keepdims=True)
    ex = jnp.exp(shifted)
    den = jnp.sum(ex, axis=0, keepdims=True)
    aff = ex / den
    score = shifted - jnp.log(den)
    capf = float(cap)

    def count(mask):
        return jnp.sum(jnp.where(mask, 1.0, 0.0), axis=1, keepdims=True)

    def halve(_, bounds):
        lo, hi = bounds
        mid = 0.5 * (lo + hi)
        enough = count(score >= mid) >= capf
        return jnp.where(enough, mid, lo), jnp.where(enough, hi, mid)

    lo, hi = lax.fori_loop(0, THRESHOLD_STEPS, halve,
                           (jnp.min(score, axis=1, keepdims=True), jnp.ones((e_n, 1), F32)))
    gt = score >= hi
    eq = (score >= lo) & (score < hi)
    need = capf - count(gt)
    tok = lax.broadcasted_iota(I32, (e_n, n), 1)
    n_bits = n.bit_length()

    def tie_bit(i, bound):
        cand = bound | jnp.left_shift(jnp.int32(1), n_bits - 1 - i)
        ok = (cand <= n) & (count(eq & (tok < cand)) <= need)
        return jnp.where(ok, cand, bound)

    bound = lax.fori_loop(0, n_bits, tie_bit, jnp.zeros((e_n, 1), I32))
    sel = gt | (eq & (tok < bound))
    cnt_ref[...] = jnp.broadcast_to(count(sel & (tok < TOK_HALF)), cnt_ref.shape)
    blk = 2 * LANE
    per_chunk = ROUTE_CHUNK // blk
    n_chunks = n // ROUTE_CHUNK
    upper = (lax.broadcasted_iota(I32, (blk, blk), 0) <= lax.broadcasted_iota(I32, (blk, blk), 1))
    upper = jnp.where(upper, 1.0, 0.0).astype(BF)
    lane = _lane((e_n, LANE))
    carry = jnp.zeros((e_n, 1), F32)
    c_in = jnp.zeros((e_n, LANE), F32)
    c_ex = jnp.zeros((e_n, LANE), F32)
    for c in range(n // blk):
        chunk, part = divmod(c, per_chunk)
        if part == 0:
            c_ex = jnp.where(lane == chunk, carry, c_ex)
            a_ref[chunk] = aff[:, chunk * ROUTE_CHUNK:(chunk + 1) * ROUTE_CHUNK]
        s_blk = sel[:, c * blk:(c + 1) * blk]
        rank = _dot(jnp.where(s_blk, 1.0, 0.0).astype(BF), upper) + carry
        q_ref[chunk, :, part * blk:(part + 1) * blk] = jnp.where(s_blk, rank, 0.0)
        carry = rank[:, blk - 1:blk]
        if part == per_chunk - 1:
            c_in = jnp.where(lane == chunk, carry, c_in)
    valid = lane < n_chunks
    lo = jnp.zeros((e_n, LANE), I32)
    hi = jnp.zeros((e_n, LANE), I32)
    for rt in range(cap // ROUTE_TILE):
        lo_rt = count(valid & (c_in < float(rt * ROUTE_TILE + 1)))
        hi_rt = count(valid & (c_ex < float((rt + 1) * ROUTE_TILE)))
        lo = jnp.where(lane == rt, lo_rt.astype(I32), lo)
        hi = jnp.where(lane == rt, hi_rt.astype(I32), hi)
    lo_ref[...] = lo
    hi_ref[...] = hi


def _compact_kernel(lo_ref, hi_ref, q_ref, a_ref, idx_ref, g_ref):
    e = pl.program_id(0)
    n_tiles = idx_ref.shape[0]
    eye = lax.broadcasted_iota(I32, (ROUTE_TILE, LANE), 0) == lax.broadcasted_iota(I32, (ROUTE_TILE, LANE), 1)

    def per_tile(rt, _):
        slot = (rt * ROUTE_TILE + 1 + lax.broadcasted_iota(I32, (ROUTE_TILE, 1), 0)).astype(F32)

        def per_chunk(c, acc):
            idx_acc, g_acc = acc
            q_row = q_ref[c, pl.ds(e, 1), :]
            a_row = a_ref[c, pl.ds(e, 1), :]
            for part in range(ROUTE_CHUNK // LANE):
                lanes = slice(part * LANE, (part + 1) * LANE)
                hit = q_row[:, lanes] == slot
                tok = (c * ROUTE_CHUNK + part * LANE + lax.broadcasted_iota(I32, (1, LANE), 1)).astype(F32)
                idx_acc = idx_acc + jnp.where(hit, tok, 0.0)
                g_acc = g_acc + jnp.where(hit, a_row[:, lanes], 0.0)
            return idx_acc, g_acc

        zero = jnp.zeros((ROUTE_TILE, LANE), F32)
        idx_acc, g_acc = lax.fori_loop(lo_ref[e, rt], hi_ref[e, rt], per_chunk, (zero, zero))
        idx_v = jnp.sum(idx_acc, axis=1, keepdims=True)
        g_v = jnp.sum(g_acc, axis=1, keepdims=True)
        idx_ref[pl.ds(rt, 1), :] = jnp.sum(jnp.where(eye, idx_v, 0.0), axis=0, keepdims=True).astype(I32)
        g_ref[pl.ds(rt, 1), :] = jnp.sum(jnp.where(eye, g_v, 0.0), axis=0, keepdims=True)
        return 0

    lax.fori_loop(0, n_tiles, per_tile, 0)


def _router_call(logits_t, cap):
    n = logits_t.shape[1]
    n_chunks = n // ROUTE_CHUNK
    n_tiles = cap // ROUTE_TILE
    chunked = jax.ShapeDtypeStruct((n_chunks, N_EXPERTS, ROUTE_CHUNK), F32)
    q, a, lo, hi, cnt = pl.pallas_call(
        functools.partial(_router_kernel, cap),
        out_shape=[chunked, chunked, jax.ShapeDtypeStruct((N_EXPERTS, LANE), I32),
                   jax.ShapeDtypeStruct((N_EXPERTS, LANE), I32), jax.ShapeDtypeStruct((N_EXPERTS, LANE), F32)],
        compiler_params=pltpu.CompilerParams(vmem_limit_bytes=40 * 1024 * 1024), name="router",
    )(logits_t)
    whole = pl.BlockSpec((n_chunks, N_EXPERTS, ROUTE_CHUNK), lambda e, *_: (0, 0, 0))
    slots = pl.BlockSpec((None, n_tiles, ROUTE_TILE), lambda e, *_: (e, 0, 0))
    idx, g = pl.pallas_call(
        _compact_kernel,
        grid_spec=pltpu.PrefetchScalarGridSpec(
            num_scalar_prefetch=2, grid=(N_EXPERTS,), in_specs=[whole, whole], out_specs=[slots, slots]),
        out_shape=[jax.ShapeDtypeStruct((N_EXPERTS, n_tiles, ROUTE_TILE), I32),
                   jax.ShapeDtypeStruct((N_EXPERTS, n_tiles, ROUTE_TILE), F32)],
        compiler_params=_cp(("parallel",), 32), name="compact",
    )(lo[:, :n_tiles], hi[:, :n_tiles], q, a)
    return idx.reshape(N_EXPERTS, cap), g.reshape(N_EXPERTS, cap), cnt


def _row_copy(src_hbm, xe, sem, buf, tok, slot):
    dst = xe.at[buf, pl.ds(pl.multiple_of(slot * SUB, SUB), SUB)]
    return pltpu.make_async_copy(src_hbm.at[tok], dst, sem.at[buf])


def _rows_wait(rows_hbm, xe, sem, buf, slot, n):
    dst = xe.at[buf, pl.ds(pl.multiple_of(slot * SUB, SUB), n * SUB)]
    return pltpu.make_async_copy(rows_hbm.at[pl.ds(0, n * SUB)], dst, sem.at[buf])


def _ffn_kernel(idx_ref, idxn_ref, hc_hbm, hl_hbm, rows_hbm, w1_ref, w3_ref, w2_ref, yc_ref, yl_ref,
                xe, xb, w1b, w3b, w2b, sem):
    e = pl.program_id(0)
    f = pl.program_id(1)
    buf = e % 2

    @pl.when((e == 0) & (f == 0))
    def _():
        def start_ctx(r, _):
            _row_copy(hc_hbm, xe, sem, 0, idx_ref[0, r], r).start()
            return 0

        def start_lat(r, _):
            _row_copy(hl_hbm, xe, sem, 0, idx_ref[0, r], r).start()
            return 0

        lax.fori_loop(0, CAP_CTX, start_ctx, 0)
        lax.fori_loop(CAP_CTX, SLOTS, start_lat, 0)
        _rows_wait(rows_hbm, xe, sem, 0, 0, SLOTS).wait()

    @pl.when(f == 0)
    def _():
        for k in range(D_CHUNKS):
            xb[:, k * LANE:(k + 1) * LANE] = xe[buf, pl.ds(k, SLOTS, stride=D_CHUNKS), :].astype(BF)

    w1b[...] = w1_ref[...].astype(BF)
    w3b[...] = w3_ref[...].astype(BF)
    w2b[...] = w2_ref[...].astype(BF)

    part_c, part_l = CAP_CTX // FF_SPLIT, CAP_LAT // FF_SPLIT
    next_rows = ([(hc_hbm, f * part_c + u) for u in range(part_c)]
                 + [(hl_hbm, CAP_CTX + f * part_l + u) for u in range(part_l)])

    def next_copy(src, r):
        return _row_copy(src, xe, sem, 1 - buf, idxn_ref[0, r], r)

    n_tiles = SLOTS // FFN_ROWS
    for t in range(n_tiles):
        if t < n_tiles - 1:
            for src, r in next_rows[t::n_tiles - 1]:
                next_copy(src, r).start()
        rows = slice(t * FFN_ROWS, (t + 1) * FFN_ROWS)
        x = xb[rows, :]
        a = _dot(x, w1b[...])
        hid = (a * jax.nn.sigmoid(a) * _dot(x, w3b[...])).astype(BF)
        y = _dot(hid, w2b[...])
        lo, hi = t * FFN_ROWS, (t + 1) * FFN_ROWS
        segs = []
        if lo < CAP_CTX:
            segs.append((yc_ref, lo, 0, min(hi, CAP_CTX) - lo))
        if hi > CAP_CTX:
            first = max(lo, CAP_CTX)
            segs.append((yl_ref, first - CAP_CTX, first - lo, hi - first))

        def pieces():
            for y_ref, row0, y0, n in segs:
                for k in range(D_CHUNKS):
                    yield y_ref, pl.ds(row0 * D_CHUNKS + k, n, stride=D_CHUNKS), y[y0:y0 + n, k * LANE:(k + 1) * LANE]

        @pl.when(f == 0)
        def _():
            for y_ref, dst, val in pieces():
                y_ref[dst, :] = val

        @pl.when(f > 0)
        def _():
            for y_ref, dst, val in pieces():
                y_ref[dst, :] = y_ref[dst, :] + val

    _rows_wait(rows_hbm, xe, sem, 1 - buf, f * part_c, part_c).wait()
    _rows_wait(rows_hbm, xe, sem, 1 - buf, CAP_CTX + f * part_l, part_l).wait()


def _ffn_call(layer, idx, h_ctx, h_lat, w1, w3, w2):
    ff = EXPERT_FF // FF_SPLIT
    wspec = pl.BlockSpec((None, None, D_MODEL, ff), lambda e, f: (layer, e, 0, f))
    last = N_EXPERTS - 1
    return pl.pallas_call(
        _ffn_kernel,
        grid=(N_EXPERTS, FF_SPLIT),
        in_specs=[pl.BlockSpec((None, 1, SLOTS), lambda e, f: (e, 0, 0), memory_space=pltpu.SMEM),
                  pl.BlockSpec((None, 1, SLOTS), lambda e, f: (jnp.minimum(e + 1, last), 0, 0),
                               memory_space=pltpu.SMEM),
                  pl.BlockSpec(memory_space=pl.ANY), pl.BlockSpec(memory_space=pl.ANY),
                  pl.BlockSpec(memory_space=pl.ANY), wspec, wspec, pl.BlockSpec((None, None, ff, D_MODEL), lambda e, f: (layer, e, f, 0))],
        out_specs=[pl.BlockSpec((None, CAP_CTX * D_CHUNKS, LANE), lambda e, f: (e, 0, 0)),
                   pl.BlockSpec((None, CAP_LAT * D_CHUNKS, LANE), lambda e, f: (e, 0, 0))],
        out_shape=[jax.ShapeDtypeStruct((N_EXPERTS, CAP_CTX * D_CHUNKS, LANE), F32),
                   jax.ShapeDtypeStruct((N_EXPERTS, CAP_LAT * D_CHUNKS, LANE), F32)],
        scratch_shapes=[pltpu.VMEM((2, SLOTS * D_CHUNKS, LANE), F32), pltpu.VMEM((SLOTS, D_MODEL), BF),
                        pltpu.VMEM((D_MODEL, ff), BF), pltpu.VMEM((D_MODEL, ff), BF),
                        pltpu.VMEM((ff, D_MODEL), BF), pltpu.SemaphoreType.DMA((2,))],
        compiler_params=_cp(("arbitrary", "arbitrary"), 54), name="ffn",
    )(idx, idx, h_ctx, h_lat, h_lat.reshape(N_LAT * SUB, LANE), w1, w3, w2)


def _combine_kernel(idx_ref, g_ref, bnd_ref, yc_ref, yl_ref, acc_ref):
    h = pl.program_id(0)
    e = pl.program_id(1)

    @pl.when(e == 0)
    def _():
        acc_ref[...] = jnp.zeros(acc_ref.shape, F32)

    def add_rows(y_ref, slot0, lo, hi, base):
        def group(i, _):
            r0 = lo + i * COMBINE_GROUP
            toks = [idx_ref[0, r0 + u] - base for u in range(COMBINE_GROUP)]
            sums = [acc_ref[toks[u]] + y_ref[r0 - slot0 + u] * g_ref[0, r0 + u] for u in range(COMBINE_GROUP)]
            for u in range(COMBINE_GROUP):
                acc_ref[toks[u]] = sums[u]
            return 0

        def single(r, _):
            t = idx_ref[0, r] - base
            acc_ref[t] = acc_ref[t] + y_ref[r - slot0] * g_ref[0, r]
            return 0

        n_groups = (hi - lo) // COMBINE_GROUP
        lax.fori_loop(0, n_groups, group, 0)
        lax.fori_loop(lo + n_groups * COMBINE_GROUP, hi, single, 0)

    @pl.when(h == 0)
    def _():
        add_rows(yc_ref, 0, 0, CAP_CTX, 0)

    @pl.when(h > 0)
    def _():
        add_rows(yl_ref, CAP_CTX, bnd_ref[0, h], bnd_ref[0, h + 1], (h - 1) * TOK_HALF)


def _combine_call(idx, g, bnd, y_ctx, y_lat):
    last = N_EXPERTS - 1
    return pl.pallas_call(
        _combine_kernel,
        grid=(N_HALVES, N_EXPERTS),
        in_specs=[pl.BlockSpec((None, 1, SLOTS), lambda h, e: (e, 0, 0), memory_space=pltpu.SMEM),
                  pl.BlockSpec((None, 1, SLOTS), lambda h, e: (e, 0, 0), memory_space=pltpu.SMEM),
                  pl.BlockSpec((None, 1, N_HALVES + 1), lambda h, e: (e, 0, 0), memory_space=pltpu.SMEM),
                  pl.BlockSpec((None, CAP_CTX, SUB, LANE), lambda h, e: (jnp.where(h == 0, e, last), 0, 0, 0)),
                  pl.BlockSpec((None, CAP_LAT, SUB, LANE), lambda h, e: (jnp.where(h == 0, 0, e), 0, 0, 0))],
        out_specs=pl.BlockSpec((None, TOK_HALF, SUB, LANE), lambda h, e: (h, 0, 0, 0)),
        out_shape=jax.ShapeDtypeStruct((N_HALVES, TOK_HALF, SUB, LANE), F32),
        compiler_params=_cp(("parallel", "arbitrary"), 52), name="combine",
    )(idx, g, bnd, y_ctx.reshape(N_EXPERTS, CAP_CTX, SUB, LANE), y_lat.reshape(N_EXPERTS, CAP_LAT, SUB, LANE))


def _final_kernel(x_ref, moe_ref, mod_ref, g_ref, o_ref):
    x = x_ref[...] + mod_ref[...][5:6] * _moe_rows(moe_ref, TM_FINAL)
    o_ref[...] = _rms(x, g_ref[...])


def _final_call(x, moe, mod, g, latent):
    n = x.shape[0]
    tm = TM_FINAL
    moe_off = (N_CTX // tm) if latent else 0
    row = pl.BlockSpec((tm, D_MODEL), lambda i: (i, 0))
    return pl.pallas_call(
        _final_kernel, grid=(n // tm,),
        in_specs=[row, pl.BlockSpec((tm * D_CHUNKS, LANE), lambda i: (i + moe_off, 0)),
                  _mod_spec(DEPTH - 1, tm, latent), _layer(0, (1, D_MODEL))],
        out_specs=row, out_shape=jax.ShapeDtypeStruct((n, D_MODEL), F32),
        compiler_params=_cp(("parallel",), 32), name="final",
    )(x, moe, mod, g)


def _axial_tables(rot_dim):
    rows = DEC_SEQ // GRID_W
    row = jnp.repeat(jnp.arange(rows, dtype=F32), GRID_W)
    col = jnp.tile(jnp.arange(GRID_W, dtype=F32), rows)
    n_freq = rot_dim // 4
    inv = ROPE_BASE ** (-jnp.arange(n_freq, dtype=F32) / n_freq)
    ang = jnp.concatenate([row[:, None] * inv, col[:, None] * inv], axis=-1)
    sign = jnp.tile(jnp.array([-1.0, 1.0], F32), rot_dim // 2)
    return jnp.repeat(jnp.cos(ang), 2, axis=1), jnp.repeat(jnp.sin(ang), 2, axis=1) * sign


def _rope_tables():
    c64, s64 = _axial_tables(DA_DH)
    c32, s32 = _axial_tables(MLA_ROPE)
    one = lambda w: jnp.ones((DEC_SEQ, w), F32)
    zero = lambda w: jnp.zeros((DEC_SEQ, w), F32)
    pad = LANE - MLA_NOPE - MLA_ROPE
    return jnp.stack([
        jnp.tile(c64, (1, 2)), jnp.tile(s64, (1, 2)),
        jnp.concatenate([one(MLA_NOPE), c32, one(pad)], axis=1),
        jnp.concatenate([zero(MLA_NOPE), s32, zero(pad)], axis=1),
        jnp.concatenate([c32, one(LANE - MLA_ROPE)], axis=1),
        jnp.concatenate([s32, zero(LANE - MLA_ROPE)], axis=1)])


def _prep_weights(w_in, w_gate, mla_w_qb, mla_w_kvb, w_br_da, w_br_mla, w_br_swa, w_o, w_router):
    kr_end = C_MKR + MLA_ROPE
    win = jnp.concatenate([w_in[:, :, :kr_end].astype(BF), jnp.zeros((DEPTH, D_MODEL, LANE - MLA_ROPE), BF),
                           w_in[:, :, kr_end:].astype(BF)], axis=2)
    dk = MLA_NOPE + MLA_ROPE
    wqb = jnp.pad(mla_w_qb.reshape(DEPTH, MLA_Q_RANK, MLA_HEADS, dk), ((0, 0), (0, 0), (0, 0), (0, LANE - dk)))
    wqb = wqb.reshape(DEPTH, MLA_Q_RANK, MLA_HEADS * LANE).astype(BF)
    kvb = mla_w_kvb.reshape(DEPTH, MLA_KV_RANK, MLA_HEADS, MLA_NOPE + MLA_V)
    wk = jnp.pad(kvb[..., :MLA_NOPE], ((0, 0), (0, 0), (0, 0), (0, LANE - MLA_NOPE)))
    wk = wk.reshape(DEPTH, MLA_KV_RANK, MLA_HEADS * LANE)
    wv = kvb[..., MLA_NOPE:].reshape(DEPTH, MLA_KV_RANK, MLA_OUT)
    wkv = jnp.concatenate([wk, wv], axis=2).astype(BF)
    wrh, wrl = _split(jnp.swapaxes(w_router, 1, 2))
    return dict(win=win, wg=w_gate.astype(BF), wqb=wqb, wkv=wkv, wda=w_br_da.astype(BF),
                wmla=w_br_mla.astype(BF), wswa=w_br_swa.astype(BF), wo=w_o.astype(BF), wrh=wrh, wrl=wrl)


def kernel(x_prompt, x_sample, cache_da_k, cache_da_v, cache_mla_ckv, cache_mla_krope, cache_swa_k, cache_swa_v, c, c_ctx, w_ada, b_ada, norm1, norm2, w_in, da_lq1, da_lk1, da_lq2, da_lk2, da_subln, mla_q_norm, mla_w_qb, mla_kv_norm, mla_w_kvb, swa_sink, w_gate, w_br_da, w_br_mla, w_br_swa, w_o, w_router, w_ff1, w_ff3, w_ff2, final_norm):
    c_all = jnp.concatenate([c_ctx[None], c, jnp.zeros((SUB - 1 - DEC_BATCH, D_MODEL), F32)], axis=0)
    mod = _ada_call(c_all, w_ada, b_ada)[:, :1 + DEC_BATCH].reshape(DEPTH, 1 + DEC_BATCH, 6, D_MODEL)
    lam_all = _lam_call(da_lq1, da_lk1, da_lq2, da_lk2)
    tab = _rope_tables()
    swa_bias = _swa_bias()
    w = _prep_weights(w_in, w_gate, mla_w_qb, mla_w_kvb, w_br_da, w_br_mla, w_br_swa, w_o, w_router)
    n1, n2 = norm1[:, None], norm2[:, None]
    subln, qn, kvn = da_subln[:, None], mla_q_norm[:, None], mla_kv_norm[:, None]
    caches = (cache_da_k.reshape(DEC_BATCH, DEPTH, PAST_LEN, 512),
              cache_da_v.reshape(DEC_BATCH, DEPTH, PAST_LEN, 512),
              cache_mla_ckv,
              jnp.pad(cache_mla_krope, ((0, 0), (0, 0), (0, 0), (0, LANE - MLA_ROPE))),
              cache_swa_k.reshape(DEC_BATCH, DEPTH, PAST_LEN, LANE),
              cache_swa_v.reshape(DEC_BATCH, DEPTH, PAST_LEN, LANE))

    xp = x_prompt.reshape(N_CTX, D_MODEL)
    xs = x_sample.reshape(N_LAT, D_MODEL)
    moe = None
    new = [[] for _ in range(6)]
    for l in range(DEPTH):
        lam = lam_all[l, :1]
        sink = swa_sink[l]

        xp, proj_c, gate_c = _pre_call(l, xp, moe, mod, n1, w["win"], w["wg"], latent=False)
        xs, proj_l, gate_l = _pre_call(l, xs, moe, mod, n1, w["win"], w["wg"], latent=True)

        oda_c, omla_c, oswa_c, ckv_c, k_c, v_c, kr_c, sk_c, sv_c = _ctx_attn_call(
            l, lam, sink, proj_c, subln, qn, w["wqb"], kvn, w["wkv"])
        new[0].append(k_c.reshape(BATCH, SEQ, DA_HEADS, 2 * DA_DH))
        new[1].append(v_c.reshape(BATCH, SEQ, DA_HEADS, 2 * DA_DH))
        new[2].append(ckv_c.reshape(BATCH, SEQ, MLA_KV_RANK))
        new[3].append(kr_c.reshape(BATCH, SEQ, MLA_ROPE))
        new[4].append(sk_c.reshape(BATCH, SEQ, SWA_KV_HEADS, SWA_DH))
        new[5].append(sv_c.reshape(BATCH, SEQ, SWA_KV_HEADS, SWA_DH))

        daq, mlaq, swaq, dak, dav, mlak, mlav, swak, swav = _lat_prep_call(
            l, proj_l, tab, caches, qn, w["wqb"], kvn, w["wkv"])
        oda_l = _lat_da_call(l, lam, daq, dak, dav, subln).reshape(N_LAT, 512)
        omla_l = _lat_mla_call(mlaq, mlak, mlav).reshape(N_LAT, 512)
        oswa_l = _lat_swa_call(sink, swa_bias, swaq, swak, swav).reshape(N_LAT, 512)

        post_w = (w["wda"], w["wmla"], w["wswa"], w["wo"], w["wrh"], w["wrl"])
        xp, h2_c, lg_c = _post_call(l, xp, oda_c, omla_c, oswa_c, gate_c, mod, n2, *post_w, latent=False)
        xs, h2_l, lg_l = _post_call(l, xs, oda_l, omla_l, oswa_l, gate_l, mod, n2, *post_w, latent=True)

        idx_c, g_c, _ = _router_call(lg_c, CAP_CTX)
        idx_l, g_l, cnt_l = _router_call(lg_l, CAP_LAT)
        idx = jnp.concatenate([idx_c, idx_l], axis=1).reshape(N_EXPERTS, 1, SLOTS)
        gsel = jnp.concatenate([g_c, g_l], axis=1).reshape(N_EXPERTS, 1, SLOTS)
        n0 = cnt_l[:, 0].astype(I32)
        bnd = jnp.stack([jnp.zeros_like(n0), jnp.full_like(n0, CAP_CTX), CAP_CTX + n0,
                         jnp.full_like(n0, SLOTS)], axis=1)
        y_c, y_l = _ffn_call(l, idx, h2_c.reshape(N_CTX, SUB, LANE), h2_l.reshape(N_LAT, SUB, LANE),
                             w_ff1, w_ff3, w_ff2)
        acc = _combine_call(idx, gsel, bnd.reshape(N_EXPERTS, 1, N_HALVES + 1), y_c, y_l)
        moe = acc.reshape(N_HALVES * TOK_HALF * D_CHUNKS, LANE)

    fn = final_norm[None, None]
    y_prompt = _final_call(xp, moe, mod, fn, latent=False).reshape(BATCH, SEQ, D_MODEL)
    y_sample = _final_call(xs, moe, mod, fn, latent=True).reshape(DEC_BATCH, DEC_SEQ, D_MODEL)
    return (y_prompt, y_sample) + tuple(jnp.stack(n, axis=1) for n in new)
```

```python
import functools
import math

import jax
import jax.numpy as jnp
from jax import lax
from jax.experimental import pallas as pl
from jax.experimental.pallas import tpu as pltpu

F32 = jnp.float32
BF = jnp.bfloat16
I32 = jnp.int32

D_MODEL = 1024
BATCH = 16
SEQ = 256
DEPTH = 2
DEC_BATCH = 2
DEC_SEQ = 4096
PAST_LEN = 512
GRID_W = 64
ROPE_BASE = 10000.0
EPS = 1e-6
DA_HEADS = 4
DA_DH = 64
DA_OUT = DA_HEADS * 2 * DA_DH
MLA_HEADS = 8
MLA_Q_RANK = 256
MLA_KV_RANK = 128
MLA_NOPE = 64
MLA_ROPE = 32
MLA_V = 64
MLA_OUT = MLA_HEADS * MLA_V
SWA_HEADS = 8
SWA_KV_HEADS = 2
SWA_DH = 64
W_BLOCK = 128
SWA_OUT = SWA_HEADS * SWA_DH
N_EXPERTS = 16
EXPERT_FF = 1024
CAPACITY_FACTOR = 2

LANE = 128
SUB = 8
HALF = 64
N_CTX = BATCH * SEQ
N_LAT = DEC_BATCH * DEC_SEQ
KEYS = PAST_LEN + DEC_SEQ
CAP_CTX = CAPACITY_FACTOR * N_CTX // N_EXPERTS
CAP_LAT = CAPACITY_FACTOR * N_LAT // N_EXPERTS
SLOTS = CAP_CTX + CAP_LAT
TOK_HALF = 4096
N_HALVES = (N_CTX + N_LAT) // TOK_HALF
D_CHUNKS = D_MODEL // LANE

C_DAQ, C_DAK, C_DAV = 0, 512, 1024
C_MQ, C_MKV, C_MKR = 1536, 1792, 1920
C_SQ, C_SK, C_SV = 2048, 2560, 2688
PROJ_COLS = 2816
GATE_COLS = 3 * D_MODEL
MLA_SCALE = (MLA_NOPE + MLA_ROPE) ** -0.5
HEAD_SCALE = DA_DH ** -0.5
LOG2E = math.log2(math.e)

TM_PRE = 256
TM_FINAL = 512
TM_POST = 512
CHUNK = 512
TQ = 256
DA_STEP_HEADS = 4
MLA_STEP_PAIRS = 4
ROUTE_TILE = 128
ROUTE_CHUNK = 512
THRESHOLD_STEPS = 48
FFN_ROWS = 512
FF_SPLIT = 2
COMBINE_GROUP = 8


def _cp(sem, vmem_mb):
    return pltpu.CompilerParams(dimension_semantics=sem, vmem_limit_bytes=vmem_mb * 1024 * 1024)


def _dot(a, b):
    return jnp.dot(a, b, preferred_element_type=F32)


def _dot_nt(a, b):
    return lax.dot_general(a, b, (((1,), (1,)), ((), ())), preferred_element_type=F32)


def _split(a):
    hi = a.astype(BF)
    lo = (a - hi.astype(F32)).astype(BF)
    return hi, lo


def _dot3(a, w):
    ah, al = _split(a)
    wh, wl = _split(w)
    return _dot(ah, wh) + _dot(ah, wl) + _dot(al, wh)


def _rms(x, g):
    return x * lax.rsqrt(jnp.mean(x * x, axis=-1, keepdims=True) + EPS) * g


def _lane(shape):
    return lax.broadcasted_iota(I32, shape, len(shape) - 1)


def _softmax_pv(segs, sink=None):
    m = None
    for s, _ in segs:
        ms = jnp.max(s, axis=-1, keepdims=True)
        m = ms if m is None else jnp.maximum(m, ms)
    if sink is not None:
        m = jnp.maximum(m, sink)
    l = None
    o = None
    for s, v in segs:
        e = jnp.exp(s - m)
        ls = jnp.sum(e, axis=-1, keepdims=True)
        os_ = _dot(e.astype(BF), v)
        l = ls if l is None else l + ls
        o = os_ if o is None else o + os_
    if sink is not None:
        l = l + jnp.exp(sink - m)
    return o / l


def _mask_halves(q):
    lo = _lane(q.shape) < HALF
    return jnp.where(lo, q, 0.0).astype(BF), jnp.where(lo, 0.0, q).astype(BF)


def _pair(o_a, o_b):
    return jnp.where(_lane(o_a.shape) < HALF, o_a, o_b)


def _dup(x):
    r = pltpu.roll(x, HALF, 1)
    lo = _lane(x.shape) < HALF
    return jnp.where(lo, x, r), jnp.where(lo, r, x)


def _rope(x, c, s):
    n = x.shape[-1]
    even = (_lane(x.shape) % 2) == 0
    sw = jnp.where(even, pltpu.roll(x, n - 1, 1), pltpu.roll(x, 1, 1))
    return x * c + sw * s


def _da_head(q, k, v, lam, subln, lam_init):
    q1, q2 = _mask_halves(q)
    o1 = _softmax_pv([(_dot_nt(q1, k), v)])
    o2 = _softmax_pv([(_dot_nt(q2, k), v)])
    return _rms(o1 - lam * o2, subln) * (1.0 - lam_init)


def _ada_kernel(c_ref, w_ref, b_ref, o_ref):
    c = c_ref[...]
    a = c * jax.nn.sigmoid(c)
    o_ref[...] = _dot3(a, w_ref[...]) + b_ref[...]


def _ada_call(c_all, w_ada, b_ada):
    tn = 1536
    return pl.pallas_call(
        _ada_kernel,
        grid=(DEPTH, 6 * D_MODEL // tn),
        in_specs=[pl.BlockSpec((SUB, D_MODEL), lambda l, j: (0, 0)),
                  pl.BlockSpec((None, D_MODEL, tn), lambda l, j: (l, 0, j)),
                  pl.BlockSpec((None, 1, tn), lambda l, j: (l, 0, j))],
        out_specs=pl.BlockSpec((None, SUB, tn), lambda l, j: (l, 0, j)),
        out_shape=jax.ShapeDtypeStruct((DEPTH, SUB, 6 * D_MODEL), F32),
        compiler_params=_cp(("parallel", "parallel"), 40),
        name="ada",
    )(c_all, w_ada, b_ada.reshape(DEPTH, 1, 6 * D_MODEL))


def _lam_kernel(q1, k1, q2, k2, o_ref):
    s1 = jnp.sum(q1[...] * k1[...], axis=-1, keepdims=True)
    s2 = jnp.sum(q2[...] * k2[...], axis=-1, keepdims=True)
    row = lax.broadcasted_iota(I32, (DEPTH, 1), 0)
    init = jnp.zeros((DEPTH, 1), F32)
    for l in range(DEPTH):
        init = jnp.where(row == l, _lam_init(l), init)
    o_ref[...] = jnp.broadcast_to(jnp.exp(s1) - jnp.exp(s2) + init, o_ref.shape)


def _lam_init(layer):
    return 0.8 - 0.6 * math.exp(-0.3 * layer)


def _lam_call(q1, k1, q2, k2):
    return pl.pallas_call(
        _lam_kernel,
        out_shape=jax.ShapeDtypeStruct((DEPTH, LANE), F32),
        name="lam",
    )(q1, k1, q2, k2)


def _moe_rows(moe_ref, tm):
    return jnp.concatenate([moe_ref[pl.ds(k, tm, stride=D_CHUNKS), :] for k in range(D_CHUNKS)], axis=1)


def _pre_body(x, mod_ref, n1_ref, win_ref, wg_ref, proj_ref, gate_ref):
    m = mod_ref[...]
    h = _rms(x, n1_ref[...]) * (1.0 + m[1:2]) + m[0:1]
    hb = h.astype(BF)
    proj_ref[...] = _dot(hb, win_ref[...]).astype(proj_ref.dtype)
    gate_ref[...] = jax.nn.sigmoid(_dot(hb, wg_ref[...])).astype(BF)


def _pre_first_kernel(x_ref, mod_ref, n1_ref, win_ref, wg_ref, proj_ref, gate_ref):
    _pre_body(x_ref[...], mod_ref, n1_ref, win_ref, wg_ref, proj_ref, gate_ref)


def _pre_next_kernel(x_ref, moe_ref, modp_ref, mod_ref, n1_ref, win_ref, wg_ref, xo_ref, proj_ref, gate_ref):
    x = x_ref[...] + modp_ref[...][5:6] * _moe_rows(moe_ref, TM_PRE)
    xo_ref[...] = x
    _pre_body(x, mod_ref, n1_ref, win_ref, wg_ref, proj_ref, gate_ref)


def _mod_spec(layer, tm, latent):
    per = DEC_SEQ // tm
    if latent:
        return pl.BlockSpec((None, None, 6, D_MODEL), lambda i: (layer, 1 + i // per, 0, 0))
    return pl.BlockSpec((None, None, 6, D_MODEL), lambda i: (layer, 0, 0, 0))


def _layer(layer, shape):
    nd = len(shape)
    return pl.BlockSpec((None,) + shape, lambda *_: (layer,) + (0,) * nd, pipeline_mode=pl.Buffered(1))


def _pre_call(layer, x, moe, mod, n1, win, wg, latent):
    n = x.shape[0]
    tm = TM_PRE
    row = pl.BlockSpec((tm, D_MODEL), lambda i: (i, 0))
    w_specs = [_mod_spec(layer, tm, latent), _layer(layer, (1, D_MODEL)), _layer(layer, (D_MODEL, PROJ_COLS)),
               _layer(layer, (D_MODEL, GATE_COLS))]
    outs = [jax.ShapeDtypeStruct((n, PROJ_COLS), BF if latent else F32), jax.ShapeDtypeStruct((n, GATE_COLS), BF)]
    out_specs = [pl.BlockSpec((tm, PROJ_COLS), lambda i: (i, 0)), pl.BlockSpec((tm, GATE_COLS), lambda i: (i, 0))]
    if moe is None:
        proj, gate = pl.pallas_call(
            _pre_first_kernel, grid=(n // tm,), in_specs=[row] + w_specs, out_specs=out_specs, out_shape=outs,
            compiler_params=_cp(("parallel",), 52), name="pre_first",
        )(x, mod, n1, win, wg)
        return x, proj, gate
    moe_off = (N_CTX // tm) if latent else 0
    moe_spec = pl.BlockSpec((tm * D_CHUNKS, LANE), lambda i: (i + moe_off, 0))
    xo, proj, gate = pl.pallas_call(
        _pre_next_kernel, grid=(n // tm,),
        in_specs=[row, moe_spec, _mod_spec(layer - 1, tm, latent)] + w_specs,
        out_specs=[row] + out_specs,
        out_shape=[jax.ShapeDtypeStruct((n, D_MODEL), F32)] + outs,
        compiler_params=_cp(("parallel",), 52), name="pre_next",
    )(x, moe, mod, mod, n1, win, wg)
    return xo, proj, gate


def _ctx_attn_kernel(lam_init, lam_ref, sink_ref, p_ref, subln_ref, qn_ref, wqb_ref, kvn_ref, wkv_ref,
                     oda_ref, omla_ref, oswa_ref, ckv_ref, kc_ref, vc_ref, kr_ref, sk_ref, sv_ref):
    p = p_ref[...]
    lam = lam_ref[0]
    for h in range(DA_HEADS):
        kc_ref[pl.ds(h, SEQ, stride=DA_HEADS), :] = p[:, C_DAK + h * LANE:C_DAK + (h + 1) * LANE]
        vc_ref[pl.ds(h, SEQ, stride=DA_HEADS), :] = p[:, C_DAV + h * LANE:C_DAV + (h + 1) * LANE]
    kr_ref[...] = p[:, C_MKR:C_MKR + MLA_ROPE]
    sk_ref[...] = p[:, C_SK:C_SK + LANE]
    sv_ref[...] = p[:, C_SV:C_SV + LANE]
    for h in range(DA_HEADS):
        blk = slice(h * LANE, (h + 1) * LANE)
        q = p[:, C_DAQ + h * LANE:C_DAQ + (h + 1) * LANE] * HEAD_SCALE
        k = p[:, C_DAK + h * LANE:C_DAK + (h + 1) * LANE].astype(BF)
        v = p[:, C_DAV + h * LANE:C_DAV + (h + 1) * LANE].astype(BF)
        oda_ref[:, blk] = _da_head(q, k, v, lam, subln_ref[...], lam_init).astype(BF)
    qn = _rms(p[:, C_MQ:C_MQ + MLA_Q_RANK], qn_ref[...]).astype(BF)
    cq = _dot(qn, wqb_ref[...])
    ckv = _rms(p[:, C_MKV:C_MKV + MLA_KV_RANK], kvn_ref[...])
    ckv_ref[...] = ckv
    kv = _dot(ckv.astype(BF), wkv_ref[...])
    kr_sh = pltpu.roll(p[:, C_MKR:C_MKR + LANE], HALF, 1)
    for m in range(MLA_HEADS // 2):
        vpair = kv[:, MLA_HEADS * LANE + m * LANE:MLA_HEADS * LANE + (m + 1) * LANE].astype(BF)
        outs = []
        for h in (2 * m, 2 * m + 1):
            q = (cq[:, h * LANE:(h + 1) * LANE] * MLA_SCALE).astype(BF)
            k = (kv[:, h * LANE:(h + 1) * LANE] + kr_sh).astype(BF)
            outs.append(_softmax_pv([(_dot_nt(q, k), vpair)]))
        omla_ref[:, m * LANE:(m + 1) * LANE] = _pair(outs[0], outs[1]).astype(BF)
    kd = _dup(p[:, C_SK:C_SK + LANE])
    vd = _dup(p[:, C_SV:C_SV + LANE])
    for m in range(SWA_HEADS // 2):
        g = (2 * m) // (SWA_HEADS // SWA_KV_HEADS)
        k = kd[g].astype(BF)
        v = vd[g].astype(BF)
        qa, qb = _mask_halves(p[:, C_SQ + m * LANE:C_SQ + (m + 1) * LANE] * HEAD_SCALE)
        oa = _softmax_pv([(_dot_nt(qa, k), v)], sink_ref[2 * m])
        ob = _softmax_pv([(_dot_nt(qb, k), v)], sink_ref[2 * m + 1])
        oswa_ref[:, m * LANE:(m + 1) * LANE] = _pair(oa, ob).astype(BF)


def _smem():
    return pl.BlockSpec(memory_space=pltpu.SMEM)


def _ctx_attn_call(layer, lam, sink, proj, subln, qn, wqb, kvn, wkv):
    row512 = pl.BlockSpec((SEQ, 512), lambda b: (b, 0))
    row128 = pl.BlockSpec((SEQ, LANE), lambda b: (b, 0))
    heads = pl.BlockSpec((SEQ * DA_HEADS, LANE), lambda b: (b, 0))
    return pl.pallas_call(
        functools.partial(_ctx_attn_kernel, _lam_init(layer)),
        grid=(BATCH,),
        in_specs=[_smem(), _smem(), pl.BlockSpec((SEQ, PROJ_COLS), lambda b: (b, 0)),
                  _layer(layer, (1, LANE)), _layer(layer, (1, MLA_Q_RANK)),
                  _layer(layer, (MLA_Q_RANK, MLA_HEADS * LANE)), _layer(layer, (1, MLA_KV_RANK)),
                  _layer(layer, (MLA_KV_RANK, MLA_HEADS * LANE + MLA_OUT))],
        out_specs=[row512, row512, row512, row128, heads, heads,
                   pl.BlockSpec((SEQ, MLA_ROPE), lambda b: (b, 0)), row128, row128],
        out_shape=[jax.ShapeDtypeStruct((N_CTX, 512), BF)] * 3
        + [jax.ShapeDtypeStruct((N_CTX, LANE), F32)]
        + [jax.ShapeDtypeStruct((N_CTX * DA_HEADS, LANE), F32)] * 2
        + [jax.ShapeDtypeStruct((N_CTX, MLA_ROPE), F32)] + [jax.ShapeDtypeStruct((N_CTX, LANE), F32)] * 2,
        compiler_params=_cp(("parallel",), 40), name="ctx_attn",
    )(lam, sink, proj, subln, qn, wqb, kvn, wkv)


def _lat_prep_kernel(p_ref, tab_ref, cdk_ref, cdv_ref, cckv_ref, ckr_ref, csk_ref, csv_ref,
                     qn_ref, wqb_ref, kvn_ref, wkv_ref,
                     daq_ref, mlaq_ref, swaq_ref, dak_ref, dav_ref, mlak_ref, mlav_ref, swak_ref, swav_ref):
    j = pl.program_id(1)

    def write_kv(dk, dv, ckv, kr_sh, sk, sv):
        ones_col = jnp.where(_lane((CHUNK, LANE)) == 0, 1.0, 0.0).astype(BF)
        dak_ref[...] = dk.astype(BF)
        for h in range(DA_HEADS):
            dav_ref[:, 2 * h * LANE:(2 * h + 1) * LANE] = dv[:, h * LANE:(h + 1) * LANE].astype(BF)
            dav_ref[:, (2 * h + 1) * LANE:(2 * h + 2) * LANE] = ones_col
        kv = _dot(ckv.astype(BF), wkv_ref[...])
        for h in range(MLA_HEADS):
            mlak_ref[:, h * LANE:(h + 1) * LANE] = (kv[:, h * LANE:(h + 1) * LANE] + kr_sh).astype(BF)
        for p in range(MLA_HEADS // 2):
            mlav_ref[:, 2 * p * LANE:(2 * p + 1) * LANE] = kv[:, (MLA_HEADS + p) * LANE:(MLA_HEADS + p + 1) * LANE].astype(BF)
            mlav_ref[:, (2 * p + 1) * LANE:(2 * p + 2) * LANE] = ones_col
        k0, k1 = _dup(sk)
        v0, v1 = _dup(sv)
        swak_ref[:, 0:LANE] = k0.astype(BF)
        swak_ref[:, LANE:2 * LANE] = k1.astype(BF)
        swav_ref[:, 0:LANE] = v0.astype(BF)
        swav_ref[:, LANE:2 * LANE] = ones_col
        swav_ref[:, 2 * LANE:3 * LANE] = v1.astype(BF)
        swav_ref[:, 3 * LANE:4 * LANE] = ones_col

    @pl.when(j == 0)
    def _():
        write_kv(cdk_ref[...], cdv_ref[...], cckv_ref[...], pltpu.roll(ckr_ref[...], HALF, 1),
                 csk_ref[...], csv_ref[...])

    @pl.when(j > 0)
    def _():
        c64, s64 = tab_ref[0], tab_ref[1]
        cmq, smq = tab_ref[2], tab_ref[3]
        ckr, skr = tab_ref[4], tab_ref[5]

        def cols(c0, width):
            return p_ref[:, c0:c0 + width].astype(F32)

        def blk(c0, h):
            return cols(c0 + h * LANE, LANE)

        for h in range(DA_HEADS):
            daq_ref[:, h * LANE:(h + 1) * LANE] = (_rope(blk(C_DAQ, h), c64, s64) * (HEAD_SCALE * LOG2E)).astype(BF)
            swaq_ref[:, h * LANE:(h + 1) * LANE] = (_rope(blk(C_SQ, h), c64, s64) * (HEAD_SCALE * LOG2E)).astype(BF)
        dk = jnp.concatenate([_rope(blk(C_DAK, h), c64, s64) for h in range(DA_HEADS)], axis=1)
        qn = _rms(cols(C_MQ, MLA_Q_RANK), qn_ref[...]).astype(BF)
        cq = _dot(qn, wqb_ref[...])
        for h in range(MLA_HEADS):
            mlaq_ref[:, h * LANE:(h + 1) * LANE] = (
                _rope(cq[:, h * LANE:(h + 1) * LANE], cmq, smq) * (MLA_SCALE * LOG2E)).astype(BF)
        ckv = _rms(cols(C_MKV, MLA_KV_RANK), kvn_ref[...])
        kr_sh = pltpu.roll(_rope(cols(C_MKR, LANE), ckr, skr), HALF, 1)
        sk = _rope(cols(C_SK, LANE), c64, s64)
        write_kv(dk, cols(C_DAV, 512), ckv, kr_sh, sk, cols(C_SV, LANE))


def _lat_prep_call(layer, proj, tab, caches, qn, wqb, kvn, wkv):
    cdk, cdv, cckv, ckr, csk, csv = caches
    nj = 1 + DEC_SEQ // CHUNK

    def own(width):
        return pl.BlockSpec((None, CHUNK, width), lambda b, j: (b, jnp.maximum(j - 1, 0), 0))

    def cache(width):
        return pl.BlockSpec((None, None, PAST_LEN, width), lambda b, j: (b, layer, 0, 0))

    def allk(width):
        return pl.BlockSpec((None, CHUNK, width), lambda b, j: (b, j, 0))

    def shp(rows, width):
        return jax.ShapeDtypeStruct((DEC_BATCH, rows, width), BF)

    return pl.pallas_call(
        _lat_prep_kernel,
        grid=(DEC_BATCH, nj),
        in_specs=[own(PROJ_COLS),
                  pl.BlockSpec((6, CHUNK, LANE), lambda b, j: (0, jnp.maximum(j - 1, 0), 0)),
                  cache(512), cache(512), cache(LANE), cache(LANE), cache(LANE), cache(LANE),
                  _layer(layer, (1, MLA_Q_RANK)), _layer(layer, (MLA_Q_RANK, MLA_HEADS * LANE)),
                  _layer(layer, (1, MLA_KV_RANK)), _layer(layer, (MLA_KV_RANK, MLA_HEADS * LANE + MLA_OUT))],
        out_specs=[own(512), own(MLA_HEADS * LANE), own(512),
                   allk(512), allk(1024), allk(MLA_HEADS * LANE), allk(1024), allk(2 * LANE), allk(4 * LANE)],
        out_shape=[shp(DEC_SEQ, 512), shp(DEC_SEQ, MLA_HEADS * LANE), shp(DEC_SEQ, 512),
                   shp(KEYS, 512), shp(KEYS, 1024), shp(KEYS, MLA_HEADS * LANE), shp(KEYS, 1024),
                   shp(KEYS, 2 * LANE), shp(KEYS, 4 * LANE)],
        compiler_params=_cp(("parallel", "arbitrary"), 48), name="lat_prep",
    )(proj.reshape(DEC_BATCH, DEC_SEQ, PROJ_COLS), tab, cdk, cdv, cckv, ckr, csk, csv, qn, wqb, kvn, wkv)


def _softmax2_pv(s, v_aug):
    e = jnp.exp2(s - jnp.max(s, axis=-1, keepdims=True)).astype(BF)
    o = _dot(e, v_aug)
    return o[:, 0:LANE] / o[:, LANE:LANE + 1]


def _lat_da_kernel(lam_init, lam_ref, q_ref, k_ref, v_ref, subln_ref, o_ref):
    for h in range(DA_STEP_HEADS):
        cols = slice(h * LANE, (h + 1) * LANE)
        q1, q2 = _mask_halves(q_ref[:, cols].astype(F32))
        k = k_ref[:, cols]
        v = v_ref[:, 2 * h * LANE:(2 * h + 2) * LANE]
        o = _softmax2_pv(_dot_nt(q1, k), v) - lam_ref[0] * _softmax2_pv(_dot_nt(q2, k), v)
        o_ref[:, cols] = (_rms(o, subln_ref[...]) * (1.0 - lam_init)).astype(BF)


def _lat_da_call(layer, lam, q, k, v, subln):
    return pl.pallas_call(
        functools.partial(_lat_da_kernel, _lam_init(layer)),
        grid=(DEC_BATCH, DA_HEADS // DA_STEP_HEADS, DEC_SEQ // TQ),
        in_specs=[_smem(),
                  pl.BlockSpec((None, TQ, DA_STEP_HEADS * LANE), lambda b, h, i: (b, i, h)),
                  pl.BlockSpec((None, KEYS, DA_STEP_HEADS * LANE), lambda b, h, i: (b, 0, h)),
                  pl.BlockSpec((None, KEYS, 2 * DA_STEP_HEADS * LANE), lambda b, h, i: (b, 0, h)),
                  _layer(layer, (1, LANE))],
        out_specs=pl.BlockSpec((None, TQ, DA_STEP_HEADS * LANE), lambda b, h, i: (b, i, h)),
        out_shape=jax.ShapeDtypeStruct((DEC_BATCH, DEC_SEQ, 512), BF),
        compiler_params=_cp(("parallel", "parallel", "arbitrary"), 56), name="lat_da",
    )(lam, q, k, v, subln)


def _lat_mla_kernel(q_ref, k_ref, v_ref, o_ref):
    for p in range(MLA_STEP_PAIRS):
        v = v_ref[:, 2 * p * LANE:(2 * p + 2) * LANE]
        outs = []
        for h in (2 * p, 2 * p + 1):
            cols = slice(h * LANE, (h + 1) * LANE)
            outs.append(_softmax2_pv(_dot_nt(q_ref[:, cols], k_ref[:, cols]), v))
        o_ref[:, p * LANE:(p + 1) * LANE] = _pair(outs[0], outs[1]).astype(BF)


def _lat_mla_call(q, k, v):
    sp = MLA_STEP_PAIRS
    return pl.pallas_call(
        _lat_mla_kernel,
        grid=(DEC_BATCH, MLA_HEADS // (2 * sp), DEC_SEQ // TQ),
        in_specs=[pl.BlockSpec((None, TQ, 2 * sp * LANE), lambda b, m, i: (b, i, m)),
                  pl.BlockSpec((None, KEYS, 2 * sp * LANE), lambda b, m, i: (b, 0, m)),
                  pl.BlockSpec((None, KEYS, 2 * sp * LANE), lambda b, m, i: (b, 0, m))],
        out_specs=pl.BlockSpec((None, TQ, sp * LANE), lambda b, m, i: (b, i, m)),
        out_shape=jax.ShapeDtypeStruct((DEC_BATCH, DEC_SEQ, 512), BF),
        compiler_params=_cp(("parallel", "parallel", "arbitrary"), 56), name="lat_mla",
    )(q, k, v)


def _swa_bias():
    rows = (SWA_HEADS // SWA_KV_HEADS) * W_BLOCK
    qi = (jnp.arange(rows) % W_BLOCK)[:, None]
    loc = (jnp.arange(PAST_LEN + 3 * W_BLOCK) - PAST_LEN)[None, :]
    always = (loc < 0) | ((loc >= W_BLOCK) & (loc < 2 * W_BLOCK))
    prev = (loc >= 0) & (loc < W_BLOCK) & (loc >= qi)
    nxt = (loc >= 2 * W_BLOCK) & (loc - 2 * W_BLOCK <= qi)
    ok = jnp.stack([always | nxt, always | prev | nxt, always | prev])
    return jnp.where(ok, 0.0, -jnp.inf).astype(F32)


def _lat_swa_kernel(sink_ref, bias_a_ref, bias_b_ref, q_ref, kc_ref, kp_ref, ka_ref, kb_ref, kn_ref,
                    vc_ref, vp_ref, va_ref, vb_ref, vn_ref, o_ref):
    group = SWA_HEADS // SWA_KV_HEADS
    rows = group * W_BLOCK
    head_of_row = lax.broadcasted_iota(I32, (rows, 1), 0) // W_BLOCK
    windows = ((bias_a_ref, (kc_ref, kp_ref, ka_ref, kb_ref), (vc_ref, vp_ref, va_ref, vb_ref)),
               (bias_b_ref, (kc_ref, ka_ref, kb_ref, kn_ref), (vc_ref, va_ref, vb_ref, vn_ref)))
    for sub, (bias_ref, k_refs, v_refs) in enumerate(windows):
        q_rows = slice(sub * W_BLOCK, (sub + 1) * W_BLOCK)
        for g in range(SWA_KV_HEADS):
            gs = slice(g * LANE, (g + 1) * LANE)
            vs = slice(2 * g * LANE, (2 * g + 2) * LANE)
            q_parts = []
            sink = jnp.zeros((rows, 1), F32)
            for j in range(group):
                sink = jnp.where(head_of_row == j, sink_ref[g * group + j] * LOG2E, sink)
            for p in range(group // 2):
                blk = g * (group // 2) + p
                q_parts.extend(_mask_halves(q_ref[q_rows, blk * LANE:(blk + 1) * LANE].astype(F32)))
            q = jnp.concatenate(q_parts, axis=0)
            k = jnp.concatenate([r[:, gs] for r in k_refs], axis=0)
            v = jnp.concatenate([r[:, vs] for r in v_refs], axis=0)
            s = _dot_nt(q, k) + bias_ref[...]
            m = jnp.maximum(jnp.max(s, axis=-1, keepdims=True), sink)
            ov = _dot(jnp.exp2(s - m).astype(BF), v)
            o = ov[:, 0:LANE] / (ov[:, LANE:LANE + 1] + jnp.exp2(sink - m))
            for p in range(group // 2):
                blk = g * (group // 2) + p
                o_a = o[(2 * p) * W_BLOCK:(2 * p + 1) * W_BLOCK]
                o_b = o[(2 * p + 1) * W_BLOCK:(2 * p + 2) * W_BLOCK]
                o_ref[q_rows, blk * LANE:(blk + 1) * LANE] = _pair(o_a, o_b).astype(BF)


def _lat_swa_call(sink, bias, q, k, v):
    nb = DEC_SEQ // W_BLOCK
    cb = PAST_LEN // W_BLOCK
    steps = nb // 2

    def ctx(width):
        return pl.BlockSpec((None, PAST_LEN, width), lambda b, n: (b, 0, 0))

    def loc(d, width):
        return pl.BlockSpec((None, W_BLOCK, width),
                            lambda b, n: (b, cb + jnp.clip(2 * n + d, 0, nb - 1), 0))

    def bias_spec(pick):
        return pl.BlockSpec((None,) + bias.shape[1:], lambda b, n: (pick(n), 0, 0))

    kw, vw = 2 * LANE, 4 * LANE
    return pl.pallas_call(
        _lat_swa_kernel,
        grid=(DEC_BATCH, steps),
        in_specs=[_smem(),
                  bias_spec(lambda n: jnp.where(n == 0, 0, 1)), bias_spec(lambda n: jnp.where(n == steps - 1, 2, 1)),
                  pl.BlockSpec((None, 2 * W_BLOCK, 512), lambda b, n: (b, n, 0)),
                  ctx(kw), loc(-1, kw), loc(0, kw), loc(1, kw), loc(2, kw),
                  ctx(vw), loc(-1, vw), loc(0, vw), loc(1, vw), loc(2, vw)],
        out_specs=pl.BlockSpec((None, 2 * W_BLOCK, 512), lambda b, n: (b, n, 0)),
        out_shape=jax.ShapeDtypeStruct((DEC_BATCH, DEC_SEQ, 512), BF),
        compiler_params=_cp(("parallel", "arbitrary"), 40), name="lat_swa",
    )(sink, bias, bias, q, k, k, k, k, k, v, v, v, v, v)


def _rows_to_tiles(o_ref, val, tm):
    for k in range(D_CHUNKS):
        o_ref[pl.ds(k, tm, stride=D_CHUNKS), :] = val[:, k * LANE:(k + 1) * LANE]


def _post_kernel(x_ref, oda_ref, omla_ref, oswa_ref, gate_ref, mod_ref, n2_ref, wda_ref, wmla_ref, wswa_ref,
                 wo_ref, wrh_ref, wrl_ref, xo_ref, h2_ref, lg_ref):
    m = mod_ref[...]
    g = gate_ref[...].astype(F32)
    merged = (g[:, 0:D_MODEL] * _dot(oda_ref[...], wda_ref[...])
              + g[:, D_MODEL:2 * D_MODEL] * _dot(omla_ref[...], wmla_ref[...])
              + g[:, 2 * D_MODEL:] * _dot(oswa_ref[...], wswa_ref[...]))
    x = x_ref[...] + m[2:3] * _dot(merged.astype(BF), wo_ref[...])
    xo_ref[...] = x
    h2 = _rms(x, n2_ref[...]) * (1.0 + m[4:5]) + m[3:4]
    _rows_to_tiles(h2_ref, h2, TM_POST)
    hh, hl = _split(h2)
    lg_ref[...] = _dot_nt(wrh_ref[...], hh) + _dot_nt(wrl_ref[...], hh) + _dot_nt(wrh_ref[...], hl)


def _post_call(layer, x, oda, omla, oswa, gate, mod, n2, wda, wmla, wswa, wo, wrh, wrl, latent):
    n = x.shape[0]
    tm = TM_POST
    row = pl.BlockSpec((tm, D_MODEL), lambda i: (i, 0))
    o512 = pl.BlockSpec((tm, 512), lambda i: (i, 0))
    return pl.pallas_call(
        _post_kernel, grid=(n // tm,),
        in_specs=[row, o512, o512, o512, pl.BlockSpec((tm, GATE_COLS), lambda i: (i, 0)),
                  _mod_spec(layer, tm, latent), _layer(layer, (1, D_MODEL)),
                  _layer(layer, (512, D_MODEL)), _layer(layer, (512, D_MODEL)), _layer(layer, (512, D_MODEL)),
                  _layer(layer, (D_MODEL, D_MODEL)), _layer(layer, (N_EXPERTS, D_MODEL)),
                  _layer(layer, (N_EXPERTS, D_MODEL))],
        out_specs=[row, pl.BlockSpec((tm * D_CHUNKS, LANE), lambda i: (i, 0)),
                   pl.BlockSpec((N_EXPERTS, tm), lambda i: (0, i))],
        out_shape=[jax.ShapeDtypeStruct((n, D_MODEL), F32),
                   jax.ShapeDtypeStruct((n * D_CHUNKS, LANE), F32),
                   jax.ShapeDtypeStruct((N_EXPERTS, n), F32)],
        compiler_params=_cp(("parallel",), 48), name="post",
    )(x, oda, omla, oswa, gate, mod, n2, wda, wmla, wswa, wo, wrh, wrl)


def _router_kernel(cap, lg_ref, q_ref, a_ref, lo_ref, hi_ref, cnt_ref):
    e_n, n = lg_ref.shape
    lg = lg_ref[...]
    shifted = lg - jnp.max(lg, axis=0, keepdims=True)
    ex = jnp.exp(shifted)
    den = jnp.sum(ex, axis=0, keepdims=True)
    aff = ex / den
    score = shifted - jnp.log(den)
    capf = float(cap)

    def count(mask):
        return jnp.sum(jnp.where(mask, 1.0, 0.0), axis=1, keepdims=True)

    def halve(_, bounds):
        lo, hi = bounds
        mid = 0.5 * (lo + hi)
        enough = count(score >= mid) >= capf
        return jnp.where(enough, mid, lo), jnp.where(enough, hi, mid)

    lo, hi = lax.fori_loop(0, THRESHOLD_STEPS, halve,
                           (jnp.min(score, axis=1, keepdims=True), jnp.ones((e_n, 1), F32)))
    gt = score >= hi
    eq = (score >= lo) & (score < hi)
    need = capf - count(gt)
    tok = lax.broadcasted_iota(I32, (e_n, n), 1)
    n_bits = n.bit_length()

    def tie_bit(i, bound):
        cand = bound | jnp.left_shift(jnp.int32(1), n_bits - 1 - i)
        ok = (cand <= n) & (count(eq & (tok < cand)) <= need)
        return jnp.where(ok, cand, bound)

    bound = lax.fori_loop(0, n_bits, tie_bit, jnp.zeros((e_n, 1), I32))
    sel = gt | (eq & (tok < bound))
    cnt_ref[...] = jnp.broadcast_to(count(sel & (tok < TOK_HALF)), cnt_ref.shape)
    blk = 2 * LANE
    per_chunk = ROUTE_CHUNK // blk
    n_chunks = n // ROUTE_CHUNK
    upper = (lax.broadcasted_iota(I32, (blk, blk), 0) <= lax.broadcasted_iota(I32, (blk, blk), 1))
    upper = jnp.where(upper, 1.0, 0.0).astype(BF)
    lane = _lane((e_n, LANE))
    carry = jnp.zeros((e_n, 1), F32)
    c_in = jnp.zeros((e_n, LANE), F32)
    c_ex = jnp.zeros((e_n, LANE), F32)
    for c in range(n // blk):
        chunk, part = divmod(c, per_chunk)
        if part == 0:
            c_ex = jnp.where(lane == chunk, carry, c_ex)
            a_ref[chunk] = aff[:, chunk * ROUTE_CHUNK:(chunk + 1) * ROUTE_CHUNK]
        s_blk = sel[:, c * blk:(c + 1) * blk]
        rank = _dot(jnp.where(s_blk, 1.0, 0.0).astype(BF), upper) + carry
        q_ref[chunk, :, part * blk:(part + 1) * blk] = jnp.where(s_blk, rank, 0.0)
        carry = rank[:, blk - 1:blk]
        if part == per_chunk - 1:
            c_in = jnp.where(lane == chunk, carry, c_in)
    valid = lane < n_chunks
    lo = jnp.zeros((e_n, LANE), I32)
    hi = jnp.zeros((e_n, LANE), I32)
    for rt in range(cap // ROUTE_TILE):
        lo_rt = count(valid & (c_in < float(rt * ROUTE_TILE + 1)))
        hi_rt = count(valid & (c_ex < float((rt + 1) * ROUTE_TILE)))
        lo = jnp.where(lane == rt, lo_rt.astype(I32), lo)
        hi = jnp.where(lane == rt, hi_rt.astype(I32), hi)
    lo_ref[...] = lo
    hi_ref[...] = hi


def _compact_kernel(lo_ref, hi_ref, q_ref, a_ref, idx_ref, g_ref):
    e = pl.program_id(0)
    n_tiles = idx_ref.shape[0]
    eye = lax.broadcasted_iota(I32, (ROUTE_TILE, LANE), 0) == lax.broadcasted_iota(I32, (ROUTE_TILE, LANE), 1)

    def per_tile(rt, _):
        slot = (rt * ROUTE_TILE + 1 + lax.broadcasted_iota(I32, (ROUTE_TILE, 1), 0)).astype(F32)

        def per_chunk(c, acc):
            idx_acc, g_acc = acc
            q_row = q_ref[c, pl.ds(e, 1), :]
            a_row = a_ref[c, pl.ds(e, 1), :]
            for part in range(ROUTE_CHUNK // LANE):
                lanes = slice(part * LANE, (part + 1) * LANE)
                hit = q_row[:, lanes] == slot
                tok = (c * ROUTE_CHUNK + part * LANE + lax.broadcasted_iota(I32, (1, LANE), 1)).astype(F32)
                idx_acc = idx_acc + jnp.where(hit, tok, 0.0)
                g_acc = g_acc + jnp.where(hit, a_row[:, lanes], 0.0)
            return idx_acc, g_acc

        zero = jnp.zeros((ROUTE_TILE, LANE), F32)
        idx_acc, g_acc = lax.fori_loop(lo_ref[e, rt], hi_ref[e, rt], per_chunk, (zero, zero))
        idx_v = jnp.sum(idx_acc, axis=1, keepdims=True)
        g_v = jnp.sum(g_acc, axis=1, keepdims=True)
        idx_ref[pl.ds(rt, 1), :] = jnp.sum(jnp.where(eye, idx_v, 0.0), axis=0, keepdims=True).astype(I32)
        g_ref[pl.ds(rt, 1), :] = jnp.sum(jnp.where(eye, g_v, 0.0), axis=0, keepdims=True)
        return 0

    lax.fori_loop(0, n_tiles, per_tile, 0)


def _router_call(logits_t, cap):
    n = logits_t.shape[1]
    n_chunks = n // ROUTE_CHUNK
    n_tiles = cap // ROUTE_TILE
    chunked = jax.ShapeDtypeStruct((n_chunks, N_EXPERTS, ROUTE_CHUNK), F32)
    q, a, lo, hi, cnt = pl.pallas_call(
        functools.partial(_router_kernel, cap),
        out_shape=[chunked, chunked, jax.ShapeDtypeStruct((N_EXPERTS, LANE), I32),
                   jax.ShapeDtypeStruct((N_EXPERTS, LANE), I32), jax.ShapeDtypeStruct((N_EXPERTS, LANE), F32)],
        compiler_params=pltpu.CompilerParams(vmem_limit_bytes=40 * 1024 * 1024), name="router",
    )(logits_t)
    whole = pl.BlockSpec((n_chunks, N_EXPERTS, ROUTE_CHUNK), lambda e, *_: (0, 0, 0))
    slots = pl.BlockSpec((None, n_tiles, ROUTE_TILE), lambda e, *_: (e, 0, 0))
    idx, g = pl.pallas_call(
        _compact_kernel,
        grid_spec=pltpu.PrefetchScalarGridSpec(
            num_scalar_prefetch=2, grid=(N_EXPERTS,), in_specs=[whole, whole], out_specs=[slots, slots]),
        out_shape=[jax.ShapeDtypeStruct((N_EXPERTS, n_tiles, ROUTE_TILE), I32),
                   jax.ShapeDtypeStruct((N_EXPERTS, n_tiles, ROUTE_TILE), F32)],
        compiler_params=_cp(("parallel",), 32), name="compact",
    )(lo[:, :n_tiles], hi[:, :n_tiles], q, a)
    return idx.reshape(N_EXPERTS, cap), g.reshape(N_EXPERTS, cap), cnt


def _row_copy(src_hbm, xe, sem, buf, tok, slot):
    dst = xe.at[buf, pl.ds(pl.multiple_of(slot * SUB, SUB), SUB)]
    return pltpu.make_async_copy(src_hbm.at[tok], dst, sem.at[buf])


def _rows_wait(rows_hbm, xe, sem, buf, slot, n):
    dst = xe.at[buf, pl.ds(pl.multiple_of(slot * SUB, SUB), n * SUB)]
    return pltpu.make_async_copy(rows_hbm.at[pl.ds(0, n * SUB)], dst, sem.at[buf])


def _ffn_kernel(idx_ref, idxn_ref, hc_hbm, hl_hbm, rows_hbm, w1_ref, w3_ref, w2_ref, yc_ref, yl_ref,
                xe, xb, y_part, w1b, w3b, w2b, sem):
    e = pl.program_id(0)
    f = pl.program_id(1)
    buf = e % 2

    @pl.when((e == 0) & (f == 0))
    def _():
        def start_ctx(r, _):
            _row_copy(hc_hbm, xe, sem, 0, idx_ref[0, r], r).start()
            return 0

        def start_lat(r, _):
            _row_copy(hl_hbm, xe, sem, 0, idx_ref[0, r], r).start()
            return 0

        lax.fori_loop(0, CAP_CTX, start_ctx, 0)
        lax.fori_loop(CAP_CTX, SLOTS, start_lat, 0)
        _rows_wait(rows_hbm, xe, sem, 0, 0, SLOTS).wait()

    @pl.when(f == 0)
    def _():
        for k in range(D_CHUNKS):
            xb[:, k * LANE:(k + 1) * LANE] = xe[buf, pl.ds(k, SLOTS, stride=D_CHUNKS), :].astype(BF)

    w1b[...] = w1_ref[...].astype(BF)
    w3b[...] = w3_ref[...].astype(BF)
    w2b[...] = w2_ref[...].astype(BF)

    part_c, part_l = CAP_CTX // FF_SPLIT, CAP_LAT // FF_SPLIT
    next_rows = ([(hc_hbm, f * part_c + u) for u in range(part_c)]
                 + [(hl_hbm, CAP_CTX + f * part_l + u) for u in range(part_l)])

    def next_copy(src, r):
        return _row_copy(src, xe, sem, 1 - buf, idxn_ref[0, r], r)

    n_tiles = SLOTS // FFN_ROWS
    for t in range(n_tiles):
        if t < n_tiles - 1:
            for src, r in next_rows[t::n_tiles - 1]:
                next_copy(src, r).start()
        rows = slice(t * FFN_ROWS, (t + 1) * FFN_ROWS)
        x = xb[rows, :]
        a = _dot(x, w1b[...])
        hid = (a * jax.nn.sigmoid(a) * _dot(x, w3b[...])).astype(BF)
        y = _dot(hid, w2b[...])
        lo, hi = t * FFN_ROWS, (t + 1) * FFN_ROWS
        segs = []
        if lo < CAP_CTX:
            segs.append((yc_ref, lo, 0, min(hi, CAP_CTX) - lo))
        if hi > CAP_CTX:
            first = max(lo, CAP_CTX)
            segs.append((yl_ref, first - CAP_CTX, first - lo, hi - first))

        @pl.when(f == 0)
        def _():
            y_part[rows, :] = y

        if FF_SPLIT > 2:
            @pl.when((f > 0) & (f < FF_SPLIT - 1))
            def _():
                y_part[rows, :] = y_part[rows, :] + y

        @pl.when(f == FF_SPLIT - 1)
        def _():
            total = y + y_part[rows, :]
            for y_ref, row0, y0, n in segs:
                for k in range(D_CHUNKS):
                    y_ref[pl.ds(row0 * D_CHUNKS + k, n, stride=D_CHUNKS), :] = total[y0:y0 + n, k * LANE:(k + 1) * LANE]

    _rows_wait(rows_hbm, xe, sem, 1 - buf, f * part_c, part_c).wait()
    _rows_wait(rows_hbm, xe, sem, 1 - buf, CAP_CTX + f * part_l, part_l).wait()


def _ffn_call(layer, idx, h_ctx, h_lat, w1, w3, w2):
    ff = EXPERT_FF // FF_SPLIT
    wspec = pl.BlockSpec((None, None, D_MODEL, ff), lambda e, f: (layer, e, 0, f))
    last = N_EXPERTS - 1
    return pl.pallas_call(
        _ffn_kernel,
        grid=(N_EXPERTS, FF_SPLIT),
        in_specs=[pl.BlockSpec((None, 1, SLOTS), lambda e, f: (e, 0, 0), memory_space=pltpu.SMEM),
                  pl.BlockSpec((None, 1, SLOTS), lambda e, f: (jnp.minimum(e + 1, last), 0, 0),
                               memory_space=pltpu.SMEM),
                  pl.BlockSpec(memory_space=pl.ANY), pl.BlockSpec(memory_space=pl.ANY),
                  pl.BlockSpec(memory_space=pl.ANY), wspec, wspec, pl.BlockSpec((None, None, ff, D_MODEL), lambda e, f: (layer, e, f, 0))],
        out_specs=[pl.BlockSpec((None, CAP_CTX * D_CHUNKS, LANE), lambda e, f: (e, 0, 0)),
                   pl.BlockSpec((None, CAP_LAT * D_CHUNKS, LANE), lambda e, f: (e, 0, 0))],
        out_shape=[jax.ShapeDtypeStruct((N_EXPERTS, CAP_CTX * D_CHUNKS, LANE), F32),
                   jax.ShapeDtypeStruct((N_EXPERTS, CAP_LAT * D_CHUNKS, LANE), F32)],
        scratch_shapes=[pltpu.VMEM((2, SLOTS * D_CHUNKS, LANE), F32), pltpu.VMEM((SLOTS, D_MODEL), BF),
                        pltpu.VMEM((SLOTS, D_MODEL), F32),
                        pltpu.VMEM((D_MODEL, ff), BF), pltpu.VMEM((D_MODEL, ff), BF),
                        pltpu.VMEM((ff, D_MODEL), BF), pltpu.SemaphoreType.DMA((2,))],
        compiler_params=_cp(("arbitrary", "arbitrary"), 58), name="ffn",
    )(idx, idx, h_ctx, h_lat, h_lat.reshape(N_LAT * SUB, LANE), w1, w3, w2)


def _combine_kernel(idx_ref, g_ref, bnd_ref, yc_ref, yl_ref, acc_ref):
    h = pl.program_id(0)
    e = pl.program_id(1)

    @pl.when(e == 0)
    def _():
        acc_ref[...] = jnp.zeros(acc_ref.shape, F32)

    def add_rows(y_ref, slot0, lo, hi, base):
        def group(i, _):
            r0 = lo + i * COMBINE_GROUP
            toks = [idx_ref[0, r0 + u] - base for u in range(COMBINE_GROUP)]
            sums = [acc_ref[toks[u]] + y_ref[r0 - slot0 + u] * g_ref[0, r0 + u] for u in range(COMBINE_GROUP)]
            for u in range(COMBINE_GROUP):
                acc_ref[toks[u]] = sums[u]
            return 0

        def single(r, _):
            t = idx_ref[0, r] - base
            acc_ref[t] = acc_ref[t] + y_ref[r - slot0] * g_ref[0, r]
            return 0

        n_groups = (hi - lo) // COMBINE_GROUP
        lax.fori_loop(0, n_groups, group, 0)
        lax.fori_loop(lo + n_groups * COMBINE_GROUP, hi, single, 0)

    @pl.when(h == 0)
    def _():
        add_rows(yc_ref, 0, 0, CAP_CTX, 0)

    @pl.when(h > 0)
    def _():
        add_rows(yl_ref, CAP_CTX, bnd_ref[0, h], bnd_ref[0, h + 1], (h - 1) * TOK_HALF)


def _combine_call(idx, g, bnd, y_ctx, y_lat):
    last = N_EXPERTS - 1
    return pl.pallas_call(
        _combine_kernel,
        grid=(N_HALVES, N_EXPERTS),
        in_specs=[pl.BlockSpec((None, 1, SLOTS), lambda h, e: (e, 0, 0), memory_space=pltpu.SMEM),
                  pl.BlockSpec((None, 1, SLOTS), lambda h, e: (e, 0, 0), memory_space=pltpu.SMEM),
                  pl.BlockSpec((None, 1, N_HALVES + 1), lambda h, e: (e, 0, 0), memory_space=pltpu.SMEM),
                  pl.BlockSpec((None, CAP_CTX, SUB, LANE), lambda h, e: (jnp.where(h == 0, e, last), 0, 0, 0)),
                  pl.BlockSpec((None, CAP_LAT, SUB, LANE), lambda h, e: (jnp.where(h == 0, 0, e), 0, 0, 0))],
        out_specs=pl.BlockSpec((None, TOK_HALF, SUB, LANE), lambda h, e: (h, 0, 0, 0)),
        out_shape=jax.ShapeDtypeStruct((N_HALVES, TOK_HALF, SUB, LANE), F32),
        compiler_params=_cp(("parallel", "arbitrary"), 52), name="combine",
    )(idx, g, bnd, y_ctx.reshape(N_EXPERTS, CAP_CTX, SUB, LANE), y_lat.reshape(N_EXPERTS, CAP_LAT, SUB, LANE))


def _final_kernel(x_ref, moe_ref, mod_ref, g_ref, o_ref):
    x = x_ref[...] + mod_ref[...][5:6] * _moe_rows(moe_ref, TM_FINAL)
    o_ref[...] = _rms(x, g_ref[...])


def _final_call(x, moe, mod, g, latent):
    n = x.shape[0]
    tm = TM_FINAL
    moe_off = (N_CTX // tm) if latent else 0
    row = pl.BlockSpec((tm, D_MODEL), lambda i: (i, 0))
    return pl.pallas_call(
        _final_kernel, grid=(n // tm,),
        in_specs=[row, pl.BlockSpec((tm * D_CHUNKS, LANE), lambda i: (i + moe_off, 0)),
                  _mod_spec(DEPTH - 1, tm, latent), _layer(0, (1, D_MODEL))],
        out_specs=row, out_shape=jax.ShapeDtypeStruct((n, D_MODEL), F32),
        compiler_params=_cp(("parallel",), 32), name="final",
    )(x, moe, mod, g)


def _axial_tables(rot_dim):
    rows = DEC_SEQ // GRID_W
    row = jnp.repeat(jnp.arange(rows, dtype=F32), GRID_W)
    col = jnp.tile(jnp.arange(GRID_W, dtype=F32), rows)
    n_freq = rot_dim // 4
    inv = ROPE_BASE ** (-jnp.arange(n_freq, dtype=F32) / n_freq)
    ang = jnp.concatenate([row[:, None] * inv, col[:, None] * inv], axis=-1)
    sign = jnp.tile(jnp.array([-1.0, 1.0], F32), rot_dim // 2)
    return jnp.repeat(jnp.cos(ang), 2, axis=1), jnp.repeat(jnp.sin(ang), 2, axis=1) * sign


def _rope_tables():
    c64, s64 = _axial_tables(DA_DH)
    c32, s32 = _axial_tables(MLA_ROPE)
    one = lambda w: jnp.ones((DEC_SEQ, w), F32)
    zero = lambda w: jnp.zeros((DEC_SEQ, w), F32)
    pad = LANE - MLA_NOPE - MLA_ROPE
    return jnp.stack([
        jnp.tile(c64, (1, 2)), jnp.tile(s64, (1, 2)),
        jnp.concatenate([one(MLA_NOPE), c32, one(pad)], axis=1),
        jnp.concatenate([zero(MLA_NOPE), s32, zero(pad)], axis=1),
        jnp.concatenate([c32, one(LANE - MLA_ROPE)], axis=1),
        jnp.concatenate([s32, zero(LANE - MLA_ROPE)], axis=1)])


def _prep_weights(w_in, w_gate, mla_w_qb, mla_w_kvb, w_br_da, w_br_mla, w_br_swa, w_o, w_router):
    kr_end = C_MKR + MLA_ROPE
    win = jnp.concatenate([w_in[:, :, :kr_end].astype(BF), jnp.zeros((DEPTH, D_MODEL, LANE - MLA_ROPE), BF),
                           w_in[:, :, kr_end:].astype(BF)], axis=2)
    dk = MLA_NOPE + MLA_ROPE
    wqb = jnp.pad(mla_w_qb.reshape(DEPTH, MLA_Q_RANK, MLA_HEADS, dk), ((0, 0), (0, 0), (0, 0), (0, LANE - dk)))
    wqb = wqb.reshape(DEPTH, MLA_Q_RANK, MLA_HEADS * LANE).astype(BF)
    kvb = mla_w_kvb.reshape(DEPTH, MLA_KV_RANK, MLA_HEADS, MLA_NOPE + MLA_V)
    wk = jnp.pad(kvb[..., :MLA_NOPE], ((0, 0), (0, 0), (0, 0), (0, LANE - MLA_NOPE)))
    wk = wk.reshape(DEPTH, MLA_KV_RANK, MLA_HEADS * LANE)
    wv = kvb[..., MLA_NOPE:].reshape(DEPTH, MLA_KV_RANK, MLA_OUT)
    wkv = jnp.concatenate([wk, wv], axis=2).astype(BF)
    wrh, wrl = _split(jnp.swapaxes(w_router, 1, 2))
    return dict(win=win, wg=w_gate.astype(BF), wqb=wqb, wkv=wkv, wda=w_br_da.astype(BF),
                wmla=w_br_mla.astype(BF), wswa=w_br_swa.astype(BF), wo=w_o.astype(BF), wrh=wrh, wrl=wrl)


def kernel(x_prompt, x_sample, cache_da_k, cache_da_v, cache_mla_ckv, cache_mla_krope, cache_swa_k, cache_swa_v, c, c_ctx, w_ada, b_ada, norm1, norm2, w_in, da_lq1, da_lk1, da_lq2, da_lk2, da_subln, mla_q_norm, mla_w_qb, mla_kv_norm, mla_w_kvb, swa_sink, w_gate, w_br_da, w_br_mla, w_br_swa, w_o, w_router, w_ff1, w_ff3, w_ff2, final_norm):
    c_all = jnp.concatenate([c_ctx[None], c, jnp.zeros((SUB - 1 - DEC_BATCH, D_MODEL), F32)], axis=0)
    mod = _ada_call(c_all, w_ada, b_ada)[:, :1 + DEC_BATCH].reshape(DEPTH, 1 + DEC_BATCH, 6, D_MODEL)
    lam_all = _lam_call(da_lq1, da_lk1, da_lq2, da_lk2)
    tab = _rope_tables()
    swa_bias = _swa_bias()
    w = _prep_weights(w_in, w_gate, mla_w_qb, mla_w_kvb, w_br_da, w_br_mla, w_br_swa, w_o, w_router)
    n1, n2 = norm1[:, None], norm2[:, None]
    subln, qn, kvn = da_subln[:, None], mla_q_norm[:, None], mla_kv_norm[:, None]
    caches = (cache_da_k.reshape(DEC_BATCH, DEPTH, PAST_LEN, 512),
              cache_da_v.reshape(DEC_BATCH, DEPTH, PAST_LEN, 512),
              cache_mla_ckv,
              jnp.pad(cache_mla_krope, ((0, 0), (0, 0), (0, 0), (0, LANE - MLA_ROPE))),
              cache_swa_k.reshape(DEC_BATCH, DEPTH, PAST_LEN, LANE),
              cache_swa_v.reshape(DEC_BATCH, DEPTH, PAST_LEN, LANE))

    xp = x_prompt.reshape(N_CTX, D_MODEL)
    xs = x_sample.reshape(N_LAT, D_MODEL)
    moe = None
    new = [[] for _ in range(6)]
    for l in range(DEPTH):
        lam = lam_all[l, :1]
        sink = swa_sink[l]

        xp, proj_c, gate_c = _pre_call(l, xp, moe, mod, n1, w["win"], w["wg"], latent=False)
        xs, proj_l, gate_l = _pre_call(l, xs, moe, mod, n1, w["win"], w["wg"], latent=True)

        oda_c, omla_c, oswa_c, ckv_c, k_c, v_c, kr_c, sk_c, sv_c = _ctx_attn_call(
            l, lam, sink, proj_c, subln, qn, w["wqb"], kvn, w["wkv"])
        new[0].append(k_c.reshape(BATCH, SEQ, DA_HEADS, 2 * DA_DH))
        new[1].append(v_c.reshape(BATCH, SEQ, DA_HEADS, 2 * DA_DH))
        new[2].append(ckv_c.reshape(BATCH, SEQ, MLA_KV_RANK))
        new[3].append(kr_c.reshape(BATCH, SEQ, MLA_ROPE))
        new[4].append(sk_c.reshape(BATCH, SEQ, SWA_KV_HEADS, SWA_DH))
        new[5].append(sv_c.reshape(BATCH, SEQ, SWA_KV_HEADS, SWA_DH))

        daq, mlaq, swaq, dak, dav, mlak, mlav, swak, swav = _lat_prep_call(
            l, proj_l, tab, caches, qn, w["wqb"], kvn, w["wkv"])
        oda_l = _lat_da_call(l, lam, daq, dak, dav, subln).reshape(N_LAT, 512)
        omla_l = _lat_mla_call(mlaq, mlak, mlav).reshape(N_LAT, 512)
        oswa_l = _lat_swa_call(sink, swa_bias, swaq, swak, swav).reshape(N_LAT, 512)

        post_w = (w["wda"], w["wmla"], w["wswa"], w["wo"], w["wrh"], w["wrl"])
        xp, h2_c, lg_c = _post_call(l, xp, oda_c, omla_c, oswa_c, gate_c, mod, n2, *post_w, latent=False)
        xs, h2_l, lg_l = _post_call(l, xs, oda_l, omla_l, oswa_l, gate_l, mod, n2, *post_w, latent=True)

        idx_c, g_c, _ = _router_call(lg_c, CAP_CTX)
        idx_l, g_l, cnt_l = _router_call(lg_l, CAP_LAT)
        idx = jnp.concatenate([idx_c, idx_l], axis=1).reshape(N_EXPERTS, 1, SLOTS)
        gsel = jnp.concatenate([g_c, g_l], axis=1).reshape(N_EXPERTS, 1, SLOTS)
        n0 = cnt_l[:, 0].astype(I32)
        bnd = jnp.stack([jnp.zeros_like(n0), jnp.full_like(n0, CAP_CTX), CAP_CTX + n0,
                         jnp.full_like(n0, SLOTS)], axis=1)
        y_c, y_l = _ffn_call(l, idx, h2_c.reshape(N_CTX, SUB, LANE), h2_l.reshape(N_LAT, SUB, LANE),
                             w_ff1, w_ff3, w_ff2)
        acc = _combine_call(idx, gsel, bnd.reshape(N_EXPERTS, 1, N_HALVES + 1), y_c, y_l)
        moe = acc.reshape(N_HALVES * TOK_HALF * D_CHUNKS, LANE)

    fn = final_norm[None, None]
    y_prompt = _final_call(xp, moe, mod, fn, latent=False).reshape(BATCH, SEQ, D_MODEL)
    y_sample = _final_call(xs, moe, mod, fn, latent=True).reshape(DEC_BATCH, DEC_SEQ, D_MODEL)
    return (y_prompt, y_sample) + tuple(jnp.stack(n, axis=1) for n in new)
```

```python
import functools
import math

import jax
import jax.numpy as jnp
from jax import lax
from jax.experimental import pallas as pl
from jax.experimental.pallas import tpu as pltpu

F32 = jnp.float32
BF = jnp.bfloat16
I32 = jnp.int32

D_MODEL = 1024
BATCH = 16
SEQ = 256
DEPTH = 2
DEC_BATCH = 2
DEC_SEQ = 4096
PAST_LEN = 512
GRID_W = 64
ROPE_BASE = 10000.0
EPS = 1e-6
DA_HEADS = 4
DA_DH = 64
DA_OUT = DA_HEADS * 2 * DA_DH
MLA_HEADS = 8
MLA_Q_RANK = 256
MLA_KV_RANK = 128
MLA_NOPE = 64
MLA_ROPE = 32
MLA_V = 64
MLA_OUT = MLA_HEADS * MLA_V
SWA_HEADS = 8
SWA_KV_HEADS = 2
SWA_DH = 64
W_BLOCK = 128
SWA_OUT = SWA_HEADS * SWA_DH
N_EXPERTS = 16
EXPERT_FF = 1024
CAPACITY_FACTOR = 2

LANE = 128
SUB = 8
HALF = 64
N_CTX = BATCH * SEQ
N_LAT = DEC_BATCH * DEC_SEQ
KEYS = PAST_LEN + DEC_SEQ
CAP_CTX = CAPACITY_FACTOR * N_CTX // N_EXPERTS
CAP_LAT = CAPACITY_FACTOR * N_LAT // N_EXPERTS
SLOTS = CAP_CTX + CAP_LAT
TOK_HALF = 4096
N_HALVES = (N_CTX + N_LAT) // TOK_HALF
D_CHUNKS = D_MODEL // LANE

C_DAQ, C_DAK, C_DAV = 0, 512, 1024
C_MQ, C_MKV, C_MKR = 1536, 1792, 1920
C_SQ, C_SK, C_SV = 2048, 2560, 2688
PROJ_COLS = 2816
GATE_COLS = 3 * D_MODEL
MLA_SCALE = (MLA_NOPE + MLA_ROPE) ** -0.5
HEAD_SCALE = DA_DH ** -0.5
LOG2E = math.log2(math.e)

TM_PRE = 256
TM_FINAL = 512
TM_POST = 512
CTX_STEP_ROWS = 2
CHUNK = 512
TQ = 256
DA_STEP_HEADS = 4
MLA_STEP_PAIRS = 4
ROUTE_TILE = 128
ROUTE_CHUNK = 256
THRESHOLD_STEPS = 48
FFN_ROWS = 512
FF_SPLIT = 2
COMBINE_GROUP = 8


def _cp(sem, vmem_mb):
    return pltpu.CompilerParams(dimension_semantics=sem, vmem_limit_bytes=vmem_mb * 1024 * 1024)


def _dot(a, b):
    return jnp.dot(a, b, preferred_element_type=F32)


def _dot_nt(a, b):
    return lax.dot_general(a, b, (((1,), (1,)), ((), ())), preferred_element_type=F32)


def _split(a):
    hi = a.astype(BF)
    lo = (a - hi.astype(F32)).astype(BF)
    return hi, lo


def _dot3(a, w):
    ah, al = _split(a)
    wh, wl = _split(w)
    return _dot(ah, wh) + _dot(ah, wl) + _dot(al, wh)


def _rms(x, g):
    return x * lax.rsqrt(jnp.mean(x * x, axis=-1, keepdims=True) + EPS) * g


def _lane(shape):
    return lax.broadcasted_iota(I32, shape, len(shape) - 1)


def _softmax_pv(segs, sink=None):
    m = None
    for s, _ in segs:
        ms = jnp.max(s, axis=-1, keepdims=True)
        m = ms if m is None else jnp.maximum(m, ms)
    if sink is not None:
        m = jnp.maximum(m, sink)
    l = None
    o = None
    for s, v in segs:
        e = jnp.exp(s - m)
        ls = jnp.sum(e, axis=-1, keepdims=True)
        os_ = _dot(e.astype(BF), v)
        l = ls if l is None else l + ls
        o = os_ if o is None else o + os_
    if sink is not None:
        l = l + jnp.exp(sink - m)
    return o / l


def _mask_halves(q):
    lo = _lane(q.shape) < HALF
    return jnp.where(lo, q, 0.0).astype(BF), jnp.where(lo, 0.0, q).astype(BF)


def _pair(o_a, o_b):
    return jnp.where(_lane(o_a.shape) < HALF, o_a, o_b)


def _dup(x):
    r = pltpu.roll(x, HALF, 1)
    lo = _lane(x.shape) < HALF
    return jnp.where(lo, x, r), jnp.where(lo, r, x)


def _rope(x, c, s):
    n = x.shape[-1]
    even = (_lane(x.shape) % 2) == 0
    sw = jnp.where(even, pltpu.roll(x, n - 1, 1), pltpu.roll(x, 1, 1))
    return x * c + sw * s


def _da_head(q, k, v, lam, subln, lam_init):
    q1, q2 = _mask_halves(q)
    o1 = _softmax_pv([(_dot_nt(q1, k), v)])
    o2 = _softmax_pv([(_dot_nt(q2, k), v)])
    return _rms(o1 - lam * o2, subln) * (1.0 - lam_init)


def _ada_kernel(c_ref, w_ref, b_ref, o_ref):
    c = c_ref[...]
    a = c * jax.nn.sigmoid(c)
    o_ref[...] = _dot3(a, w_ref[...]) + b_ref[...]


def _ada_call(c_all, w_ada, b_ada):
    tn = 1536
    return pl.pallas_call(
        _ada_kernel,
        grid=(DEPTH, 6 * D_MODEL // tn),
        in_specs=[pl.BlockSpec((SUB, D_MODEL), lambda l, j: (0, 0)),
                  pl.BlockSpec((None, D_MODEL, tn), lambda l, j: (l, 0, j)),
                  pl.BlockSpec((None, 1, tn), lambda l, j: (l, 0, j))],
        out_specs=pl.BlockSpec((None, SUB, tn), lambda l, j: (l, 0, j)),
        out_shape=jax.ShapeDtypeStruct((DEPTH, SUB, 6 * D_MODEL), F32),
        compiler_params=_cp(("parallel", "parallel"), 40),
        name="ada",
    )(c_all, w_ada, b_ada.reshape(DEPTH, 1, 6 * D_MODEL))


def _lam_kernel(q1, k1, q2, k2, o_ref):
    s1 = jnp.sum(q1[...] * k1[...], axis=-1, keepdims=True)
    s2 = jnp.sum(q2[...] * k2[...], axis=-1, keepdims=True)
    row = lax.broadcasted_iota(I32, (DEPTH, 1), 0)
    init = jnp.zeros((DEPTH, 1), F32)
    for l in range(DEPTH):
        init = jnp.where(row == l, _lam_init(l), init)
    o_ref[...] = jnp.broadcast_to(jnp.exp(s1) - jnp.exp(s2) + init, o_ref.shape)


def _lam_init(layer):
    return 0.8 - 0.6 * math.exp(-0.3 * layer)


def _lam_call(q1, k1, q2, k2):
    return pl.pallas_call(
        _lam_kernel,
        out_shape=jax.ShapeDtypeStruct((DEPTH, LANE), F32),
        name="lam",
    )(q1, k1, q2, k2)


def _moe_rows(moe_ref, tm):
    return jnp.concatenate([moe_ref[pl.ds(k, tm, stride=D_CHUNKS), :] for k in range(D_CHUNKS)], axis=1)


def _pre_body(x, mod_ref, n1_ref, win_ref, wg_ref, proj_ref, gate_ref):
    m = mod_ref[...]
    h = _rms(x, n1_ref[...]) * (1.0 + m[1:2]) + m[0:1]
    hb = h.astype(BF)
    proj_ref[...] = _dot(hb, win_ref[...]).astype(proj_ref.dtype)
    gate_ref[...] = jax.nn.sigmoid(_dot(hb, wg_ref[...])).astype(BF)


def _pre_first_kernel(x_ref, mod_ref, n1_ref, win_ref, wg_ref, proj_ref, gate_ref):
    _pre_body(x_ref[...], mod_ref, n1_ref, win_ref, wg_ref, proj_ref, gate_ref)


def _pre_next_kernel(x_ref, moe_ref, modp_ref, mod_ref, n1_ref, win_ref, wg_ref, xo_ref, proj_ref, gate_ref):
    x = x_ref[...] + modp_ref[...][5:6] * _moe_rows(moe_ref, TM_PRE)
    xo_ref[...] = x
    _pre_body(x, mod_ref, n1_ref, win_ref, wg_ref, proj_ref, gate_ref)


def _mod_spec(layer, tm, latent):
    per = DEC_SEQ // tm
    if latent:
        return pl.BlockSpec((None, None, 6, D_MODEL), lambda i: (layer, 1 + i // per, 0, 0))
    return pl.BlockSpec((None, None, 6, D_MODEL), lambda i: (layer, 0, 0, 0))


def _layer(layer, shape):
    nd = len(shape)
    return pl.BlockSpec((None,) + shape, lambda *_: (layer,) + (0,) * nd, pipeline_mode=pl.Buffered(1))


def _pre_call(layer, x, moe, mod, n1, win, wg, latent):
    n = x.shape[0]
    tm = TM_PRE
    row = pl.BlockSpec((tm, D_MODEL), lambda i: (i, 0))
    w_specs = [_mod_spec(layer, tm, latent), _layer(layer, (1, D_MODEL)), _layer(layer, (D_MODEL, PROJ_COLS)),
               _layer(layer, (D_MODEL, GATE_COLS))]
    outs = [jax.ShapeDtypeStruct((n, PROJ_COLS), BF if latent else F32), jax.ShapeDtypeStruct((n, GATE_COLS), BF)]
    out_specs = [pl.BlockSpec((tm, PROJ_COLS), lambda i: (i, 0)), pl.BlockSpec((tm, GATE_COLS), lambda i: (i, 0))]
    if moe is None:
        proj, gate = pl.pallas_call(
            _pre_first_kernel, grid=(n // tm,), in_specs=[row] + w_specs, out_specs=out_specs, out_shape=outs,
            compiler_params=_cp(("parallel",), 52), name="pre_first",
        )(x, mod, n1, win, wg)
        return x, proj, gate
    moe_off = (N_CTX // tm) if latent else 0
    moe_spec = pl.BlockSpec((tm * D_CHUNKS, LANE), lambda i: (i + moe_off, 0))
    xo, proj, gate = pl.pallas_call(
        _pre_next_kernel, grid=(n // tm,),
        in_specs=[row, moe_spec, _mod_spec(layer - 1, tm, latent)] + w_specs,
        out_specs=[row] + out_specs,
        out_shape=[jax.ShapeDtypeStruct((n, D_MODEL), F32)] + outs,
        compiler_params=_cp(("parallel",), 52), name="pre_next",
    )(x, moe, mod, mod, n1, win, wg)
    return xo, proj, gate


def _ctx_attn_kernel(lam_init, lam_ref, sink_ref, p_ref, subln_ref, qn_ref, wqb_ref, kvn_ref, wkv_ref,
                     oda_ref, omla_ref, oswa_ref, ckv_ref, kc_ref, vc_ref, kr_ref, sk_ref, sv_ref):
    lam = lam_ref[0]
    for b in range(CTX_STEP_ROWS):
        r = slice(b * SEQ, (b + 1) * SEQ)
        p = p_ref[r, :]
        for h in range(DA_HEADS):
            dst = pl.ds(b * SEQ * DA_HEADS + h, SEQ, stride=DA_HEADS)
            kc_ref[dst, :] = p[:, C_DAK + h * LANE:C_DAK + (h + 1) * LANE]
            vc_ref[dst, :] = p[:, C_DAV + h * LANE:C_DAV + (h + 1) * LANE]
        kr_ref[r, :] = p[:, C_MKR:C_MKR + MLA_ROPE]
        sk_ref[r, :] = p[:, C_SK:C_SK + LANE]
        sv_ref[r, :] = p[:, C_SV:C_SV + LANE]
        for h in range(DA_HEADS):
            blk = slice(h * LANE, (h + 1) * LANE)
            q = p[:, C_DAQ + h * LANE:C_DAQ + (h + 1) * LANE] * HEAD_SCALE
            k = p[:, C_DAK + h * LANE:C_DAK + (h + 1) * LANE].astype(BF)
            v = p[:, C_DAV + h * LANE:C_DAV + (h + 1) * LANE].astype(BF)
            oda_ref[r, blk] = _da_head(q, k, v, lam, subln_ref[...], lam_init).astype(BF)
        qn = _rms(p[:, C_MQ:C_MQ + MLA_Q_RANK], qn_ref[...]).astype(BF)
        cq = _dot(qn, wqb_ref[...])
        ckv = _rms(p[:, C_MKV:C_MKV + MLA_KV_RANK], kvn_ref[...])
        ckv_ref[r, :] = ckv
        kv = _dot(ckv.astype(BF), wkv_ref[...])
        kr_sh = pltpu.roll(p[:, C_MKR:C_MKR + LANE], HALF, 1)
        for m in range(MLA_HEADS // 2):
            vpair = kv[:, MLA_HEADS * LANE + m * LANE:MLA_HEADS * LANE + (m + 1) * LANE].astype(BF)
            outs = []
            for h in (2 * m, 2 * m + 1):
                q = (cq[:, h * LANE:(h + 1) * LANE] * MLA_SCALE).astype(BF)
                k = (kv[:, h * LANE:(h + 1) * LANE] + kr_sh).astype(BF)
                outs.append(_softmax_pv([(_dot_nt(q, k), vpair)]))
            omla_ref[r, m * LANE:(m + 1) * LANE] = _pair(outs[0], outs[1]).astype(BF)
        kd = _dup(p[:, C_SK:C_SK + LANE])
        vd = _dup(p[:, C_SV:C_SV + LANE])
        for m in range(SWA_HEADS // 2):
            g = (2 * m) // (SWA_HEADS // SWA_KV_HEADS)
            k = kd[g].astype(BF)
            v = vd[g].astype(BF)
            qa, qb = _mask_halves(p[:, C_SQ + m * LANE:C_SQ + (m + 1) * LANE] * HEAD_SCALE)
            oa = _softmax_pv([(_dot_nt(qa, k), v)], sink_ref[2 * m])
            ob = _softmax_pv([(_dot_nt(qb, k), v)], sink_ref[2 * m + 1])
            oswa_ref[r, m * LANE:(m + 1) * LANE] = _pair(oa, ob).astype(BF)


def _smem():
    return pl.BlockSpec(memory_space=pltpu.SMEM)


def _ctx_attn_call(layer, lam, sink, proj, subln, qn, wqb, kvn, wkv):
    rows = CTX_STEP_ROWS * SEQ
    row512 = pl.BlockSpec((rows, 512), lambda b: (b, 0))
    row128 = pl.BlockSpec((rows, LANE), lambda b: (b, 0))
    heads = pl.BlockSpec((rows * DA_HEADS, LANE), lambda b: (b, 0))
    return pl.pallas_call(
        functools.partial(_ctx_attn_kernel, _lam_init(layer)),
        grid=(BATCH // CTX_STEP_ROWS,),
        in_specs=[_smem(), _smem(), pl.BlockSpec((rows, PROJ_COLS), lambda b: (b, 0)),
                  _layer(layer, (1, LANE)), _layer(layer, (1, MLA_Q_RANK)),
                  _layer(layer, (MLA_Q_RANK, MLA_HEADS * LANE)), _layer(layer, (1, MLA_KV_RANK)),
                  _layer(layer, (MLA_KV_RANK, MLA_HEADS * LANE + MLA_OUT))],
        out_specs=[row512, row512, row512, row128, heads, heads,
                   pl.BlockSpec((rows, MLA_ROPE), lambda b: (b, 0)), row128, row128],
        out_shape=[jax.ShapeDtypeStruct((N_CTX, 512), BF)] * 3
        + [jax.ShapeDtypeStruct((N_CTX, LANE), F32)]
        + [jax.ShapeDtypeStruct((N_CTX * DA_HEADS, LANE), F32)] * 2
        + [jax.ShapeDtypeStruct((N_CTX, MLA_ROPE), F32)] + [jax.ShapeDtypeStruct((N_CTX, LANE), F32)] * 2,
        compiler_params=_cp(("parallel",), 40), name="ctx_attn",
    )(lam, sink, proj, subln, qn, wqb, kvn, wkv)


def _lat_prep_kernel(p_ref, tab_ref, cdk_ref, cdv_ref, cckv_ref, ckr_ref, csk_ref, csv_ref,
                     qn_ref, wqb_ref, kvn_ref, wkv_ref,
                     daq_ref, mlaq_ref, swaq_ref, dak_ref, dav_ref, mlak_ref, mlav_ref, swak_ref, swav_ref):
    j = pl.program_id(1)

    def write_kv(dk, dv, ckv, kr_sh, sk, sv):
        ones_col = jnp.where(_lane((CHUNK, LANE)) == 0, 1.0, 0.0).astype(BF)
        dak_ref[...] = dk.astype(BF)
        for h in range(DA_HEADS):
            dav_ref[:, 2 * h * LANE:(2 * h + 1) * LANE] = dv[:, h * LANE:(h + 1) * LANE].astype(BF)
            dav_ref[:, (2 * h + 1) * LANE:(2 * h + 2) * LANE] = ones_col
        kv = _dot(ckv.astype(BF), wkv_ref[...])
        for h in range(MLA_HEADS):
            mlak_ref[:, h * LANE:(h + 1) * LANE] = (kv[:, h * LANE:(h + 1) * LANE] + kr_sh).astype(BF)
        for p in range(MLA_HEADS // 2):
            mlav_ref[:, 2 * p * LANE:(2 * p + 1) * LANE] = kv[:, (MLA_HEADS + p) * LANE:(MLA_HEADS + p + 1) * LANE].astype(BF)
            mlav_ref[:, (2 * p + 1) * LANE:(2 * p + 2) * LANE] = ones_col
        k0, k1 = _dup(sk)
        v0, v1 = _dup(sv)
        swak_ref[:, 0:LANE] = k0.astype(BF)
        swak_ref[:, LANE:2 * LANE] = k1.astype(BF)
        swav_ref[:, 0:LANE] = v0.astype(BF)
        swav_ref[:, LANE:2 * LANE] = ones_col
        swav_ref[:, 2 * LANE:3 * LANE] = v1.astype(BF)
        swav_ref[:, 3 * LANE:4 * LANE] = ones_col

    @pl.when(j == 0)
    def _():
        write_kv(cdk_ref[...], cdv_ref[...], cckv_ref[...], pltpu.roll(ckr_ref[...], HALF, 1),
                 csk_ref[...], csv_ref[...])

    @pl.when(j > 0)
    def _():
        c64, s64 = tab_ref[0], tab_ref[1]
        cmq, smq = tab_ref[2], tab_ref[3]
        ckr, skr = tab_ref[4], tab_ref[5]

        def cols(c0, width):
            return p_ref[:, c0:c0 + width].astype(F32)

        def blk(c0, h):
            return cols(c0 + h * LANE, LANE)

        for h in range(DA_HEADS):
            daq_ref[:, h * LANE:(h + 1) * LANE] = (_rope(blk(C_DAQ, h), c64, s64) * (HEAD_SCALE * LOG2E)).astype(BF)
            swaq_ref[:, h * LANE:(h + 1) * LANE] = (_rope(blk(C_SQ, h), c64, s64) * (HEAD_SCALE * LOG2E)).astype(BF)
        dk = jnp.concatenate([_rope(blk(C_DAK, h), c64, s64) for h in range(DA_HEADS)], axis=1)
        qn = _rms(cols(C_MQ, MLA_Q_RANK), qn_ref[...]).astype(BF)
        cq = _dot(qn, wqb_ref[...])
        for h in range(MLA_HEADS):
            mlaq_ref[:, h * LANE:(h + 1) * LANE] = (
                _rope(cq[:, h * LANE:(h + 1) * LANE], cmq, smq) * (MLA_SCALE * LOG2E)).astype(BF)
        ckv = _rms(cols(C_MKV, MLA_KV_RANK), kvn_ref[...])
        kr_sh = pltpu.roll(_rope(cols(C_MKR, LANE), ckr, skr), HALF, 1)
        sk = _rope(cols(C_SK, LANE), c64, s64)
        write_kv(dk, cols(C_DAV, 512), ckv, kr_sh, sk, cols(C_SV, LANE))


def _lat_prep_call(layer, proj, tab, caches, qn, wqb, kvn, wkv):
    cdk, cdv, cckv, ckr, csk, csv = caches
    nj = 1 + DEC_SEQ // CHUNK

    def own(width):
        return pl.BlockSpec((None, CHUNK, width), lambda b, j: (b, jnp.maximum(j - 1, 0), 0))

    def cache(width):
        return pl.BlockSpec((None, None, PAST_LEN, width), lambda b, j: (b, layer, 0, 0))

    def allk(width):
        return pl.BlockSpec((None, CHUNK, width), lambda b, j: (b, j, 0))

    def shp(rows, width):
        return jax.ShapeDtypeStruct((DEC_BATCH, rows, width), BF)

    return pl.pallas_call(
        _lat_prep_kernel,
        grid=(DEC_BATCH, nj),
        in_specs=[own(PROJ_COLS),
                  pl.BlockSpec((6, CHUNK, LANE), lambda b, j: (0, jnp.maximum(j - 1, 0), 0)),
                  cache(512), cache(512), cache(LANE), cache(LANE), cache(LANE), cache(LANE),
                  _layer(layer, (1, MLA_Q_RANK)), _layer(layer, (MLA_Q_RANK, MLA_HEADS * LANE)),
                  _layer(layer, (1, MLA_KV_RANK)), _layer(layer, (MLA_KV_RANK, MLA_HEADS * LANE + MLA_OUT))],
        out_specs=[own(512), own(MLA_HEADS * LANE), own(512),
                   allk(512), allk(1024), allk(MLA_HEADS * LANE), allk(1024), allk(2 * LANE), allk(4 * LANE)],
        out_shape=[shp(DEC_SEQ, 512), shp(DEC_SEQ, MLA_HEADS * LANE), shp(DEC_SEQ, 512),
                   shp(KEYS, 512), shp(KEYS, 1024), shp(KEYS, MLA_HEADS * LANE), shp(KEYS, 1024),
                   shp(KEYS, 2 * LANE), shp(KEYS, 4 * LANE)],
        compiler_params=_cp(("parallel", "arbitrary"), 48), name="lat_prep",
    )(proj.reshape(DEC_BATCH, DEC_SEQ, PROJ_COLS), tab, cdk, cdv, cckv, ckr, csk, csv, qn, wqb, kvn, wkv)


def _softmax2_pv(s, v_aug):
    e = jnp.exp2(s - jnp.max(s, axis=-1, keepdims=True)).astype(BF)
    o = _dot(e, v_aug)
    return o[:, 0:LANE] / o[:, LANE:LANE + 1]


def _lat_da_kernel(lam_init, lam_ref, q_ref, k_ref, v_ref, subln_ref, o_ref):
    for h in range(DA_STEP_HEADS):
        cols = slice(h * LANE, (h + 1) * LANE)
        q1, q2 = _mask_halves(q_ref[:, cols].astype(F32))
        k = k_ref[:, cols]
        v = v_ref[:, 2 * h * LANE:(2 * h + 2) * LANE]
        o = _softmax2_pv(_dot_nt(q1, k), v) - lam_ref[0] * _softmax2_pv(_dot_nt(q2, k), v)
        o_ref[:, cols] = (_rms(o, subln_ref[...]) * (1.0 - lam_init)).astype(BF)


def _lat_da_call(layer, lam, q, k, v, subln):
    return pl.pallas_call(
        functools.partial(_lat_da_kernel, _lam_init(layer)),
        grid=(DEC_BATCH, DA_HEADS // DA_STEP_HEADS, DEC_SEQ // TQ),
        in_specs=[_smem(),
                  pl.BlockSpec((None, TQ, DA_STEP_HEADS * LANE), lambda b, h, i: (b, i, h)),
                  pl.BlockSpec((None, KEYS, DA_STEP_HEADS * LANE), lambda b, h, i: (b, 0, h)),
                  pl.BlockSpec((None, KEYS, 2 * DA_STEP_HEADS * LANE), lambda b, h, i: (b, 0, h)),
                  _layer(layer, (1, LANE))],
        out_specs=pl.BlockSpec((None, TQ, DA_STEP_HEADS * LANE), lambda b, h, i: (b, i, h)),
        out_shape=jax.ShapeDtypeStruct((DEC_BATCH, DEC_SEQ, 512), BF),
        compiler_params=_cp(("parallel", "parallel", "arbitrary"), 56), name="lat_da",
    )(lam, q, k, v, subln)


def _lat_mla_kernel(q_ref, k_ref, v_ref, o_ref):
    for p in range(MLA_STEP_PAIRS):
        v = v_ref[:, 2 * p * LANE:(2 * p + 2) * LANE]
        outs = []
        for h in (2 * p, 2 * p + 1):
            cols = slice(h * LANE, (h + 1) * LANE)
            outs.append(_softmax2_pv(_dot_nt(q_ref[:, cols], k_ref[:, cols]), v))
        o_ref[:, p * LANE:(p + 1) * LANE] = _pair(outs[0], outs[1]).astype(BF)


def _lat_mla_call(q, k, v):
    sp = MLA_STEP_PAIRS
    return pl.pallas_call(
        _lat_mla_kernel,
        grid=(DEC_BATCH, MLA_HEADS // (2 * sp), DEC_SEQ // TQ),
        in_specs=[pl.BlockSpec((None, TQ, 2 * sp * LANE), lambda b, m, i: (b, i, m)),
                  pl.BlockSpec((None, KEYS, 2 * sp * LANE), lambda b, m, i: (b, 0, m)),
                  pl.BlockSpec((None, KEYS, 2 * sp * LANE), lambda b, m, i: (b, 0, m))],
        out_specs=pl.BlockSpec((None, TQ, sp * LANE), lambda b, m, i: (b, i, m)),
        out_shape=jax.ShapeDtypeStruct((DEC_BATCH, DEC_SEQ, 512), BF),
        compiler_params=_cp(("parallel", "parallel", "arbitrary"), 56), name="lat_mla",
    )(q, k, v)


def _swa_bias():
    rows = (SWA_HEADS // SWA_KV_HEADS) * W_BLOCK
    qi = (jnp.arange(rows) % W_BLOCK)[:, None]
    loc = (jnp.arange(PAST_LEN + 3 * W_BLOCK) - PAST_LEN)[None, :]
    always = (loc < 0) | ((loc >= W_BLOCK) & (loc < 2 * W_BLOCK))
    prev = (loc >= 0) & (loc < W_BLOCK) & (loc >= qi)
    nxt = (loc >= 2 * W_BLOCK) & (loc - 2 * W_BLOCK <= qi)
    ok = jnp.stack([always | nxt, always | prev | nxt, always | prev])
    return jnp.where(ok, 0.0, -jnp.inf).astype(F32)


def _lat_swa_kernel(sink_ref, bias_a_ref, bias_b_ref, q_ref, kc_ref, kp_ref, ka_ref, kb_ref, kn_ref,
                    vc_ref, vp_ref, va_ref, vb_ref, vn_ref, o_ref):
    group = SWA_HEADS // SWA_KV_HEADS
    rows = group * W_BLOCK
    head_of_row = lax.broadcasted_iota(I32, (rows, 1), 0) // W_BLOCK
    windows = ((bias_a_ref, (kc_ref, kp_ref, ka_ref, kb_ref), (vc_ref, vp_ref, va_ref, vb_ref)),
               (bias_b_ref, (kc_ref, ka_ref, kb_ref, kn_ref), (vc_ref, va_ref, vb_ref, vn_ref)))
    for sub, (bias_ref, k_refs, v_refs) in enumerate(windows):
        q_rows = slice(sub * W_BLOCK, (sub + 1) * W_BLOCK)
        for g in range(SWA_KV_HEADS):
            gs = slice(g * LANE, (g + 1) * LANE)
            vs = slice(2 * g * LANE, (2 * g + 2) * LANE)
            q_parts = []
            sink = jnp.zeros((rows, 1), F32)
            for j in range(group):
                sink = jnp.where(head_of_row == j, sink_ref[g * group + j] * LOG2E, sink)
            for p in range(group // 2):
                blk = g * (group // 2) + p
                q_parts.extend(_mask_halves(q_ref[q_rows, blk * LANE:(blk + 1) * LANE].astype(F32)))
            q = jnp.concatenate(q_parts, axis=0)
            k = jnp.concatenate([r[:, gs] for r in k_refs], axis=0)
            v = jnp.concatenate([r[:, vs] for r in v_refs], axis=0)
            s = _dot_nt(q, k) + bias_ref[...]
            m = jnp.maximum(jnp.max(s, axis=-1, keepdims=True), sink)
            ov = _dot(jnp.exp2(s - m).astype(BF), v)
            o = ov[:, 0:LANE] / (ov[:, LANE:LANE + 1] + jnp.exp2(sink - m))
            for p in range(group // 2):
                blk = g * (group // 2) + p
                o_a = o[(2 * p) * W_BLOCK:(2 * p + 1) * W_BLOCK]
                o_b = o[(2 * p + 1) * W_BLOCK:(2 * p + 2) * W_BLOCK]
                o_ref[q_rows, blk * LANE:(blk + 1) * LANE] = _pair(o_a, o_b).astype(BF)


def _lat_swa_call(sink, bias, q, k, v):
    nb = DEC_SEQ // W_BLOCK
    cb = PAST_LEN // W_BLOCK
    steps = nb // 2

    def ctx(width):
        return pl.BlockSpec((None, PAST_LEN, width), lambda b, n: (b, 0, 0))

    def loc(d, width):
        return pl.BlockSpec((None, W_BLOCK, width),
                            lambda b, n: (b, cb + jnp.clip(2 * n + d, 0, nb - 1), 0))

    def bias_spec(pick):
        return pl.BlockSpec((None,) + bias.shape[1:], lambda b, n: (pick(n), 0, 0))

    kw, vw = 2 * LANE, 4 * LANE
    return pl.pallas_call(
        _lat_swa_kernel,
        grid=(DEC_BATCH, steps),
        in_specs=[_smem(),
                  bias_spec(lambda n: jnp.where(n == 0, 0, 1)), bias_spec(lambda n: jnp.where(n == steps - 1, 2, 1)),
                  pl.BlockSpec((None, 2 * W_BLOCK, 512), lambda b, n: (b, n, 0)),
                  ctx(kw), loc(-1, kw), loc(0, kw), loc(1, kw), loc(2, kw),
                  ctx(vw), loc(-1, vw), loc(0, vw), loc(1, vw), loc(2, vw)],
        out_specs=pl.BlockSpec((None, 2 * W_BLOCK, 512), lambda b, n: (b, n, 0)),
        out_shape=jax.ShapeDtypeStruct((DEC_BATCH, DEC_SEQ, 512), BF),
        compiler_params=_cp(("parallel", "arbitrary"), 40), name="lat_swa",
    )(sink, bias, bias, q, k, k, k, k, k, v, v, v, v, v)


def _rows_to_tiles(o_ref, val, tm):
    for k in range(D_CHUNKS):
        o_ref[pl.ds(k, tm, stride=D_CHUNKS), :] = val[:, k * LANE:(k + 1) * LANE]


def _post_kernel(x_ref, oda_ref, omla_ref, oswa_ref, gate_ref, mod_ref, n2_ref, wda_ref, wmla_ref, wswa_ref,
                 wo_ref, wrh_ref, wrl_ref, xo_ref, h2_ref, lg_ref):
    m = mod_ref[...]
    g = gate_ref[...].astype(F32)
    merged = (g[:, 0:D_MODEL] * _dot(oda_ref[...], wda_ref[...])
              + g[:, D_MODEL:2 * D_MODEL] * _dot(omla_ref[...], wmla_ref[...])
              + g[:, 2 * D_MODEL:] * _dot(oswa_ref[...], wswa_ref[...]))
    x = x_ref[...] + m[2:3] * _dot(merged.astype(BF), wo_ref[...])
    xo_ref[...] = x
    h2 = _rms(x, n2_ref[...]) * (1.0 + m[4:5]) + m[3:4]
    _rows_to_tiles(h2_ref, h2, TM_POST)
    hh, hl = _split(h2)
    lg_ref[...] = _dot_nt(wrh_ref[...], hh) + _dot_nt(wrl_ref[...], hh) + _dot_nt(wrh_ref[...], hl)


def _post_call(layer, x, oda, omla, oswa, gate, mod, n2, wda, wmla, wswa, wo, wrh, wrl, latent):
    n = x.shape[0]
    tm = TM_POST
    row = pl.BlockSpec((tm, D_MODEL), lambda i: (i, 0))
    o512 = pl.BlockSpec((tm, 512), lambda i: (i, 0))
    return pl.pallas_call(
        _post_kernel, grid=(n // tm,),
        in_specs=[row, o512, o512, o512, pl.BlockSpec((tm, GATE_COLS), lambda i: (i, 0)),
                  _mod_spec(layer, tm, latent), _layer(layer, (1, D_MODEL)),
                  _layer(layer, (512, D_MODEL)), _layer(layer, (512, D_MODEL)), _layer(layer, (512, D_MODEL)),
                  _layer(layer, (D_MODEL, D_MODEL)), _layer(layer, (N_EXPERTS, D_MODEL)),
                  _layer(layer, (N_EXPERTS, D_MODEL))],
        out_specs=[row, pl.BlockSpec((tm * D_CHUNKS, LANE), lambda i: (i, 0)),
                   pl.BlockSpec((N_EXPERTS, tm), lambda i: (0, i))],
        out_shape=[jax.ShapeDtypeStruct((n, D_MODEL), F32),
                   jax.ShapeDtypeStruct((n * D_CHUNKS, LANE), F32),
                   jax.ShapeDtypeStruct((N_EXPERTS, n), F32)],
        compiler_params=_cp(("parallel",), 48), name="post",
    )(x, oda, omla, oswa, gate, mod, n2, wda, wmla, wswa, wo, wrh, wrl)


def _router_kernel(cap, lg_ref, q_ref, a_ref, lo_ref, hi_ref, cnt_ref):
    e_n, n = lg_ref.shape
    lg = lg_ref[...]
    shifted = lg - jnp.max(lg, axis=0, keepdims=True)
    ex = jnp.exp(shifted)
    den = jnp.sum(ex, axis=0, keepdims=True)
    aff = ex / den
    score = shifted - jnp.log(den)
    capf = float(cap)

    def count(mask):
        return jnp.sum(jnp.where(mask, 1.0, 0.0), axis=1, keepdims=True)

    def halve(_, bounds):
        lo, hi = bounds
        mid = 0.5 * (lo + hi)
        enough = count(score >= mid) >= capf
        return jnp.where(enough, mid, lo), jnp.where(enough, hi, mid)

    lo, hi = lax.fori_loop(0, THRESHOLD_STEPS, halve,
                           (jnp.min(score, axis=1, keepdims=True), jnp.ones((e_n, 1), F32)))
    gt = score >= hi
    eq = (score >= lo) & (score < hi)
    need = capf - count(gt)
    tok = lax.broadcasted_iota(I32, (e_n, n), 1)
    n_bits = n.bit_length()

    def tie_bit(i, bound):
        cand = bound | jnp.left_shift(jnp.int32(1), n_bits - 1 - i)
        ok = (cand <= n) & (count(eq & (tok < cand)) <= need)
        return jnp.where(ok, cand, bound)

    bound = lax.fori_loop(0, n_bits, tie_bit, jnp.zeros((e_n, 1), I32))
    sel = gt | (eq & (tok < bound))
    cnt_ref[...] = jnp.broadcast_to(count(sel & (tok < TOK_HALF)), cnt_ref.shape)
    blk = 2 * LANE
    per_chunk = ROUTE_CHUNK // blk
    n_chunks = n // ROUTE_CHUNK
    upper = (lax.broadcasted_iota(I32, (blk, blk), 0) <= lax.broadcasted_iota(I32, (blk, blk), 1))
    upper = jnp.where(upper, 1.0, 0.0).astype(BF)
    lane = _lane((e_n, LANE))
    carry = jnp.zeros((e_n, 1), F32)
    c_in = jnp.zeros((e_n, LANE), F32)
    c_ex = jnp.zeros((e_n, LANE), F32)
    for c in range(n // blk):
        chunk, part = divmod(c, per_chunk)
        if part == 0:
            c_ex = jnp.where(lane == chunk, carry, c_ex)
            a_ref[chunk] = aff[:, chunk * ROUTE_CHUNK:(chunk + 1) * ROUTE_CHUNK]
        s_blk = sel[:, c * blk:(c + 1) * blk]
        rank = _dot(jnp.where(s_blk, 1.0, 0.0).astype(BF), upper) + carry
        q_ref[chunk, :, part * blk:(part + 1) * blk] = jnp.where(s_blk, rank, 0.0)
        carry = rank[:, blk - 1:blk]
        if part == per_chunk - 1:
            c_in = jnp.where(lane == chunk, carry, c_in)
    valid = lane < n_chunks
    lo = jnp.zeros((e_n, LANE), I32)
    hi = jnp.zeros((e_n, LANE), I32)
    for rt in range(cap // ROUTE_TILE):
        lo_rt = count(valid & (c_in < float(rt * ROUTE_TILE + 1)))
        hi_rt = count(valid & (c_ex < float((rt + 1) * ROUTE_TILE)))
        lo = jnp.where(lane == rt, lo_rt.astype(I32), lo)
        hi = jnp.where(lane == rt, hi_rt.astype(I32), hi)
    lo_ref[...] = lo
    hi_ref[...] = hi


def _compact_kernel(lo_ref, hi_ref, q_ref, a_ref, idx_ref, g_ref):
    e = pl.program_id(0)
    n_tiles = idx_ref.shape[0]
    eye = lax.broadcasted_iota(I32, (ROUTE_TILE, LANE), 0) == lax.broadcasted_iota(I32, (ROUTE_TILE, LANE), 1)

    def per_tile(rt, _):
        slot = (rt * ROUTE_TILE + 1 + lax.broadcasted_iota(I32, (ROUTE_TILE, 1), 0)).astype(F32)

        def per_chunk(c, acc):
            idx_acc, g_acc = acc
            q_row = q_ref[c, pl.ds(e, 1), :]
            a_row = a_ref[c, pl.ds(e, 1), :]
            for part in range(ROUTE_CHUNK // LANE):
                lanes = slice(part * LANE, (part + 1) * LANE)
                hit = q_row[:, lanes] == slot
                tok = (c * ROUTE_CHUNK + part * LANE + lax.broadcasted_iota(I32, (1, LANE), 1)).astype(F32)
                idx_acc = idx_acc + jnp.where(hit, tok, 0.0)
                g_acc = g_acc + jnp.where(hit, a_row[:, lanes], 0.0)
            return idx_acc, g_acc

        zero = jnp.zeros((ROUTE_TILE, LANE), F32)
        idx_acc, g_acc = lax.fori_loop(lo_ref[e, rt], hi_ref[e, rt], per_chunk, (zero, zero))
        idx_v = jnp.sum(idx_acc, axis=1, keepdims=True)
        g_v = jnp.sum(g_acc, axis=1, keepdims=True)
        idx_ref[pl.ds(rt, 1), :] = jnp.sum(jnp.where(eye, idx_v, 0.0), axis=0, keepdims=True).astype(I32)
        g_ref[pl.ds(rt, 1), :] = jnp.sum(jnp.where(eye, g_v, 0.0), axis=0, keepdims=True)
        return 0

    lax.fori_loop(0, n_tiles, per_tile, 0)


def _router_call(logits_t, cap):
    n = logits_t.shape[1]
    n_chunks = n // ROUTE_CHUNK
    n_tiles = cap // ROUTE_TILE
    chunked = jax.ShapeDtypeStruct((n_chunks, N_EXPERTS, ROUTE_CHUNK), F32)
    q, a, lo, hi, cnt = pl.pallas_call(
        functools.partial(_router_kernel, cap),
        out_shape=[chunked, chunked, jax.ShapeDtypeStruct((N_EXPERTS, LANE), I32),
                   jax.ShapeDtypeStruct((N_EXPERTS, LANE), I32), jax.ShapeDtypeStruct((N_EXPERTS, LANE), F32)],
        compiler_params=pltpu.CompilerParams(vmem_limit_bytes=40 * 1024 * 1024), name="router",
    )(logits_t)
    whole = pl.BlockSpec((n_chunks, N_EXPERTS, ROUTE_CHUNK), lambda e, *_: (0, 0, 0))
    slots = pl.BlockSpec((None, n_tiles, ROUTE_TILE), lambda e, *_: (e, 0, 0))
    idx, g = pl.pallas_call(
        _compact_kernel,
        grid_spec=pltpu.PrefetchScalarGridSpec(
            num_scalar_prefetch=2, grid=(N_EXPERTS,), in_specs=[whole, whole], out_specs=[slots, slots]),
        out_shape=[jax.ShapeDtypeStruct((N_EXPERTS, n_tiles, ROUTE_TILE), I32),
                   jax.ShapeDtypeStruct((N_EXPERTS, n_tiles, ROUTE_TILE), F32)],
        compiler_params=_cp(("parallel",), 32), name="compact",
    )(lo[:, :n_tiles], hi[:, :n_tiles], q, a)
    return idx.reshape(N_EXPERTS, cap), g.reshape(N_EXPERTS, cap), cnt


def _row_copy(src_hbm, xe, sem, buf, tok, slot):
    dst = xe.at[buf, pl.ds(pl.multiple_of(slot * SUB, SUB), SUB)]
    return pltpu.make_async_copy(src_hbm.at[tok], dst, sem.at[buf])


def _rows_wait(rows_hbm, xe, sem, buf, slot, n):
    dst = xe.at[buf, pl.ds(pl.multiple_of(slot * SUB, SUB), n * SUB)]
    return pltpu.make_async_copy(rows_hbm.at[pl.ds(0, n * SUB)], dst, sem.at[buf])


def _ffn_kernel(idx_ref, idxn_ref, hc_hbm, hl_hbm, rows_hbm, w1_ref, w3_ref, w2_ref, yc_ref, yl_ref,
                xe, xb, y_part, w1b, w3b, w2b, sem):
    e = pl.program_id(0)
    f = pl.program_id(1)
    buf = e % 2

    @pl.when((e == 0) & (f == 0))
    def _():
        def start_ctx(r, _):
            _row_copy(hc_hbm, xe, sem, 0, idx_ref[0, r], r).start()
            return 0

        def start_lat(r, _):
            _row_copy(hl_hbm, xe, sem, 0, idx_ref[0, r], r).start()
            return 0

        lax.fori_loop(0, CAP_CTX, start_ctx, 0)
        lax.fori_loop(CAP_CTX, SLOTS, start_lat, 0)
        _rows_wait(rows_hbm, xe, sem, 0, 0, SLOTS).wait()

    @pl.when(f == 0)
    def _():
        for k in range(D_CHUNKS):
            xb[:, k * LANE:(k + 1) * LANE] = xe[buf, pl.ds(k, SLOTS, stride=D_CHUNKS), :].astype(BF)

    w1b[...] = w1_ref[...].astype(BF)
    w3b[...] = w3_ref[...].astype(BF)
    w2b[...] = w2_ref[...].astype(BF)

    part_c, part_l = CAP_CTX // FF_SPLIT, CAP_LAT // FF_SPLIT
    next_rows = ([(hc_hbm, f * part_c + u) for u in range(part_c)]
                 + [(hl_hbm, CAP_CTX + f * part_l + u) for u in range(part_l)])

    def next_copy(src, r):
        return _row_copy(src, xe, sem, 1 - buf, idxn_ref[0, r], r)

    n_tiles = SLOTS // FFN_ROWS
    for t in range(n_tiles):
        if t < n_tiles - 1:
            for src, r in next_rows[t::n_tiles - 1]:
                next_copy(src, r).start()
        rows = slice(t * FFN_ROWS, (t + 1) * FFN_ROWS)
        x = xb[rows, :]
        a = _dot(x, w1b[...])
        hid = (a * jax.nn.sigmoid(a) * _dot(x, w3b[...])).astype(BF)
        y = _dot(hid, w2b[...])
        lo, hi = t * FFN_ROWS, (t + 1) * FFN_ROWS
        segs = []
        if lo < CAP_CTX:
            segs.append((yc_ref, lo, 0, min(hi, CAP_CTX) - lo))
        if hi > CAP_CTX:
            first = max(lo, CAP_CTX)
            segs.append((yl_ref, first - CAP_CTX, first - lo, hi - first))

        @pl.when(f == 0)
        def _():
            y_part[rows, :] = y

        if FF_SPLIT > 2:
            @pl.when((f > 0) & (f < FF_SPLIT - 1))
            def _():
                y_part[rows, :] = y_part[rows, :] + y

        @pl.when(f == FF_SPLIT - 1)
        def _():
            total = y + y_part[rows, :]
            for y_ref, row0, y0, n in segs:
                for k in range(D_CHUNKS):
                    y_ref[pl.ds(row0 * D_CHUNKS + k, n, stride=D_CHUNKS), :] = total[y0:y0 + n, k * LANE:(k + 1) * LANE]

    _rows_wait(rows_hbm, xe, sem, 1 - buf, f * part_c, part_c).wait()
    _rows_wait(rows_hbm, xe, sem, 1 - buf, CAP_CTX + f * part_l, part_l).wait()


def _ffn_call(layer, idx, h_ctx, h_lat, w1, w3, w2):
    ff = EXPERT_FF // FF_SPLIT
    wspec = pl.BlockSpec((None, None, D_MODEL, ff), lambda e, f: (layer, e, 0, f))
    last = N_EXPERTS - 1
    return pl.pallas_call(
        _ffn_kernel,
        grid=(N_EXPERTS, FF_SPLIT),
        in_specs=[pl.BlockSpec((None, 1, SLOTS), lambda e, f: (e, 0, 0), memory_space=pltpu.SMEM),
                  pl.BlockSpec((None, 1, SLOTS), lambda e, f: (jnp.minimum(e + 1, last), 0, 0),
                               memory_space=pltpu.SMEM),
                  pl.BlockSpec(memory_space=pl.ANY), pl.BlockSpec(memory_space=pl.ANY),
                  pl.BlockSpec(memory_space=pl.ANY), wspec, wspec, pl.BlockSpec((None, None, ff, D_MODEL), lambda e, f: (layer, e, f, 0))],
        out_specs=[pl.BlockSpec((None, CAP_CTX * D_CHUNKS, LANE), lambda e, f: (e, 0, 0)),
                   pl.BlockSpec((None, CAP_LAT * D_CHUNKS, LANE), lambda e, f: (e, 0, 0))],
        out_shape=[jax.ShapeDtypeStruct((N_EXPERTS, CAP_CTX * D_CHUNKS, LANE), F32),
                   jax.ShapeDtypeStruct((N_EXPERTS, CAP_LAT * D_CHUNKS, LANE), F32)],
        scratch_shapes=[pltpu.VMEM((2, SLOTS * D_CHUNKS, LANE), F32), pltpu.VMEM((SLOTS, D_MODEL), BF),
                        pltpu.VMEM((SLOTS, D_MODEL), F32),
                        pltpu.VMEM((D_MODEL, ff), BF), pltpu.VMEM((D_MODEL, ff), BF),
                        pltpu.VMEM((ff, D_MODEL), BF), pltpu.SemaphoreType.DMA((2,))],
        compiler_params=_cp(("arbitrary", "arbitrary"), 58), name="ffn",
    )(idx, idx, h_ctx, h_lat, h_lat.reshape(N_LAT * SUB, LANE), w1, w3, w2)


def _combine_kernel(idx_ref, g_ref, bnd_ref, yc_ref, yl_ref, acc_ref):
    h = pl.program_id(0)
    e = pl.program_id(1)

    @pl.when(e == 0)
    def _():
        acc_ref[...] = jnp.zeros(acc_ref.shape, F32)

    def add_rows(y_ref, slot0, lo, hi, base):
        def group(i, _):
            r0 = lo + i * COMBINE_GROUP
            toks = [idx_ref[0, r0 + u] - base for u in range(COMBINE_GROUP)]
            sums = [acc_ref[toks[u]] + y_ref[r0 - slot0 + u] * g_ref[0, r0 + u] for u in range(COMBINE_GROUP)]
            for u in range(COMBINE_GROUP):
                acc_ref[toks[u]] = sums[u]
            return 0

        def single(r, _):
            t = idx_ref[0, r] - base
            acc_ref[t] = acc_ref[t] + y_ref[r - slot0] * g_ref[0, r]
            return 0

        n_groups = (hi - lo) // COMBINE_GROUP
        lax.fori_loop(0, n_groups, group, 0)
        lax.fori_loop(lo + n_groups * COMBINE_GROUP, hi, single, 0)

    @pl.when(h == 0)
    def _():
        add_rows(yc_ref, 0, 0, CAP_CTX, 0)

    @pl.when(h > 0)
    def _():
        add_rows(yl_ref, CAP_CTX, bnd_ref[0, h], bnd_ref[0, h + 1], (h - 1) * TOK_HALF)


def _combine_call(idx, g, bnd, y_ctx, y_lat):
    last = N_EXPERTS - 1
    return pl.pallas_call(
        _combine_kernel,
        grid=(N_HALVES, N_EXPERTS),
        in_specs=[pl.BlockSpec((None, 1, SLOTS), lambda h, e: (e, 0, 0), memory_space=pltpu.SMEM),
                  pl.BlockSpec((None, 1, SLOTS), lambda h, e: (e, 0, 0), memory_space=pltpu.SMEM),
                  pl.BlockSpec((None, 1, N_HALVES + 1), lambda h, e: (e, 0, 0), memory_space=pltpu.SMEM),
                  pl.BlockSpec((None, CAP_CTX, SUB, LANE), lambda h, e: (jnp.where(h == 0, e, last), 0, 0, 0)),
                  pl.BlockSpec((None, CAP_LAT, SUB, LANE), lambda h, e: (jnp.where(h == 0, 0, e), 0, 0, 0))],
        out_specs=pl.BlockSpec((None, TOK_HALF, SUB, LANE), lambda h, e: (h, 0, 0, 0)),
        out_shape=jax.ShapeDtypeStruct((N_HALVES, TOK_HALF, SUB, LANE), F32),
        compiler_params=_cp(("parallel", "arbitrary"), 52), name="combine",
    )(idx, g, bnd, y_ctx.reshape(N_EXPERTS, CAP_CTX, SUB, LANE), y_lat.reshape(N_EXPERTS, CAP_LAT, SUB, LANE))


def _final_kernel(x_ref, moe_ref, mod_ref, g_ref, o_ref):
    x = x_ref[...] + mod_ref[...][5:6] * _moe_rows(moe_ref, TM_FINAL)
    o_ref[...] = _rms(x, g_ref[...])


def _final_call(x, moe, mod, g, latent):
    n = x.shape[0]
    tm = TM_FINAL
    moe_off = (N_CTX // tm) if latent else 0
    row = pl.BlockSpec((tm, D_MODEL), lambda i: (i, 0))
    return pl.pallas_call(
        _final_kernel, grid=(n // tm,),
        in_specs=[row, pl.BlockSpec((tm * D_CHUNKS, LANE), lambda i: (i + moe_off, 0)),
                  _mod_spec(DEPTH - 1, tm, latent), _layer(0, (1, D_MODEL))],
        out_specs=row, out_shape=jax.ShapeDtypeStruct((n, D_MODEL), F32),
        compiler_params=_cp(("parallel",), 32), name="final",
    )(x, moe, mod, g)


def _axial_tables(rot_dim):
    rows = DEC_SEQ // GRID_W
    row = jnp.repeat(jnp.arange(rows, dtype=F32), GRID_W)
    col = jnp.tile(jnp.arange(GRID_W, dtype=F32), rows)
    n_freq = rot_dim // 4
    inv = ROPE_BASE ** (-jnp.arange(n_freq, dtype=F32) / n_freq)
    ang = jnp.concatenate([row[:, None] * inv, col[:, None] * inv], axis=-1)
    sign = jnp.tile(jnp.array([-1.0, 1.0], F32), rot_dim // 2)
    return jnp.repeat(jnp.cos(ang), 2, axis=1), jnp.repeat(jnp.sin(ang), 2, axis=1) * sign


def _rope_tables():
    c64, s64 = _axial_tables(DA_DH)
    c32, s32 = _axial_tables(MLA_ROPE)
    one = lambda w: jnp.ones((DEC_SEQ, w), F32)
    zero = lambda w: jnp.zeros((DEC_SEQ, w), F32)
    pad = LANE - MLA_NOPE - MLA_ROPE
    return jnp.stack([
        jnp.tile(c64, (1, 2)), jnp.tile(s64, (1, 2)),
        jnp.concatenate([one(MLA_NOPE), c32, one(pad)], axis=1),
        jnp.concatenate([zero(MLA_NOPE), s32, zero(pad)], axis=1),
        jnp.concatenate([c32, one(LANE - MLA_ROPE)], axis=1),
        jnp.concatenate([s32, zero(LANE - MLA_ROPE)], axis=1)])


def _prep_weights(w_in, w_gate, mla_w_qb, mla_w_kvb, w_br_da, w_br_mla, w_br_swa, w_o, w_router):
    kr_end = C_MKR + MLA_ROPE
    win = jnp.concatenate([w_in[:, :, :kr_end].astype(BF), jnp.zeros((DEPTH, D_MODEL, LANE - MLA_ROPE), BF),
                           w_in[:, :, kr_end:].astype(BF)], axis=2)
    dk = MLA_NOPE + MLA_ROPE
    wqb = jnp.pad(mla_w_qb.reshape(DEPTH, MLA_Q_RANK, MLA_HEADS, dk), ((0, 0), (0, 0), (0, 0), (0, LANE - dk)))
    wqb = wqb.reshape(DEPTH, MLA_Q_RANK, MLA_HEADS * LANE).astype(BF)
    kvb = mla_w_kvb.reshape(DEPTH, MLA_KV_RANK, MLA_HEADS, MLA_NOPE + MLA_V)
    wk = jnp.pad(kvb[..., :MLA_NOPE], ((0, 0), (0, 0), (0, 0), (0, LANE - MLA_NOPE)))
    wk = wk.reshape(DEPTH, MLA_KV_RANK, MLA_HEADS * LANE)
    wv = kvb[..., MLA_NOPE:].reshape(DEPTH, MLA_KV_RANK, MLA_OUT)
    wkv = jnp.concatenate([wk, wv], axis=2).astype(BF)
    wrh, wrl = _split(jnp.swapaxes(w_router, 1, 2))
    return dict(win=win, wg=w_gate.astype(BF), wqb=wqb, wkv=wkv, wda=w_br_da.astype(BF),
                wmla=w_br_mla.astype(BF), wswa=w_br_swa.astype(BF), wo=w_o.astype(BF), wrh=wrh, wrl=wrl)


def kernel(x_prompt, x_sample, cache_da_k, cache_da_v, cache_mla_ckv, cache_mla_krope, cache_swa_k, cache_swa_v, c, c_ctx, w_ada, b_ada, norm1, norm2, w_in, da_lq1, da_lk1, da_lq2, da_lk2, da_subln, mla_q_norm, mla_w_qb, mla_kv_norm, mla_w_kvb, swa_sink, w_gate, w_br_da, w_br_mla, w_br_swa, w_o, w_router, w_ff1, w_ff3, w_ff2, final_norm):
    c_all = jnp.concatenate([c_ctx[None], c, jnp.zeros((SUB - 1 - DEC_BATCH, D_MODEL), F32)], axis=0)
    mod = _ada_call(c_all, w_ada, b_ada)[:, :1 + DEC_BATCH].reshape(DEPTH, 1 + DEC_BATCH, 6, D_MODEL)
    lam_all = _lam_call(da_lq1, da_lk1, da_lq2, da_lk2)
    tab = _rope_tables()
    swa_bias = _swa_bias()
    w = _prep_weights(w_in, w_gate, mla_w_qb, mla_w_kvb, w_br_da, w_br_mla, w_br_swa, w_o, w_router)
    n1, n2 = norm1[:, None], norm2[:, None]
    subln, qn, kvn = da_subln[:, None], mla_q_norm[:, None], mla_kv_norm[:, None]
    caches = (cache_da_k.reshape(DEC_BATCH, DEPTH, PAST_LEN, 512),
              cache_da_v.reshape(DEC_BATCH, DEPTH, PAST_LEN, 512),
              cache_mla_ckv,
              jnp.pad(cache_mla_krope, ((0, 0), (0, 0), (0, 0), (0, LANE - MLA_ROPE))),
              cache_swa_k.reshape(DEC_BATCH, DEPTH, PAST_LEN, LANE),
              cache_swa_v.reshape(DEC_BATCH, DEPTH, PAST_LEN, LANE))

    xp = x_prompt.reshape(N_CTX, D_MODEL)
    xs = x_sample.reshape(N_LAT, D_MODEL)
    moe = None
    new = [[] for _ in range(6)]
    for l in range(DEPTH):
        lam = lam_all[l, :1]
        sink = swa_sink[l]

        xp, proj_c, gate_c = _pre_call(l, xp, moe, mod, n1, w["win"], w["wg"], latent=False)
        xs, proj_l, gate_l = _pre_call(l, xs, moe, mod, n1, w["win"], w["wg"], latent=True)

        oda_c, omla_c, oswa_c, ckv_c, k_c, v_c, kr_c, sk_c, sv_c = _ctx_attn_call(
            l, lam, sink, proj_c, subln, qn, w["wqb"], kvn, w["wkv"])
        new[0].append(k_c.reshape(BATCH, SEQ, DA_HEADS, 2 * DA_DH))
        new[1].append(v_c.reshape(BATCH, SEQ, DA_HEADS, 2 * DA_DH))
        new[2].append(ckv_c.reshape(BATCH, SEQ, MLA_KV_RANK))
        new[3].append(kr_c.reshape(BATCH, SEQ, MLA_ROPE))
        new[4].append(sk_c.reshape(BATCH, SEQ, SWA_KV_HEADS, SWA_DH))
        new[5].append(sv_c.reshape(BATCH, SEQ, SWA_KV_HEADS, SWA_DH))

        daq, mlaq, swaq, dak, dav, mlak, mlav, swak, swav = _lat_prep_call(
            l, proj_l, tab, caches, qn, w["wqb"], kvn, w["wkv"])
        oda_l = _lat_da_call(l, lam, daq, dak, dav, subln).reshape(N_LAT, 512)
        omla_l = _lat_mla_call(mlaq, mlak, mlav).reshape(N_LAT, 512)
        oswa_l = _lat_swa_call(sink, swa_bias, swaq, swak, swav).reshape(N_LAT, 512)

        post_w = (w["wda"], w["wmla"], w["wswa"], w["wo"], w["wrh"], w["wrl"])
        xp, h2_c, lg_c = _post_call(l, xp, oda_c, omla_c, oswa_c, gate_c, mod, n2, *post_w, latent=False)
        xs, h2_l, lg_l = _post_call(l, xs, oda_l, omla_l, oswa_l, gate_l, mod, n2, *post_w, latent=True)

        idx_c, g_c, _ = _router_call(lg_c, CAP_CTX)
        idx_l, g_l, cnt_l = _router_call(lg_l, CAP_LAT)
        idx = jnp.concatenate([idx_c, idx_l], axis=1).reshape(N_EXPERTS, 1, SLOTS)
        gsel = jnp.concatenate([g_c, g_l], axis=1).reshape(N_EXPERTS, 1, SLOTS)
        n0 = cnt_l[:, 0].astype(I32)
        bnd = jnp.stack([jnp.zeros_like(n0), jnp.full_like(n0, CAP_CTX), CAP_CTX + n0,
                         jnp.full_like(n0, SLOTS)], axis=1)
        y_c, y_l = _ffn_call(l, idx, h2_c.reshape(N_CTX, SUB, LANE), h2_l.reshape(N_LAT, SUB, LANE),
                             w_ff1, w_ff3, w_ff2)
        acc = _combine_call(idx, gsel, bnd.reshape(N_EXPERTS, 1, N_HALVES + 1), y_c, y_l)
        moe = acc.reshape(N_HALVES * TOK_HALF * D_CHUNKS, LANE)

    fn = final_norm[None, None]
    y_prompt = _final_call(xp, moe, mod, fn, latent=False).reshape(BATCH, SEQ, D_MODEL)
    y_sample = _final_call(xs, moe, mod, fn, latent=True).reshape(DEC_BATCH, DEC_SEQ, D_MODEL)
    return (y_prompt, y_sample) + tuple(jnp.stack(n, axis=1) for n in new)
```

```python
import functools
import math

import jax
import jax.numpy as jnp
from jax import lax
from jax.experimental import pallas as pl
from jax.experimental.pallas import tpu as pltpu

F32 = jnp.float32
BF = jnp.bfloat16
I32 = jnp.int32

D_MODEL = 1024
BATCH = 16
SEQ = 256
DEPTH = 2
DEC_BATCH = 2
DEC_SEQ = 4096
PAST_LEN = 512
GRID_W = 64
ROPE_BASE = 10000.0
EPS = 1e-6
DA_HEADS = 4
DA_DH = 64
DA_OUT = DA_HEADS * 2 * DA_DH
MLA_HEADS = 8
MLA_Q_RANK = 256
MLA_KV_RANK = 128
MLA_NOPE = 64
MLA_ROPE = 32
MLA_V = 64
MLA_OUT = MLA_HEADS * MLA_V
SWA_HEADS = 8
SWA_KV_HEADS = 2
SWA_DH = 64
W_BLOCK = 128
SWA_OUT = SWA_HEADS * SWA_DH
N_EXPERTS = 16
EXPERT_FF = 1024
CAPACITY_FACTOR = 2

LANE = 128
SUB = 8
HALF = 64
N_CTX = BATCH * SEQ
N_LAT = DEC_BATCH * DEC_SEQ
KEYS = PAST_LEN + DEC_SEQ
CAP_CTX = CAPACITY_FACTOR * N_CTX // N_EXPERTS
CAP_LAT = CAPACITY_FACTOR * N_LAT // N_EXPERTS
SLOTS = CAP_CTX + CAP_LAT
TOK_HALF = 4096
N_HALVES = (N_CTX + N_LAT) // TOK_HALF
D_CHUNKS = D_MODEL // LANE

C_DAQ, C_DAK, C_DAV = 0, 512, 1024
C_MQ, C_MKV, C_MKR = 1536, 1792, 1920
C_SQ, C_SK, C_SV = 2048, 2560, 2688
PROJ_COLS = 2816
GATE_COLS = 3 * D_MODEL
MLA_SCALE = (MLA_NOPE + MLA_ROPE) ** -0.5
HEAD_SCALE = DA_DH ** -0.5
LOG2E = math.log2(math.e)

TM_PRE = 256
TM_FINAL = 512
TM_POST = 512
CTX_STEP_ROWS = 2
CHUNK = 512
TQ = 256
DA_STEP_HEADS = 4
MLA_STEP_PAIRS = 4
ROUTE_TILE = 128
ROUTE_CHUNK = 256
THRESHOLD_STEPS = 48
FFN_ROWS = 512
FF_SPLIT = 2
COMBINE_GROUP = 8


def _cp(sem, vmem_mb):
    return pltpu.CompilerParams(dimension_semantics=sem, vmem_limit_bytes=vmem_mb * 1024 * 1024)


def _dot(a, b):
    return jnp.dot(a, b, preferred_element_type=F32)


def _dot_nt(a, b):
    return lax.dot_general(a, b, (((1,), (1,)), ((), ())), preferred_element_type=F32)


def _split(a):
    hi = a.astype(BF)
    lo = (a - hi.astype(F32)).astype(BF)
    return hi, lo


def _dot3(a, w):
    ah, al = _split(a)
    wh, wl = _split(w)
    return _dot(ah, wh) + _dot(ah, wl) + _dot(al, wh)


def _rms(x, g):
    return x * lax.rsqrt(jnp.mean(x * x, axis=-1, keepdims=True) + EPS) * g


def _lane(shape):
    return lax.broadcasted_iota(I32, shape, len(shape) - 1)


def _softmax_pv(segs, sink=None):
    m = None
    for s, _ in segs:
        ms = jnp.max(s, axis=-1, keepdims=True)
        m = ms if m is None else jnp.maximum(m, ms)
    if sink is not None:
        m = jnp.maximum(m, sink)
    l = None
    o = None
    for s, v in segs:
        e = jnp.exp(s - m)
        ls = jnp.sum(e, axis=-1, keepdims=True)
        os_ = _dot(e.astype(BF), v)
        l = ls if l is None else l + ls
        o = os_ if o is None else o + os_
    if sink is not None:
        l = l + jnp.exp(sink - m)
    return o / l


def _mask_halves(q):
    lo = _lane(q.shape) < HALF
    return jnp.where(lo, q, 0.0).astype(BF), jnp.where(lo, 0.0, q).astype(BF)


def _pair(o_a, o_b):
    return jnp.where(_lane(o_a.shape) < HALF, o_a, o_b)


def _dup(x):
    r = pltpu.roll(x, HALF, 1)
    lo = _lane(x.shape) < HALF
    return jnp.where(lo, x, r), jnp.where(lo, r, x)


def _rope(x, c, s):
    n = x.shape[-1]
    even = (_lane(x.shape) % 2) == 0
    sw = jnp.where(even, pltpu.roll(x, n - 1, 1), pltpu.roll(x, 1, 1))
    return x * c + sw * s


def _da_head(q, k, v, lam, subln, lam_init):
    q1, q2 = _mask_halves(q)
    o1 = _softmax_pv([(_dot_nt(q1, k), v)])
    o2 = _softmax_pv([(_dot_nt(q2, k), v)])
    return _rms(o1 - lam * o2, subln) * (1.0 - lam_init)


def _ada_kernel(c_ref, w_ref, b_ref, o_ref):
    c = c_ref[...]
    a = c * jax.nn.sigmoid(c)
    o_ref[...] = _dot3(a, w_ref[...]) + b_ref[...]


def _ada_call(c_all, w_ada, b_ada):
    tn = 1536
    return pl.pallas_call(
        _ada_kernel,
        grid=(DEPTH, 6 * D_MODEL // tn),
        in_specs=[pl.BlockSpec((SUB, D_MODEL), lambda l, j: (0, 0)),
                  pl.BlockSpec((None, D_MODEL, tn), lambda l, j: (l, 0, j)),
                  pl.BlockSpec((None, 1, tn), lambda l, j: (l, 0, j))],
        out_specs=pl.BlockSpec((None, SUB, tn), lambda l, j: (l, 0, j)),
        out_shape=jax.ShapeDtypeStruct((DEPTH, SUB, 6 * D_MODEL), F32),
        compiler_params=_cp(("parallel", "parallel"), 40),
        name="ada",
    )(c_all, w_ada, b_ada.reshape(DEPTH, 1, 6 * D_MODEL))


def _lam_kernel(q1, k1, q2, k2, o_ref):
    s1 = jnp.sum(q1[...] * k1[...], axis=-1, keepdims=True)
    s2 = jnp.sum(q2[...] * k2[...], axis=-1, keepdims=True)
    row = lax.broadcasted_iota(I32, (DEPTH, 1), 0)
    init = jnp.zeros((DEPTH, 1), F32)
    for l in range(DEPTH):
        init = jnp.where(row == l, _lam_init(l), init)
    o_ref[...] = jnp.broadcast_to(jnp.exp(s1) - jnp.exp(s2) + init, o_ref.shape)


def _lam_init(layer):
    return 0.8 - 0.6 * math.exp(-0.3 * layer)


def _lam_call(q1, k1, q2, k2):
    return pl.pallas_call(
        _lam_kernel,
        out_shape=jax.ShapeDtypeStruct((DEPTH, LANE), F32),
        name="lam",
    )(q1, k1, q2, k2)


def _moe_rows(moe_ref, tm):
    return jnp.concatenate([moe_ref[pl.ds(k, tm, stride=D_CHUNKS), :] for k in range(D_CHUNKS)], axis=1)


def _pre_body(x, mod_ref, n1_ref, win_ref, wg_ref, proj_ref, gate_ref):
    m = mod_ref[...]
    h = _rms(x, n1_ref[...]) * (1.0 + m[1:2]) + m[0:1]
    hb = h.astype(BF)
    proj_ref[...] = _dot(hb, win_ref[...]).astype(proj_ref.dtype)
    gate_ref[...] = jax.nn.sigmoid(_dot(hb, wg_ref[...])).astype(BF)


def _pre_first_kernel(x_ref, mod_ref, n1_ref, win_ref, wg_ref, proj_ref, gate_ref):
    _pre_body(x_ref[...], mod_ref, n1_ref, win_ref, wg_ref, proj_ref, gate_ref)


def _pre_next_kernel(x_ref, moe_ref, modp_ref, mod_ref, n1_ref, win_ref, wg_ref, xo_ref, proj_ref, gate_ref):
    x = x_ref[...] + modp_ref[...][5:6] * _moe_rows(moe_ref, TM_PRE)
    xo_ref[...] = x
    _pre_body(x, mod_ref, n1_ref, win_ref, wg_ref, proj_ref, gate_ref)


def _mod_spec(layer, tm, latent):
    per = DEC_SEQ // tm
    if latent:
        return pl.BlockSpec((None, None, 6, D_MODEL), lambda i: (layer, 1 + i // per, 0, 0))
    return pl.BlockSpec((None, None, 6, D_MODEL), lambda i: (layer, 0, 0, 0))


def _layer(layer, shape):
    nd = len(shape)
    return pl.BlockSpec((None,) + shape, lambda *_: (layer,) + (0,) * nd, pipeline_mode=pl.Buffered(1))


def _pre_call(layer, x, moe, mod, n1, win, wg, latent):
    n = x.shape[0]
    tm = TM_PRE
    row = pl.BlockSpec((tm, D_MODEL), lambda i: (i, 0))
    w_specs = [_mod_spec(layer, tm, latent), _layer(layer, (1, D_MODEL)), _layer(layer, (D_MODEL, PROJ_COLS)),
               _layer(layer, (D_MODEL, GATE_COLS))]
    outs = [jax.ShapeDtypeStruct((n, PROJ_COLS), BF if latent else F32), jax.ShapeDtypeStruct((n, GATE_COLS), BF)]
    out_specs = [pl.BlockSpec((tm, PROJ_COLS), lambda i: (i, 0)), pl.BlockSpec((tm, GATE_COLS), lambda i: (i, 0))]
    if moe is None:
        proj, gate = pl.pallas_call(
            _pre_first_kernel, grid=(n // tm,), in_specs=[row] + w_specs, out_specs=out_specs, out_shape=outs,
            compiler_params=_cp(("parallel",), 52), name="pre_first",
        )(x, mod, n1, win, wg)
        return x, proj, gate
    moe_off = (N_CTX // tm) if latent else 0
    moe_spec = pl.BlockSpec((tm * D_CHUNKS, LANE), lambda i: (i + moe_off, 0))
    xo, proj, gate = pl.pallas_call(
        _pre_next_kernel, grid=(n // tm,),
        in_specs=[row, moe_spec, _mod_spec(layer - 1, tm, latent)] + w_specs,
        out_specs=[row] + out_specs,
        out_shape=[jax.ShapeDtypeStruct((n, D_MODEL), F32)] + outs,
        compiler_params=_cp(("parallel",), 52), name="pre_next",
    )(x, moe, mod, mod, n1, win, wg)
    return xo, proj, gate


def _ctx_attn_kernel(lam_init, lam_ref, sink_ref, p_ref, subln_ref, qn_ref, wqb_ref, kvn_ref, wkv_ref,
                     oda_ref, omla_ref, oswa_ref, ckv_ref, kc_ref, vc_ref, kr_ref, sk_ref, sv_ref):
    lam = lam_ref[0]
    for b in range(CTX_STEP_ROWS):
        r = slice(b * SEQ, (b + 1) * SEQ)
        p = p_ref[r, :]
        for h in range(DA_HEADS):
            dst = pl.ds(b * SEQ * DA_HEADS + h, SEQ, stride=DA_HEADS)
            kc_ref[dst, :] = p[:, C_DAK + h * LANE:C_DAK + (h + 1) * LANE]
            vc_ref[dst, :] = p[:, C_DAV + h * LANE:C_DAV + (h + 1) * LANE]
        kr_ref[r, :] = p[:, C_MKR:C_MKR + MLA_ROPE]
        sk_ref[r, :] = p[:, C_SK:C_SK + LANE]
        sv_ref[r, :] = p[:, C_SV:C_SV + LANE]
        for h in range(DA_HEADS):
            blk = slice(h * LANE, (h + 1) * LANE)
            q = p[:, C_DAQ + h * LANE:C_DAQ + (h + 1) * LANE] * HEAD_SCALE
            k = p[:, C_DAK + h * LANE:C_DAK + (h + 1) * LANE].astype(BF)
            v = p[:, C_DAV + h * LANE:C_DAV + (h + 1) * LANE].astype(BF)
            oda_ref[r, blk] = _da_head(q, k, v, lam, subln_ref[...], lam_init).astype(BF)
        qn = _rms(p[:, C_MQ:C_MQ + MLA_Q_RANK], qn_ref[...]).astype(BF)
        cq = _dot(qn, wqb_ref[...])
        ckv = _rms(p[:, C_MKV:C_MKV + MLA_KV_RANK], kvn_ref[...])
        ckv_ref[r, :] = ckv
        kv = _dot(ckv.astype(BF), wkv_ref[...])
        kr_sh = pltpu.roll(p[:, C_MKR:C_MKR + LANE], HALF, 1)
        for m in range(MLA_HEADS // 2):
            vpair = kv[:, MLA_HEADS * LANE + m * LANE:MLA_HEADS * LANE + (m + 1) * LANE].astype(BF)
            outs = []
            for h in (2 * m, 2 * m + 1):
                q = (cq[:, h * LANE:(h + 1) * LANE] * MLA_SCALE).astype(BF)
                k = (kv[:, h * LANE:(h + 1) * LANE] + kr_sh).astype(BF)
                outs.append(_softmax_pv([(_dot_nt(q, k), vpair)]))
            omla_ref[r, m * LANE:(m + 1) * LANE] = _pair(outs[0], outs[1]).astype(BF)
        kd = _dup(p[:, C_SK:C_SK + LANE])
        vd = _dup(p[:, C_SV:C_SV + LANE])
        for m in range(SWA_HEADS // 2):
            g = (2 * m) // (SWA_HEADS // SWA_KV_HEADS)
            k = kd[g].astype(BF)
            v = vd[g].astype(BF)
            qa, qb = _mask_halves(p[:, C_SQ + m * LANE:C_SQ + (m + 1) * LANE] * HEAD_SCALE)
            oa = _softmax_pv([(_dot_nt(qa, k), v)], sink_ref[2 * m])
            ob = _softmax_pv([(_dot_nt(qb, k), v)], sink_ref[2 * m + 1])
            oswa_ref[r, m * LANE:(m + 1) * LANE] = _pair(oa, ob).astype(BF)


def _smem():
    return pl.BlockSpec(memory_space=pltpu.SMEM)


def _ctx_attn_call(layer, lam, sink, proj, subln, qn, wqb, kvn, wkv):
    rows = CTX_STEP_ROWS * SEQ
    row512 = pl.BlockSpec((rows, 512), lambda b: (b, 0))
    row128 = pl.BlockSpec((rows, LANE), lambda b: (b, 0))
    heads = pl.BlockSpec((rows * DA_HEADS, LANE), lambda b: (b, 0))
    return pl.pallas_call(
        functools.partial(_ctx_attn_kernel, _lam_init(layer)),
        grid=(BATCH // CTX_STEP_ROWS,),
        in_specs=[_smem(), _smem(), pl.BlockSpec((rows, PROJ_COLS), lambda b: (b, 0)),
                  _layer(layer, (1, LANE)), _layer(layer, (1, MLA_Q_RANK)),
                  _layer(layer, (MLA_Q_RANK, MLA_HEADS * LANE)), _layer(layer, (1, MLA_KV_RANK)),
                  _layer(layer, (MLA_KV_RANK, MLA_HEADS * LANE + MLA_OUT))],
        out_specs=[row512, row512, row512, row128, heads, heads,
                   pl.BlockSpec((rows, MLA_ROPE), lambda b: (b, 0)), row128, row128],
        out_shape=[jax.ShapeDtypeStruct((N_CTX, 512), BF)] * 3
        + [jax.ShapeDtypeStruct((N_CTX, LANE), F32)]
        + [jax.ShapeDtypeStruct((N_CTX * DA_HEADS, LANE), F32)] * 2
        + [jax.ShapeDtypeStruct((N_CTX, MLA_ROPE), F32)] + [jax.ShapeDtypeStruct((N_CTX, LANE), F32)] * 2,
        compiler_params=_cp(("parallel",), 40), name="ctx_attn",
    )(lam, sink, proj, subln, qn, wqb, kvn, wkv)


def _lat_prep_kernel(p_ref, tab_ref, cdk_ref, cdv_ref, cckv_ref, ckr_ref, csk_ref, csv_ref,
                     qn_ref, wqb_ref, kvn_ref, wkv_ref,
                     daq_ref, mlaq_ref, swaq_ref, dak_ref, dav_ref, mlak_ref, mlav_ref, swak_ref, swav_ref):
    j = pl.program_id(1)

    def write_kv(dk, dv, ckv, kr_sh, sk, sv):
        ones_col = jnp.where(_lane((CHUNK, LANE)) == 0, 1.0, 0.0).astype(BF)
        dak_ref[...] = dk.astype(BF)
        for h in range(DA_HEADS):
            dav_ref[:, 2 * h * LANE:(2 * h + 1) * LANE] = dv[:, h * LANE:(h + 1) * LANE].astype(BF)
            dav_ref[:, (2 * h + 1) * LANE:(2 * h + 2) * LANE] = ones_col
        kv = _dot(ckv.astype(BF), wkv_ref[...])
        for h in range(MLA_HEADS):
            mlak_ref[:, h * LANE:(h + 1) * LANE] = (kv[:, h * LANE:(h + 1) * LANE] + kr_sh).astype(BF)
        for p in range(MLA_HEADS // 2):
            mlav_ref[:, 2 * p * LANE:(2 * p + 1) * LANE] = kv[:, (MLA_HEADS + p) * LANE:(MLA_HEADS + p + 1) * LANE].astype(BF)
            mlav_ref[:, (2 * p + 1) * LANE:(2 * p + 2) * LANE] = ones_col
        k0, k1 = _dup(sk)
        v0, v1 = _dup(sv)
        swak_ref[:, 0:LANE] = k0.astype(BF)
        swak_ref[:, LANE:2 * LANE] = k1.astype(BF)
        swav_ref[:, 0:LANE] = v0.astype(BF)
        swav_ref[:, LANE:2 * LANE] = ones_col
        swav_ref[:, 2 * LANE:3 * LANE] = v1.astype(BF)
        swav_ref[:, 3 * LANE:4 * LANE] = ones_col

    @pl.when(j == 0)
    def _():
        write_kv(cdk_ref[...], cdv_ref[...], cckv_ref[...], pltpu.roll(ckr_ref[...], HALF, 1),
                 csk_ref[...], csv_ref[...])

    @pl.when(j > 0)
    def _():
        c64, s64 = tab_ref[0], tab_ref[1]
        cmq, smq = tab_ref[2], tab_ref[3]
        ckr, skr = tab_ref[4], tab_ref[5]

        def cols(c0, width):
            return p_ref[:, c0:c0 + width].astype(F32)

        def blk(c0, h):
            return cols(c0 + h * LANE, LANE)

        for h in range(DA_HEADS):
            daq_ref[:, h * LANE:(h + 1) * LANE] = (_rope(blk(C_DAQ, h), c64, s64) * (HEAD_SCALE * LOG2E)).astype(BF)
            swaq_ref[:, h * LANE:(h + 1) * LANE] = (_rope(blk(C_SQ, h), c64, s64) * (HEAD_SCALE * LOG2E)).astype(BF)
        dk = jnp.concatenate([_rope(blk(C_DAK, h), c64, s64) for h in range(DA_HEADS)], axis=1)
        qn = _rms(cols(C_MQ, MLA_Q_RANK), qn_ref[...]).astype(BF)
        cq = _dot(qn, wqb_ref[...])
        for h in range(MLA_HEADS):
            mlaq_ref[:, h * LANE:(h + 1) * LANE] = (
                _rope(cq[:, h * LANE:(h + 1) * LANE], cmq, smq) * (MLA_SCALE * LOG2E)).astype(BF)
        ckv = _rms(cols(C_MKV, MLA_KV_RANK), kvn_ref[...])
        kr_sh = pltpu.roll(_rope(cols(C_MKR, LANE), ckr, skr), HALF, 1)
        sk = _rope(cols(C_SK, LANE), c64, s64)
        write_kv(dk, cols(C_DAV, 512), ckv, kr_sh, sk, cols(C_SV, LANE))


def _lat_prep_call(layer, proj, tab, caches, qn, wqb, kvn, wkv):
    cdk, cdv, cckv, ckr, csk, csv = caches
    nj = 1 + DEC_SEQ // CHUNK

    def own(width):
        return pl.BlockSpec((None, CHUNK, width), lambda b, j: (b, jnp.maximum(j - 1, 0), 0))

    def cache(width):
        return pl.BlockSpec((None, None, PAST_LEN, width), lambda b, j: (b, layer, 0, 0))

    def allk(width):
        return pl.BlockSpec((None, CHUNK, width), lambda b, j: (b, j, 0))

    def shp(rows, width):
        return jax.ShapeDtypeStruct((DEC_BATCH, rows, width), BF)

    return pl.pallas_call(
        _lat_prep_kernel,
        grid=(DEC_BATCH, nj),
        in_specs=[own(PROJ_COLS),
                  pl.BlockSpec((6, CHUNK, LANE), lambda b, j: (0, jnp.maximum(j - 1, 0), 0)),
                  cache(512), cache(512), cache(LANE), cache(LANE), cache(LANE), cache(LANE),
                  _layer(layer, (1, MLA_Q_RANK)), _layer(layer, (MLA_Q_RANK, MLA_HEADS * LANE)),
                  _layer(layer, (1, MLA_KV_RANK)), _layer(layer, (MLA_KV_RANK, MLA_HEADS * LANE + MLA_OUT))],
        out_specs=[own(512), own(MLA_HEADS * LANE), own(512),
                   allk(512), allk(1024), allk(MLA_HEADS * LANE), allk(1024), allk(2 * LANE), allk(4 * LANE)],
        out_shape=[shp(DEC_SEQ, 512), shp(DEC_SEQ, MLA_HEADS * LANE), shp(DEC_SEQ, 512),
                   shp(KEYS, 512), shp(KEYS, 1024), shp(KEYS, MLA_HEADS * LANE), shp(KEYS, 1024),
                   shp(KEYS, 2 * LANE), shp(KEYS, 4 * LANE)],
        compiler_params=_cp(("parallel", "arbitrary"), 48), name="lat_prep",
    )(proj.reshape(DEC_BATCH, DEC_SEQ, PROJ_COLS), tab, cdk, cdv, cckv, ckr, csk, csv, qn, wqb, kvn, wkv)


def _softmax2_pv(s, v_aug):
    e = jnp.exp2(s - jnp.max(s, axis=-1, keepdims=True)).astype(BF)
    o = _dot(e, v_aug)
    return o[:, 0:LANE] / o[:, LANE:LANE + 1]


def _lat_da_kernel(lam_init, lam_ref, q_ref, k_ref, v_ref, subln_ref, o_ref):
    for h in range(DA_STEP_HEADS):
        cols = slice(h * LANE, (h + 1) * LANE)
        q1, q2 = _mask_halves(q_ref[:, cols].astype(F32))
        k = k_ref[:, cols]
        v = v_ref[:, 2 * h * LANE:(2 * h + 2) * LANE]
        o = _softmax2_pv(_dot_nt(q1, k), v) - lam_ref[0] * _softmax2_pv(_dot_nt(q2, k), v)
        o_ref[:, cols] = (_rms(o, subln_ref[...]) * (1.0 - lam_init)).astype(BF)


def _lat_da_call(layer, lam, q, k, v, subln):
    return pl.pallas_call(
        functools.partial(_lat_da_kernel, _lam_init(layer)),
        grid=(DEC_BATCH, DA_HEADS // DA_STEP_HEADS, DEC_SEQ // TQ),
        in_specs=[_smem(),
                  pl.BlockSpec((None, TQ, DA_STEP_HEADS * LANE), lambda b, h, i: (b, i, h)),
                  pl.BlockSpec((None, KEYS, DA_STEP_HEADS * LANE), lambda b, h, i: (b, 0, h)),
                  pl.BlockSpec((None, KEYS, 2 * DA_STEP_HEADS * LANE), lambda b, h, i: (b, 0, h)),
                  _layer(layer, (1, LANE))],
        out_specs=pl.BlockSpec((None, TQ, DA_STEP_HEADS * LANE), lambda b, h, i: (b, i, h)),
        out_shape=jax.ShapeDtypeStruct((DEC_BATCH, DEC_SEQ, 512), BF),
        compiler_params=_cp(("parallel", "parallel", "arbitrary"), 56), name="lat_da",
    )(lam, q, k, v, subln)


def _lat_mla_kernel(q_ref, k_ref, v_ref, o_ref):
    for p in range(MLA_STEP_PAIRS):
        v = v_ref[:, 2 * p * LANE:(2 * p + 2) * LANE]
        outs = []
        for h in (2 * p, 2 * p + 1):
            cols = slice(h * LANE, (h + 1) * LANE)
            outs.append(_softmax2_pv(_dot_nt(q_ref[:, cols], k_ref[:, cols]), v))
        o_ref[:, p * LANE:(p + 1) * LANE] = _pair(outs[0], outs[1]).astype(BF)


def _lat_mla_call(q, k, v):
    sp = MLA_STEP_PAIRS
    return pl.pallas_call(
        _lat_mla_kernel,
        grid=(DEC_BATCH, MLA_HEADS // (2 * sp), DEC_SEQ // TQ),
        in_specs=[pl.BlockSpec((None, TQ, 2 * sp * LANE), lambda b, m, i: (b, i, m)),
                  pl.BlockSpec((None, KEYS, 2 * sp * LANE), lambda b, m, i: (b, 0, m)),
                  pl.BlockSpec((None, KEYS, 2 * sp * LANE), lambda b, m, i: (b, 0, m))],
        out_specs=pl.BlockSpec((None, TQ, sp * LANE), lambda b, m, i: (b, i, m)),
        out_shape=jax.ShapeDtypeStruct((DEC_BATCH, DEC_SEQ, 512), BF),
        compiler_params=_cp(("parallel", "parallel", "arbitrary"), 56), name="lat_mla",
    )(q, k, v)


def _swa_bias():
    rows = (SWA_HEADS // SWA_KV_HEADS) * W_BLOCK
    qi = (jnp.arange(rows) % W_BLOCK)[:, None]
    loc = (jnp.arange(PAST_LEN + 3 * W_BLOCK) - PAST_LEN)[None, :]
    always = (loc < 0) | ((loc >= W_BLOCK) & (loc < 2 * W_BLOCK))
    prev = (loc >= 0) & (loc < W_BLOCK) & (loc >= qi)
    nxt = (loc >= 2 * W_BLOCK) & (loc - 2 * W_BLOCK <= qi)
    ok = jnp.stack([always | nxt, always | prev | nxt, always | prev])
    return jnp.where(ok, 0.0, -jnp.inf).astype(F32)


def _lat_swa_kernel(sink_ref, bias_a_ref, bias_b_ref, q_ref, kc_ref, kp_ref, ka_ref, kb_ref, kn_ref,
                    vc_ref, vp_ref, va_ref, vb_ref, vn_ref, o_ref):
    group = SWA_HEADS // SWA_KV_HEADS
    rows = group * W_BLOCK
    head_of_row = lax.broadcasted_iota(I32, (rows, 1), 0) // W_BLOCK
    windows = ((bias_a_ref, (kc_ref, kp_ref, ka_ref, kb_ref), (vc_ref, vp_ref, va_ref, vb_ref)),
               (bias_b_ref, (kc_ref, ka_ref, kb_ref, kn_ref), (vc_ref, va_ref, vb_ref, vn_ref)))
    for sub, (bias_ref, k_refs, v_refs) in enumerate(windows):
        q_rows = slice(sub * W_BLOCK, (sub + 1) * W_BLOCK)
        for g in range(SWA_KV_HEADS):
            gs = slice(g * LANE, (g + 1) * LANE)
            vs = slice(2 * g * LANE, (2 * g + 2) * LANE)
            q_parts = []
            sink = jnp.zeros((rows, 1), F32)
            for j in range(group):
                sink = jnp.where(head_of_row == j, sink_ref[g * group + j] * LOG2E, sink)
            for p in range(group // 2):
                blk = g * (group // 2) + p
                q_parts.extend(_mask_halves(q_ref[q_rows, blk * LANE:(blk + 1) * LANE].astype(F32)))
            q = jnp.concatenate(q_parts, axis=0)
            k = jnp.concatenate([r[:, gs] for r in k_refs], axis=0)
            v = jnp.concatenate([r[:, vs] for r in v_refs], axis=0)
            s = _dot_nt(q, k) + bias_ref[...]
            m = jnp.maximum(jnp.max(s, axis=-1, keepdims=True), sink)
            ov = _dot(jnp.exp2(s - m).astype(BF), v)
            o = ov[:, 0:LANE] / (ov[:, LANE:LANE + 1] + jnp.exp2(sink - m))
            for p in range(group // 2):
                blk = g * (group // 2) + p
                o_a = o[(2 * p) * W_BLOCK:(2 * p + 1) * W_BLOCK]
                o_b = o[(2 * p + 1) * W_BLOCK:(2 * p + 2) * W_BLOCK]
                o_ref[q_rows, blk * LANE:(blk + 1) * LANE] = _pair(o_a, o_b).astype(BF)


def _lat_swa_call(sink, bias, q, k, v):
    nb = DEC_SEQ // W_BLOCK
    cb = PAST_LEN // W_BLOCK
    steps = nb // 2

    def ctx(width):
        return pl.BlockSpec((None, PAST_LEN, width), lambda b, n: (b, 0, 0))

    def loc(d, width):
        return pl.BlockSpec((None, W_BLOCK, width),
                            lambda b, n: (b, cb + jnp.clip(2 * n + d, 0, nb - 1), 0))

    def bias_spec(pick):
        return pl.BlockSpec((None,) + bias.shape[1:], lambda b, n: (pick(n), 0, 0))

    kw, vw = 2 * LANE, 4 * LANE
    return pl.pallas_call(
        _lat_swa_kernel,
        grid=(DEC_BATCH, steps),
        in_specs=[_smem(),
                  bias_spec(lambda n: jnp.where(n == 0, 0, 1)), bias_spec(lambda n: jnp.where(n == steps - 1, 2, 1)),
                  pl.BlockSpec((None, 2 * W_BLOCK, 512), lambda b, n: (b, n, 0)),
                  ctx(kw), loc(-1, kw), loc(0, kw), loc(1, kw), loc(2, kw),
                  ctx(vw), loc(-1, vw), loc(0, vw), loc(1, vw), loc(2, vw)],
        out_specs=pl.BlockSpec((None, 2 * W_BLOCK, 512), lambda b, n: (b, n, 0)),
        out_shape=jax.ShapeDtypeStruct((DEC_BATCH, DEC_SEQ, 512), BF),
        compiler_params=_cp(("parallel", "arbitrary"), 40), name="lat_swa",
    )(sink, bias, bias, q, k, k, k, k, k, v, v, v, v, v)


def _rows_to_tiles(o_ref, val, tm):
    for k in range(D_CHUNKS):
        o_ref[pl.ds(k, tm, stride=D_CHUNKS), :] = val[:, k * LANE:(k + 1) * LANE]


def _post_kernel(x_ref, oda_ref, omla_ref, oswa_ref, gate_ref, mod_ref, n2_ref, wda_ref, wmla_ref, wswa_ref,
                 wo_ref, wrh_ref, wrl_ref, xo_ref, h2_ref, lg_ref):
    m = mod_ref[...]
    g = gate_ref[...].astype(F32)
    merged = (g[:, 0:D_MODEL] * _dot(oda_ref[...], wda_ref[...])
              + g[:, D_MODEL:2 * D_MODEL] * _dot(omla_ref[...], wmla_ref[...])
              + g[:, 2 * D_MODEL:] * _dot(oswa_ref[...], wswa_ref[...]))
    x = x_ref[...] + m[2:3] * _dot(merged.astype(BF), wo_ref[...])
    xo_ref[...] = x
    h2 = _rms(x, n2_ref[...]) * (1.0 + m[4:5]) + m[3:4]
    _rows_to_tiles(h2_ref, h2, TM_POST)
    hh, hl = _split(h2)
    lg_ref[...] = _dot_nt(wrh_ref[...], hh) + _dot_nt(wrl_ref[...], hh) + _dot_nt(wrh_ref[...], hl)


def _post_call(layer, x, oda, omla, oswa, gate, mod, n2, wda, wmla, wswa, wo, wrh, wrl, latent):
    n = x.shape[0]
    tm = TM_POST
    row = pl.BlockSpec((tm, D_MODEL), lambda i: (i, 0))
    o512 = pl.BlockSpec((tm, 512), lambda i: (i, 0))
    return pl.pallas_call(
        _post_kernel, grid=(n // tm,),
        in_specs=[row, o512, o512, o512, pl.BlockSpec((tm, GATE_COLS), lambda i: (i, 0)),
                  _mod_spec(layer, tm, latent), _layer(layer, (1, D_MODEL)),
                  _layer(layer, (512, D_MODEL)), _layer(layer, (512, D_MODEL)), _layer(layer, (512, D_MODEL)),
                  _layer(layer, (D_MODEL, D_MODEL)), _layer(layer, (N_EXPERTS, D_MODEL)),
                  _layer(layer, (N_EXPERTS, D_MODEL))],
        out_specs=[row, pl.BlockSpec((tm * D_CHUNKS, LANE), lambda i: (i, 0)),
                   pl.BlockSpec((N_EXPERTS, tm), lambda i: (0, i))],
        out_shape=[jax.ShapeDtypeStruct((n, D_MODEL), F32),
                   jax.ShapeDtypeStruct((n * D_CHUNKS, LANE), F32),
                   jax.ShapeDtypeStruct((N_EXPERTS, n), F32)],
        compiler_params=_cp(("parallel",), 48), name="post",
    )(x, oda, omla, oswa, gate, mod, n2, wda, wmla, wswa, wo, wrh, wrl)


def _router_kernel(cap, lg_ref, q_ref, a_ref, lo_ref, hi_ref, cnt_ref):
    e_n, n = lg_ref.shape
    lg = lg_ref[...]
    shifted = lg - jnp.max(lg, axis=0, keepdims=True)
    ex = jnp.exp(shifted)
    den = jnp.sum(ex, axis=0, keepdims=True)
    aff = ex / den
    score = shifted - jnp.log(den)
    capf = float(cap)

    def count(mask):
        return jnp.sum(jnp.where(mask, 1.0, 0.0), axis=1, keepdims=True)

    def halve(_, bounds):
        lo, hi = bounds
        mid = 0.5 * (lo + hi)
        enough = count(score >= mid) >= capf
        return jnp.where(enough, mid, lo), jnp.where(enough, hi, mid)

    lo, hi = lax.fori_loop(0, THRESHOLD_STEPS, halve,
                           (jnp.min(score, axis=1, keepdims=True), jnp.ones((e_n, 1), F32)))
    gt = score >= hi
    eq = (score >= lo) & (score < hi)
    need = capf - count(gt)
    tok = lax.broadcasted_iota(I32, (e_n, n), 1)
    n_bits = n.bit_length()

    def tie_bit(i, bound):
        cand = bound | jnp.left_shift(jnp.int32(1), n_bits - 1 - i)
        ok = (cand <= n) & (count(eq & (tok < cand)) <= need)
        return jnp.where(ok, cand, bound)

    bound = lax.fori_loop(0, n_bits, tie_bit, jnp.zeros((e_n, 1), I32))
    sel = gt | (eq & (tok < bound))
    cnt_ref[...] = jnp.broadcast_to(count(sel & (tok < TOK_HALF)), cnt_ref.shape)
    blk = 2 * LANE
    per_chunk = ROUTE_CHUNK // blk
    n_chunks = n // ROUTE_CHUNK
    upper = (lax.broadcasted_iota(I32, (blk, blk), 0) <= lax.broadcasted_iota(I32, (blk, blk), 1))
    upper = jnp.where(upper, 1.0, 0.0).astype(BF)
    lane = _lane((e_n, LANE))
    carry = jnp.zeros((e_n, 1), F32)
    c_in = jnp.zeros((e_n, LANE), F32)
    c_ex = jnp.zeros((e_n, LANE), F32)
    for c in range(n // blk):
        chunk, part = divmod(c, per_chunk)
        if part == 0:
            c_ex = jnp.where(lane == chunk, carry, c_ex)
            a_ref[chunk] = aff[:, chunk * ROUTE_CHUNK:(chunk + 1) * ROUTE_CHUNK]
        s_blk = sel[:, c * blk:(c + 1) * blk]
        rank = _dot(jnp.where(s_blk, 1.0, 0.0).astype(BF), upper) + carry
        q_ref[chunk, :, part * blk:(part + 1) * blk] = jnp.where(s_blk, rank, 0.0)
        carry = rank[:, blk - 1:blk]
        if part == per_chunk - 1:
            c_in = jnp.where(lane == chunk, carry, c_in)
    valid = lane < n_chunks
    lo = jnp.zeros((e_n, LANE), I32)
    hi = jnp.zeros((e_n, LANE), I32)
    for rt in range(cap // ROUTE_TILE):
        lo_rt = count(valid & (c_in < float(rt * ROUTE_TILE + 1)))
        hi_rt = count(valid & (c_ex < float((rt + 1) * ROUTE_TILE)))
        lo = jnp.where(lane == rt, lo_rt.astype(I32), lo)
        hi = jnp.where(lane == rt, hi_rt.astype(I32), hi)
    lo_ref[...] = lo
    hi_ref[...] = hi


def _compact_kernel(lo_ref, hi_ref, q_ref, a_ref, idx_ref, g_ref):
    e = pl.program_id(0)
    n_tiles = idx_ref.shape[0]
    eye = lax.broadcasted_iota(I32, (ROUTE_TILE, LANE), 0) == lax.broadcasted_iota(I32, (ROUTE_TILE, LANE), 1)

    def per_tile(rt, _):
        slot = (rt * ROUTE_TILE + 1 + lax.broadcasted_iota(I32, (ROUTE_TILE, 1), 0)).astype(F32)

        def per_chunk(c, acc):
            idx_acc, g_acc = acc
            q_row = q_ref[c, pl.ds(e, 1), :]
            a_row = a_ref[c, pl.ds(e, 1), :]
            for part in range(ROUTE_CHUNK // LANE):
                lanes = slice(part * LANE, (part + 1) * LANE)
                hit = q_row[:, lanes] == slot
                tok = (c * ROUTE_CHUNK + part * LANE + lax.broadcasted_iota(I32, (1, LANE), 1)).astype(F32)
                idx_acc = idx_acc + jnp.where(hit, tok, 0.0)
                g_acc = g_acc + jnp.where(hit, a_row[:, lanes], 0.0)
            return idx_acc, g_acc

        zero = jnp.zeros((ROUTE_TILE, LANE), F32)
        idx_acc, g_acc = lax.fori_loop(lo_ref[e, rt], hi_ref[e, rt], per_chunk, (zero, zero))
        idx_v = jnp.sum(idx_acc, axis=1, keepdims=True)
        g_v = jnp.sum(g_acc, axis=1, keepdims=True)
        idx_ref[pl.ds(rt, 1), :] = jnp.sum(jnp.where(eye, idx_v, 0.0), axis=0, keepdims=True).astype(I32)
        g_ref[pl.ds(rt, 1), :] = jnp.sum(jnp.where(eye, g_v, 0.0), axis=0, keepdims=True)
        return 0

    lax.fori_loop(0, n_tiles, per_tile, 0)


def _router_call(logits_t, cap):
    n = logits_t.shape[1]
    n_chunks = n // ROUTE_CHUNK
    n_tiles = cap // ROUTE_TILE
    chunked = jax.ShapeDtypeStruct((n_chunks, N_EXPERTS, ROUTE_CHUNK), F32)
    q, a, lo, hi, cnt = pl.pallas_call(
        functools.partial(_router_kernel, cap),
        out_shape=[chunked, chunked, jax.ShapeDtypeStruct((N_EXPERTS, LANE), I32),
                   jax.ShapeDtypeStruct((N_EXPERTS, LANE), I32), jax.ShapeDtypeStruct((N_EXPERTS, LANE), F32)],
        compiler_params=pltpu.CompilerParams(vmem_limit_bytes=40 * 1024 * 1024), name="router",
    )(logits_t)
    whole = pl.BlockSpec((n_chunks, N_EXPERTS, ROUTE_CHUNK), lambda e, *_: (0, 0, 0))
    slots = pl.BlockSpec((None, n_tiles, ROUTE_TILE), lambda e, *_: (e, 0, 0))
    idx, g = pl.pallas_call(
        _compact_kernel,
        grid_spec=pltpu.PrefetchScalarGridSpec(
            num_scalar_prefetch=2, grid=(N_EXPERTS,), in_specs=[whole, whole], out_specs=[slots, slots]),
        out_shape=[jax.ShapeDtypeStruct((N_EXPERTS, n_tiles, ROUTE_TILE), I32),
                   jax.ShapeDtypeStruct((N_EXPERTS, n_tiles, ROUTE_TILE), F32)],
        compiler_params=_cp(("parallel",), 32), name="compact",
    )(lo[:, :n_tiles], hi[:, :n_tiles], q, a)
    return idx.reshape(N_EXPERTS, cap), g.reshape(N_EXPERTS, cap), cnt


def _row_copy(src_hbm, xe, sem, buf, tok, slot):
    dst = xe.at[buf, pl.ds(pl.multiple_of(slot * SUB, SUB), SUB)]
    return pltpu.make_async_copy(src_hbm.at[tok], dst, sem.at[buf])


def _rows_wait(rows_hbm, xe, sem, buf, slot, n):
    dst = xe.at[buf, pl.ds(pl.multiple_of(slot * SUB, SUB), n * SUB)]
    return pltpu.make_async_copy(rows_hbm.at[pl.ds(0, n * SUB)], dst, sem.at[buf])


def _ffn_kernel(idx_ref, idxn_ref, hc_hbm, hl_hbm, rows_hbm, w1_ref, w3_ref, w2_ref, yc_ref, yl_ref,
                xe, xb, y_part, w1b, w3b, w2b, sem):
    e = pl.program_id(0)
    f = pl.program_id(1)
    buf = e % 2

    @pl.when((e == 0) & (f == 0))
    def _():
        def start_ctx(r, _):
            _row_copy(hc_hbm, xe, sem, 0, idx_ref[0, r], r).start()
            return 0

        def start_lat(r, _):
            _row_copy(hl_hbm, xe, sem, 0, idx_ref[0, r], r).start()
            return 0

        lax.fori_loop(0, CAP_CTX, start_ctx, 0)
        lax.fori_loop(CAP_CTX, SLOTS, start_lat, 0)
        _rows_wait(rows_hbm, xe, sem, 0, 0, SLOTS).wait()

    @pl.when(f == 0)
    def _():
        for k in range(D_CHUNKS):
            xb[:, k * LANE:(k + 1) * LANE] = xe[buf, pl.ds(k, SLOTS, stride=D_CHUNKS), :].astype(BF)

    w1b[...] = w1_ref[...].astype(BF)
    w3b[...] = w3_ref[...].astype(BF)
    w2b[...] = w2_ref[...].astype(BF)

    part_c, part_l = CAP_CTX // FF_SPLIT, CAP_LAT // FF_SPLIT
    next_rows = ([(hc_hbm, f * part_c + u) for u in range(part_c)]
                 + [(hl_hbm, CAP_CTX + f * part_l + u) for u in range(part_l)])

    def next_copy(src, r):
        return _row_copy(src, xe, sem, 1 - buf, idxn_ref[0, r], r)

    n_tiles = SLOTS // FFN_ROWS
    for t in range(n_tiles):
        if t < n_tiles - 1:
            for i, (src, r) in enumerate(next_rows[t::n_tiles - 1]):
                next_copy(src, r).start(priority=i % 2)
        rows = slice(t * FFN_ROWS, (t + 1) * FFN_ROWS)
        x = xb[rows, :]
        a = _dot(x, w1b[...])
        hid = (a * jax.nn.sigmoid(a) * _dot(x, w3b[...])).astype(BF)
        y = _dot(hid, w2b[...])
        lo, hi = t * FFN_ROWS, (t + 1) * FFN_ROWS
        segs = []
        if lo < CAP_CTX:
            segs.append((yc_ref, lo, 0, min(hi, CAP_CTX) - lo))
        if hi > CAP_CTX:
            first = max(lo, CAP_CTX)
            segs.append((yl_ref, first - CAP_CTX, first - lo, hi - first))

        @pl.when(f == 0)
        def _():
            y_part[rows, :] = y

        if FF_SPLIT > 2:
            @pl.when((f > 0) & (f < FF_SPLIT - 1))
            def _():
                y_part[rows, :] = y_part[rows, :] + y

        @pl.when(f == FF_SPLIT - 1)
        def _():
            total = y + y_part[rows, :]
            for y_ref, row0, y0, n in segs:
                for k in range(D_CHUNKS):
                    y_ref[pl.ds(row0 * D_CHUNKS + k, n, stride=D_CHUNKS), :] = total[y0:y0 + n, k * LANE:(k + 1) * LANE]

    _rows_wait(rows_hbm, xe, sem, 1 - buf, f * part_c, part_c).wait()
    _rows_wait(rows_hbm, xe, sem, 1 - buf, CAP_CTX + f * part_l, part_l).wait()


def _ffn_call(layer, idx, h_ctx, h_lat, w1, w3, w2):
    ff = EXPERT_FF // FF_SPLIT
    wspec = pl.BlockSpec((None, None, D_MODEL, ff), lambda e, f: (layer, e, 0, f))
    last = N_EXPERTS - 1
    return pl.pallas_call(
        _ffn_kernel,
        grid=(N_EXPERTS, FF_SPLIT),
        in_specs=[pl.BlockSpec((None, 1, SLOTS), lambda e, f: (e, 0, 0), memory_space=pltpu.SMEM),
                  pl.BlockSpec((None, 1, SLOTS), lambda e, f: (jnp.minimum(e + 1, last), 0, 0),
                               memory_space=pltpu.SMEM),
                  pl.BlockSpec(memory_space=pl.ANY), pl.BlockSpec(memory_space=pl.ANY),
                  pl.BlockSpec(memory_space=pl.ANY), wspec, wspec, pl.BlockSpec((None, None, ff, D_MODEL), lambda e, f: (layer, e, f, 0))],
        out_specs=[pl.BlockSpec((None, CAP_CTX * D_CHUNKS, LANE), lambda e, f: (e, 0, 0)),
                   pl.BlockSpec((None, CAP_LAT * D_CHUNKS, LANE), lambda e, f: (e, 0, 0))],
        out_shape=[jax.ShapeDtypeStruct((N_EXPERTS, CAP_CTX * D_CHUNKS, LANE), F32),
                   jax.ShapeDtypeStruct((N_EXPERTS, CAP_LAT * D_CHUNKS, LANE), F32)],
        scratch_shapes=[pltpu.VMEM((2, SLOTS * D_CHUNKS, LANE), F32), pltpu.VMEM((SLOTS, D_MODEL), BF),
                        pltpu.VMEM((SLOTS, D_MODEL), F32),
                        pltpu.VMEM((D_MODEL, ff), BF), pltpu.VMEM((D_MODEL, ff), BF),
                        pltpu.VMEM((ff, D_MODEL), BF), pltpu.SemaphoreType.DMA((2,))],
        compiler_params=_cp(("arbitrary", "arbitrary"), 58), name="ffn",
    )(idx, idx, h_ctx, h_lat, h_lat.reshape(N_LAT * SUB, LANE), w1, w3, w2)


def _combine_kernel(idx_ref, g_ref, bnd_ref, yc_ref, yl_ref, acc_ref):
    h = pl.program_id(0)
    e = pl.program_id(1)

    @pl.when(e == 0)
    def _():
        acc_ref[...] = jnp.zeros(acc_ref.shape, F32)

    def add_rows(y_ref, slot0, lo, hi, base):
        def group(i, _):
            r0 = lo + i * COMBINE_GROUP
            toks = [idx_ref[0, r0 + u] - base for u in range(COMBINE_GROUP)]
            sums = [acc_ref[toks[u]] + y_ref[r0 - slot0 + u] * g_ref[0, r0 + u] for u in range(COMBINE_GROUP)]
            for u in range(COMBINE_GROUP):
                acc_ref[toks[u]] = sums[u]
            return 0

        def single(r, _):
            t = idx_ref[0, r] - base
            acc_ref[t] = acc_ref[t] + y_ref[r - slot0] * g_ref[0, r]
            return 0

        n_groups = (hi - lo) // COMBINE_GROUP
        lax.fori_loop(0, n_groups, group, 0)
        lax.fori_loop(lo + n_groups * COMBINE_GROUP, hi, single, 0)

    @pl.when(h == 0)
    def _():
        add_rows(yc_ref, 0, 0, CAP_CTX, 0)

    @pl.when(h > 0)
    def _():
        add_rows(yl_ref, CAP_CTX, bnd_ref[0, h], bnd_ref[0, h + 1], (h - 1) * TOK_HALF)


def _combine_call(idx, g, bnd, y_ctx, y_lat):
    last = N_EXPERTS - 1
    return pl.pallas_call(
        _combine_kernel,
        grid=(N_HALVES, N_EXPERTS),
        in_specs=[pl.BlockSpec((None, 1, SLOTS), lambda h, e: (e, 0, 0), memory_space=pltpu.SMEM),
                  pl.BlockSpec((None, 1, SLOTS), lambda h, e: (e, 0, 0), memory_space=pltpu.SMEM),
                  pl.BlockSpec((None, 1, N_HALVES + 1), lambda h, e: (e, 0, 0), memory_space=pltpu.SMEM),
                  pl.BlockSpec((None, CAP_CTX, SUB, LANE), lambda h, e: (jnp.where(h == 0, e, last), 0, 0, 0)),
                  pl.BlockSpec((None, CAP_LAT, SUB, LANE), lambda h, e: (jnp.where(h == 0, 0, e), 0, 0, 0))],
        out_specs=pl.BlockSpec((None, TOK_HALF, SUB, LANE), lambda h, e: (h, 0, 0, 0)),
        out_shape=jax.ShapeDtypeStruct((N_HALVES, TOK_HALF, SUB, LANE), F32),
        compiler_params=_cp(("parallel", "arbitrary"), 52), name="combine",
    )(idx, g, bnd, y_ctx.reshape(N_EXPERTS, CAP_CTX, SUB, LANE), y_lat.reshape(N_EXPERTS, CAP_LAT, SUB, LANE))


def _final_kernel(x_ref, moe_ref, mod_ref, g_ref, o_ref):
    x = x_ref[...] + mod_ref[...][5:6] * _moe_rows(moe_ref, TM_FINAL)
    o_ref[...] = _rms(x, g_ref[...])


def _final_call(x, moe, mod, g, latent):
    n = x.shape[0]
    tm = TM_FINAL
    moe_off = (N_CTX // tm) if latent else 0
    row = pl.BlockSpec((tm, D_MODEL), lambda i: (i, 0))
    return pl.pallas_call(
        _final_kernel, grid=(n // tm,),
        in_specs=[row, pl.BlockSpec((tm * D_CHUNKS, LANE), lambda i: (i + moe_off, 0)),
                  _mod_spec(DEPTH - 1, tm, latent), _layer(0, (1, D_MODEL))],
        out_specs=row, out_shape=jax.ShapeDtypeStruct((n, D_MODEL), F32),
        compiler_params=_cp(("parallel",), 32), name="final",
    )(x, moe, mod, g)


def _axial_tables(rot_dim):
    rows = DEC_SEQ // GRID_W
    row = jnp.repeat(jnp.arange(rows, dtype=F32), GRID_W)
    col = jnp.tile(jnp.arange(GRID_W, dtype=F32), rows)
    n_freq = rot_dim // 4
    inv = ROPE_BASE ** (-jnp.arange(n_freq, dtype=F32) / n_freq)
    ang = jnp.concatenate([row[:, None] * inv, col[:, None] * inv], axis=-1)
    sign = jnp.tile(jnp.array([-1.0, 1.0], F32), rot_dim // 2)
    return jnp.repeat(jnp.cos(ang), 2, axis=1), jnp.repeat(jnp.sin(ang), 2, axis=1) * sign


def _rope_tables():
    c64, s64 = _axial_tables(DA_DH)
    c32, s32 = _axial_tables(MLA_ROPE)
    one = lambda w: jnp.ones((DEC_SEQ, w), F32)
    zero = lambda w: jnp.zeros((DEC_SEQ, w), F32)
    pad = LANE - MLA_NOPE - MLA_ROPE
    return jnp.stack([
        jnp.tile(c64, (1, 2)), jnp.tile(s64, (1, 2)),
        jnp.concatenate([one(MLA_NOPE), c32, one(pad)], axis=1),
        jnp.concatenate([zero(MLA_NOPE), s32, zero(pad)], axis=1),
        jnp.concatenate([c32, one(LANE - MLA_ROPE)], axis=1),
        jnp.concatenate([s32, zero(LANE - MLA_ROPE)], axis=1)])


def _prep_weights(w_in, w_gate, mla_w_qb, mla_w_kvb, w_br_da, w_br_mla, w_br_swa, w_o, w_router):
    kr_end = C_MKR + MLA_ROPE
    win = jnp.concatenate([w_in[:, :, :kr_end].astype(BF), jnp.zeros((DEPTH, D_MODEL, LANE - MLA_ROPE), BF),
                           w_in[:, :, kr_end:].astype(BF)], axis=2)
    dk = MLA_NOPE + MLA_ROPE
    wqb = jnp.pad(mla_w_qb.reshape(DEPTH, MLA_Q_RANK, MLA_HEADS, dk), ((0, 0), (0, 0), (0, 0), (0, LANE - dk)))
    wqb = wqb.reshape(DEPTH, MLA_Q_RANK, MLA_HEADS * LANE).astype(BF)
    kvb = mla_w_kvb.reshape(DEPTH, MLA_KV_RANK, MLA_HEADS, MLA_NOPE + MLA_V)
    wk = jnp.pad(kvb[..., :MLA_NOPE], ((0, 0), (0, 0), (0, 0), (0, LANE - MLA_NOPE)))
    wk = wk.reshape(DEPTH, MLA_KV_RANK, MLA_HEADS * LANE)
    wv = kvb[..., MLA_NOPE:].reshape(DEPTH, MLA_KV_RANK, MLA_OUT)
    wkv = jnp.concatenate([wk, wv], axis=2).astype(BF)
    wrh, wrl = _split(jnp.swapaxes(w_router, 1, 2))
    return dict(win=win, wg=w_gate.astype(BF), wqb=wqb, wkv=wkv, wda=w_br_da.astype(BF),
                wmla=w_br_mla.astype(BF), wswa=w_br_swa.astype(BF), wo=w_o.astype(BF), wrh=wrh, wrl=wrl)


def kernel(x_prompt, x_sample, cache_da_k, cache_da_v, cache_mla_ckv, cache_mla_krope, cache_swa_k, cache_swa_v, c, c_ctx, w_ada, b_ada, norm1, norm2, w_in, da_lq1, da_lk1, da_lq2, da_lk2, da_subln, mla_q_norm, mla_w_qb, mla_kv_norm, mla_w_kvb, swa_sink, w_gate, w_br_da, w_br_mla, w_br_swa, w_o, w_router, w_ff1, w_ff3, w_ff2, final_norm):
    c_all = jnp.concatenate([c_ctx[None], c, jnp.zeros((SUB - 1 - DEC_BATCH, D_MODEL), F32)], axis=0)
    mod = _ada_call(c_all, w_ada, b_ada)[:, :1 + DEC_BATCH].reshape(DEPTH, 1 + DEC_BATCH, 6, D_MODEL)
    lam_all = _lam_call(da_lq1, da_lk1, da_lq2, da_lk2)
    tab = _rope_tables()
    swa_bias = _swa_bias()
    w = _prep_weights(w_in, w_gate, mla_w_qb, mla_w_kvb, w_br_da, w_br_mla, w_br_swa, w_o, w_router)
    n1, n2 = norm1[:, None], norm2[:, None]
    subln, qn, kvn = da_subln[:, None], mla_q_norm[:, None], mla_kv_norm[:, None]
    caches = (cache_da_k.reshape(DEC_BATCH, DEPTH, PAST_LEN, 512),
              cache_da_v.reshape(DEC_BATCH, DEPTH, PAST_LEN, 512),
              cache_mla_ckv,
              jnp.pad(cache_mla_krope, ((0, 0), (0, 0), (0, 0), (0, LANE - MLA_ROPE))),
              cache_swa_k.reshape(DEC_BATCH, DEPTH, PAST_LEN, LANE),
              cache_swa_v.reshape(DEC_BATCH, DEPTH, PAST_LEN, LANE))

    xp = x_prompt.reshape(N_CTX, D_MODEL)
    xs = x_sample.reshape(N_LAT, D_MODEL)
    moe = None
    new = [[] for _ in range(6)]
    for l in range(DEPTH):
        lam = lam_all[l, :1]
        sink = swa_sink[l]

        xp, proj_c, gate_c = _pre_call(l, xp, moe, mod, n1, w["win"], w["wg"], latent=False)
        xs, proj_l, gate_l = _pre_call(l, xs, moe, mod, n1, w["win"], w["wg"], latent=True)

        oda_c, omla_c, oswa_c, ckv_c, k_c, v_c, kr_c, sk_c, sv_c = _ctx_attn_call(
            l, lam, sink, proj_c, subln, qn, w["wqb"], kvn, w["wkv"])
        new[0].append(k_c.reshape(BATCH, SEQ, DA_HEADS, 2 * DA_DH))
        new[1].append(v_c.reshape(BATCH, SEQ, DA_HEADS, 2 * DA_DH))
        new[2].append(ckv_c.reshape(BATCH, SEQ, MLA_KV_RANK))
        new[3].append(kr_c.reshape(BATCH, SEQ, MLA_ROPE))
        new[4].append(sk_c.reshape(BATCH, SEQ, SWA_KV_HEADS, SWA_DH))
        new[5].append(sv_c.reshape(BATCH, SEQ, SWA_KV_HEADS, SWA_DH))

        daq, mlaq, swaq, dak, dav, mlak, mlav, swak, swav = _lat_prep_call(
            l, proj_l, tab, caches, qn, w["wqb"], kvn, w["wkv"])
        oda_l = _lat_da_call(l, lam, daq, dak, dav, subln).reshape(N_LAT, 512)
        omla_l = _lat_mla_call(mlaq, mlak, mlav).reshape(N_LAT, 512)
        oswa_l = _lat_swa_call(sink, swa_bias, swaq, swak, swav).reshape(N_LAT, 512)

        post_w = (w["wda"], w["wmla"], w["wswa"], w["wo"], w["wrh"], w["wrl"])
        xp, h2_c, lg_c = _post_call(l, xp, oda_c, omla_c, oswa_c, gate_c, mod, n2, *post_w, latent=False)
        xs, h2_l, lg_l = _post_call(l, xs, oda_l, omla_l, oswa_l, gate_l, mod, n2, *post_w, latent=True)

        idx_c, g_c, _ = _router_call(lg_c, CAP_CTX)
        idx_l, g_l, cnt_l = _router_call(lg_l, CAP_LAT)
        idx = jnp.concatenate([idx_c, idx_l], axis=1).reshape(N_EXPERTS, 1, SLOTS)
        gsel = jnp.concatenate([g_c, g_l], axis=1).reshape(N_EXPERTS, 1, SLOTS)
        n0 = cnt_l[:, 0].astype(I32)
        bnd = jnp.stack([jnp.zeros_like(n0), jnp.full_like(n0, CAP_CTX), CAP_CTX + n0,
                         jnp.full_like(n0, SLOTS)], axis=1)
        y_c, y_l = _ffn_call(l, idx, h2_c.reshape(N_CTX, SUB, LANE), h2_l.reshape(N_LAT, SUB, LANE),
                             w_ff1, w_ff3, w_ff2)
        acc = _combine_call(idx, gsel, bnd.reshape(N_EXPERTS, 1, N_HALVES + 1), y_c, y_l)
        moe = acc.reshape(N_HALVES * TOK_HALF * D_CHUNKS, LANE)

    fn = final_norm[None, None]
    y_prompt = _final_call(xp, moe, mod, fn, latent=False).reshape(BATCH, SEQ, D_MODEL)
    y_sample = _final_call(xs, moe, mod, fn, latent=True).reshape(DEC_BATCH, DEC_SEQ, D_MODEL)
    return (y_prompt, y_sample) + tuple(jnp.stack(n, axis=1) for n in new)
```
